```python
import math
import jax
import jax.numpy as jnp
from jax import lax
import numpy as np

D_MODEL = 2048
BATCH = 4
SEQ = 4096
DEPTH = 2

N_MEM = 256
MEM_HEADS = 4
MEM_HEAD_DIM = D_MODEL // MEM_HEADS
D_FF = ((8 * D_MODEL // 3 + 255) // 256) * 256
MACARON_WEIGHT = 0.5
HYENA_WIDTH = D_MODEL // 2
HYENA_ORDER = 2
HYENA_SHORT_CONV = 3
FILTER_EMB_DIM = 33
FILTER_BANDS = (FILTER_EMB_DIM - 1) // 2
FILTER_HIDDEN = 64
DECAY_TARGET = 1e-2
FAST_DECAY_PCT = 0.3
SLOW_DECAY_PCT = 1.5
N_DIRECTIONS = 2
FNET_WIDTH = D_MODEL // 4
FNET_GROUPS = 4
FNET_GROUP = FNET_WIDTH // FNET_GROUPS
POOL_WINDOWS = (2, 4, 8, 16)
N_POOL = len(POOL_WINDOWS)
POOL_WIDTH = D_MODEL // 4
POOL_GROUP = POOL_WIDTH // N_POOL
POOL_OUT_GROUP = D_MODEL // N_POOL
N_BRANCH = 3
HYENA_COLS = (1 + HYENA_ORDER) * HYENA_WIDTH
FNET_START = HYENA_COLS
POOL_START = FNET_START + FNET_WIDTH
GATE_START = POOL_START + POOL_WIDTH
IN_COLS = GATE_START + N_BRANCH * D_MODEL
RMS_EPS = 1e-6

kernel_name = 'hybrid_hyena_fnet_pool_encoder'


def rms_norm(x, g):
    xf = x.astype(jnp.float32)
    y = xf * lax.rsqrt(jnp.mean(xf * xf, axis=-1, keepdims=True) + RMS_EPS)
    return (y * g.astype(jnp.float32)).astype(x.dtype)


def half_step_swiglu(x, g_pre, w_gu, w_down, g_post):
    a, b = jnp.split(rms_norm(x, g_pre) @ w_gu, 2, axis=-1)
    return x + MACARON_WEIGHT * rms_norm((jax.nn.silu(a) * b) @ w_down, g_post)


def centred_short_conv(u, w, b):
    L = u.shape[1]
    pad = HYENA_SHORT_CONV // 2
    up = jnp.pad(u, ((0, 0), (pad, HYENA_SHORT_CONV - 1 - pad), (0, 0)))
    y = b + up[:, 0:L] * w[0]
    for j in range(1, HYENA_SHORT_CONV):
        y = y + up[:, j:j + L] * w[j]
    return y


def hyena_filter_spectrum(L, fw1, fb1, fw2, fb2, fw3, fb3, fw4, freq):
    f32 = jnp.float32
    t = jnp.linspace(0.0, 1.0, L, dtype=f32)[:, None]
    bands = jnp.linspace(1e-4, FILTER_BANDS - 1, FILTER_BANDS, dtype=f32)
    ang = (2.0 * math.pi / L) * jnp.arange(L, dtype=f32)[:, None] * bands[None, :]
    z = jnp.concatenate([t, jnp.cos(ang), -jnp.sin(ang)], axis=-1)
    fr = freq.astype(f32)
    h = jnp.sin(fr * (z @ fw1.astype(f32) + fb1.astype(f32)))
    h = jnp.sin(fr * (h @ fw2.astype(f32) + fb2.astype(f32)))
    h = jnp.sin(fr * (h @ fw3.astype(f32) + fb3.astype(f32)))
    h = (h @ fw4.astype(f32)).reshape(L, HYENA_ORDER, N_DIRECTIONS, HYENA_WIDTH)
    max_decay = math.log(DECAY_TARGET) / FAST_DECAY_PCT
    min_decay = math.log(DECAY_TARGET) / SLOW_DECAY_PCT
    deltas = jnp.abs(jnp.linspace(min_decay, max_decay, HYENA_WIDTH, dtype=f32))
    h = h * jnp.exp(-t * deltas)[:, None, None, :]
    fwd = h[:, :, 0]
    bwd = h[:, :, 1]
    two_sided = jnp.concatenate(
        [fwd, jnp.zeros((1, HYENA_ORDER, HYENA_WIDTH), f32), bwd[:0:-1]], axis=0)
    return jnp.fft.rfft(two_sided, axis=0)


def two_sided_fft_conv(z, spec):
    L = z.shape[1]
    zf = jnp.fft.rfft(z, n=2 * L, axis=1)
    return jnp.fft.irfft(zf * spec[None], n=2 * L, axis=1)[:, :L]


def hyena_branch(hy, spec, d_skip):
    parts = jnp.split(hy, 1 + HYENA_ORDER, axis=-1)
    z = parts[0]
    for o in range(HYENA_ORDER):
        z = parts[1 + o] * (two_sided_fft_conv(z, spec[:, o]) + z * d_skip[o])
    return z


def fnet_branch(f):
    B, L, _ = f.shape
    fg = f.astype(jnp.float32).reshape(B, L, FNET_GROUPS, FNET_GROUP)
    y = jnp.fft.fftn(fg, axes=(1, 3), norm='ortho').real
    return y.reshape(B, L, FNET_WIDTH).astype(f.dtype)


def pool_branch(p, w_pool, scale):
    B, L, _ = p.shape
    pg = p.astype(jnp.float32).reshape(B, L, N_POOL, POOL_GROUP)
    csum = jnp.pad(jnp.cumsum(pg, axis=1), ((0, 0), (1, 0), (0, 0), (0, 0)))
    pos = jnp.arange(L)
    outs = []
    for g, w in enumerate(POOL_WINDOWS):
        before = w // 2
        after = w - 1 - before
        lo = jnp.clip(pos - before, 0, L - 1)
        hi = jnp.clip(pos + after, 0, L - 1)
        window_sum = jnp.take(csum[:, :, g], hi + 1, axis=1) - jnp.take(csum[:, :, g], lo, axis=1)
        count = (hi - lo + 1).astype(jnp.float32)[None, :, None]
        outs.append(window_sum / count - pg[:, :, g])
    m = jnp.stack(outs, axis=2).astype(p.dtype)
    y = jnp.einsum('blgc,gcd->blgd', m, w_pool).reshape(B, L, N_POOL * POOL_OUT_GROUP)
    return y * scale


def memory_cross_attention(h, m, w_q, w_kv, w_o):
    B, L, D = h.shape
    q = (h @ w_q).reshape(B, L, MEM_HEADS, MEM_HEAD_DIM)
    kv = (m @ w_kv).reshape(B, m.shape[1], 2, MEM_HEADS, MEM_HEAD_DIM)
    k = kv[:, :, 0]
    v = kv[:, :, 1]
    s = jnp.einsum('bqhd,bkhd->bhqk', q, k, preferred_element_type=jnp.float32)
    p = jax.nn.softmax(s * (MEM_HEAD_DIM ** -0.5), axis=-1).astype(h.dtype)
    o = jnp.einsum('bhqk,bkhd->bqhd', p, v).reshape(B, L, D)
    return o @ w_o


def setup_inputs(seed: int = 0) -> dict:
    key = jax.random.key(seed)
    keys = iter(jax.random.split(key, 40))
    f32 = jnp.float32

    def nrm(shape, scale):
        return scale * jax.random.normal(next(keys), shape, f32)

    def gain(shape):
        return 1.0 + 0.05 * jax.random.normal(next(keys), shape, f32)

    Lr = DEPTH
    D = D_MODEL
    return {
        'x': nrm((BATCH, SEQ, D), 1.0),
        'mem': nrm((BATCH, N_MEM, D), 1.0),
        'g_ffn1_pre': gain((Lr, D)),
        'w_ffn1_gu': nrm((Lr, D, 2 * D_FF), D ** -0.5),
        'w_ffn1_down': nrm((Lr, D_FF, D), D_FF ** -0.5),
        'g_ffn1_post': gain((Lr, D)),
        'g_mix_pre': gain((Lr, D)),
        'w_in': nrm((Lr, D, IN_COLS), D ** -0.5),
        'hyena_conv_w': nrm((Lr, HYENA_SHORT_CONV, HYENA_COLS), HYENA_SHORT_CONV ** -0.5),
        'hyena_conv_b': nrm((Lr, HYENA_COLS), 0.02),
        'filt_w1': nrm((Lr, FILTER_EMB_DIM, FILTER_HIDDEN), FILTER_EMB_DIM ** -0.5),
        'filt_b1': nrm((Lr, FILTER_HIDDEN), 0.02),
        'filt_w2': nrm((Lr, FILTER_HIDDEN, FILTER_HIDDEN), FILTER_HIDDEN ** -0.5),
        'filt_b2': nrm((Lr, FILTER_HIDDEN), 0.02),
        'filt_w3': nrm((Lr, FILTER_HIDDEN, FILTER_HIDDEN), FILTER_HIDDEN ** -0.5),
        'filt_b3': nrm((Lr, FILTER_HIDDEN), 0.02),
        'filt_w4': nrm((Lr, FILTER_HIDDEN, HYENA_ORDER * N_DIRECTIONS * HYENA_WIDTH),
                       0.05 * FILTER_HIDDEN ** -0.5),
        'filt_freq': gain((Lr, FILTER_HIDDEN)),
        'hyena_d': nrm((Lr, HYENA_ORDER, HYENA_WIDTH), 0.5),
        'w_hyena_out': nrm((Lr, HYENA_WIDTH, D), HYENA_WIDTH ** -0.5),
        'w_fnet_out': nrm((Lr, FNET_WIDTH, D), FNET_WIDTH ** -0.5),
        'w_pool': nrm((Lr, N_POOL, POOL_GROUP, POOL_OUT_GROUP), POOL_GROUP ** -0.5),
        'pool_scale': gain((Lr, D)),
        'w_out': nrm((Lr, D, D), D ** -0.5),
        'g_mix_post': gain((Lr, D)),
        'g_mem_pre': gain((Lr, D)),
        'g_mem_kv': gain((Lr, D)),
        'w_q': nrm((Lr, D, D), D ** -0.5),
        'w_kv': nrm((Lr, D, 2 * D), D ** -0.5),
        'w_o': nrm((Lr, D, D), D ** -0.5),
        'g_mem_post': gain((Lr, D)),
        'g_ffn2_pre': gain((Lr, D)),
        'w_ffn2_gu': nrm((Lr, D, 2 * D_FF), D ** -0.5),
        'w_ffn2_down': nrm((Lr, D_FF, D), D_FF ** -0.5),
        'g_ffn2_post': gain((Lr, D)),
    }


def reference(x, mem, g_ffn1_pre, w_ffn1_gu, w_ffn1_down, g_ffn1_post, g_mix_pre, w_in,
              hyena_conv_w, hyena_conv_b, filt_w1, filt_b1, filt_w2, filt_b2, filt_w3, filt_b3,
              filt_w4, filt_freq, hyena_d, w_hyena_out, w_fnet_out, w_pool, pool_scale, w_out,
              g_mix_post, g_mem_pre, g_mem_kv, w_q, w_kv, w_o, g_mem_post,
              g_ffn2_pre, w_ffn2_gu, w_ffn2_down, g_ffn2_post):
    dt = x.dtype
    B, L, D = x.shape
    for l in range(DEPTH):
        x = half_step_swiglu(x, g_ffn1_pre[l], w_ffn1_gu[l], w_ffn1_down[l], g_ffn1_post[l])

        u = rms_norm(x, g_mix_pre[l])
        proj = u @ w_in[l]
        hy = proj[..., :FNET_START]
        fn = proj[..., FNET_START:POOL_START]
        po = proj[..., POOL_START:GATE_START]
        gates = jax.nn.sigmoid(proj[..., GATE_START:].astype(jnp.float32))
        gates = gates.reshape(B, L, N_BRANCH, D).astype(dt)

        hy = centred_short_conv(hy, hyena_conv_w[l], hyena_conv_b[l]).astype(jnp.float32)
        spec = hyena_filter_spectrum(L, filt_w1[l], filt_b1[l], filt_w2[l], filt_b2[l],
                                     filt_w3[l], filt_b3[l], filt_w4[l], filt_freq[l])
        y_a = hyena_branch(hy, spec, hyena_d[l].astype(jnp.float32)).astype(dt) @ w_hyena_out[l]
        y_b = fnet_branch(fn) @ w_fnet_out[l]
        y_c = pool_branch(po, w_pool[l], pool_scale[l])

        merged = gates[:, :, 0] * y_a + gates[:, :, 1] * y_b + gates[:, :, 2] * y_c
        x = x + rms_norm(merged @ w_out[l], g_mix_post[l])

        h = rms_norm(x, g_mem_pre[l])
        m = rms_norm(mem, g_mem_kv[l])
        x = x + rms_norm(memory_cross_attention(h, m, w_q[l], w_kv[l], w_o[l]), g_mem_post[l])

        x = half_step_swiglu(x, g_ffn2_pre[l], w_ffn2_gu[l], w_ffn2_down[l], g_ffn2_post[l])
    return x
```

```python
import functools
import math

import jax
import jax.numpy as jnp
import numpy as np
from jax import lax
from jax.experimental import pallas as pl
from jax.experimental.pallas import tpu as pltpu

F32 = jnp.float32
BF16 = jnp.bfloat16

RMS_EPS = 1e-6
MACARON_WEIGHT = 0.5
N_HEADS = 4
POOL_WINDOWS = (2, 4, 8, 16)
FNET_GROUPS = 4
FILTER_BANDS = 16
DECAY_TARGET = 1e-2
FAST_DECAY_PCT = 0.3
SLOW_DECAY_PCT = 1.5

V7X_SUBLANES = 8
V7X_BF16_ROWS = 16
V7X_VMEM_LIMIT = 60 * 2**20

HYENA_NB = 128
HALO = V7X_BF16_ROWS


def _params(n_axes):
    return pltpu.CompilerParams(
        dimension_semantics=("arbitrary",) * n_axes,
        vmem_limit_bytes=V7X_VMEM_LIMIT,
    )


def _resident(shape):
    zeros = (0,) * len(shape)
    return pl.BlockSpec(shape, lambda *_: zeros, pipeline_mode=pl.Buffered(1))


def _rms(x, g):
    ms = jnp.mean(x * x, axis=-1, keepdims=True)
    return x * lax.rsqrt(ms + RMS_EPS) * g


def _dot(a, b):
    return jnp.dot(a, b, preferred_element_type=F32)


def _ffn_body(x_ref, gpre_ref, wg_ref, wu_ref, wd_ref, gpost_ref, o_ref, u_ref, acc_ref):
    k = pl.program_id(1)

    @pl.when(k == 0)
    def _():
        u_ref[...] = _rms(x_ref[...], gpre_ref[...]).astype(BF16)
        acc_ref[...] = jnp.zeros_like(acc_ref)

    u = u_ref[...]
    a = _dot(u, wg_ref[...])
    b = _dot(u, wu_ref[...])
    h = (a * (1.0 / (1.0 + jnp.exp(-a))) * b).astype(BF16)
    acc_ref[...] += _dot(h, wd_ref[...])

    @pl.when(k == pl.num_programs(1) - 1)
    def _():
        o_ref[...] = x_ref[...] + MACARON_WEIGHT * _rms(acc_ref[...], gpost_ref[...])


def _ffn(x2, g_pre, w_gu, w_down, g_post, *, tm, tf):
    n, d = x2.shape
    f = w_down.shape[0]
    nk = f // tf
    return pl.pallas_call(
        _ffn_body,
        grid=(n // tm, nk),
        in_specs=[
            pl.BlockSpec((tm, d), lambda i, k: (i, 0)),
            pl.BlockSpec((1, d), lambda i, k: (0, 0)),
            pl.BlockSpec((d, tf), lambda i, k: (0, k)),
            pl.BlockSpec((d, tf), lambda i, k: (0, k + nk)),
            pl.BlockSpec((tf, d), lambda i, k: (k, 0)),
            pl.BlockSpec((1, d), lambda i, k: (0, 0)),
        ],
        out_specs=pl.BlockSpec((tm, d), lambda i, k: (i, 0)),
        out_shape=jax.ShapeDtypeStruct((n, d), F32),
        scratch_shapes=[pltpu.VMEM((tm, d), BF16), pltpu.VMEM((tm, d), F32)],
        compiler_params=_params(2),
        name="ffn",
    )(x2, g_pre.reshape(1, d), w_gu, w_gu, w_down, g_post.reshape(1, d))


def _proj_body(x_ref, xp_ref, xn_ref, g_ref, w_ref, cw_ref, cb_ref, cs_ref,
               hy_ref, pq_ref, po_ref, gt_ref, u_ref, h_ref, *, tm, seq, n_hy, fw):
    i = pl.program_id(0)
    j = pl.program_id(1)

    @pl.when(j == 0)
    def _():
        g = g_ref[...]
        has_prev = lax.rem(i * tm, seq) != 0
        has_next = lax.rem((i + 1) * tm, seq) != 0
        u_ref[pl.ds(0, HALO), :] = jnp.where(has_prev, _rms(xp_ref[...], g), 0.0).astype(BF16)
        u_ref[pl.ds(HALO, tm), :] = _rms(x_ref[...], g).astype(BF16)
        u_ref[pl.ds(HALO + tm, HALO), :] = jnp.where(has_next, _rms(xn_ref[...], g), 0.0).astype(BF16)

    @pl.when(j < n_hy)
    def _():
        h_ref[...] = _dot(u_ref[...], w_ref[...])
        y = cb_ref[...] + h_ref[pl.ds(HALO - 1, tm), :] * cw_ref[0:1, :]
        y = y + h_ref[pl.ds(HALO, tm), :] * cw_ref[1:2, :]
        y = y + h_ref[pl.ds(HALO + 1, tm), :] * cw_ref[2:3, :]
        hy_ref[...] = y.astype(BF16)

    @pl.when(j == n_hy)
    def _():
        c = _dot(u_ref[pl.ds(HALO, tm), :], w_ref[...])
        po_ref[...] = c[:, fw:]
        fb = c[:, :fw].astype(BF16)
        gw = fw // FNET_GROUPS
        res = [_dot(fb[:, q * gw:(q + 1) * gw], cs_ref[...]) for q in range(FNET_GROUPS)]
        pq_ref[...] = jnp.concatenate([r[:, :gw] for r in res] + [r[:, gw:] for r in res], axis=1)

    @pl.when(j > n_hy)
    def _():
        c = _dot(u_ref[pl.ds(HALO, tm), :], w_ref[...])
        gt_ref[...] = (1.0 / (1.0 + jnp.exp(-c))).astype(BF16)


def _proj(x2, g, w_in, conv_w, conv_b, cs, *, tm, tn, seq, hy_cols, fw, pw):
    n, d = x2.shape
    cols = w_in.shape[1]
    n_hy = hy_cols // tn
    assert hy_cols % tn == 0 and fw + pw == tn and (cols - hy_cols - tn) % tn == 0
    nj = cols // tn
    n_gate = cols - hy_cols - tn
    hb = tm // HALO
    last_hb = n // HALO - 1
    body = functools.partial(_proj_body, tm=tm, seq=seq, n_hy=n_hy, fw=fw)
    return pl.pallas_call(
        body,
        grid=(n // tm, nj),
        in_specs=[
            pl.BlockSpec((tm, d), lambda i, j: (i, 0)),
            pl.BlockSpec((HALO, d), lambda i, j: (jnp.maximum(i * hb - 1, 0), 0)),
            pl.BlockSpec((HALO, d), lambda i, j: (jnp.minimum((i + 1) * hb, last_hb), 0)),
            pl.BlockSpec((1, d), lambda i, j: (0, 0)),
            pl.BlockSpec((d, tn), lambda i, j: (0, j)),
            pl.BlockSpec((3, tn), lambda i, j: (0, jnp.minimum(j, n_hy - 1))),
            pl.BlockSpec((1, tn), lambda i, j: (0, jnp.minimum(j, n_hy - 1))),
            _resident(cs.shape),
        ],
        out_specs=[
            pl.BlockSpec((tm, tn), lambda i, j: (i, jnp.minimum(j, n_hy - 1))),
            pl.BlockSpec((tm, 2 * fw), lambda i, j: (i, 0)),
            pl.BlockSpec((tm, pw), lambda i, j: (i, 0)),
            pl.BlockSpec((tm, tn), lambda i, j: (i, jnp.maximum(j - n_hy - 1, 0))),
        ],
        out_shape=[
            jax.ShapeDtypeStruct((n, hy_cols), BF16),
            jax.ShapeDtypeStruct((n, 2 * fw), F32),
            jax.ShapeDtypeStruct((n, pw), F32),
            jax.ShapeDtypeStruct((n, n_gate), BF16),
        ],
        scratch_shapes=[pltpu.VMEM((tm + 2 * HALO, d), BF16), pltpu.VMEM((tm + 2 * HALO, tn), F32)],
        compiler_params=_params(2),
        name="mixer_proj",
    )(x2, x2, x2, g.reshape(1, d), w_in, conv_w, conv_b.reshape(1, hy_cols), cs)


def _hyena_slot_block(s, na):
    half = na // 2
    return s if s <= half else (half + 1) + (s - half)


def _hyena_mats(seq):
    nb = HYENA_NB
    na = 2 * seq // nb
    half = na // 2
    eye = np.eye(V7X_SUBLANES)
    a = np.arange(half)[None, :]
    f1 = np.zeros((na, half))
    k_re = np.arange(half + 1)[:, None]
    f1[: half + 1] = np.cos(2 * np.pi * k_re * a / na)
    k_im = np.arange(1, half)[:, None]
    f1[half + 1:] = -np.sin(2 * np.pi * k_im * a / na)
    lhs1 = np.kron(f1, eye)
    b = np.arange(nb)[None, :]
    kb = np.arange(nb)[:, None]
    g = np.zeros((half + 1, 2 * nb, 2 * nb))
    for ka in range(half + 1):
        ang = 2 * np.pi * (kb * b / nb + b * ka / (2 * seq))
        gre, gim = np.cos(ang), -np.sin(ang)
        g[ka] = np.block([[gre, -gim], [gim, gre]])
    a_col = np.arange(half)[:, None]
    f2 = np.zeros((half, na))
    wgt = np.full(half + 1, 2.0)
    wgt[0] = wgt[half] = 1.0
    f2[:, : half + 1] = wgt[None, :] * np.cos(2 * np.pi * a_col * np.arange(half + 1)[None, :] / na)
    f2[:, half + 1:] = -2.0 * np.sin(2 * np.pi * a_col * np.arange(1, half)[None, :] / na)
    lhs2 = np.kron(f2 / (2 * seq), eye)
    return (jnp.asarray(lhs1, BF16), jnp.asarray(g, BF16), jnp.asarray(lhs2, BF16))


def _fnet_mats(seq, group):
    na = int(round(math.sqrt(seq)))
    nb = seq // na
    assert na * nb == seq and nb % V7X_SUBLANES == 0
    eye = np.eye(V7X_SUBLANES)
    n2 = np.arange(nb)[None, :]
    k2 = np.arange(nb)[:, None]
    ang = 2 * np.pi * k2 * n2 / nb
    c, s = np.cos(ang), np.sin(ang)
    blk = np.block([[c, s], [-s, c]])
    a1 = np.zeros((2, V7X_SUBLANES, nb, 2, nb, V7X_SUBLANES))
    for po in range(2):
        for pi_ in range(2):
            sub = blk[po * nb:(po + 1) * nb, pi_ * nb:(pi_ + 1) * nb]
            for lo in range(V7X_SUBLANES):
                a1[po, lo, :, pi_, :, lo] = sub
    a1 = a1.reshape(2 * V7X_SUBLANES * nb, 2 * V7X_SUBLANES * nb)
    a2 = np.zeros((nb // V7X_SUBLANES, na, V7X_SUBLANES, 2, na, V7X_SUBLANES))
    k1 = np.arange(na)[:, None]
    n1 = np.arange(na)[None, :]
    for hi in range(nb // V7X_SUBLANES):
        for lo in range(V7X_SUBLANES):
            k2v = hi * V7X_SUBLANES + lo
            phi = 2 * np.pi * (n1 * k1 / na + n1 * k2v / seq)
            a2[hi, :, lo, 0, :, lo] = np.cos(phi)
            a2[hi, :, lo, 1, :, lo] = np.sin(phi)
    a2 = a2.reshape(nb // V7X_SUBLANES, na * V7X_SUBLANES, 2 * na * V7X_SUBLANES) / math.sqrt(seq)
    cc = np.arange(group)[:, None]
    mm = np.arange(group)[None, :]
    angc = 2 * np.pi * cc * mm / group
    cs = np.concatenate([np.cos(angc), -np.sin(angc)], axis=1) / math.sqrt(group)
    return jnp.asarray(a1, BF16), jnp.asarray(a2, BF16), jnp.asarray(cs, BF16), na, nb


def _gather_tiles(ref, starts):
    tiles = [ref[pl.ds(pl.multiple_of(s, V7X_SUBLANES), V7X_SUBLANES), :] for s in starts]
    return jnp.concatenate(tiles, axis=0)


def _hyena_stage1(z_ref, e_ref, lhs1_ref, seq):
    nb = HYENA_NB
    na = 2 * seq // nb
    half = na // 2

    def body(i, carry):
        off = i * V7X_SUBLANES
        xg = _gather_tiles(z_ref, [nb * a + off for a in range(half)]).astype(BF16)
        out = _dot(lhs1_ref[...], xg)
        for s in range(na):
            row = _hyena_slot_block(s, na) * nb + off
            e_ref[pl.ds(pl.multiple_of(row, V7X_SUBLANES), V7X_SUBLANES), :] = (
                out[s * V7X_SUBLANES:(s + 1) * V7X_SUBLANES])
        return carry

    lax.fori_loop(0, nb // V7X_SUBLANES, body, 0)


def _hyena_zero_unused(e_ref, seq):
    nb = HYENA_NB
    half = seq // nb
    zero = jnp.zeros((nb, e_ref.shape[1]), F32)
    e_ref[pl.ds((half + 1) * nb, nb), :] = zero
    e_ref[pl.ds((2 * half + 1) * nb, nb), :] = zero


def _hyena_load_ka(e_ref, ka, seq):
    nb = HYENA_NB
    half = seq // nb
    re = e_ref[pl.ds(pl.multiple_of(ka * nb, nb), nb), :]
    im = e_ref[pl.ds(pl.multiple_of((half + 1 + ka) * nb, nb), nb), :]
    return jnp.concatenate([re, im], axis=0)


def _filter_mlp_body(bands_ref, w1t_ref, w1c_ref, w1s_ref, b1_ref, w2_ref, b2_ref, w3_ref, b3_ref,
                     fr_ref, o_ref, *, seq):
    hi = lax.Precision.HIGHEST
    idx = lax.broadcasted_iota(jnp.int32, (1, 2 * seq), 1)
    p = jnp.where(idx < seq, idx, 2 * seq - idx).astype(F32)
    t = p / (seq - 1.0)
    ang = bands_ref[...] * ((2.0 * math.pi / seq) * p)
    fr = fr_ref[...]
    pre = (w1t_ref[...] * t
           + jnp.dot(w1c_ref[...], jnp.cos(ang), precision=hi, preferred_element_type=F32)
           + jnp.dot(w1s_ref[...], -jnp.sin(ang), precision=hi, preferred_element_type=F32)
           + b1_ref[...])
    h = jnp.sin(fr * pre)
    h = jnp.sin(fr * (jnp.dot(w2_ref[...], h, precision=hi, preferred_element_type=F32) + b2_ref[...]))
    h = jnp.sin(fr * (jnp.dot(w3_ref[...], h, precision=hi, preferred_element_type=F32) + b3_ref[...]))
    o_ref[...] = h.T


def _filter_mlp(fw1, fb1, fw2, fb2, fw3, fb3, freq, *, seq):
    hid = fw2.shape[0]
    bands = np.linspace(1e-4, FILTER_BANDS - 1, FILTER_BANDS).astype(np.float32).reshape(-1, 1)
    col = lambda v: v.reshape(hid, 1)
    args = (jnp.asarray(bands), fw1[0].reshape(hid, 1), fw1[1:1 + FILTER_BANDS].T,
            fw1[1 + FILTER_BANDS:].T, col(fb1), fw2.T, col(fb2), fw3.T, col(fb3), col(freq))
    return pl.pallas_call(
        functools.partial(_filter_mlp_body, seq=seq),
        grid=(1,),
        in_specs=[_resident(a.shape) for a in args],
        out_specs=pl.BlockSpec((2 * seq, hid), lambda i: (0, 0)),
        out_shape=jax.ShapeDtypeStruct((2 * seq, hid), F32),
        compiler_params=_params(1),
        name="filter_mlp",
    )(*args)


def _filter_spec_body(h3_ref, w4f_ref, w4b_ref, dl_ref, d_ref, lhs1_ref, g_ref,
                      o_ref, z_ref, e_ref, *, seq, cw):
    hi = lax.Precision.HIGHEST
    nb = HYENA_NB
    half = seq // nb
    row = lax.broadcasted_iota(jnp.int32, (seq, 1), 0)
    rowf = row.astype(F32)
    dl = dl_ref[...]
    hf = jnp.dot(h3_ref[pl.ds(0, seq), :], w4f_ref[...], precision=hi, preferred_element_type=F32)
    z_ref[:, 0:cw] = hf * jnp.exp(-(rowf / (seq - 1.0)) * dl)
    hb = jnp.dot(h3_ref[pl.ds(seq, seq), :], w4b_ref[...], precision=hi, preferred_element_type=F32)
    hb = hb * jnp.exp(-((seq - rowf) / (seq - 1.0)) * dl)
    z_ref[:, cw:2 * cw] = jnp.where(row == 0, 0.0, hb)

    _hyena_zero_unused(e_ref, seq)
    _hyena_stage1(z_ref, e_ref, lhs1_ref, seq)
    dsk = d_ref[0]

    def body(ka, carry):
        ein = _hyena_load_ka(e_ref, ka, seq).astype(BF16)
        x = _dot(g_ref[ka], ein)
        sgn = jnp.where(lax.rem(ka, 2) == 0, 1.0, -1.0)
        hsp = x[:, 0:cw] + sgn * x[:, cw:2 * cw]
        part = lax.broadcasted_iota(jnp.int32, (2 * nb, 1), 0)
        hsp = hsp + jnp.where(part < nb, dsk, 0.0)
        o_ref[0, pl.ds(pl.multiple_of(ka * 2 * nb, 2 * nb), 2 * nb), :] = hsp.astype(BF16)
        return carry

    lax.fori_loop(0, half + 1, body, 0)


def _filter_spectrum(h3, fw4, d_skip, lhs1, gmat, *, seq, width, cw):
    hid = h3.shape[1]
    orders = d_skip.shape[0]
    nct = width // cw
    nb = HYENA_NB
    half = seq // nb
    max_decay = math.log(DECAY_TARGET) / FAST_DECAY_PCT
    min_decay = math.log(DECAY_TARGET) / SLOW_DECAY_PCT
    deltas = np.abs(np.linspace(min_decay, max_decay, width)).astype(np.float32).reshape(1, width)
    rows = (half + 1) * 2 * nb
    body = functools.partial(_filter_spec_body, seq=seq, cw=cw)
    return pl.pallas_call(
        body,
        grid=(orders, nct),
        in_specs=[
            _resident(h3.shape),
            pl.BlockSpec((hid, cw), lambda o, c: (0, o * 2 * nct + c)),
            pl.BlockSpec((hid, cw), lambda o, c: (0, o * 2 * nct + nct + c)),
            pl.BlockSpec((1, cw), lambda o, c: (0, c)),
            pl.BlockSpec((1, 1, cw), lambda o, c: (o, 0, c)),
            _resident(lhs1.shape),
            _resident(gmat.shape),
        ],
        out_specs=pl.BlockSpec((1, rows, cw), lambda o, c: (o, 0, c)),
        out_shape=jax.ShapeDtypeStruct((orders, rows, width), BF16),
        scratch_shapes=[pltpu.VMEM((seq, 2 * cw), F32),
                        pltpu.VMEM((2 * (half + 1) * nb, 2 * cw), F32)],
        compiler_params=_params(2),
        name="filter_spectrum",
    )(h3, fw4, fw4, jnp.asarray(deltas), d_skip.reshape(orders, 1, width), lhs1, gmat)


def _hyena_body(v_ref, g1_ref, g2_ref, h_ref, lhs1_ref, g_ref, lhs2_ref, o_ref, z_ref, e_ref, *, seq):
    nb = HYENA_NB
    na = 2 * seq // nb
    half = na // 2
    pair = V7X_BF16_ROWS
    chunk = 512 if seq % 512 == 0 else seq

    def load_v(c, carry):
        r = pl.multiple_of(c * chunk, chunk)
        z_ref[pl.ds(r, chunk), :] = v_ref[pl.ds(r, chunk), :].astype(F32)
        return carry

    lax.fori_loop(0, seq // chunk, load_v, 0)
    _hyena_zero_unused(e_ref, seq)

    for order, gate_ref in enumerate((g1_ref, g2_ref)):
        _hyena_stage1(z_ref, e_ref, lhs1_ref, seq)

        def spectral(ka, carry):
            ein = _hyena_load_ka(e_ref, ka, seq).astype(BF16)
            gk = g_ref[ka]
            x = _dot(gk, ein)
            hsp = h_ref[order, pl.ds(pl.multiple_of(ka * 2 * nb, 2 * nb), 2 * nb), :].astype(F32)
            xre, xim, hre, him = x[:nb], x[nb:], hsp[:nb], hsp[nb:]
            zsp = jnp.concatenate([xre * hre - xim * him, xre * him + xim * hre], axis=0).astype(BF16)
            y = lax.dot_general(gk, zsp, (((0,), (0,)), ((), ())), preferred_element_type=F32)
            e_ref[pl.ds(pl.multiple_of(ka * nb, nb), nb), :] = y[:nb]
            keep = jnp.where((ka == 0) | (ka == half), 0.0, 1.0)
            e_ref[pl.ds(pl.multiple_of((half + 1 + ka) * nb, nb), nb), :] = y[nb:] * keep
            return carry

        lax.fori_loop(0, half + 1, spectral, 0)

        def inverse(i, carry):
            outs = []
            for q in range(pair // V7X_SUBLANES):
                off = i * pair + q * V7X_SUBLANES
                yg = _gather_tiles(e_ref, [_hyena_slot_block(s, na) * nb + off for s in range(na)])
                outs.append(_dot(lhs2_ref[...], yg.astype(BF16)))
            for a in range(half):
                sl = slice(a * V7X_SUBLANES, (a + 1) * V7X_SUBLANES)
                conv = jnp.concatenate([o_[sl] for o_ in outs], axis=0)
                r = pl.multiple_of(a * nb + i * pair, pair)
                res = gate_ref[pl.ds(r, pair), :].astype(F32) * conv
                if order == 0:
                    z_ref[pl.ds(r, pair), :] = res
                else:
                    o_ref[pl.ds(r, pair), :] = res.astype(BF16)
            return carry

        lax.fori_loop(0, nb // pair, inverse, 0)


def _hyena(hyc, hspec, lhs1, gmat, lhs2, *, seq, width, ct):
    n = hyc.shape[0]
    nct = width // ct
    nb = HYENA_NB
    half = seq // nb
    return pl.pallas_call(
        functools.partial(_hyena_body, seq=seq),
        grid=(nct, n // seq),
        in_specs=[
            pl.BlockSpec((seq, ct), lambda c, b: (b, c)),
            pl.BlockSpec((seq, ct), lambda c, b: (b, nct + c)),
            pl.BlockSpec((seq, ct), lambda c, b: (b, 2 * nct + c)),
            pl.BlockSpec((hspec.shape[0], hspec.shape[1], ct), lambda c, b: (0, 0, c),
                         pipeline_mode=pl.Buffered(1)),
            _resident(lhs1.shape),
            _resident(gmat.shape),
            _resident(lhs2.shape),
        ],
        out_specs=pl.BlockSpec((seq, ct), lambda c, b: (b, c)),
        out_shape=jax.ShapeDtypeStruct((n, width), BF16),
        scratch_shapes=[pltpu.VMEM((seq, ct), F32),
                        pltpu.VMEM((2 * (half + 1) * nb, ct), F32)],
        compiler_params=_params(2),
        name="hyena",
    )(hyc, hyc, hyc, hspec, lhs1, gmat, lhs2)


def _fnet_body(p_ref, q_ref, a1_ref, a2_ref, o_ref, t_ref, *, seq, na, nb):
    sub = V7X_SUBLANES

    def stage1(i, carry):
        off = i * sub
        starts = [na * n2 + off for n2 in range(nb)]
        xin = jnp.concatenate([_gather_tiles(p_ref, starts), _gather_tiles(q_ref, starts)],
                              axis=0).astype(BF16)
        out = _dot(a1_ref[...], xin)
        for part in range(2):
            for lo in range(sub):
                src = (part * sub + lo) * nb
                dst = part * seq + (off + lo) * nb
                t_ref[pl.ds(pl.multiple_of(dst, sub), nb), :] = out[src:src + nb]
        return carry

    lax.fori_loop(0, na // sub, stage1, 0)

    def stage2(i, carry):
        off = i * sub
        starts = [part * seq + n1 * nb + off for part in range(2) for n1 in range(na)]
        tin = _gather_tiles(t_ref, starts).astype(BF16)
        out = _dot(a2_ref[i], tin)
        for k1 in range(na):
            o_ref[pl.ds(pl.multiple_of(k1 * nb + off, sub), sub), :] = out[k1 * sub:(k1 + 1) * sub]
        return carry

    lax.fori_loop(0, nb // sub, stage2, 0)


def _fnet(pq, a1, a2, *, seq, fw, ct, na, nb):
    n = pq.shape[0]
    nct = fw // ct
    return pl.pallas_call(
        functools.partial(_fnet_body, seq=seq, na=na, nb=nb),
        grid=(n // seq, nct),
        in_specs=[
            pl.BlockSpec((seq, ct), lambda b, c: (b, c)),
            pl.BlockSpec((seq, ct), lambda b, c: (b, nct + c)),
            _resident(a1.shape),
            _resident(a2.shape),
        ],
        out_specs=pl.BlockSpec((seq, ct), lambda b, c: (b, c)),
        out_shape=jax.ShapeDtypeStruct((n, fw), F32),
        scratch_shapes=[pltpu.VMEM((2 * seq, ct), F32)],
        compiler_params=_params(2),
        name="fnet_seq",
    )(pq, pq, a1, a2)


def _merge_body(x_ref, z_ref, f_ref, po_ref, pp_ref, pn_ref, gt_ref, wha_ref, wfo_ref, wpl_ref,
                ps_ref, wo_ref, gpost_ref, o_ref, ext_ref, m_ref, *, tm, seq, d):
    i = pl.program_id(0)
    sub = V7X_SUBLANES
    has_prev = lax.rem(i * tm, seq) != 0
    has_next = lax.rem((i + 1) * tm, seq) != 0
    ext_ref[pl.ds(0, sub), :] = jnp.where(has_prev, pp_ref[...], 0.0)
    ext_ref[pl.ds(sub, tm), :] = po_ref[...]
    ext_ref[pl.ds(sub + tm, sub), :] = jnp.where(has_next, pn_ref[...], 0.0)

    gate = lambda q, lo, hi: gt_ref[:, q * d + lo:q * d + hi].astype(F32)
    m_ref[...] = gate(0, 0, d) * _dot(z_ref[...], wha_ref[...])
    m_ref[...] += gate(1, 0, d) * _dot(f_ref[...].astype(BF16), wfo_ref[...])

    pos = lax.rem(i * tm, seq) + lax.broadcasted_iota(jnp.int32, (tm, 1), 0)
    n_pool = len(POOL_WINDOWS)
    gw = po_ref.shape[1] // n_pool
    ow = d // n_pool
    for q, w in enumerate(POOL_WINDOWS):
        before = w // 2
        after = w - 1 - before
        cols = slice(q * gw, (q + 1) * gw)
        tot = ext_ref[pl.ds(sub - before, tm), cols]
        for s in range(-before + 1, after + 1):
            tot = tot + ext_ref[pl.ds(sub + s, tm), cols]
        lo = jnp.maximum(pos - before, 0)
        hi = jnp.minimum(pos + after, seq - 1)
        cnt = (hi - lo + 1).astype(F32)
        mq = (tot / cnt - po_ref[:, cols]).astype(BF16)
        yc = _dot(mq, wpl_ref[q]) * ps_ref[:, q * ow:(q + 1) * ow]
        m_ref[:, q * ow:(q + 1) * ow] += gate(2, q * ow, (q + 1) * ow) * yc

    y = _dot(m_ref[...].astype(BF16), wo_ref[...])
    o_ref[...] = x_ref[...] + _rms(y, gpost_ref[...])


def _merge(x2, z2, fy, po, gates, wha, wfo, wpool, pscale, w_out, g_post, *, tm, seq):
    n, d = x2.shape
    pw = po.shape[1]
    sub = V7X_SUBLANES
    hb = tm // sub
    last_hb = n // sub - 1
    body = functools.partial(_merge_body, tm=tm, seq=seq, d=d)
    return pl.pallas_call(
        body,
        grid=(n // tm,),
        in_specs=[
            pl.BlockSpec((tm, d), lambda i: (i, 0)),
            pl.BlockSpec((tm, z2.shape[1]), lambda i: (i, 0)),
            pl.BlockSpec((tm, fy.shape[1]), lambda i: (i, 0)),
            pl.BlockSpec((tm, pw), lambda i: (i, 0)),
            pl.BlockSpec((sub, pw), lambda i: (jnp.maximum(i * hb - 1, 0), 0)),
            pl.BlockSpec((sub, pw), lambda i: (jnp.minimum((i + 1) * hb, last_hb), 0)),
            pl.BlockSpec((tm, gates.shape[1]), lambda i: (i, 0)),
            _resident(wha.shape),
            _resident(wfo.shape),
            _resident(wpool.shape),
            _resident((1, d)),
            _resident(w_out.shape),
            _resident((1, d)),
        ],
        out_specs=pl.BlockSpec((tm, d), lambda i: (i, 0)),
        out_shape=jax.ShapeDtypeStruct((n, d), F32),
        scratch_shapes=[pltpu.VMEM((tm + 2 * sub, pw), F32), pltpu.VMEM((tm, d), F32)],
        compiler_params=_params(1),
        name="mixer_merge",
    )(x2, z2, fy, po, po, po, gates, wha, wfo, wpool, pscale.reshape(1, d), w_out, g_post.reshape(1, d))


def _kv_body(m_ref, g_ref, w_ref, o_ref):
    o_ref[...] = _dot(_rms(m_ref[...], g_ref[...]).astype(BF16), w_ref[...]).astype(BF16)


def _kv(mem2, g, w_kv, *, n_mem, tn):
    n, d = mem2.shape
    cols = w_kv.shape[1]
    return pl.pallas_call(
        _kv_body,
        grid=(cols // tn, n // n_mem),
        in_specs=[
            pl.BlockSpec((n_mem, d), lambda j, b: (b, 0)),
            pl.BlockSpec((1, d), lambda j, b: (0, 0)),
            pl.BlockSpec((d, tn), lambda j, b: (0, j)),
        ],
        out_specs=pl.BlockSpec((n_mem, tn), lambda j, b: (b, j)),
        out_shape=jax.ShapeDtypeStruct((n, cols), BF16),
        compiler_params=_params(2),
        name="mem_kv",
    )(mem2, g.reshape(1, d), w_kv)


def _attn_body(x_ref, gpre_ref, wq_ref, k_ref, v_ref, wo_ref, gpost_ref, o_ref, *, d):
    dh = d // N_HEADS
    x = x_ref[...]
    q = _dot(_rms(x, gpre_ref[...]).astype(BF16), wq_ref[...])
    heads = []
    for h in range(N_HEADS):
        cols = slice(h * dh, (h + 1) * dh)
        s = lax.dot_general(q[:, cols].astype(BF16), k_ref[:, cols], (((1,), (1,)), ((), ())),
                            preferred_element_type=F32) * (dh ** -0.5)
        e = jnp.exp(s - jnp.max(s, axis=-1, keepdims=True))
        p = e / jnp.sum(e, axis=-1, keepdims=True)
        heads.append(_dot(p.astype(BF16), v_ref[:, cols]))
    o = jnp.concatenate(heads, axis=1).astype(BF16)
    o_ref[...] = x + _rms(_dot(o, wo_ref[...]), gpost_ref[...])


def _attn(x2, g_pre, w_q, kv, w_o, g_post, *, tm, seq, n_mem):
    n, d = x2.shape
    tiles_per_seq = seq // tm
    return pl.pallas_call(
        functools.partial(_attn_body, d=d),
        grid=(n // tm,),
        in_specs=[
            pl.BlockSpec((tm, d), lambda i: (i, 0)),
            _resident((1, d)),
            _resident(w_q.shape),
            pl.BlockSpec((n_mem, d), lambda i: (i // tiles_per_seq, 0)),
            pl.BlockSpec((n_mem, d), lambda i: (i // tiles_per_seq, 1)),
            _resident(w_o.shape),
            _resident((1, d)),
        ],
        out_specs=pl.BlockSpec((tm, d), lambda i: (i, 0)),
        out_shape=jax.ShapeDtypeStruct((n, d), F32),
        compiler_params=_params(1),
        name="mem_attn",
    )(x2, g_pre.reshape(1, d), w_q, kv, kv, w_o, g_post.reshape(1, d))


def kernel(x, mem, g_ffn1_pre, w_ffn1_gu, w_ffn1_down, g_ffn1_post, g_mix_pre, w_in, hyena_conv_w,
           hyena_conv_b, filt_w1, filt_b1, filt_w2, filt_b2, filt_w3, filt_b3, filt_w4, filt_freq,
           hyena_d, w_hyena_out, w_fnet_out, w_pool, pool_scale, w_out, g_mix_post, g_mem_pre,
           g_mem_kv, w_q, w_kv, w_o, g_mem_post, g_ffn2_pre, w_ffn2_gu, w_ffn2_down, g_ffn2_post):
    batch, seq, d = x.shape
    n_mem = mem.shape[1]
    depth = w_in.shape[0]
    hy_w = w_hyena_out.shape[1]
    hy_cols = hyena_conv_w.shape[2]
    fw = w_fnet_out.shape[1]
    pw = w_pool.shape[1] * w_pool.shape[2]

    tm = 512
    tn = fw + pw
    tf = 512
    ct = 256

    lhs1, gmat, lhs2 = _hyena_mats(seq)
    a1, a2, cs, fna, fnb = _fnet_mats(seq, fw // FNET_GROUPS)

    bf = lambda w: w.astype(BF16)
    x2 = x.reshape(batch * seq, d)
    mem2 = mem.reshape(batch * n_mem, d)
    for l in range(depth):
        x2 = _ffn(x2, g_ffn1_pre[l], bf(w_ffn1_gu[l]), bf(w_ffn1_down[l]), g_ffn1_post[l], tm=tm, tf=tf)

        hyc, pq, po, gates = _proj(x2, g_mix_pre[l], bf(w_in[l]), hyena_conv_w[l], hyena_conv_b[l], cs,
                                   tm=tm, tn=tn, seq=seq, hy_cols=hy_cols, fw=fw, pw=pw)
        h3 = _filter_mlp(filt_w1[l], filt_b1[l], filt_w2[l], filt_b2[l], filt_w3[l], filt_b3[l],
                         filt_freq[l], seq=seq)
        hspec = _filter_spectrum(h3, filt_w4[l], hyena_d[l], lhs1, gmat, seq=seq, width=hy_w, cw=ct // 2)
        z2 = _hyena(hyc, hspec, lhs1, gmat, lhs2, seq=seq, width=hy_w, ct=ct)
        fy = _fnet(pq, a1, a2, seq=seq, fw=fw, ct=ct, na=fna, nb=fnb)
        x2 = _merge(x2, z2, fy, po, gates, bf(w_hyena_out[l]), bf(w_fnet_out[l]), bf(w_pool[l]),
                    pool_scale[l], bf(w_out[l]), g_mix_post[l], tm=tm // 2, seq=seq)

        kv = _kv(mem2, g_mem_kv[l], bf(w_kv[l]), n_mem=n_mem, tn=tn)
        x2 = _attn(x2, g_mem_pre[l], bf(w_q[l]), kv, bf(w_o[l]), g_mem_post[l], tm=tm, seq=seq, n_mem=n_mem)

        x2 = _ffn(x2, g_ffn2_pre[l], bf(w_ffn2_gu[l]), bf(w_ffn2_down[l]), g_ffn2_post[l], tm=tm, tf=tf)
    return x2.reshape(batch, seq, d)
```

```python
import functools
import math

import jax
import jax.numpy as jnp
import numpy as np
from jax import lax
from jax.experimental import pallas as pl
from jax.experimental.pallas import tpu as pltpu

F32 = jnp.float32
BF16 = jnp.bfloat16

RMS_EPS = 1e-6
MACARON_WEIGHT = 0.5
N_HEADS = 4
POOL_WINDOWS = (2, 4, 8, 16)
FNET_GROUPS = 4
FILTER_BANDS = 16
DECAY_TARGET = 1e-2
FAST_DECAY_PCT = 0.3
SLOW_DECAY_PCT = 1.5

V7X_SUBLANES = 8
V7X_BF16_ROWS = 16
V7X_VMEM_LIMIT = 60 * 2**20

HYENA_NB = 128
HALO = V7X_BF16_ROWS


def _params(n_axes):
    return pltpu.CompilerParams(
        dimension_semantics=("arbitrary",) * n_axes,
        vmem_limit_bytes=V7X_VMEM_LIMIT,
    )


def _resident(shape):
    zeros = (0,) * len(shape)
    return pl.BlockSpec(shape, lambda *_: zeros, pipeline_mode=pl.Buffered(1))


def _rms(x, g):
    ms = jnp.mean(x * x, axis=-1, keepdims=True)
    return x * lax.rsqrt(ms + RMS_EPS) * g


def _dot(a, b):
    return jnp.dot(a, b, preferred_element_type=F32)


def _dot3(a, b):
    ah = a.astype(BF16)
    al = (a - ah.astype(F32)).astype(BF16)
    bh = b.astype(BF16)
    bl = (b - bh.astype(F32)).astype(BF16)
    return _dot(ah, bh) + (_dot(al, bh) + _dot(ah, bl))


def _ffn_body(x_ref, gpre_ref, wg_ref, wu_ref, wd_ref, gpost_ref, o_ref, u_ref, acc_ref):
    k = pl.program_id(1)

    @pl.when(k == 0)
    def _():
        u_ref[...] = _rms(x_ref[...], gpre_ref[...]).astype(BF16)
        acc_ref[...] = jnp.zeros_like(acc_ref)

    u = u_ref[...]
    a = _dot(u, wg_ref[...])
    b = _dot(u, wu_ref[...])
    h = (a * (1.0 / (1.0 + jnp.exp(-a))) * b).astype(BF16)
    acc_ref[...] += _dot(h, wd_ref[...])

    @pl.when(k == pl.num_programs(1) - 1)
    def _():
        o_ref[...] = x_ref[...] + MACARON_WEIGHT * _rms(acc_ref[...], gpost_ref[...])


def _ffn(x2, g_pre, w_gu, w_down, g_post, *, tm, tf):
    n, d = x2.shape
    f = w_down.shape[0]
    nk = f // tf
    return pl.pallas_call(
        _ffn_body,
        grid=(n // tm, nk),
        in_specs=[
            pl.BlockSpec((tm, d), lambda i, k: (i, 0)),
            pl.BlockSpec((1, d), lambda i, k: (0, 0)),
            pl.BlockSpec((d, tf), lambda i, k: (0, k)),
            pl.BlockSpec((d, tf), lambda i, k: (0, k + nk)),
            pl.BlockSpec((tf, d), lambda i, k: (k, 0)),
            pl.BlockSpec((1, d), lambda i, k: (0, 0)),
        ],
        out_specs=pl.BlockSpec((tm, d), lambda i, k: (i, 0)),
        out_shape=jax.ShapeDtypeStruct((n, d), F32),
        scratch_shapes=[pltpu.VMEM((tm, d), BF16), pltpu.VMEM((tm, d), F32)],
        compiler_params=_params(2),
        name="ffn",
    )(x2, g_pre.reshape(1, d), w_gu, w_gu, w_down, g_post.reshape(1, d))


def _proj_body(x_ref, xp_ref, xn_ref, g_ref, w_ref, cw_ref, cb_ref, cs_ref,
               hy_ref, pq_ref, po_ref, gt_ref, u_ref, h_ref, *, tm, seq, n_hy, fw):
    i = pl.program_id(0)
    j = pl.program_id(1)

    @pl.when(j == 0)
    def _():
        g = g_ref[...]
        has_prev = lax.rem(i * tm, seq) != 0
        has_next = lax.rem((i + 1) * tm, seq) != 0
        u_ref[pl.ds(0, HALO), :] = jnp.where(has_prev, _rms(xp_ref[...], g), 0.0).astype(BF16)
        u_ref[pl.ds(HALO, tm), :] = _rms(x_ref[...], g).astype(BF16)
        u_ref[pl.ds(HALO + tm, HALO), :] = jnp.where(has_next, _rms(xn_ref[...], g), 0.0).astype(BF16)

    @pl.when(j < n_hy)
    def _():
        h_ref[...] = _dot(u_ref[...], w_ref[...])
        y = cb_ref[...] + h_ref[pl.ds(HALO - 1, tm), :] * cw_ref[0:1, :]
        y = y + h_ref[pl.ds(HALO, tm), :] * cw_ref[1:2, :]
        y = y + h_ref[pl.ds(HALO + 1, tm), :] * cw_ref[2:3, :]
        hy_ref[...] = y.astype(BF16)

    @pl.when(j == n_hy)
    def _():
        c = _dot(u_ref[pl.ds(HALO, tm), :], w_ref[...])
        po_ref[...] = c[:, fw:]
        fb = c[:, :fw].astype(BF16)
        gw = fw // FNET_GROUPS
        res = [_dot(fb[:, q * gw:(q + 1) * gw], cs_ref[...]) for q in range(FNET_GROUPS)]
        pq_ref[...] = jnp.concatenate([r[:, :gw] for r in res] + [r[:, gw:] for r in res], axis=1)

    @pl.when(j > n_hy)
    def _():
        c = _dot(u_ref[pl.ds(HALO, tm), :], w_ref[...])
        gt_ref[...] = (1.0 / (1.0 + jnp.exp(-c))).astype(BF16)


def _proj(x2, g, w_in, conv_w, conv_b, cs, *, tm, tn, seq, hy_cols, fw, pw):
    n, d = x2.shape
    cols = w_in.shape[1]
    n_hy = hy_cols // tn
    assert hy_cols % tn == 0 and fw + pw == tn and (cols - hy_cols - tn) % tn == 0
    nj = cols // tn
    n_gate = cols - hy_cols - tn
    hb = tm // HALO
    last_hb = n // HALO - 1
    body = functools.partial(_proj_body, tm=tm, seq=seq, n_hy=n_hy, fw=fw)
    return pl.pallas_call(
        body,
        grid=(n // tm, nj),
        in_specs=[
            pl.BlockSpec((tm, d), lambda i, j: (i, 0)),
            pl.BlockSpec((HALO, d), lambda i, j: (jnp.maximum(i * hb - 1, 0), 0)),
            pl.BlockSpec((HALO, d), lambda i, j: (jnp.minimum((i + 1) * hb, last_hb), 0)),
            pl.BlockSpec((1, d), lambda i, j: (0, 0)),
            pl.BlockSpec((d, tn), lambda i, j: (0, j)),
            pl.BlockSpec((3, tn), lambda i, j: (0, jnp.minimum(j, n_hy - 1))),
            pl.BlockSpec((1, tn), lambda i, j: (0, jnp.minimum(j, n_hy - 1))),
            _resident(cs.shape),
        ],
        out_specs=[
            pl.BlockSpec((tm, tn), lambda i, j: (i, jnp.minimum(j, n_hy - 1))),
            pl.BlockSpec((tm, 2 * fw), lambda i, j: (i, 0)),
            pl.BlockSpec((tm, pw), lambda i, j: (i, 0)),
            pl.BlockSpec((tm, tn), lambda i, j: (i, jnp.maximum(j - n_hy - 1, 0))),
        ],
        out_shape=[
            jax.ShapeDtypeStruct((n, hy_cols), BF16),
            jax.ShapeDtypeStruct((n, 2 * fw), F32),
            jax.ShapeDtypeStruct((n, pw), F32),
            jax.ShapeDtypeStruct((n, n_gate), BF16),
        ],
        scratch_shapes=[pltpu.VMEM((tm + 2 * HALO, d), BF16), pltpu.VMEM((tm + 2 * HALO, tn), F32)],
        compiler_params=_params(2),
        name="mixer_proj",
    )(x2, x2, x2, g.reshape(1, d), w_in, conv_w, conv_b.reshape(1, hy_cols), cs)


def _hyena_slot_block(s, na):
    half = na // 2
    return s if s <= half else (half + 1) + (s - half)


def _hyena_mats(seq):
    nb = HYENA_NB
    na = 2 * seq // nb
    half = na // 2
    eye = np.eye(V7X_SUBLANES)
    a = np.arange(half)[None, :]
    f1 = np.zeros((na, half))
    k_re = np.arange(half + 1)[:, None]
    f1[: half + 1] = np.cos(2 * np.pi * k_re * a / na)
    k_im = np.arange(1, half)[:, None]
    f1[half + 1:] = -np.sin(2 * np.pi * k_im * a / na)
    lhs1 = np.kron(f1, eye)
    b = np.arange(nb)[None, :]
    kb = np.arange(nb)[:, None]
    g = np.zeros((half + 1, 2 * nb, 2 * nb))
    for ka in range(half + 1):
        ang = 2 * np.pi * (kb * b / nb + b * ka / (2 * seq))
        gre, gim = np.cos(ang), -np.sin(ang)
        g[ka] = np.block([[gre, -gim], [gim, gre]])
    a_col = np.arange(half)[:, None]
    f2 = np.zeros((half, na))
    wgt = np.full(half + 1, 2.0)
    wgt[0] = wgt[half] = 1.0
    f2[:, : half + 1] = wgt[None, :] * np.cos(2 * np.pi * a_col * np.arange(half + 1)[None, :] / na)
    f2[:, half + 1:] = -2.0 * np.sin(2 * np.pi * a_col * np.arange(1, half)[None, :] / na)
    lhs2 = np.kron(f2 / (2 * seq), eye)
    return (jnp.asarray(lhs1, BF16), jnp.asarray(g, BF16), jnp.asarray(lhs2, BF16))


def _fnet_mats(seq, group):
    na = int(round(math.sqrt(seq)))
    nb = seq // na
    assert na * nb == seq and nb % V7X_SUBLANES == 0
    eye = np.eye(V7X_SUBLANES)
    n2 = np.arange(nb)[None, :]
    k2 = np.arange(nb)[:, None]
    ang = 2 * np.pi * k2 * n2 / nb
    c, s = np.cos(ang), np.sin(ang)
    blk = np.block([[c, s], [-s, c]])
    a1 = np.zeros((2, V7X_SUBLANES, nb, 2, nb, V7X_SUBLANES))
    for po in range(2):
        for pi_ in range(2):
            sub = blk[po * nb:(po + 1) * nb, pi_ * nb:(pi_ + 1) * nb]
            for lo in range(V7X_SUBLANES):
                a1[po, lo, :, pi_, :, lo] = sub
    a1 = a1.reshape(2 * V7X_SUBLANES * nb, 2 * V7X_SUBLANES * nb)
    a2 = np.zeros((nb // V7X_SUBLANES, na, V7X_SUBLANES, 2, na, V7X_SUBLANES))
    k1 = np.arange(na)[:, None]
    n1 = np.arange(na)[None, :]
    for hi in range(nb // V7X_SUBLANES):
        for lo in range(V7X_SUBLANES):
            k2v = hi * V7X_SUBLANES + lo
            phi = 2 * np.pi * (n1 * k1 / na + n1 * k2v / seq)
            a2[hi, :, lo, 0, :, lo] = np.cos(phi)
            a2[hi, :, lo, 1, :, lo] = np.sin(phi)
    a2 = a2.reshape(nb // V7X_SUBLANES, na * V7X_SUBLANES, 2 * na * V7X_SUBLANES) / math.sqrt(seq)
    cc = np.arange(group)[:, None]
    mm = np.arange(group)[None, :]
    angc = 2 * np.pi * cc * mm / group
    cs = np.concatenate([np.cos(angc), -np.sin(angc)], axis=1) / math.sqrt(group)
    return jnp.asarray(a1, BF16), jnp.asarray(a2, BF16), jnp.asarray(cs, BF16), na, nb


def _loop(n, body, unroll):
    if unroll is None:
        for i in range(n):
            body(i, 0)
        return
    while n % unroll:
        unroll -= 1
    lax.fori_loop(0, n, body, 0, unroll=unroll)


def _gather_tiles(ref, starts):
    tiles = [ref[pl.ds(pl.multiple_of(s, V7X_SUBLANES), V7X_SUBLANES), :] for s in starts]
    return jnp.concatenate(tiles, axis=0)


def _hyena_stage1(z_ref, e_ref, lhs1_ref, seq):
    nb = HYENA_NB
    na = 2 * seq // nb
    half = na // 2

    def body(i, carry):
        off = i * V7X_SUBLANES
        xg = _gather_tiles(z_ref, [nb * a + off for a in range(half)]).astype(BF16)
        out = _dot(lhs1_ref[...], xg)
        for s in range(na):
            row = _hyena_slot_block(s, na) * nb + off
            e_ref[pl.ds(pl.multiple_of(row, V7X_SUBLANES), V7X_SUBLANES), :] = (
                out[s * V7X_SUBLANES:(s + 1) * V7X_SUBLANES])
        return carry

    _loop(nb // V7X_SUBLANES, body, None)


def _hyena_zero_unused(e_ref, seq):
    nb = HYENA_NB
    half = seq // nb
    zero = jnp.zeros((nb, e_ref.shape[1]), F32)
    e_ref[pl.ds((half + 1) * nb, nb), :] = zero
    e_ref[pl.ds((2 * half + 1) * nb, nb), :] = zero


def _hyena_load_ka(e_ref, ka, seq):
    nb = HYENA_NB
    half = seq // nb
    re = e_ref[pl.ds(pl.multiple_of(ka * nb, nb), nb), :]
    im = e_ref[pl.ds(pl.multiple_of((half + 1 + ka) * nb, nb), nb), :]
    return jnp.concatenate([re, im], axis=0)


def _filter_mlp_body(bands_ref, w1t_ref, w1c_ref, w1s_ref, b1_ref, w2_ref, b2_ref, w3_ref, b3_ref,
                     fr_ref, o_ref, *, seq):
    idx = lax.broadcasted_iota(jnp.int32, (1, 2 * seq), 1)
    p = jnp.where(idx < seq, idx, 2 * seq - idx).astype(F32)
    t = p / (seq - 1.0)
    ang = bands_ref[...] * ((2.0 * math.pi / seq) * p)
    fr = fr_ref[...]
    pre = (w1t_ref[...] * t
           + _dot3(w1c_ref[...], jnp.cos(ang))
           + _dot3(w1s_ref[...], -jnp.sin(ang))
           + b1_ref[...])
    h = jnp.sin(fr * pre)
    h = jnp.sin(fr * (_dot3(w2_ref[...], h) + b2_ref[...]))
    h = jnp.sin(fr * (_dot3(w3_ref[...], h) + b3_ref[...]))
    o_ref[...] = jnp.concatenate([h[:, :seq], h[:, seq:]], axis=0).T


def _filter_mlp(fw1, fb1, fw2, fb2, fw3, fb3, freq, *, seq):
    hid = fw2.shape[0]
    bands = np.linspace(1e-4, FILTER_BANDS - 1, FILTER_BANDS).astype(np.float32).reshape(-1, 1)
    col = lambda v: v.reshape(hid, 1)
    args = (jnp.asarray(bands), fw1[0].reshape(hid, 1), fw1[1:1 + FILTER_BANDS].T,
            fw1[1 + FILTER_BANDS:].T, col(fb1), fw2.T, col(fb2), fw3.T, col(fb3), col(freq))
    return pl.pallas_call(
        functools.partial(_filter_mlp_body, seq=seq),
        grid=(1,),
        in_specs=[_resident(a.shape) for a in args],
        out_specs=pl.BlockSpec((seq, 2 * hid), lambda i: (0, 0)),
        out_shape=jax.ShapeDtypeStruct((seq, 2 * hid), F32),
        compiler_params=_params(1),
        name="filter_mlp",
    )(*args)


def _filter_spec_body(h3_ref, w4f_ref, w4b_ref, dl_ref, d_ref, lhs1_ref, g_ref,
                      o_ref, z_ref, e_ref, *, seq, cw):
    nb = HYENA_NB
    half = seq // nb
    row = lax.broadcasted_iota(jnp.int32, (seq, 1), 0)
    rowf = row.astype(F32)
    dl = dl_ref[...]
    zero = jnp.zeros_like(w4f_ref[...])
    w4 = jnp.concatenate([jnp.concatenate([w4f_ref[...], zero], axis=1),
                          jnp.concatenate([zero, w4b_ref[...]], axis=1)], axis=0)
    h4 = _dot3(h3_ref[...], w4)
    dec = jnp.concatenate([jnp.exp(-(rowf / (seq - 1.0)) * dl),
                           jnp.exp(-((seq - rowf) / (seq - 1.0)) * dl)], axis=1)
    lane = lax.broadcasted_iota(jnp.int32, (1, 2 * cw), 1)
    z_ref[...] = jnp.where((row == 0) & (lane >= cw), 0.0, h4 * dec)

    _hyena_zero_unused(e_ref, seq)
    _hyena_stage1(z_ref, e_ref, lhs1_ref, seq)
    dsk = d_ref[0]

    def body(ka, carry):
        ein = _hyena_load_ka(e_ref, ka, seq).astype(BF16)
        x = _dot(g_ref[ka], ein)
        sgn = 1.0 - 2.0 * (ka % 2)
        hsp = x[:, 0:cw] + sgn * x[:, cw:2 * cw]
        part = lax.broadcasted_iota(jnp.int32, (2 * nb, 1), 0)
        hsp = hsp + jnp.where(part < nb, dsk, 0.0)
        o_ref[0, pl.ds(pl.multiple_of(ka * 2 * nb, 2 * nb), 2 * nb), :] = hsp.astype(BF16)
        return carry

    _loop(half + 1, body, None)


def _filter_spectrum(h3, fw4, d_skip, lhs1, gmat, *, seq, width, cw):
    hid = h3.shape[1] // 2
    orders = d_skip.shape[0]
    nct = width // cw
    nb = HYENA_NB
    half = seq // nb
    max_decay = math.log(DECAY_TARGET) / FAST_DECAY_PCT
    min_decay = math.log(DECAY_TARGET) / SLOW_DECAY_PCT
    deltas = np.abs(np.linspace(min_decay, max_decay, width)).astype(np.float32).reshape(1, width)
    rows = (half + 1) * 2 * nb
    body = functools.partial(_filter_spec_body, seq=seq, cw=cw)
    return pl.pallas_call(
        body,
        grid=(orders, nct),
        in_specs=[
            _resident(h3.shape),
            pl.BlockSpec((hid, cw), lambda o, c: (0, o * 2 * nct + c)),
            pl.BlockSpec((hid, cw), lambda o, c: (0, o * 2 * nct + nct + c)),
            pl.BlockSpec((1, cw), lambda o, c: (0, c)),
            pl.BlockSpec((1, 1, cw), lambda o, c: (o, 0, c)),
            _resident(lhs1.shape),
            _resident(gmat.shape),
        ],
        out_specs=pl.BlockSpec((1, rows, cw), lambda o, c: (o, 0, c)),
        out_shape=jax.ShapeDtypeStruct((orders, rows, width), BF16),
        scratch_shapes=[pltpu.VMEM((seq, 2 * cw), F32),
                        pltpu.VMEM((2 * (half + 1) * nb, 2 * cw), F32)],
        compiler_params=_params(2),
        name="filter_spectrum",
    )(h3, fw4, fw4, jnp.asarray(deltas), d_skip.reshape(orders, 1, width), lhs1, gmat)


def _hyena_body(v_ref, g1_ref, g2_ref, h_ref, lhs1_ref, g_ref, lhs2_ref, o_ref, z_ref, e_ref, *, seq):
    nb = HYENA_NB
    na = 2 * seq // nb
    half = na // 2
    pair = V7X_BF16_ROWS
    chunk = 512 if seq % 512 == 0 else seq

    def load_v(c, carry):
        r = pl.multiple_of(c * chunk, chunk)
        z_ref[pl.ds(r, chunk), :] = v_ref[pl.ds(r, chunk), :].astype(F32)
        return carry

    lax.fori_loop(0, seq // chunk, load_v, 0)
    _hyena_zero_unused(e_ref, seq)

    for order, gate_ref in enumerate((g1_ref, g2_ref)):
        _hyena_stage1(z_ref, e_ref, lhs1_ref, seq)

        group = 3 if (half + 1) % 3 == 0 else 1

        def spectral(it, carry):
            kas = [it * group + u for u in range(group)]
            eins = [_hyena_load_ka(e_ref, ka, seq).astype(BF16) for ka in kas]
            ys = []
            for ka, ein in zip(kas, eins):
                gk = g_ref[ka]
                x = _dot(gk, ein)
                hsp = h_ref[order, pl.ds(pl.multiple_of(ka * 2 * nb, 2 * nb), 2 * nb), :].astype(F32)
                xre, xim, hre, him = x[:nb], x[nb:], hsp[:nb], hsp[nb:]
                zsp = jnp.concatenate([xre * hre - xim * him, xre * him + xim * hre], axis=0).astype(BF16)
                ys.append(lax.dot_general(gk, zsp, (((0,), (0,)), ((), ())), preferred_element_type=F32))
            for ka, y in zip(kas, ys):
                e_ref[pl.ds(pl.multiple_of(ka * nb, nb), nb), :] = y[:nb]
                keep = jnp.where((ka == 0) | (ka == half), 0.0, 1.0)
                e_ref[pl.ds(pl.multiple_of((half + 1 + ka) * nb, nb), nb), :] = y[nb:] * keep
            return carry

        _loop((half + 1) // group, spectral, None)

        def inverse(i, carry):
            outs = []
            for q in range(pair // V7X_SUBLANES):
                off = i * pair + q * V7X_SUBLANES
                yg = _gather_tiles(e_ref, [_hyena_slot_block(s, na) * nb + off for s in range(na)])
                outs.append(_dot(lhs2_ref[...], yg.astype(BF16)))
            for a in range(half):
                sl = slice(a * V7X_SUBLANES, (a + 1) * V7X_SUBLANES)
                conv = jnp.concatenate([o_[sl] for o_ in outs], axis=0)
                r = pl.multiple_of(a * nb + i * pair, pair)
                res = gate_ref[pl.ds(r, pair), :].astype(F32) * conv
                if order == 0:
                    z_ref[pl.ds(r, pair), :] = res
                else:
                    o_ref[pl.ds(r, pair), :] = res.astype(BF16)
            return carry

        _loop(nb // pair, inverse, None)


def _hyena(hyc, hspec, lhs1, gmat, lhs2, *, seq, width, ct):
    n = hyc.shape[0]
    nct = width // ct
    nb = HYENA_NB
    half = seq // nb
    return pl.pallas_call(
        functools.partial(_hyena_body, seq=seq),
        grid=(nct, n // seq),
        in_specs=[
            pl.BlockSpec((seq, ct), lambda c, b: (b, c)),
            pl.BlockSpec((seq, ct), lambda c, b: (b, nct + c)),
            pl.BlockSpec((seq, ct), lambda c, b: (b, 2 * nct + c)),
            pl.BlockSpec((hspec.shape[0], hspec.shape[1], ct), lambda c, b: (0, 0, c),
                         pipeline_mode=pl.Buffered(1)),
            _resident(lhs1.shape),
            _resident(gmat.shape),
            _resident(lhs2.shape),
        ],
        out_specs=pl.BlockSpec((seq, ct), lambda c, b: (b, c)),
        out_shape=jax.ShapeDtypeStruct((n, width), BF16),
        scratch_shapes=[pltpu.VMEM((seq, ct), F32),
                        pltpu.VMEM((2 * (half + 1) * nb, ct), F32)],
        compiler_params=_params(2),
        name="hyena",
    )(hyc, hyc, hyc, hspec, lhs1, gmat, lhs2)


def _fnet_body(p_ref, q_ref, a1_ref, a2_ref, o_ref, t_ref, *, seq, na, nb):
    sub = V7X_SUBLANES

    def stage1(i, carry):
        off = i * sub
        starts = [na * n2 + off for n2 in range(nb)]
        xin = jnp.concatenate([_gather_tiles(p_ref, starts), _gather_tiles(q_ref, starts)],
                              axis=0).astype(BF16)
        out = _dot(a1_ref[...], xin)
        for part in range(2):
            for lo in range(sub):
                src = (part * sub + lo) * nb
                dst = part * seq + (off + lo) * nb
                t_ref[pl.ds(pl.multiple_of(dst, sub), nb), :] = out[src:src + nb]
        return carry

    _loop(na // sub, stage1, None)

    def stage2(i, carry):
        off = i * sub
        starts = [part * seq + n1 * nb + off for part in range(2) for n1 in range(na)]
        tin = _gather_tiles(t_ref, starts).astype(BF16)
        out = _dot(a2_ref[i], tin)
        for k1 in range(na):
            o_ref[pl.ds(pl.multiple_of(k1 * nb + off, sub), sub), :] = out[k1 * sub:(k1 + 1) * sub]
        return carry

    _loop(nb // sub, stage2, None)


def _fnet(pq, a1, a2, *, seq, fw, ct, na, nb):
    n = pq.shape[0]
    nct = fw // ct
    return pl.pallas_call(
        functools.partial(_fnet_body, seq=seq, na=na, nb=nb),
        grid=(n // seq, nct),
        in_specs=[
            pl.BlockSpec((seq, ct), lambda b, c: (b, c)),
            pl.BlockSpec((seq, ct), lambda b, c: (b, nct + c)),
            _resident(a1.shape),
            _resident(a2.shape),
        ],
        out_specs=pl.BlockSpec((seq, ct), lambda b, c: (b, c)),
        out_shape=jax.ShapeDtypeStruct((n, fw), F32),
        scratch_shapes=[pltpu.VMEM((2 * seq, ct), F32)],
        compiler_params=_params(2),
        name="fnet_seq",
    )(pq, pq, a1, a2)


def _merge_body(x_ref, z_ref, f_ref, po_ref, pp_ref, pn_ref, gt_ref, wha_ref, wfo_ref, wpl_ref,
                ps_ref, wo_ref, gpost_ref, o_ref, ext_ref, m_ref, *, tm, seq, d):
    i = pl.program_id(0)
    sub = V7X_SUBLANES
    has_prev = lax.rem(i * tm, seq) != 0
    has_next = lax.rem((i + 1) * tm, seq) != 0
    ext_ref[pl.ds(0, sub), :] = jnp.where(has_prev, pp_ref[...], 0.0)
    ext_ref[pl.ds(sub, tm), :] = po_ref[...]
    ext_ref[pl.ds(sub + tm, sub), :] = jnp.where(has_next, pn_ref[...], 0.0)

    gate = lambda q, lo, hi: gt_ref[:, q * d + lo:q * d + hi].astype(F32)
    m_ref[...] = gate(0, 0, d) * _dot(z_ref[...], wha_ref[...])
    m_ref[...] += gate(1, 0, d) * _dot(f_ref[...].astype(BF16), wfo_ref[...])

    pos = lax.rem(i * tm, seq) + lax.broadcasted_iota(jnp.int32, (tm, 1), 0)
    n_pool = len(POOL_WINDOWS)
    gw = po_ref.shape[1] // n_pool
    ow = d // n_pool
    for q, w in enumerate(POOL_WINDOWS):
        before = w // 2
        after = w - 1 - before
        cols = slice(q * gw, (q + 1) * gw)
        tot = ext_ref[pl.ds(sub - before, tm), cols]
        for s in range(-before + 1, after + 1):
            tot = tot + ext_ref[pl.ds(sub + s, tm), cols]
        lo = jnp.maximum(pos - before, 0)
        hi = jnp.minimum(pos + after, seq - 1)
        cnt = (hi - lo + 1).astype(F32)
        mq = (tot / cnt - po_ref[:, cols]).astype(BF16)
        yc = _dot(mq, wpl_ref[q]) * ps_ref[:, q * ow:(q + 1) * ow]
        m_ref[:, q * ow:(q + 1) * ow] += gate(2, q * ow, (q + 1) * ow) * yc

    y = _dot(m_ref[...].astype(BF16), wo_ref[...])
    o_ref[...] = x_ref[...] + _rms(y, gpost_ref[...])


def _merge(x2, z2, fy, po, gates, wha, wfo, wpool, pscale, w_out, g_post, *, tm, seq):
    n, d = x2.shape
    pw = po.shape[1]
    sub = V7X_SUBLANES
    hb = tm // sub
    last_hb = n // sub - 1
    body = functools.partial(_merge_body, tm=tm, seq=seq, d=d)
    return pl.pallas_call(
        body,
        grid=(n // tm,),
        in_specs=[
            pl.BlockSpec((tm, d), lambda i: (i, 0)),
            pl.BlockSpec((tm, z2.shape[1]), lambda i: (i, 0)),
            pl.BlockSpec((tm, fy.shape[1]), lambda i: (i, 0)),
            pl.BlockSpec((tm, pw), lambda i: (i, 0)),
            pl.BlockSpec((sub, pw), lambda i: (jnp.maximum(i * hb - 1, 0), 0)),
            pl.BlockSpec((sub, pw), lambda i: (jnp.minimum((i + 1) * hb, last_hb), 0)),
            pl.BlockSpec((tm, gates.shape[1]), lambda i: (i, 0)),
            _resident(wha.shape),
            _resident(wfo.shape),
            _resident(wpool.shape),
            _resident((1, d)),
            _resident(w_out.shape),
            _resident((1, d)),
        ],
        out_specs=pl.BlockSpec((tm, d), lambda i: (i, 0)),
        out_shape=jax.ShapeDtypeStruct((n, d), F32),
        scratch_shapes=[pltpu.VMEM((tm + 2 * sub, pw), F32), pltpu.VMEM((tm, d), F32)],
        compiler_params=_params(1),
        name="mixer_merge",
    )(x2, z2, fy, po, po, po, gates, wha, wfo, wpool, pscale.reshape(1, d), w_out, g_post.reshape(1, d))


def _kv_body(m_ref, g_ref, w_ref, o_ref):
    o_ref[...] = _dot(_rms(m_ref[...], g_ref[...]).astype(BF16), w_ref[...]).astype(BF16)


def _kv(mem2, g, w_kv, *, n_mem, tn):
    n, d = mem2.shape
    cols = w_kv.shape[1]
    return pl.pallas_call(
        _kv_body,
        grid=(cols // tn, n // n_mem),
        in_specs=[
            pl.BlockSpec((n_mem, d), lambda j, b: (b, 0)),
            pl.BlockSpec((1, d), lambda j, b: (0, 0)),
            pl.BlockSpec((d, tn), lambda j, b: (0, j)),
        ],
        out_specs=pl.BlockSpec((n_mem, tn), lambda j, b: (b, j)),
        out_shape=jax.ShapeDtypeStruct((n, cols), BF16),
        compiler_params=_params(2),
        name="mem_kv",
    )(mem2, g.reshape(1, d), w_kv)


def _attn_body(x_ref, gpre_ref, wq_ref, k_ref, v_ref, wo_ref, gpost_ref, o_ref, *, d):
    dh = d // N_HEADS
    x = x_ref[...]
    q = _dot(_rms(x, gpre_ref[...]).astype(BF16), wq_ref[...])
    heads = []
    for h in range(N_HEADS):
        cols = slice(h * dh, (h + 1) * dh)
        s = lax.dot_general(q[:, cols].astype(BF16), k_ref[:, cols], (((1,), (1,)), ((), ())),
                            preferred_element_type=F32) * (dh ** -0.5)
        e = jnp.exp(s - jnp.max(s, axis=-1, keepdims=True))
        p = e / jnp.sum(e, axis=-1, keepdims=True)
        heads.append(_dot(p.astype(BF16), v_ref[:, cols]))
    o = jnp.concatenate(heads, axis=1).astype(BF16)
    o_ref[...] = x + _rms(_dot(o, wo_ref[...]), gpost_ref[...])


def _attn(x2, g_pre, w_q, kv, w_o, g_post, *, tm, seq, n_mem):
    n, d = x2.shape
    tiles_per_seq = seq // tm
    return pl.pallas_call(
        functools.partial(_attn_body, d=d),
        grid=(n // tm,),
        in_specs=[
            pl.BlockSpec((tm, d), lambda i: (i, 0)),
            _resident((1, d)),
            _resident(w_q.shape),
            pl.BlockSpec((n_mem, d), lambda i: (i // tiles_per_seq, 0)),
            pl.BlockSpec((n_mem, d), lambda i: (i // tiles_per_seq, 1)),
            _resident(w_o.shape),
            _resident((1, d)),
        ],
        out_specs=pl.BlockSpec((tm, d), lambda i: (i, 0)),
        out_shape=jax.ShapeDtypeStruct((n, d), F32),
        compiler_params=_params(1),
        name="mem_attn",
    )(x2, g_pre.reshape(1, d), w_q, kv, kv, w_o, g_post.reshape(1, d))


def kernel(x, mem, g_ffn1_pre, w_ffn1_gu, w_ffn1_down, g_ffn1_post, g_mix_pre, w_in, hyena_conv_w,
           hyena_conv_b, filt_w1, filt_b1, filt_w2, filt_b2, filt_w3, filt_b3, filt_w4, filt_freq,
           hyena_d, w_hyena_out, w_fnet_out, w_pool, pool_scale, w_out, g_mix_post, g_mem_pre,
           g_mem_kv, w_q, w_kv, w_o, g_mem_post, g_ffn2_pre, w_ffn2_gu, w_ffn2_down, g_ffn2_post):
    batch, seq, d = x.shape
    n_mem = mem.shape[1]
    depth = w_in.shape[0]
    hy_w = w_hyena_out.shape[1]
    hy_cols = hyena_conv_w.shape[2]
    fw = w_fnet_out.shape[1]
    pw = w_pool.shape[1] * w_pool.shape[2]

    tm = 512
    tn = fw + pw
    tf = 512
    ct = 256

    lhs1, gmat, lhs2 = _hyena_mats(seq)
    a1, a2, cs, fna, fnb = _fnet_mats(seq, fw // FNET_GROUPS)

    bf = lambda w: w.astype(BF16)
    x2 = x.reshape(batch * seq, d)
    mem2 = mem.reshape(batch * n_mem, d)
    for l in range(depth):
        x2 = _ffn(x2, g_ffn1_pre[l], bf(w_ffn1_gu[l]), bf(w_ffn1_down[l]), g_ffn1_post[l], tm=tm, tf=tf)

        hyc, pq, po, gates = _proj(x2, g_mix_pre[l], bf(w_in[l]), hyena_conv_w[l], hyena_conv_b[l], cs,
                                   tm=tm, tn=tn, seq=seq, hy_cols=hy_cols, fw=fw, pw=pw)
        h3 = _filter_mlp(filt_w1[l], filt_b1[l], filt_w2[l], filt_b2[l], filt_w3[l], filt_b3[l],
                         filt_freq[l], seq=seq)
        hspec = _filter_spectrum(h3, filt_w4[l], hyena_d[l], lhs1, gmat, seq=seq, width=hy_w, cw=ct // 2)
        z2 = _hyena(hyc, hspec, lhs1, gmat, lhs2, seq=seq, width=hy_w, ct=ct)
        fy = _fnet(pq, a1, a2, seq=seq, fw=fw, ct=ct, na=fna, nb=fnb)
        x2 = _merge(x2, z2, fy, po, gates, bf(w_hyena_out[l]), bf(w_fnet_out[l]), bf(w_pool[l]),
                    pool_scale[l], bf(w_out[l]), g_mix_post[l], tm=tm // 2, seq=seq)

        kv = _kv(mem2, g_mem_kv[l], bf(w_kv[l]), n_mem=n_mem, tn=tn)
        x2 = _attn(x2, g_mem_pre[l], bf(w_q[l]), kv, bf(w_o[l]), g_mem_post[l], tm=tm, seq=seq, n_mem=n_mem)

        x2 = _ffn(x2, g_ffn2_pre[l], bf(w_ffn2_gu[l]), bf(w_ffn2_down[l]), g_ffn2_post[l], tm=tm, tf=tf)
    return x2.reshape(batch, seq, d)
```

```python
import functools
import math

import jax
import jax.numpy as jnp
import numpy as np
from jax import lax
from jax.experimental import pallas as pl
from jax.experimental.pallas import tpu as pltpu

F32 = jnp.float32
BF16 = jnp.bfloat16

RMS_EPS = 1e-6
MACARON_WEIGHT = 0.5
N_HEADS = 4
POOL_WINDOWS = (2, 4, 8, 16)
FNET_GROUPS = 4
FILTER_BANDS = 16
DECAY_TARGET = 1e-2
FAST_DECAY_PCT = 0.3
SLOW_DECAY_PCT = 1.5

V7X_SUBLANES = 8
V7X_BF16_ROWS = 16
V7X_VMEM_LIMIT = 60 * 2**20

HYENA_NB = 128
HALO = V7X_BF16_ROWS
ROW_CHUNK = 256


def _params(n_axes):
    return pltpu.CompilerParams(
        dimension_semantics=("arbitrary",) * n_axes,
        vmem_limit_bytes=V7X_VMEM_LIMIT,
    )


def _resident(shape):
    zeros = (0,) * len(shape)
    return pl.BlockSpec(shape, lambda *_: zeros, pipeline_mode=pl.Buffered(1))


def _resident_layer(stacked, layer):
    tail = (0,) * (stacked.ndim - 1)
    return pl.BlockSpec((None,) + stacked.shape[1:], lambda *_: (layer,) + tail,
                        pipeline_mode=pl.Buffered(1))


def _rms(x, g):
    ms = jnp.mean(x * x, axis=-1, keepdims=True)
    return x * lax.rsqrt(ms + RMS_EPS) * g


def _dot(a, b):
    return jnp.dot(a, b, preferred_element_type=F32)


def _dot3(a, b):
    ah = a.astype(BF16)
    al = (a - ah.astype(F32)).astype(BF16)
    bh = b.astype(BF16)
    bl = (b - bh.astype(F32)).astype(BF16)
    return _dot(ah, bh) + (_dot(al, bh) + _dot(ah, bl))


def _row_chunks(rows, fn):
    chunk = ROW_CHUNK if rows % ROW_CHUNK == 0 else rows

    def body(c, carry):
        fn(pl.ds(pl.multiple_of(c * chunk, chunk), chunk))
        return carry

    lax.fori_loop(0, rows // chunk, body, 0)


def _ffn_body(x_ref, gpre_ref, wg_ref, wu_ref, wd_ref, gpost_ref, o_ref, u_ref):
    k = pl.program_id(1)
    rows = x_ref.shape[0]

    @pl.when(k == 0)
    def _():
        def pre(r):
            u_ref[r, :] = _rms(x_ref[r, :], gpre_ref[...]).astype(BF16)
        _row_chunks(rows, pre)

    u = u_ref[...]
    a = _dot(u, wg_ref[...])
    b = _dot(u, wu_ref[...])
    h = (a * (1.0 / (1.0 + jnp.exp(-a))) * b).astype(BF16)

    @pl.when(k == 0)
    def _():
        o_ref[...] = _dot(h, wd_ref[...])

    @pl.when(k > 0)
    def _():
        o_ref[...] += _dot(h, wd_ref[...])

    @pl.when(k == pl.num_programs(1) - 1)
    def _():
        def post(r):
            o_ref[r, :] = x_ref[r, :] + MACARON_WEIGHT * _rms(o_ref[r, :], gpost_ref[...])
        _row_chunks(rows, post)


def _ffn(x2, g_pre, w_gu, w_down, g_post, layer, *, tm, tf):
    n, d = x2.shape
    f = w_down.shape[1]
    nk = f // tf
    return pl.pallas_call(
        _ffn_body,
        grid=(n // tm, nk),
        in_specs=[
            pl.BlockSpec((tm, d), lambda i, k: (i, 0)),
            pl.BlockSpec((1, d), lambda i, k: (0, 0)),
            pl.BlockSpec((None, d, tf), lambda i, k: (layer, 0, k)),
            pl.BlockSpec((None, d, tf), lambda i, k: (layer, 0, k + nk)),
            pl.BlockSpec((None, tf, d), lambda i, k: (layer, k, 0)),
            pl.BlockSpec((1, d), lambda i, k: (0, 0)),
        ],
        out_specs=pl.BlockSpec((tm, d), lambda i, k: (i, 0)),
        out_shape=jax.ShapeDtypeStruct((n, d), F32),
        scratch_shapes=[pltpu.VMEM((tm, d), BF16)],
        compiler_params=_params(2),
        name="ffn",
    )(x2, g_pre.reshape(1, d), w_gu, w_gu, w_down, g_post.reshape(1, d))


def _proj_body(x_ref, xp_ref, xn_ref, g_ref, w_ref, cw_ref, cb_ref, cs_ref,
               hy_ref, pq_ref, po_ref, gt_ref, u_ref, h_ref, *, tm, seq, n_hy, fw):
    i = pl.program_id(0)
    j = pl.program_id(1)

    @pl.when(j == 0)
    def _():
        g = g_ref[...]
        has_prev = lax.rem(i * tm, seq) != 0
        has_next = lax.rem((i + 1) * tm, seq) != 0
        u_ref[pl.ds(0, HALO), :] = jnp.where(has_prev, _rms(xp_ref[...], g), 0.0).astype(BF16)
        chunk = ROW_CHUNK if tm % ROW_CHUNK == 0 else tm

        def pre(c, carry):
            r = pl.multiple_of(c * chunk, chunk)
            u_ref[pl.ds(pl.multiple_of(HALO + r, HALO), chunk), :] = (
                _rms(x_ref[pl.ds(r, chunk), :], g).astype(BF16))
            return carry

        lax.fori_loop(0, tm // chunk, pre, 0)
        u_ref[pl.ds(HALO + tm, HALO), :] = jnp.where(has_next, _rms(xn_ref[...], g), 0.0).astype(BF16)

    @pl.when(j < n_hy)
    def _():
        h_ref[...] = _dot(u_ref[...], w_ref[...])
        y = cb_ref[...] + h_ref[pl.ds(HALO - 1, tm), :] * cw_ref[0:1, :]
        y = y + h_ref[pl.ds(HALO, tm), :] * cw_ref[1:2, :]
        y = y + h_ref[pl.ds(HALO + 1, tm), :] * cw_ref[2:3, :]
        hy_ref[...] = y.astype(BF16)

    @pl.when(j == n_hy)
    def _():
        c = _dot(u_ref[pl.ds(HALO, tm), :], w_ref[...])
        po_ref[...] = c[:, fw:]
        fb = c[:, :fw].astype(BF16)
        gw = fw // FNET_GROUPS
        res = [_dot(fb[:, q * gw:(q + 1) * gw], cs_ref[...]) for q in range(FNET_GROUPS)]
        pq_ref[...] = jnp.concatenate([r[:, :gw] for r in res] + [r[:, gw:] for r in res], axis=1)

    @pl.when(j > n_hy)
    def _():
        c = _dot(u_ref[pl.ds(HALO, tm), :], w_ref[...])
        gt_ref[...] = (1.0 / (1.0 + jnp.exp(-c))).astype(BF16)


def _proj(x2, g, w_in, conv_w, conv_b, cs, layer, *, tm, tn, seq, hy_cols, fw, pw):
    n, d = x2.shape
    cols = w_in.shape[2]
    n_hy = hy_cols // tn
    assert hy_cols % tn == 0 and fw + pw == tn and (cols - hy_cols - tn) % tn == 0
    nj = cols // tn
    n_gate = cols - hy_cols - tn
    hb = tm // HALO
    last_hb = n // HALO - 1
    body = functools.partial(_proj_body, tm=tm, seq=seq, n_hy=n_hy, fw=fw)
    return pl.pallas_call(
        body,
        grid=(n // tm, nj),
        in_specs=[
            pl.BlockSpec((tm, d), lambda i, j: (i, 0)),
            pl.BlockSpec((HALO, d), lambda i, j: (jnp.maximum(i * hb - 1, 0), 0)),
            pl.BlockSpec((HALO, d), lambda i, j: (jnp.minimum((i + 1) * hb, last_hb), 0)),
            pl.BlockSpec((1, d), lambda i, j: (0, 0)),
            pl.BlockSpec((None, d, tn), lambda i, j: (layer, 0, j)),
            pl.BlockSpec((3, tn), lambda i, j: (0, jnp.minimum(j, n_hy - 1))),
            pl.BlockSpec((1, tn), lambda i, j: (0, jnp.minimum(j, n_hy - 1))),
            _resident(cs.shape),
        ],
        out_specs=[
            pl.BlockSpec((tm, tn), lambda i, j: (i, jnp.minimum(j, n_hy - 1))),
            pl.BlockSpec((tm, 2 * fw), lambda i, j: (i, 0)),
            pl.BlockSpec((tm, pw), lambda i, j: (i, 0)),
            pl.BlockSpec((tm, tn), lambda i, j: (i, jnp.maximum(j - n_hy - 1, 0))),
        ],
        out_shape=[
            jax.ShapeDtypeStruct((n, hy_cols), BF16),
            jax.ShapeDtypeStruct((n, 2 * fw), F32),
            jax.ShapeDtypeStruct((n, pw), F32),
            jax.ShapeDtypeStruct((n, n_gate), BF16),
        ],
        scratch_shapes=[pltpu.VMEM((tm + 2 * HALO, d), BF16), pltpu.VMEM((tm + 2 * HALO, tn), F32)],
        compiler_params=_params(2),
        name="mixer_proj",
    )(x2, x2, x2, g.reshape(1, d), w_in, conv_w, conv_b.reshape(1, hy_cols), cs)


def _hyena_slot_block(s, na):
    half = na // 2
    return s if s <= half else (half + 1) + (s - half)


def _hyena_mats(seq):
    nb = HYENA_NB
    na = 2 * seq // nb
    half = na // 2
    eye = np.eye(V7X_SUBLANES)
    a = np.arange(half)[None, :]
    f1 = np.zeros((na, half))
    k_re = np.arange(half + 1)[:, None]
    f1[: half + 1] = np.cos(2 * np.pi * k_re * a / na)
    k_im = np.arange(1, half)[:, None]
    f1[half + 1:] = -np.sin(2 * np.pi * k_im * a / na)
    lhs1 = np.kron(f1, eye)
    b = np.arange(nb)[None, :]
    kb = np.arange(nb)[:, None]
    g = np.zeros((half + 1, 2 * nb, 2 * nb))
    for ka in range(half + 1):
        ang = 2 * np.pi * (kb * b / nb + b * ka / (2 * seq))
        gre, gim = np.cos(ang), -np.sin(ang)
        g[ka] = np.block([[gre, -gim], [gim, gre]])
    a_col = np.arange(half)[:, None]
    f2 = np.zeros((half, na))
    wgt = np.full(half + 1, 2.0)
    wgt[0] = wgt[half] = 1.0
    f2[:, : half + 1] = wgt[None, :] * np.cos(2 * np.pi * a_col * np.arange(half + 1)[None, :] / na)
    f2[:, half + 1:] = -2.0 * np.sin(2 * np.pi * a_col * np.arange(1, half)[None, :] / na)
    lhs2 = np.kron(f2 / (2 * seq), eye)
    return (jnp.asarray(lhs1, BF16), jnp.asarray(g, BF16), jnp.asarray(lhs2, BF16))


def _fnet_mats(seq, group):
    na = int(round(math.sqrt(seq)))
    nb = seq // na
    assert na * nb == seq and nb % V7X_SUBLANES == 0
    eye = np.eye(V7X_SUBLANES)
    n2 = np.arange(nb)[None, :]
    k2 = np.arange(nb)[:, None]
    ang = 2 * np.pi * k2 * n2 / nb
    c, s = np.cos(ang), np.sin(ang)
    blk = np.block([[c, s], [-s, c]])
    a1 = np.zeros((2, V7X_SUBLANES, nb, 2, nb, V7X_SUBLANES))
    for po in range(2):
        for pi_ in range(2):
            sub = blk[po * nb:(po + 1) * nb, pi_ * nb:(pi_ + 1) * nb]
            for lo in range(V7X_SUBLANES):
                a1[po, lo, :, pi_, :, lo] = sub
    a1 = a1.reshape(2 * V7X_SUBLANES * nb, 2 * V7X_SUBLANES * nb)
    a2 = np.zeros((nb // V7X_SUBLANES, na, V7X_SUBLANES, 2, na, V7X_SUBLANES))
    k1 = np.arange(na)[:, None]
    n1 = np.arange(na)[None, :]
    for hi in range(nb // V7X_SUBLANES):
        for lo in range(V7X_SUBLANES):
            k2v = hi * V7X_SUBLANES + lo
            phi = 2 * np.pi * (n1 * k1 / na + n1 * k2v / seq)
            a2[hi, :, lo, 0, :, lo] = np.cos(phi)
            a2[hi, :, lo, 1, :, lo] = np.sin(phi)
    a2 = a2.reshape(nb // V7X_SUBLANES, na * V7X_SUBLANES, 2 * na * V7X_SUBLANES) / math.sqrt(seq)
    cc = np.arange(group)[:, None]
    mm = np.arange(group)[None, :]
    angc = 2 * np.pi * cc * mm / group
    cs = np.concatenate([np.cos(angc), -np.sin(angc)], axis=1) / math.sqrt(group)
    return jnp.asarray(a1, BF16), jnp.asarray(a2, BF16), jnp.asarray(cs, BF16), na, nb


def _loop(n, body, unroll):
    if unroll is None:
        for i in range(n):
            body(i, 0)
        return
    while n % unroll:
        unroll -= 1
    lax.fori_loop(0, n, body, 0, unroll=unroll)


def _gather_tiles(ref, starts):
    tiles = [ref[pl.ds(pl.multiple_of(s, V7X_SUBLANES), V7X_SUBLANES), :] for s in starts]
    return jnp.concatenate(tiles, axis=0)


def _hyena_stage1(z_ref, e_ref, lhs1_ref, seq):
    nb = HYENA_NB
    na = 2 * seq // nb
    half = na // 2

    def body(i, carry):
        off = i * V7X_SUBLANES
        xg = _gather_tiles(z_ref, [nb * a + off for a in range(half)]).astype(BF16)
        out = _dot(lhs1_ref[...], xg)
        for s in range(na):
            row = _hyena_slot_block(s, na) * nb + off
            e_ref[pl.ds(pl.multiple_of(row, V7X_SUBLANES), V7X_SUBLANES), :] = (
                out[s * V7X_SUBLANES:(s + 1) * V7X_SUBLANES])
        return carry

    _loop(nb // V7X_SUBLANES, body, None)


def _hyena_zero_unused(e_ref, seq):
    nb = HYENA_NB
    half = seq // nb
    zero = jnp.zeros((nb, e_ref.shape[1]), F32)
    e_ref[pl.ds((half + 1) * nb, nb), :] = zero
    e_ref[pl.ds((2 * half + 1) * nb, nb), :] = zero


def _hyena_load_ka(e_ref, ka, seq):
    nb = HYENA_NB
    half = seq // nb
    re = e_ref[pl.ds(pl.multiple_of(ka * nb, nb), nb), :]
    im = e_ref[pl.ds(pl.multiple_of((half + 1 + ka) * nb, nb), nb), :]
    return jnp.concatenate([re, im], axis=0)


def _filter_mlp_body(bands_ref, w1t_ref, w1c_ref, w1s_ref, b1_ref, w2_ref, b2_ref, w3_ref, b3_ref,
                     fr_ref, o_ref, *, seq):
    idx = lax.broadcasted_iota(jnp.int32, (1, 2 * seq), 1)
    p = jnp.where(idx < seq, idx, 2 * seq - idx).astype(F32)
    t = p / (seq - 1.0)
    ang = bands_ref[...] * ((2.0 * math.pi / seq) * p)
    fr = fr_ref[...]
    pre = (w1t_ref[...] * t
           + _dot3(w1c_ref[...], jnp.cos(ang))
           + _dot3(w1s_ref[...], -jnp.sin(ang))
           + b1_ref[...])
    h = jnp.sin(fr * pre)
    h = jnp.sin(fr * (_dot3(w2_ref[...], h) + b2_ref[...]))
    h = jnp.sin(fr * (_dot3(w3_ref[...], h) + b3_ref[...]))
    o_ref[...] = jnp.concatenate([h[:, :seq], h[:, seq:]], axis=0).T


def _filter_mlp(fw1, fb1, fw2, fb2, fw3, fb3, freq, *, seq):
    hid = fw2.shape[0]
    bands = np.linspace(1e-4, FILTER_BANDS - 1, FILTER_BANDS).astype(np.float32).reshape(-1, 1)
    col = lambda v: v.reshape(hid, 1)
    args = (jnp.asarray(bands), fw1[0].reshape(hid, 1), fw1[1:1 + FILTER_BANDS].T,
            fw1[1 + FILTER_BANDS:].T, col(fb1), fw2.T, col(fb2), fw3.T, col(fb3), col(freq))
    return pl.pallas_call(
        functools.partial(_filter_mlp_body, seq=seq),
        grid=(1,),
        in_specs=[_resident(a.shape) for a in args],
        out_specs=pl.BlockSpec((seq, 2 * hid), lambda i: (0, 0)),
        out_shape=jax.ShapeDtypeStruct((seq, 2 * hid), F32),
        compiler_params=_params(1),
        name="filter_mlp",
    )(*args)


def _filter_spec_body(h3_ref, w4f_ref, w4b_ref, dl_ref, d_ref, lhs1_ref, g_ref,
                      o_ref, z_ref, e_ref, *, seq, cw):
    nb = HYENA_NB
    half = seq // nb
    row = lax.broadcasted_iota(jnp.int32, (seq, 1), 0)
    rowf = row.astype(F32)
    dl = dl_ref[...]
    zero = jnp.zeros_like(w4f_ref[...])
    w4 = jnp.concatenate([jnp.concatenate([w4f_ref[...], zero], axis=1),
                          jnp.concatenate([zero, w4b_ref[...]], axis=1)], axis=0)
    h4 = _dot3(h3_ref[...], w4)
    dec = jnp.concatenate([jnp.exp(-(rowf / (seq - 1.0)) * dl),
                           jnp.exp(-((seq - rowf) / (seq - 1.0)) * dl)], axis=1)
    lane = lax.broadcasted_iota(jnp.int32, (1, 2 * cw), 1)
    z_ref[...] = jnp.where((row == 0) & (lane >= cw), 0.0, h4 * dec)

    _hyena_zero_unused(e_ref, seq)
    _hyena_stage1(z_ref, e_ref, lhs1_ref, seq)
    dsk = d_ref[0]

    def body(ka, carry):
        ein = _hyena_load_ka(e_ref, ka, seq).astype(BF16)
        x = _dot(g_ref[ka], ein)
        sgn = 1.0 - 2.0 * (ka % 2)
        hsp = x[:, 0:cw] + sgn * x[:, cw:2 * cw]
        part = lax.broadcasted_iota(jnp.int32, (2 * nb, 1), 0)
        hsp = hsp + jnp.where(part < nb, dsk, 0.0)
        o_ref[0, pl.ds(pl.multiple_of(ka * 2 * nb, 2 * nb), 2 * nb), :] = hsp.astype(BF16)
        return carry

    _loop(half + 1, body, None)


def _filter_spectrum(h3, fw4, d_skip, lhs1, gmat, *, seq, width, cw):
    hid = h3.shape[1] // 2
    orders = d_skip.shape[0]
    nct = width // cw
    nb = HYENA_NB
    half = seq // nb
    max_decay = math.log(DECAY_TARGET) / FAST_DECAY_PCT
    min_decay = math.log(DECAY_TARGET) / SLOW_DECAY_PCT
    deltas = np.abs(np.linspace(min_decay, max_decay, width)).astype(np.float32).reshape(1, width)
    rows = (half + 1) * 2 * nb
    body = functools.partial(_filter_spec_body, seq=seq, cw=cw)
    return pl.pallas_call(
        body,
        grid=(orders, nct),
        in_specs=[
            _resident(h3.shape),
            pl.BlockSpec((hid, cw), lambda o, c: (0, o * 2 * nct + c)),
            pl.BlockSpec((hid, cw), lambda o, c: (0, o * 2 * nct + nct + c)),
            pl.BlockSpec((1, cw), lambda o, c: (0, c)),
            pl.BlockSpec((1, 1, cw), lambda o, c: (o, 0, c)),
            _resident(lhs1.shape),
            _resident(gmat.shape),
        ],
        out_specs=pl.BlockSpec((1, rows, cw), lambda o, c: (o, 0, c)),
        out_shape=jax.ShapeDtypeStruct((orders, rows, width), BF16),
        scratch_shapes=[pltpu.VMEM((seq, 2 * cw), F32),
                        pltpu.VMEM((2 * (half + 1) * nb, 2 * cw), F32)],
        compiler_params=_params(2),
        name="filter_spectrum",
    )(h3, fw4, fw4, jnp.asarray(deltas), d_skip.reshape(orders, 1, width), lhs1, gmat)


def _hyena_body(v_ref, g1_ref, g2_ref, h_ref, lhs1_ref, g_ref, lhs2_ref, o_ref, z_ref, e_ref, *, seq):
    nb = HYENA_NB
    na = 2 * seq // nb
    half = na // 2
    pair = V7X_BF16_ROWS
    chunk = 512 if seq % 512 == 0 else seq

    def load_v(c, carry):
        r = pl.multiple_of(c * chunk, chunk)
        z_ref[pl.ds(r, chunk), :] = v_ref[pl.ds(r, chunk), :].astype(F32)
        return carry

    lax.fori_loop(0, seq // chunk, load_v, 0)
    _hyena_zero_unused(e_ref, seq)

    for order, gate_ref in enumerate((g1_ref, g2_ref)):
        _hyena_stage1(z_ref, e_ref, lhs1_ref, seq)

        group = 3 if (half + 1) % 3 == 0 else 1

        def spectral(it, carry):
            kas = [it * group + u for u in range(group)]
            eins = [_hyena_load_ka(e_ref, ka, seq).astype(BF16) for ka in kas]
            ys = []
            for ka, ein in zip(kas, eins):
                gk = g_ref[ka]
                x = _dot(gk, ein)
                hsp = h_ref[order, pl.ds(pl.multiple_of(ka * 2 * nb, 2 * nb), 2 * nb), :].astype(F32)
                xre, xim, hre, him = x[:nb], x[nb:], hsp[:nb], hsp[nb:]
                zsp = jnp.concatenate([xre * hre - xim * him, xre * him + xim * hre], axis=0).astype(BF16)
                ys.append(lax.dot_general(gk, zsp, (((0,), (0,)), ((), ())), preferred_element_type=F32))
            for ka, y in zip(kas, ys):
                e_ref[pl.ds(pl.multiple_of(ka * nb, nb), nb), :] = y[:nb]
                keep = jnp.where((ka == 0) | (ka == half), 0.0, 1.0)
                e_ref[pl.ds(pl.multiple_of((half + 1 + ka) * nb, nb), nb), :] = y[nb:] * keep
            return carry

        _loop((half + 1) // group, spectral, None)

        def inverse(i, carry):
            outs = []
            for q in range(pair // V7X_SUBLANES):
                off = i * pair + q * V7X_SUBLANES
                yg = _gather_tiles(e_ref, [_hyena_slot_block(s, na) * nb + off for s in range(na)])
                outs.append(_dot(lhs2_ref[...], yg.astype(BF16)))
            for a in range(half):
                sl = slice(a * V7X_SUBLANES, (a + 1) * V7X_SUBLANES)
                conv = jnp.concatenate([o_[sl] for o_ in outs], axis=0)
                r = pl.multiple_of(a * nb + i * pair, pair)
                res = gate_ref[pl.ds(r, pair), :].astype(F32) * conv
                if order == 0:
                    z_ref[pl.ds(r, pair), :] = res
                else:
                    o_ref[pl.ds(r, pair), :] = res.astype(BF16)
            return carry

        _loop(nb // pair, inverse, None)


def _hyena(hyc, hspec, lhs1, gmat, lhs2, *, seq, width, ct):
    n = hyc.shape[0]
    nct = width // ct
    nb = HYENA_NB
    half = seq // nb
    return pl.pallas_call(
        functools.partial(_hyena_body, seq=seq),
        grid=(nct, n // seq),
        in_specs=[
            pl.BlockSpec((seq, ct), lambda c, b: (b, c)),
            pl.BlockSpec((seq, ct), lambda c, b: (b, nct + c)),
            pl.BlockSpec((seq, ct), lambda c, b: (b, 2 * nct + c)),
            pl.BlockSpec((hspec.shape[0], hspec.shape[1], ct), lambda c, b: (0, 0, c),
                         pipeline_mode=pl.Buffered(1)),
            _resident(lhs1.shape),
            _resident(gmat.shape),
            _resident(lhs2.shape),
        ],
        out_specs=pl.BlockSpec((seq, ct), lambda c, b: (b, c)),
        out_shape=jax.ShapeDtypeStruct((n, width), BF16),
        scratch_shapes=[pltpu.VMEM((seq, ct), F32),
                        pltpu.VMEM((2 * (half + 1) * nb, ct), F32)],
        compiler_params=_params(2),
        name="hyena",
    )(hyc, hyc, hyc, hspec, lhs1, gmat, lhs2)


def _fnet_body(p_ref, q_ref, a1_ref, a2_ref, o_ref, t_ref, *, seq, na, nb):
    sub = V7X_SUBLANES

    def stage1(i, carry):
        off = i * sub
        starts = [na * n2 + off for n2 in range(nb)]
        xin = jnp.concatenate([_gather_tiles(p_ref, starts), _gather_tiles(q_ref, starts)],
                              axis=0).astype(BF16)
        out = _dot(a1_ref[...], xin)
        for part in range(2):
            for lo in range(sub):
                src = (part * sub + lo) * nb
                dst = part * seq + (off + lo) * nb
                t_ref[pl.ds(pl.multiple_of(dst, sub), nb), :] = out[src:src + nb]
        return carry

    _loop(na // sub, stage1, None)

    def stage2(i, carry):
        off = i * sub
        starts = [part * seq + n1 * nb + off for part in range(2) for n1 in range(na)]
        tin = _gather_tiles(t_ref, starts).astype(BF16)
        out = _dot(a2_ref[i], tin)
        for k1 in range(na):
            o_ref[pl.ds(pl.multiple_of(k1 * nb + off, sub), sub), :] = out[k1 * sub:(k1 + 1) * sub]
        return carry

    _loop(nb // sub, stage2, None)


def _fnet(pq, a1, a2, *, seq, fw, ct, na, nb):
    n = pq.shape[0]
    nct = fw // ct
    return pl.pallas_call(
        functools.partial(_fnet_body, seq=seq, na=na, nb=nb),
        grid=(n // seq, nct),
        in_specs=[
            pl.BlockSpec((seq, ct), lambda b, c: (b, c)),
            pl.BlockSpec((seq, ct), lambda b, c: (b, nct + c)),
            _resident(a1.shape),
            _resident(a2.shape),
        ],
        out_specs=pl.BlockSpec((seq, ct), lambda b, c: (b, c)),
        out_shape=jax.ShapeDtypeStruct((n, fw), F32),
        scratch_shapes=[pltpu.VMEM((2 * seq, ct), F32)],
        compiler_params=_params(2),
        name="fnet_seq",
    )(pq, pq, a1, a2)


def _merge_body(x_ref, z_ref, f_ref, po_ref, pp_ref, pn_ref, gt_ref, wha_ref, wfo_ref, wpl_ref,
                ps_ref, wo_ref, gpost_ref, o_ref, ext_ref, m_ref, *, tm, seq, d):
    i = pl.program_id(0)
    sub = V7X_SUBLANES
    has_prev = lax.rem(i * tm, seq) != 0
    has_next = lax.rem((i + 1) * tm, seq) != 0
    ext_ref[pl.ds(0, sub), :] = jnp.where(has_prev, pp_ref[...], 0.0)
    ext_ref[pl.ds(sub, tm), :] = po_ref[...]
    ext_ref[pl.ds(sub + tm, sub), :] = jnp.where(has_next, pn_ref[...], 0.0)

    gate = lambda q, lo, hi: gt_ref[:, q * d + lo:q * d + hi].astype(F32)
    m_ref[...] = gate(0, 0, d) * _dot(z_ref[...], wha_ref[...])
    m_ref[...] += gate(1, 0, d) * _dot(f_ref[...].astype(BF16), wfo_ref[...])

    pos = lax.rem(i * tm, seq) + lax.broadcasted_iota(jnp.int32, (tm, 1), 0)
    n_pool = len(POOL_WINDOWS)
    gw = po_ref.shape[1] // n_pool
    ow = d // n_pool
    for q, w in enumerate(POOL_WINDOWS):
        before = w // 2
        after = w - 1 - before
        cols = slice(q * gw, (q + 1) * gw)
        tot = ext_ref[pl.ds(sub - before, tm), cols]
        for s in range(-before + 1, after + 1):
            tot = tot + ext_ref[pl.ds(sub + s, tm), cols]
        lo = jnp.maximum(pos - before, 0)
        hi = jnp.minimum(pos + after, seq - 1)
        cnt = (hi - lo + 1).astype(F32)
        mq = (tot / cnt - po_ref[:, cols]).astype(BF16)
        yc = _dot(mq, wpl_ref[q]) * ps_ref[:, q * ow:(q + 1) * ow]
        m_ref[:, q * ow:(q + 1) * ow] += gate(2, q * ow, (q + 1) * ow) * yc

    y = _dot(m_ref[...].astype(BF16), wo_ref[...])
    o_ref[...] = x_ref[...] + _rms(y, gpost_ref[...])


def _merge(x2, z2, fy, po, gates, wha, wfo, wpool, pscale, w_out, g_post, layer, *, tm, seq):
    n, d = x2.shape
    pw = po.shape[1]
    sub = V7X_SUBLANES
    hb = tm // sub
    last_hb = n // sub - 1
    body = functools.partial(_merge_body, tm=tm, seq=seq, d=d)
    return pl.pallas_call(
        body,
        grid=(n // tm,),
        in_specs=[
            pl.BlockSpec((tm, d), lambda i: (i, 0)),
            pl.BlockSpec((tm, z2.shape[1]), lambda i: (i, 0)),
            pl.BlockSpec((tm, fy.shape[1]), lambda i: (i, 0)),
            pl.BlockSpec((tm, pw), lambda i: (i, 0)),
            pl.BlockSpec((sub, pw), lambda i: (jnp.maximum(i * hb - 1, 0), 0)),
            pl.BlockSpec((sub, pw), lambda i: (jnp.minimum((i + 1) * hb, last_hb), 0)),
            pl.BlockSpec((tm, gates.shape[1]), lambda i: (i, 0)),
            _resident_layer(wha, layer),
            _resident_layer(wfo, layer),
            _resident_layer(wpool, layer),
            _resident((1, d)),
            _resident_layer(w_out, layer),
            _resident((1, d)),
        ],
        out_specs=pl.BlockSpec((tm, d), lambda i: (i, 0)),
        out_shape=jax.ShapeDtypeStruct((n, d), F32),
        scratch_shapes=[pltpu.VMEM((tm + 2 * sub, pw), F32), pltpu.VMEM((tm, d), F32)],
        compiler_params=_params(1),
        name="mixer_merge",
    )(x2, z2, fy, po, po, po, gates, wha, wfo, wpool, pscale.reshape(1, d), w_out, g_post.reshape(1, d))


def _kv_body(m_ref, g_ref, w_ref, o_ref):
    o_ref[...] = _dot(_rms(m_ref[...], g_ref[...]).astype(BF16), w_ref[...]).astype(BF16)


def _kv(mem2, g, w_kv, layer, *, n_mem, tn):
    n, d = mem2.shape
    cols = w_kv.shape[2]
    return pl.pallas_call(
        _kv_body,
        grid=(cols // tn, n // n_mem),
        in_specs=[
            pl.BlockSpec((n_mem, d), lambda j, b: (b, 0)),
            pl.BlockSpec((1, d), lambda j, b: (0, 0)),
            pl.BlockSpec((None, d, tn), lambda j, b: (layer, 0, j)),
        ],
        out_specs=pl.BlockSpec((n_mem, tn), lambda j, b: (b, j)),
        out_shape=jax.ShapeDtypeStruct((n, cols), BF16),
        compiler_params=_params(2),
        name="mem_kv",
    )(mem2, g.reshape(1, d), w_kv)


def _attn_body(x_ref, gpre_ref, wq_ref, k_ref, v_ref, wo_ref, gpost_ref, o_ref, *, d):
    dh = d // N_HEADS
    x = x_ref[...]
    q = _dot(_rms(x, gpre_ref[...]).astype(BF16), wq_ref[...])
    heads = []
    for h in range(N_HEADS):
        cols = slice(h * dh, (h + 1) * dh)
        s = lax.dot_general(q[:, cols].astype(BF16), k_ref[:, cols], (((1,), (1,)), ((), ())),
                            preferred_element_type=F32) * (dh ** -0.5)
        e = jnp.exp(s - jnp.max(s, axis=-1, keepdims=True))
        p = e / jnp.sum(e, axis=-1, keepdims=True)
        heads.append(_dot(p.astype(BF16), v_ref[:, cols]))
    o = jnp.concatenate(heads, axis=1).astype(BF16)
    o_ref[...] = x + _rms(_dot(o, wo_ref[...]), gpost_ref[...])


def _attn(x2, g_pre, w_q, kv, w_o, g_post, layer, *, tm, seq, n_mem):
    n, d = x2.shape
    tiles_per_seq = seq // tm
    return pl.pallas_call(
        functools.partial(_attn_body, d=d),
        grid=(n // tm,),
        in_specs=[
            pl.BlockSpec((tm, d), lambda i: (i, 0)),
            _resident((1, d)),
            _resident_layer(w_q, layer),
            pl.BlockSpec((n_mem, d), lambda i: (i // tiles_per_seq, 0)),
            pl.BlockSpec((n_mem, d), lambda i: (i // tiles_per_seq, 1)),
            _resident_layer(w_o, layer),
            _resident((1, d)),
        ],
        out_specs=pl.BlockSpec((tm, d), lambda i: (i, 0)),
        out_shape=jax.ShapeDtypeStruct((n, d), F32),
        compiler_params=_params(1),
        name="mem_attn",
    )(x2, g_pre.reshape(1, d), w_q, kv, kv, w_o, g_post.reshape(1, d))


def kernel(x, mem, g_ffn1_pre, w_ffn1_gu, w_ffn1_down, g_ffn1_post, g_mix_pre, w_in, hyena_conv_w,
           hyena_conv_b, filt_w1, filt_b1, filt_w2, filt_b2, filt_w3, filt_b3, filt_w4, filt_freq,
           hyena_d, w_hyena_out, w_fnet_out, w_pool, pool_scale, w_out, g_mix_post, g_mem_pre,
           g_mem_kv, w_q, w_kv, w_o, g_mem_post, g_ffn2_pre, w_ffn2_gu, w_ffn2_down, g_ffn2_post):
    batch, seq, d = x.shape
    n_mem = mem.shape[1]
    depth = w_in.shape[0]
    hy_w = w_hyena_out.shape[1]
    hy_cols = hyena_conv_w.shape[2]
    fw = w_fnet_out.shape[1]
    pw = w_pool.shape[1] * w_pool.shape[2]

    tm = 512
    tn = fw + pw
    tf = 512
    ct = 256

    lhs1, gmat, lhs2 = _hyena_mats(seq)
    a1, a2, cs, fna, fnb = _fnet_mats(seq, fw // FNET_GROUPS)

    bf = lambda w: w.astype(BF16)
    w_ffn1_gu, w_ffn1_down, w_ffn2_gu, w_ffn2_down = map(bf, (w_ffn1_gu, w_ffn1_down, w_ffn2_gu, w_ffn2_down))
    w_in, w_hyena_out, w_fnet_out, w_pool, w_out = map(bf, (w_in, w_hyena_out, w_fnet_out, w_pool, w_out))
    w_q, w_kv, w_o = map(bf, (w_q, w_kv, w_o))
    x2 = x.reshape(batch * seq, d)
    mem2 = mem.reshape(batch * n_mem, d)
    for l in range(depth):
        x2 = _ffn(x2, g_ffn1_pre[l], w_ffn1_gu, w_ffn1_down, g_ffn1_post[l], l, tm=2 * tm, tf=tf)

        hyc, pq, po, gates = _proj(x2, g_mix_pre[l], w_in, hyena_conv_w[l], hyena_conv_b[l], cs, l,
                                   tm=2 * tm, tn=tn, seq=seq, hy_cols=hy_cols, fw=fw, pw=pw)
        h3 = _filter_mlp(filt_w1[l], filt_b1[l], filt_w2[l], filt_b2[l], filt_w3[l], filt_b3[l],
                         filt_freq[l], seq=seq)
        hspec = _filter_spectrum(h3, filt_w4[l], hyena_d[l], lhs1, gmat, seq=seq, width=hy_w, cw=ct // 2)
        z2 = _hyena(hyc, hspec, lhs1, gmat, lhs2, seq=seq, width=hy_w, ct=ct)
        fy = _fnet(pq, a1, a2, seq=seq, fw=fw, ct=ct, na=fna, nb=fnb)
        x2 = _merge(x2, z2, fy, po, gates, w_hyena_out, w_fnet_out, w_pool,
                    pool_scale[l], w_out, g_mix_post[l], l, tm=tm // 2, seq=seq)

        kv = _kv(mem2, g_mem_kv[l], w_kv, l, n_mem=n_mem, tn=tn)
        x2 = _attn(x2, g_mem_pre[l], w_q, kv, w_o, g_mem_post[l], l, tm=tm, seq=seq, n_mem=n_mem)

        x2 = _ffn(x2, g_ffn2_pre[l], w_ffn2_gu, w_ffn2_down, g_ffn2_post[l], l, tm=2 * tm, tf=tf)
    return x2.reshape(batch, seq, d)
```

```python
import functools
import math

import jax
import jax.numpy as jnp
import numpy as np
from jax import lax
from jax.experimental import pallas as pl
from jax.experimental.pallas import tpu as pltpu

F32 = jnp.float32
BF16 = jnp.bfloat16

RMS_EPS = 1e-6
MACARON_WEIGHT = 0.5
N_HEADS = 4
POOL_WINDOWS = (2, 4, 8, 16)
FNET_GROUPS = 4
FILTER_BANDS = 16
DECAY_TARGET = 1e-2
FAST_DECAY_PCT = 0.3
SLOW_DECAY_PCT = 1.5

V7X_SUBLANES = 8
V7X_BF16_ROWS = 16
V7X_VMEM_LIMIT = 60 * 2**20

HYENA_NB = 128
HALO = V7X_BF16_ROWS
ROW_CHUNK = 256


def _params(n_axes):
    return pltpu.CompilerParams(
        dimension_semantics=("arbitrary",) * n_axes,
        vmem_limit_bytes=V7X_VMEM_LIMIT,
    )


def _resident(shape):
    zeros = (0,) * len(shape)
    return pl.BlockSpec(shape, lambda *_: zeros, pipeline_mode=pl.Buffered(1))


def _resident_layer(stacked, layer):
    tail = (0,) * (stacked.ndim - 1)
    return pl.BlockSpec((None,) + stacked.shape[1:], lambda *_: (layer,) + tail,
                        pipeline_mode=pl.Buffered(1))


def _rms(x, g):
    ms = jnp.mean(x * x, axis=-1, keepdims=True)
    return x * lax.rsqrt(ms + RMS_EPS) * g


def _dot(a, b):
    return jnp.dot(a, b, preferred_element_type=F32)


def _dot3(a, b):
    ah = a.astype(BF16)
    al = (a - ah.astype(F32)).astype(BF16)
    bh = b.astype(BF16)
    bl = (b - bh.astype(F32)).astype(BF16)
    return _dot(ah, bh) + (_dot(al, bh) + _dot(ah, bl))


def _row_chunks(rows, fn):
    chunk = ROW_CHUNK if rows % ROW_CHUNK == 0 else rows

    def body(c, carry):
        fn(pl.ds(pl.multiple_of(c * chunk, chunk), chunk))
        return carry

    lax.fori_loop(0, rows // chunk, body, 0)


def _ffn_body(x_ref, gpre_ref, wg_ref, wu_ref, wd_ref, gpost_ref, o_ref, u_ref):
    k = pl.program_id(1)
    rows = x_ref.shape[0]

    @pl.when(k == 0)
    def _():
        def pre(r):
            u_ref[r, :] = _rms(x_ref[r, :], gpre_ref[...]).astype(BF16)
        _row_chunks(rows, pre)

    u = u_ref[...]
    a = _dot(u, wg_ref[...])
    b = _dot(u, wu_ref[...])
    h = (a * (1.0 / (1.0 + jnp.exp(-a))) * b).astype(BF16)

    @pl.when(k == 0)
    def _():
        o_ref[...] = _dot(h, wd_ref[...])

    @pl.when(k > 0)
    def _():
        o_ref[...] += _dot(h, wd_ref[...])

    @pl.when(k == pl.num_programs(1) - 1)
    def _():
        def post(r):
            o_ref[r, :] = x_ref[r, :] + MACARON_WEIGHT * _rms(o_ref[r, :], gpost_ref[...])
        _row_chunks(rows, post)


def _ffn(x2, g_pre, w_gu, w_down, g_post, layer, *, tm, tf):
    n, d = x2.shape
    f = w_down.shape[1]
    nk = f // tf
    return pl.pallas_call(
        _ffn_body,
        grid=(n // tm, nk),
        in_specs=[
            pl.BlockSpec((tm, d), lambda i, k: (i, 0)),
            pl.BlockSpec((1, d), lambda i, k: (0, 0)),
            pl.BlockSpec((None, d, tf), lambda i, k: (layer, 0, k)),
            pl.BlockSpec((None, d, tf), lambda i, k: (layer, 0, k + nk)),
            pl.BlockSpec((None, tf, d), lambda i, k: (layer, k, 0)),
            pl.BlockSpec((1, d), lambda i, k: (0, 0)),
        ],
        out_specs=pl.BlockSpec((tm, d), lambda i, k: (i, 0)),
        out_shape=jax.ShapeDtypeStruct((n, d), F32),
        scratch_shapes=[pltpu.VMEM((tm, d), BF16)],
        compiler_params=_params(2),
        name="ffn",
    )(x2, g_pre.reshape(1, d), w_gu, w_gu, w_down, g_post.reshape(1, d))


def _proj_body(x_ref, xp_ref, xn_ref, g_ref, w_ref, cw_ref, cb_ref, cs_ref,
               hy_ref, pq_ref, po_ref, gt_ref, u_ref, h_ref, *, tm, seq, n_hy, fw):
    i = pl.program_id(0)
    j = pl.program_id(1)

    @pl.when(j == 0)
    def _():
        g = g_ref[...]
        has_prev = lax.rem(i * tm, seq) != 0
        has_next = lax.rem((i + 1) * tm, seq) != 0
        u_ref[pl.ds(0, HALO), :] = jnp.where(has_prev, _rms(xp_ref[...], g), 0.0).astype(BF16)
        chunk = ROW_CHUNK if tm % ROW_CHUNK == 0 else tm

        def pre(c, carry):
            r = pl.multiple_of(c * chunk, chunk)
            u_ref[pl.ds(pl.multiple_of(HALO + r, HALO), chunk), :] = (
                _rms(x_ref[pl.ds(r, chunk), :], g).astype(BF16))
            return carry

        lax.fori_loop(0, tm // chunk, pre, 0)
        u_ref[pl.ds(HALO + tm, HALO), :] = jnp.where(has_next, _rms(xn_ref[...], g), 0.0).astype(BF16)

    @pl.when(j < n_hy)
    def _():
        h = _dot(u_ref[...], w_ref[...])
        rows = h.shape[0]
        prev = pltpu.roll(h, 1, axis=0)[HALO:HALO + tm]
        nxt = pltpu.roll(h, rows - 1, axis=0)[HALO:HALO + tm]
        y = cb_ref[...] + prev * cw_ref[0:1, :]
        y = y + h[HALO:HALO + tm] * cw_ref[1:2, :]
        y = y + nxt * cw_ref[2:3, :]
        hy_ref[...] = y.astype(BF16)

    @pl.when(j == n_hy)
    def _():
        c = _dot(u_ref[pl.ds(HALO, tm), :], w_ref[...])
        po_ref[...] = c[:, fw:]
        fb = c[:, :fw].astype(BF16)
        gw = fw // FNET_GROUPS
        res = [_dot(fb[:, q * gw:(q + 1) * gw], cs_ref[...]) for q in range(FNET_GROUPS)]
        pq_ref[...] = jnp.concatenate([r[:, :gw] for r in res] + [r[:, gw:] for r in res], axis=1)

    @pl.when(j > n_hy)
    def _():
        c = _dot(u_ref[pl.ds(HALO, tm), :], w_ref[...])
        gt_ref[...] = (1.0 / (1.0 + jnp.exp(-c))).astype(BF16)


def _proj(x2, g, w_in, conv_w, conv_b, cs, layer, *, tm, tn, seq, hy_cols, fw, pw):
    n, d = x2.shape
    cols = w_in.shape[2]
    n_hy = hy_cols // tn
    assert hy_cols % tn == 0 and fw + pw == tn and (cols - hy_cols - tn) % tn == 0
    nj = cols // tn
    n_gate = cols - hy_cols - tn
    hb = tm // HALO
    last_hb = n // HALO - 1
    body = functools.partial(_proj_body, tm=tm, seq=seq, n_hy=n_hy, fw=fw)
    return pl.pallas_call(
        body,
        grid=(n // tm, nj),
        in_specs=[
            pl.BlockSpec((tm, d), lambda i, j: (i, 0)),
            pl.BlockSpec((HALO, d), lambda i, j: (jnp.maximum(i * hb - 1, 0), 0)),
            pl.BlockSpec((HALO, d), lambda i, j: (jnp.minimum((i + 1) * hb, last_hb), 0)),
            pl.BlockSpec((1, d), lambda i, j: (0, 0)),
            pl.BlockSpec((None, d, tn), lambda i, j: (layer, 0, j)),
            pl.BlockSpec((3, tn), lambda i, j: (0, jnp.minimum(j, n_hy - 1))),
            pl.BlockSpec((1, tn), lambda i, j: (0, jnp.minimum(j, n_hy - 1))),
            _resident(cs.shape),
        ],
        out_specs=[
            pl.BlockSpec((tm, tn), lambda i, j: (i, jnp.minimum(j, n_hy - 1))),
            pl.BlockSpec((tm, 2 * fw), lambda i, j: (i, 0)),
            pl.BlockSpec((tm, pw), lambda i, j: (i, 0)),
            pl.BlockSpec((tm, tn), lambda i, j: (i, jnp.maximum(j - n_hy - 1, 0))),
        ],
        out_shape=[
            jax.ShapeDtypeStruct((n, hy_cols), BF16),
            jax.ShapeDtypeStruct((n, 2 * fw), F32),
            jax.ShapeDtypeStruct((n, pw), F32),
            jax.ShapeDtypeStruct((n, n_gate), BF16),
        ],
        scratch_shapes=[pltpu.VMEM((tm + 2 * HALO, d), BF16), pltpu.VMEM((tm + 2 * HALO, tn), F32)],
        compiler_params=_params(2),
        name="mixer_proj",
    )(x2, x2, x2, g.reshape(1, d), w_in, conv_w, conv_b.reshape(1, hy_cols), cs)


def _hyena_slot_block(s, na):
    half = na // 2
    return s if s <= half else (half + 1) + (s - half)


def _hyena_mats(seq):
    nb = HYENA_NB
    na = 2 * seq // nb
    half = na // 2
    eye = np.eye(V7X_SUBLANES)
    a = np.arange(half)[None, :]
    f1 = np.zeros((na, half))
    k_re = np.arange(half + 1)[:, None]
    f1[: half + 1] = np.cos(2 * np.pi * k_re * a / na)
    k_im = np.arange(1, half)[:, None]
    f1[half + 1:] = -np.sin(2 * np.pi * k_im * a / na)
    lhs1 = np.kron(f1, eye)
    b = np.arange(nb)[None, :]
    kb = np.arange(nb)[:, None]
    g = np.zeros((half + 1, 2 * nb, 2 * nb))
    for ka in range(half + 1):
        ang = 2 * np.pi * (kb * b / nb + b * ka / (2 * seq))
        gre, gim = np.cos(ang), -np.sin(ang)
        g[ka] = np.block([[gre, -gim], [gim, gre]])
    a_col = np.arange(half)[:, None]
    f2 = np.zeros((half, na))
    wgt = np.full(half + 1, 2.0)
    wgt[0] = wgt[half] = 1.0
    f2[:, : half + 1] = wgt[None, :] * np.cos(2 * np.pi * a_col * np.arange(half + 1)[None, :] / na)
    f2[:, half + 1:] = -2.0 * np.sin(2 * np.pi * a_col * np.arange(1, half)[None, :] / na)
    lhs2 = np.kron(f2 / (2 * seq), eye)
    return (jnp.asarray(lhs1, BF16), jnp.asarray(g, BF16), jnp.asarray(lhs2, BF16))


def _fnet_mats(seq, group):
    na = int(round(math.sqrt(seq)))
    nb = seq // na
    assert na * nb == seq and nb % V7X_SUBLANES == 0
    eye = np.eye(V7X_SUBLANES)
    n2 = np.arange(nb)[None, :]
    k2 = np.arange(nb)[:, None]
    ang = 2 * np.pi * k2 * n2 / nb
    c, s = np.cos(ang), np.sin(ang)
    blk = np.block([[c, s], [-s, c]])
    a1 = np.zeros((2, V7X_SUBLANES, nb, 2, nb, V7X_SUBLANES))
    for po in range(2):
        for pi_ in range(2):
            sub = blk[po * nb:(po + 1) * nb, pi_ * nb:(pi_ + 1) * nb]
            for lo in range(V7X_SUBLANES):
                a1[po, lo, :, pi_, :, lo] = sub
    a1 = a1.reshape(2 * V7X_SUBLANES * nb, 2 * V7X_SUBLANES * nb)
    a2 = np.zeros((nb // V7X_SUBLANES, na, V7X_SUBLANES, 2, na, V7X_SUBLANES))
    k1 = np.arange(na)[:, None]
    n1 = np.arange(na)[None, :]
    for hi in range(nb // V7X_SUBLANES):
        for lo in range(V7X_SUBLANES):
            k2v = hi * V7X_SUBLANES + lo
            phi = 2 * np.pi * (n1 * k1 / na + n1 * k2v / seq)
            a2[hi, :, lo, 0, :, lo] = np.cos(phi)
            a2[hi, :, lo, 1, :, lo] = np.sin(phi)
    a2 = a2.reshape(nb // V7X_SUBLANES, na * V7X_SUBLANES, 2 * na * V7X_SUBLANES) / math.sqrt(seq)
    cc = np.arange(group)[:, None]
    mm = np.arange(group)[None, :]
    angc = 2 * np.pi * cc * mm / group
    cs = np.concatenate([np.cos(angc), -np.sin(angc)], axis=1) / math.sqrt(group)
    return jnp.asarray(a1, BF16), jnp.asarray(a2, BF16), jnp.asarray(cs, BF16), na, nb


def _loop(n, body, unroll):
    if unroll is None:
        for i in range(n):
            body(i, 0)
        return
    while n % unroll:
        unroll -= 1
    lax.fori_loop(0, n, body, 0, unroll=unroll)


def _gather_tiles(ref, starts):
    tiles = [ref[pl.ds(pl.multiple_of(s, V7X_SUBLANES), V7X_SUBLANES), :] for s in starts]
    return jnp.concatenate(tiles, axis=0)


def _hyena_stage1(z_ref, e_ref, lhs1_ref, seq):
    nb = HYENA_NB
    na = 2 * seq // nb
    half = na // 2

    def body(i, carry):
        off = i * V7X_SUBLANES
        xg = _gather_tiles(z_ref, [nb * a + off for a in range(half)]).astype(BF16)
        out = _dot(lhs1_ref[...], xg)
        for s in range(na):
            row = _hyena_slot_block(s, na) * nb + off
            e_ref[pl.ds(pl.multiple_of(row, V7X_SUBLANES), V7X_SUBLANES), :] = (
                out[s * V7X_SUBLANES:(s + 1) * V7X_SUBLANES])
        return carry

    _loop(nb // V7X_SUBLANES, body, None)


def _hyena_zero_unused(e_ref, seq):
    nb = HYENA_NB
    half = seq // nb
    zero = jnp.zeros((nb, e_ref.shape[1]), F32)
    e_ref[pl.ds((half + 1) * nb, nb), :] = zero
    e_ref[pl.ds((2 * half + 1) * nb, nb), :] = zero


def _hyena_load_ka(e_ref, ka, seq):
    nb = HYENA_NB
    half = seq // nb
    re = e_ref[pl.ds(pl.multiple_of(ka * nb, nb), nb), :]
    im = e_ref[pl.ds(pl.multiple_of((half + 1 + ka) * nb, nb), nb), :]
    return jnp.concatenate([re, im], axis=0)


def _filter_mlp_body(bands_ref, w1t_ref, w1c_ref, w1s_ref, b1_ref, w2_ref, b2_ref, w3_ref, b3_ref,
                     fr_ref, o_ref, *, seq):
    idx = lax.broadcasted_iota(jnp.int32, (1, 2 * seq), 1)
    p = jnp.where(idx < seq, idx, 2 * seq - idx).astype(F32)
    t = p / (seq - 1.0)
    ang = bands_ref[...] * ((2.0 * math.pi / seq) * p)
    fr = fr_ref[...]
    pre = (w1t_ref[...] * t
           + _dot3(w1c_ref[...], jnp.cos(ang))
           + _dot3(w1s_ref[...], -jnp.sin(ang))
           + b1_ref[...])
    h = jnp.sin(fr * pre)
    h = jnp.sin(fr * (_dot3(w2_ref[...], h) + b2_ref[...]))
    h = jnp.sin(fr * (_dot3(w3_ref[...], h) + b3_ref[...]))
    o_ref[...] = jnp.concatenate([h[:, :seq], h[:, seq:]], axis=0).T


def _filter_mlp(fw1, fb1, fw2, fb2, fw3, fb3, freq, *, seq):
    hid = fw2.shape[0]
    bands = np.linspace(1e-4, FILTER_BANDS - 1, FILTER_BANDS).astype(np.float32).reshape(-1, 1)
    col = lambda v: v.reshape(hid, 1)
    args = (jnp.asarray(bands), fw1[0].reshape(hid, 1), fw1[1:1 + FILTER_BANDS].T,
            fw1[1 + FILTER_BANDS:].T, col(fb1), fw2.T, col(fb2), fw3.T, col(fb3), col(freq))
    return pl.pallas_call(
        functools.partial(_filter_mlp_body, seq=seq),
        grid=(1,),
        in_specs=[_resident(a.shape) for a in args],
        out_specs=pl.BlockSpec((seq, 2 * hid), lambda i: (0, 0)),
        out_shape=jax.ShapeDtypeStruct((seq, 2 * hid), F32),
        compiler_params=_params(1),
        name="filter_mlp",
    )(*args)


def _filter_spec_body(h3_ref, w4f_ref, w4b_ref, dl_ref, d_ref, lhs1_ref, g_ref,
                      o_ref, z_ref, e_ref, *, seq, cw):
    nb = HYENA_NB
    half = seq // nb
    dl = dl_ref[...]
    zero = jnp.zeros_like(w4f_ref[...])
    w4 = jnp.concatenate([jnp.concatenate([w4f_ref[...], zero], axis=1),
                          jnp.concatenate([zero, w4b_ref[...]], axis=1)], axis=0)
    lane = lax.broadcasted_iota(jnp.int32, (1, 2 * cw), 1)
    chunk = ROW_CHUNK if seq % ROW_CHUNK == 0 else seq
    for c in range(seq // chunk):
        row = c * chunk + lax.broadcasted_iota(jnp.int32, (chunk, 1), 0)
        rowf = row.astype(F32)
        h4 = _dot3(h3_ref[pl.ds(c * chunk, chunk), :], w4)
        dec = jnp.concatenate([jnp.exp(-(rowf / (seq - 1.0)) * dl),
                               jnp.exp(-((seq - rowf) / (seq - 1.0)) * dl)], axis=1)
        z_ref[pl.ds(c * chunk, chunk), :] = jnp.where((row == 0) & (lane >= cw), 0.0, h4 * dec)

    _hyena_zero_unused(e_ref, seq)
    _hyena_stage1(z_ref, e_ref, lhs1_ref, seq)
    dsk = d_ref[0]

    def body(ka, carry):
        ein = _hyena_load_ka(e_ref, ka, seq).astype(BF16)
        x = _dot(g_ref[ka], ein)
        sgn = 1.0 - 2.0 * (ka % 2)
        hsp = x[:, 0:cw] + sgn * x[:, cw:2 * cw]
        part = lax.broadcasted_iota(jnp.int32, (2 * nb, 1), 0)
        hsp = hsp + jnp.where(part < nb, dsk, 0.0)
        o_ref[0, pl.ds(pl.multiple_of(ka * 2 * nb, 2 * nb), 2 * nb), :] = hsp.astype(BF16)
        return carry

    _loop(half + 1, body, None)


def _filter_spectrum(h3, fw4, d_skip, lhs1, gmat, *, seq, width, cw):
    hid = h3.shape[1] // 2
    orders = d_skip.shape[0]
    nct = width // cw
    nb = HYENA_NB
    half = seq // nb
    max_decay = math.log(DECAY_TARGET) / FAST_DECAY_PCT
    min_decay = math.log(DECAY_TARGET) / SLOW_DECAY_PCT
    deltas = np.abs(np.linspace(min_decay, max_decay, width)).astype(np.float32).reshape(1, width)
    rows = (half + 1) * 2 * nb
    body = functools.partial(_filter_spec_body, seq=seq, cw=cw)
    return pl.pallas_call(
        body,
        grid=(orders, nct),
        in_specs=[
            _resident(h3.shape),
            pl.BlockSpec((hid, cw), lambda o, c: (0, o * 2 * nct + c)),
            pl.BlockSpec((hid, cw), lambda o, c: (0, o * 2 * nct + nct + c)),
            pl.BlockSpec((1, cw), lambda o, c: (0, c)),
            pl.BlockSpec((1, 1, cw), lambda o, c: (o, 0, c)),
            _resident(lhs1.shape),
            _resident(gmat.shape),
        ],
        out_specs=pl.BlockSpec((1, rows, cw), lambda o, c: (o, 0, c)),
        out_shape=jax.ShapeDtypeStruct((orders, rows, width), BF16),
        scratch_shapes=[pltpu.VMEM((seq, 2 * cw), F32),
                        pltpu.VMEM((2 * (half + 1) * nb, 2 * cw), F32)],
        compiler_params=_params(2),
        name="filter_spectrum",
    )(h3, fw4, fw4, jnp.asarray(deltas), d_skip.reshape(orders, 1, width), lhs1, gmat)


def _hyena_body(v_ref, g1_ref, g2_ref, h_ref, lhs1_ref, g_ref, lhs2_ref, o_ref, z_ref, e_ref, *, seq):
    nb = HYENA_NB
    na = 2 * seq // nb
    half = na // 2
    pair = V7X_BF16_ROWS
    chunk = 512 if seq % 512 == 0 else seq

    def load_v(c, carry):
        r = pl.multiple_of(c * chunk, chunk)
        z_ref[pl.ds(r, chunk), :] = v_ref[pl.ds(r, chunk), :].astype(F32)
        return carry

    lax.fori_loop(0, seq // chunk, load_v, 0)
    _hyena_zero_unused(e_ref, seq)

    for order, gate_ref in enumerate((g1_ref, g2_ref)):
        _hyena_stage1(z_ref, e_ref, lhs1_ref, seq)

        group = 3 if (half + 1) % 3 == 0 else 1

        def spectral(it, carry):
            kas = [it * group + u for u in range(group)]
            eins = [_hyena_load_ka(e_ref, ka, seq).astype(BF16) for ka in kas]
            ys = []
            for ka, ein in zip(kas, eins):
                gk = g_ref[ka]
                x = _dot(gk, ein)
                hsp = h_ref[order, pl.ds(pl.multiple_of(ka * 2 * nb, 2 * nb), 2 * nb), :].astype(F32)
                xre, xim, hre, him = x[:nb], x[nb:], hsp[:nb], hsp[nb:]
                zsp = jnp.concatenate([xre * hre - xim * him, xre * him + xim * hre], axis=0).astype(BF16)
                ys.append(lax.dot_general(gk, zsp, (((0,), (0,)), ((), ())), preferred_element_type=F32))
            for ka, y in zip(kas, ys):
                e_ref[pl.ds(pl.multiple_of(ka * nb, nb), nb), :] = y[:nb]
                keep = jnp.where((ka == 0) | (ka == half), 0.0, 1.0)
                e_ref[pl.ds(pl.multiple_of((half + 1 + ka) * nb, nb), nb), :] = y[nb:] * keep
            return carry

        _loop((half + 1) // group, spectral, None)

        def inverse(i, carry):
            outs = []
            for q in range(pair // V7X_SUBLANES):
                off = i * pair + q * V7X_SUBLANES
                yg = _gather_tiles(e_ref, [_hyena_slot_block(s, na) * nb + off for s in range(na)])
                outs.append(_dot(lhs2_ref[...], yg.astype(BF16)))
            for a in range(half):
                sl = slice(a * V7X_SUBLANES, (a + 1) * V7X_SUBLANES)
                conv = jnp.concatenate([o_[sl] for o_ in outs], axis=0)
                r = pl.multiple_of(a * nb + i * pair, pair)
                res = gate_ref[pl.ds(r, pair), :].astype(F32) * conv
                if order == 0:
                    z_ref[pl.ds(r, pair), :] = res
                else:
                    o_ref[pl.ds(r, pair), :] = res.astype(BF16)
            return carry

        _loop(nb // pair, inverse, None)


def _hyena(hyc, hspec, lhs1, gmat, lhs2, *, seq, width, ct):
    n = hyc.shape[0]
    nct = width // ct
    nb = HYENA_NB
    half = seq // nb
    return pl.pallas_call(
        functools.partial(_hyena_body, seq=seq),
        grid=(nct, n // seq),
        in_specs=[
            pl.BlockSpec((seq, ct), lambda c, b: (b, c)),
            pl.BlockSpec((seq, ct), lambda c, b: (b, nct + c)),
            pl.BlockSpec((seq, ct), lambda c, b: (b, 2 * nct + c)),
            pl.BlockSpec((hspec.shape[0], hspec.shape[1], ct), lambda c, b: (0, 0, c),
                         pipeline_mode=pl.Buffered(1)),
            _resident(lhs1.shape),
            _resident(gmat.shape),
            _resident(lhs2.shape),
        ],
        out_specs=pl.BlockSpec((seq, ct), lambda c, b: (b, c)),
        out_shape=jax.ShapeDtypeStruct((n, width), BF16),
        scratch_shapes=[pltpu.VMEM((seq, ct), F32),
                        pltpu.VMEM((2 * (half + 1) * nb, ct), F32)],
        compiler_params=_params(2),
        name="hyena",
    )(hyc, hyc, hyc, hspec, lhs1, gmat, lhs2)


def _fnet_body(p_ref, q_ref, a1_ref, a2_ref, o_ref, t_ref, *, seq, na, nb):
    sub = V7X_SUBLANES

    def stage1(i, carry):
        off = i * sub
        starts = [na * n2 + off for n2 in range(nb)]
        xin = jnp.concatenate([_gather_tiles(p_ref, starts), _gather_tiles(q_ref, starts)],
                              axis=0).astype(BF16)
        out = _dot(a1_ref[...], xin)
        for part in range(2):
            for lo in range(sub):
                src = (part * sub + lo) * nb
                dst = part * seq + (off + lo) * nb
                t_ref[pl.ds(pl.multiple_of(dst, sub), nb), :] = out[src:src + nb]
        return carry

    _loop(na // sub, stage1, None)

    def stage2(i, carry):
        off = i * sub
        starts = [part * seq + n1 * nb + off for part in range(2) for n1 in range(na)]
        tin = _gather_tiles(t_ref, starts).astype(BF16)
        out = _dot(a2_ref[i], tin)
        for k1 in range(na):
            o_ref[pl.ds(pl.multiple_of(k1 * nb + off, sub), sub), :] = out[k1 * sub:(k1 + 1) * sub]
        return carry

    _loop(nb // sub, stage2, None)


def _fnet(pq, a1, a2, *, seq, fw, ct, na, nb):
    n = pq.shape[0]
    nct = fw // ct
    return pl.pallas_call(
        functools.partial(_fnet_body, seq=seq, na=na, nb=nb),
        grid=(n // seq, nct),
        in_specs=[
            pl.BlockSpec((seq, ct), lambda b, c: (b, c)),
            pl.BlockSpec((seq, ct), lambda b, c: (b, nct + c)),
            _resident(a1.shape),
            _resident(a2.shape),
        ],
        out_specs=pl.BlockSpec((seq, ct), lambda b, c: (b, c)),
        out_shape=jax.ShapeDtypeStruct((n, fw), F32),
        scratch_shapes=[pltpu.VMEM((2 * seq, ct), F32)],
        compiler_params=_params(2),
        name="fnet_seq",
    )(pq, pq, a1, a2)


def _merge_body(x_ref, z_ref, f_ref, po_ref, pp_ref, pn_ref, gt_ref, wha_ref, wfo_ref, wpl_ref,
                ps_ref, wo_ref, gpost_ref, o_ref, ext_ref, m_ref, *, tm, seq, d):
    i = pl.program_id(0)
    sub = V7X_SUBLANES
    has_prev = lax.rem(i * tm, seq) != 0
    has_next = lax.rem((i + 1) * tm, seq) != 0
    ext_ref[pl.ds(0, sub), :] = jnp.where(has_prev, pp_ref[...], 0.0)
    ext_ref[pl.ds(sub, tm), :] = po_ref[...]
    ext_ref[pl.ds(sub + tm, sub), :] = jnp.where(has_next, pn_ref[...], 0.0)

    gate = lambda q, lo, hi: gt_ref[:, q * d + lo:q * d + hi].astype(F32)
    m_ref[...] = gate(0, 0, d) * _dot(z_ref[...], wha_ref[...])
    m_ref[...] += gate(1, 0, d) * _dot(f_ref[...].astype(BF16), wfo_ref[...])

    pos = lax.rem(i * tm, seq) + lax.broadcasted_iota(jnp.int32, (tm, 1), 0)
    n_pool = len(POOL_WINDOWS)
    gw = po_ref.shape[1] // n_pool
    ow = d // n_pool
    for q, w in enumerate(POOL_WINDOWS):
        before = w // 2
        after = w - 1 - before
        cols = slice(q * gw, (q + 1) * gw)
        tot = ext_ref[pl.ds(sub - before, tm), cols]
        for s in range(-before + 1, after + 1):
            tot = tot + ext_ref[pl.ds(sub + s, tm), cols]
        lo = jnp.maximum(pos - before, 0)
        hi = jnp.minimum(pos + after, seq - 1)
        cnt = (hi - lo + 1).astype(F32)
        mq = (tot / cnt - po_ref[:, cols]).astype(BF16)
        yc = _dot(mq, wpl_ref[q]) * ps_ref[:, q * ow:(q + 1) * ow]
        m_ref[:, q * ow:(q + 1) * ow] += gate(2, q * ow, (q + 1) * ow) * yc

    y = _dot(m_ref[...].astype(BF16), wo_ref[...])
    o_ref[...] = x_ref[...] + _rms(y, gpost_ref[...])


def _merge(x2, z2, fy, po, gates, wha, wfo, wpool, pscale, w_out, g_post, layer, *, tm, seq):
    n, d = x2.shape
    pw = po.shape[1]
    sub = V7X_SUBLANES
    hb = tm // sub
    last_hb = n // sub - 1
    body = functools.partial(_merge_body, tm=tm, seq=seq, d=d)
    return pl.pallas_call(
        body,
        grid=(n // tm,),
        in_specs=[
            pl.BlockSpec((tm, d), lambda i: (i, 0)),
            pl.BlockSpec((tm, z2.shape[1]), lambda i: (i, 0)),
            pl.BlockSpec((tm, fy.shape[1]), lambda i: (i, 0)),
            pl.BlockSpec((tm, pw), lambda i: (i, 0)),
            pl.BlockSpec((sub, pw), lambda i: (jnp.maximum(i * hb - 1, 0), 0)),
            pl.BlockSpec((sub, pw), lambda i: (jnp.minimum((i + 1) * hb, last_hb), 0)),
            pl.BlockSpec((tm, gates.shape[1]), lambda i: (i, 0)),
            _resident_layer(wha, layer),
            _resident_layer(wfo, layer),
            _resident_layer(wpool, layer),
            _resident((1, d)),
            _resident_layer(w_out, layer),
            _resident((1, d)),
        ],
        out_specs=pl.BlockSpec((tm, d), lambda i: (i, 0)),
        out_shape=jax.ShapeDtypeStruct((n, d), F32),
        scratch_shapes=[pltpu.VMEM((tm + 2 * sub, pw), F32), pltpu.VMEM((tm, d), F32)],
        compiler_params=_params(1),
        name="mixer_merge",
    )(x2, z2, fy, po, po, po, gates, wha, wfo, wpool, pscale.reshape(1, d), w_out, g_post.reshape(1, d))


def _kv_body(m_ref, g_ref, w_ref, o_ref):
    o_ref[...] = _dot(_rms(m_ref[...], g_ref[...]).astype(BF16), w_ref[...]).astype(BF16)


def _kv(mem2, g, w_kv, layer, *, n_mem, tn):
    n, d = mem2.shape
    cols = w_kv.shape[2]
    return pl.pallas_call(
        _kv_body,
        grid=(cols // tn, n // n_mem),
        in_specs=[
            pl.BlockSpec((n_mem, d), lambda j, b: (b, 0)),
            pl.BlockSpec((1, d), lambda j, b: (0, 0)),
            pl.BlockSpec((None, d, tn), lambda j, b: (layer, 0, j)),
        ],
        out_specs=pl.BlockSpec((n_mem, tn), lambda j, b: (b, j)),
        out_shape=jax.ShapeDtypeStruct((n, cols), BF16),
        compiler_params=_params(2),
        name="mem_kv",
    )(mem2, g.reshape(1, d), w_kv)


def _attn_body(x_ref, gpre_ref, wq_ref, k_ref, v_ref, wo_ref, gpost_ref, o_ref, *, d):
    dh = d // N_HEADS
    x = x_ref[...]
    q = _dot(_rms(x, gpre_ref[...]).astype(BF16), wq_ref[...])
    heads = []
    for h in range(N_HEADS):
        cols = slice(h * dh, (h + 1) * dh)
        s = lax.dot_general(q[:, cols].astype(BF16), k_ref[:, cols], (((1,), (1,)), ((), ())),
                            preferred_element_type=F32) * (dh ** -0.5)
        e = jnp.exp(s - jnp.max(s, axis=-1, keepdims=True))
        p = e / jnp.sum(e, axis=-1, keepdims=True)
        heads.append(_dot(p.astype(BF16), v_ref[:, cols]))
    o = jnp.concatenate(heads, axis=1).astype(BF16)
    o_ref[...] = x + _rms(_dot(o, wo_ref[...]), gpost_ref[...])


def _attn(x2, g_pre, w_q, kv, w_o, g_post, layer, *, tm, seq, n_mem):
    n, d = x2.shape
    tiles_per_seq = seq // tm
    return pl.pallas_call(
        functools.partial(_attn_body, d=d),
        grid=(n // tm,),
        in_specs=[
            pl.BlockSpec((tm, d), lambda i: (i, 0)),
            _resident((1, d)),
            _resident_layer(w_q, layer),
            pl.BlockSpec((n_mem, d), lambda i: (i // tiles_per_seq, 0)),
            pl.BlockSpec((n_mem, d), lambda i: (i // tiles_per_seq, 1)),
            _resident_layer(w_o, layer),
            _resident((1, d)),
        ],
        out_specs=pl.BlockSpec((tm, d), lambda i: (i, 0)),
        out_shape=jax.ShapeDtypeStruct((n, d), F32),
        compiler_params=_params(1),
        name="mem_attn",
    )(x2, g_pre.reshape(1, d), w_q, kv, kv, w_o, g_post.reshape(1, d))


def kernel(x, mem, g_ffn1_pre, w_ffn1_gu, w_ffn1_down, g_ffn1_post, g_mix_pre, w_in, hyena_conv_w,
           hyena_conv_b, filt_w1, filt_b1, filt_w2, filt_b2, filt_w3, filt_b3, filt_w4, filt_freq,
           hyena_d, w_hyena_out, w_fnet_out, w_pool, pool_scale, w_out, g_mix_post, g_mem_pre,
           g_mem_kv, w_q, w_kv, w_o, g_mem_post, g_ffn2_pre, w_ffn2_gu, w_ffn2_down, g_ffn2_post):
    batch, seq, d = x.shape
    n_mem = mem.shape[1]
    depth = w_in.shape[0]
    hy_w = w_hyena_out.shape[1]
    hy_cols = hyena_conv_w.shape[2]
    fw = w_fnet_out.shape[1]
    pw = w_pool.shape[1] * w_pool.shape[2]

    tm = 512
    tn = fw + pw
    tf = 512
    ct = 256

    lhs1, gmat, lhs2 = _hyena_mats(seq)
    a1, a2, cs, fna, fnb = _fnet_mats(seq, fw // FNET_GROUPS)

    bf = lambda w: w.astype(BF16)
    w_ffn1_gu, w_ffn1_down, w_ffn2_gu, w_ffn2_down = map(bf, (w_ffn1_gu, w_ffn1_down, w_ffn2_gu, w_ffn2_down))
    w_in, w_hyena_out, w_fnet_out, w_pool, w_out = map(bf, (w_in, w_hyena_out, w_fnet_out, w_pool, w_out))
    w_q, w_kv, w_o = map(bf, (w_q, w_kv, w_o))
    x2 = x.reshape(batch * seq, d)
    mem2 = mem.reshape(batch * n_mem, d)
    for l in range(depth):
        x2 = _ffn(x2, g_ffn1_pre[l], w_ffn1_gu, w_ffn1_down, g_ffn1_post[l], l, tm=2 * tm, tf=tf)

        hyc, pq, po, gates = _proj(x2, g_mix_pre[l], w_in, hyena_conv_w[l], hyena_conv_b[l], cs, l,
                                   tm=2 * tm, tn=tn, seq=seq, hy_cols=hy_cols, fw=fw, pw=pw)
        h3 = _filter_mlp(filt_w1[l], filt_b1[l], filt_w2[l], filt_b2[l], filt_w3[l], filt_b3[l],
                         filt_freq[l], seq=seq)
        hspec = _filter_spectrum(h3, filt_w4[l], hyena_d[l], lhs1, gmat, seq=seq, width=hy_w, cw=ct // 2)
        z2 = _hyena(hyc, hspec, lhs1, gmat, lhs2, seq=seq, width=hy_w, ct=ct)
        fy = _fnet(pq, a1, a2, seq=seq, fw=fw, ct=ct, na=fna, nb=fnb)
        x2 = _merge(x2, z2, fy, po, gates, w_hyena_out, w_fnet_out, w_pool,
                    pool_scale[l], w_out, g_mix_post[l], l, tm=tm, seq=seq)

        kv = _kv(mem2, g_mem_kv[l], w_kv, l, n_mem=n_mem, tn=tn)
        x2 = _attn(x2, g_mem_pre[l], w_q, kv, w_o, g_mem_post[l], l, tm=tm, seq=seq, n_mem=n_mem)

        x2 = _ffn(x2, g_ffn2_pre[l], w_ffn2_gu, w_ffn2_down, g_ffn2_post[l], l, tm=2 * tm, tf=tf)
    return x2.reshape(batch, seq, d)
```

```python
import functools
import math

import jax
import jax.numpy as jnp
import numpy as np
from jax import lax
from jax.experimental import pallas as pl
from jax.experimental.pallas import tpu as pltpu

F32 = jnp.float32
BF16 = jnp.bfloat16

RMS_EPS = 1e-6
MACARON_WEIGHT = 0.5
N_HEADS = 4
POOL_WINDOWS = (2, 4, 8, 16)
FNET_GROUPS = 4
FILTER_BANDS = 16
DECAY_TARGET = 1e-2
FAST_DECAY_PCT = 0.3
SLOW_DECAY_PCT = 1.5

V7X_SUBLANES = 8
V7X_BF16_ROWS = 16
V7X_VMEM_LIMIT = 60 * 2**20

HYENA_NB = 128
HALO = V7X_BF16_ROWS
ROW_CHUNK = 256
LANE_BLOCK = 128


def _params(n_axes):
    return pltpu.CompilerParams(
        dimension_semantics=("arbitrary",) * n_axes,
        vmem_limit_bytes=V7X_VMEM_LIMIT,
    )


def _resident(shape):
    zeros = (0,) * len(shape)
    return pl.BlockSpec(shape, lambda *_: zeros, pipeline_mode=pl.Buffered(1))


def _resident_layer(stacked, layer):
    tail = (0,) * (stacked.ndim - 1)
    return pl.BlockSpec((None,) + stacked.shape[1:], lambda *_: (layer,) + tail,
                        pipeline_mode=pl.Buffered(1))


def _rms(x, g):
    ms = jnp.mean(x * x, axis=-1, keepdims=True)
    return x * lax.rsqrt(ms + RMS_EPS) * g


def _sigmoid(x):
    return 0.5 * jnp.tanh(0.5 * x) + 0.5


def _dot(a, b):
    return jnp.dot(a, b, preferred_element_type=F32)


def _dot3(a, b):
    ah = a.astype(BF16)
    al = (a - ah.astype(F32)).astype(BF16)
    bh = b.astype(BF16)
    bl = (b - bh.astype(F32)).astype(BF16)
    return _dot(ah, bh) + (_dot(al, bh) + _dot(ah, bl))


def _row_chunks(rows, fn):
    chunk = ROW_CHUNK if rows % ROW_CHUNK == 0 else rows

    def body(c, carry):
        fn(pl.ds(pl.multiple_of(c * chunk, chunk), chunk))
        return carry

    lax.fori_loop(0, rows // chunk, body, 0)


def _norm_residual_inplace(o_ref, x_ref, g):
    rows, d = o_ref.shape

    def post(r):
        sq = None
        for c in range(0, d, LANE_BLOCK):
            yb = o_ref[r, c:c + LANE_BLOCK]
            sq = yb * yb if sq is None else sq + yb * yb
        inv = lax.rsqrt(jnp.sum(sq, axis=-1, keepdims=True) * (1.0 / d) + RMS_EPS)
        for c in range(0, d, LANE_BLOCK):
            cols = slice(c, c + LANE_BLOCK)
            o_ref[r, cols] = x_ref[r, cols] + o_ref[r, cols] * inv * g[:, cols]

    _row_chunks(rows, post)


def _ffn_body(x_ref, gpre_ref, wg_ref, wu_ref, wd_ref, gpost_ref, o_ref, u_ref):
    k = pl.program_id(1)
    rows = x_ref.shape[0]

    @pl.when(k == 0)
    def _():
        def pre(r):
            u_ref[r, :] = _rms(x_ref[r, :], gpre_ref[...]).astype(BF16)
        _row_chunks(rows, pre)

    u = u_ref[...]
    a = _dot(u, wg_ref[...])
    b = _dot(u, wu_ref[...])
    h = (a * _sigmoid(a) * b).astype(BF16)

    @pl.when(k == 0)
    def _():
        o_ref[...] = _dot(h, wd_ref[...])

    @pl.when(k > 0)
    def _():
        o_ref[...] += _dot(h, wd_ref[...])

    @pl.when(k == pl.num_programs(1) - 1)
    def _():
        _norm_residual_inplace(o_ref, x_ref, MACARON_WEIGHT * gpost_ref[...])


def _ffn(x2, g_pre, w_gu, w_down, g_post, layer, *, tm, tf):
    n, d = x2.shape
    f = w_down.shape[1]
    nk = f // tf
    return pl.pallas_call(
        _ffn_body,
        grid=(n // tm, nk),
        in_specs=[
            pl.BlockSpec((tm, d), lambda i, k: (i, 0)),
            pl.BlockSpec((1, d), lambda i, k: (0, 0)),
            pl.BlockSpec((None, d, tf), lambda i, k: (layer, 0, k)),
            pl.BlockSpec((None, d, tf), lambda i, k: (layer, 0, k + nk)),
            pl.BlockSpec((None, tf, d), lambda i, k: (layer, k, 0)),
            pl.BlockSpec((1, d), lambda i, k: (0, 0)),
        ],
        out_specs=pl.BlockSpec((tm, d), lambda i, k: (i, 0)),
        out_shape=jax.ShapeDtypeStruct((n, d), F32),
        scratch_shapes=[pltpu.VMEM((tm, d), BF16)],
        compiler_params=_params(2),
        name="ffn",
    )(x2, g_pre.reshape(1, d), w_gu, w_gu, w_down, g_post.reshape(1, d))


def _proj_body(x_ref, xp_ref, xn_ref, g_ref, w_ref, cw_ref, cb_ref, cs_ref,
               hy_ref, pq_ref, po_ref, gt_ref, u_ref, h_ref, *, tm, seq, n_hy, fw):
    i = pl.program_id(0)
    j = pl.program_id(1)

    @pl.when(j == 0)
    def _():
        g = g_ref[...]
        has_prev = lax.rem(i * tm, seq) != 0
        has_next = lax.rem((i + 1) * tm, seq) != 0
        u_ref[pl.ds(0, HALO), :] = jnp.where(has_prev, _rms(xp_ref[...], g), 0.0).astype(BF16)
        chunk = ROW_CHUNK if tm % ROW_CHUNK == 0 else tm

        def pre(c, carry):
            r = pl.multiple_of(c * chunk, chunk)
            u_ref[pl.ds(pl.multiple_of(HALO + r, HALO), chunk), :] = (
                _rms(x_ref[pl.ds(r, chunk), :], g).astype(BF16))
            return carry

        lax.fori_loop(0, tm // chunk, pre, 0)
        u_ref[pl.ds(HALO + tm, HALO), :] = jnp.where(has_next, _rms(xn_ref[...], g), 0.0).astype(BF16)

    @pl.when(j < n_hy)
    def _():
        h = _dot(u_ref[...], w_ref[...])
        rows = h.shape[0]
        prev = pltpu.roll(h, 1, axis=0)[HALO:HALO + tm]
        nxt = pltpu.roll(h, rows - 1, axis=0)[HALO:HALO + tm]
        y = cb_ref[...] + prev * cw_ref[0:1, :]
        y = y + h[HALO:HALO + tm] * cw_ref[1:2, :]
        y = y + nxt * cw_ref[2:3, :]
        hy_ref[...] = y.astype(BF16)

    @pl.when(j == n_hy)
    def _():
        c = _dot(u_ref[pl.ds(HALO, tm), :], w_ref[...])
        po_ref[...] = c[:, fw:]
        fb = c[:, :fw].astype(BF16)
        gw = fw // FNET_GROUPS
        res = [_dot(fb[:, q * gw:(q + 1) * gw], cs_ref[...]) for q in range(FNET_GROUPS)]
        pq_ref[...] = jnp.concatenate([r[:, :gw] for r in res] + [r[:, gw:] for r in res], axis=1)

    @pl.when(j > n_hy)
    def _():
        c = _dot(u_ref[pl.ds(HALO, tm), :], w_ref[...])
        gt_ref[...] = _sigmoid(c).astype(BF16)


def _proj(x2, g, w_in, conv_w, conv_b, cs, layer, *, tm, tn, seq, hy_cols, fw, pw):
    n, d = x2.shape
    cols = w_in.shape[2]
    n_hy = hy_cols // tn
    assert hy_cols % tn == 0 and fw + pw == tn and (cols - hy_cols - tn) % tn == 0
    nj = cols // tn
    n_gate = cols - hy_cols - tn
    hb = tm // HALO
    last_hb = n // HALO - 1
    body = functools.partial(_proj_body, tm=tm, seq=seq, n_hy=n_hy, fw=fw)
    return pl.pallas_call(
        body,
        grid=(n // tm, nj),
        in_specs=[
            pl.BlockSpec((tm, d), lambda i, j: (i, 0)),
            pl.BlockSpec((HALO, d), lambda i, j: (jnp.maximum(i * hb - 1, 0), 0)),
            pl.BlockSpec((HALO, d), lambda i, j: (jnp.minimum((i + 1) * hb, last_hb), 0)),
            pl.BlockSpec((1, d), lambda i, j: (0, 0)),
            pl.BlockSpec((None, d, tn), lambda i, j: (layer, 0, j)),
            pl.BlockSpec((3, tn), lambda i, j: (0, jnp.minimum(j, n_hy - 1))),
            pl.BlockSpec((1, tn), lambda i, j: (0, jnp.minimum(j, n_hy - 1))),
            _resident(cs.shape),
        ],
        out_specs=[
            pl.BlockSpec((tm, tn), lambda i, j: (i, jnp.minimum(j, n_hy - 1))),
            pl.BlockSpec((tm, 2 * fw), lambda i, j: (i, 0)),
            pl.BlockSpec((tm, pw), lambda i, j: (i, 0)),
            pl.BlockSpec((tm, tn), lambda i, j: (i, jnp.maximum(j - n_hy - 1, 0))),
        ],
        out_shape=[
            jax.ShapeDtypeStruct((n, hy_cols), BF16),
            jax.ShapeDtypeStruct((n, 2 * fw), F32),
            jax.ShapeDtypeStruct((n, pw), F32),
            jax.ShapeDtypeStruct((n, n_gate), BF16),
        ],
        scratch_shapes=[pltpu.VMEM((tm + 2 * HALO, d), BF16), pltpu.VMEM((tm + 2 * HALO, tn), F32)],
        compiler_params=_params(2),
        name="mixer_proj",
    )(x2, x2, x2, g.reshape(1, d), w_in, conv_w, conv_b.reshape(1, hy_cols), cs)


def _hyena_slot_block(s, na):
    half = na // 2
    return s if s <= half else (half + 1) + (s - half)


def _hyena_mats(seq):
    nb = HYENA_NB
    na = 2 * seq // nb
    half = na // 2
    eye = np.eye(V7X_SUBLANES)
    a = np.arange(half)[None, :]
    f1 = np.zeros((na, half))
    k_re = np.arange(half + 1)[:, None]
    f1[: half + 1] = np.cos(2 * np.pi * k_re * a / na)
    k_im = np.arange(1, half)[:, None]
    f1[half + 1:] = -np.sin(2 * np.pi * k_im * a / na)
    lhs1 = np.kron(f1, eye)
    b = np.arange(nb)[None, :]
    kb = np.arange(nb)[:, None]
    g = np.zeros((half + 1, 2 * nb, 2 * nb))
    for ka in range(half + 1):
        ang = 2 * np.pi * (kb * b / nb + b * ka / (2 * seq))
        gre, gim = np.cos(ang), -np.sin(ang)
        g[ka] = np.block([[gre, -gim], [gim, gre]])
    a_col = np.arange(half)[:, None]
    f2 = np.zeros((half, na))
    wgt = np.full(half + 1, 2.0)
    wgt[0] = wgt[half] = 1.0
    f2[:, : half + 1] = wgt[None, :] * np.cos(2 * np.pi * a_col * np.arange(half + 1)[None, :] / na)
    f2[:, half + 1:] = -2.0 * np.sin(2 * np.pi * a_col * np.arange(1, half)[None, :] / na)
    lhs2 = np.kron(f2 / (2 * seq), eye)
    return (jnp.asarray(lhs1, BF16), jnp.asarray(g, BF16), jnp.asarray(lhs2, BF16))


def _fnet_mats(seq, group):
    na = int(round(math.sqrt(seq)))
    nb = seq // na
    assert na * nb == seq and nb % V7X_SUBLANES == 0
    eye = np.eye(V7X_SUBLANES)
    n2 = np.arange(nb)[None, :]
    k2 = np.arange(nb)[:, None]
    ang = 2 * np.pi * k2 * n2 / nb
    c, s = np.cos(ang), np.sin(ang)
    blk = np.block([[c, s], [-s, c]])
    a1 = np.zeros((2, V7X_SUBLANES, nb, 2, nb, V7X_SUBLANES))
    for po in range(2):
        for pi_ in range(2):
            sub = blk[po * nb:(po + 1) * nb, pi_ * nb:(pi_ + 1) * nb]
            for lo in range(V7X_SUBLANES):
                a1[po, lo, :, pi_, :, lo] = sub
    a1 = a1.reshape(2 * V7X_SUBLANES * nb, 2 * V7X_SUBLANES * nb)
    a2 = np.zeros((nb // V7X_SUBLANES, na, V7X_SUBLANES, 2, na, V7X_SUBLANES))
    k1 = np.arange(na)[:, None]
    n1 = np.arange(na)[None, :]
    for hi in range(nb // V7X_SUBLANES):
        for lo in range(V7X_SUBLANES):
            k2v = hi * V7X_SUBLANES + lo
            phi = 2 * np.pi * (n1 * k1 / na + n1 * k2v / seq)
            a2[hi, :, lo, 0, :, lo] = np.cos(phi)
            a2[hi, :, lo, 1, :, lo] = np.sin(phi)
    a2 = a2.reshape(nb // V7X_SUBLANES, na * V7X_SUBLANES, 2 * na * V7X_SUBLANES) / math.sqrt(seq)
    cc = np.arange(group)[:, None]
    mm = np.arange(group)[None, :]
    angc = 2 * np.pi * cc * mm / group
    cs = np.concatenate([np.cos(angc), -np.sin(angc)], axis=1) / math.sqrt(group)
    return jnp.asarray(a1, BF16), jnp.asarray(a2, BF16), jnp.asarray(cs, BF16), na, nb


def _loop(n, body, unroll):
    if unroll is None:
        for i in range(n):
            body(i, 0)
        return
    while n % unroll:
        unroll -= 1
    lax.fori_loop(0, n, body, 0, unroll=unroll)


def _gather_tiles(ref, starts):
    tiles = [ref[pl.ds(pl.multiple_of(s, V7X_SUBLANES), V7X_SUBLANES), :] for s in starts]
    return jnp.concatenate(tiles, axis=0)


def _hyena_stage1(z_ref, e_ref, lhs1_ref, seq):
    nb = HYENA_NB
    na = 2 * seq // nb
    half = na // 2

    def body(i, carry):
        off = i * V7X_SUBLANES
        xg = _gather_tiles(z_ref, [nb * a + off for a in range(half)]).astype(BF16)
        out = _dot(lhs1_ref[...], xg)
        for s in range(na):
            row = _hyena_slot_block(s, na) * nb + off
            e_ref[pl.ds(pl.multiple_of(row, V7X_SUBLANES), V7X_SUBLANES), :] = (
                out[s * V7X_SUBLANES:(s + 1) * V7X_SUBLANES])
        return carry

    _loop(nb // V7X_SUBLANES, body, None)


def _hyena_zero_unused(e_ref, seq):
    nb = HYENA_NB
    half = seq // nb
    zero = jnp.zeros((nb, e_ref.shape[1]), F32)
    e_ref[pl.ds((half + 1) * nb, nb), :] = zero
    e_ref[pl.ds((2 * half + 1) * nb, nb), :] = zero


def _hyena_load_ka(e_ref, ka, seq):
    nb = HYENA_NB
    half = seq // nb
    re = e_ref[pl.ds(pl.multiple_of(ka * nb, nb), nb), :]
    im = e_ref[pl.ds(pl.multiple_of((half + 1 + ka) * nb, nb), nb), :]
    return jnp.concatenate([re, im], axis=0)


def _filter_mlp_body(bands_ref, w1t_ref, w1c_ref, w1s_ref, b1_ref, w2_ref, b2_ref, w3_ref, b3_ref,
                     fr_ref, o_ref, *, seq):
    idx = lax.broadcasted_iota(jnp.int32, (1, 2 * seq), 1)
    p = jnp.where(idx < seq, idx, 2 * seq - idx).astype(F32)
    t = p / (seq - 1.0)
    ang = bands_ref[...] * ((2.0 * math.pi / seq) * p)
    fr = fr_ref[...]
    pre = (w1t_ref[...] * t
           + _dot3(w1c_ref[...], jnp.cos(ang))
           + _dot3(w1s_ref[...], -jnp.sin(ang))
           + b1_ref[...])
    h = jnp.sin(fr * pre)
    h = jnp.sin(fr * (_dot3(w2_ref[...], h) + b2_ref[...]))
    h = jnp.sin(fr * (_dot3(w3_ref[...], h) + b3_ref[...]))
    o_ref[...] = jnp.concatenate([h[:, :seq], h[:, seq:]], axis=0).T


def _filter_mlp(fw1, fb1, fw2, fb2, fw3, fb3, freq, *, seq):
    hid = fw2.shape[0]
    bands = np.linspace(1e-4, FILTER_BANDS - 1, FILTER_BANDS).astype(np.float32).reshape(-1, 1)
    col = lambda v: v.reshape(hid, 1)
    args = (jnp.asarray(bands), fw1[0].reshape(hid, 1), fw1[1:1 + FILTER_BANDS].T,
            fw1[1 + FILTER_BANDS:].T, col(fb1), fw2.T, col(fb2), fw3.T, col(fb3), col(freq))
    return pl.pallas_call(
        functools.partial(_filter_mlp_body, seq=seq),
        grid=(1,),
        in_specs=[_resident(a.shape) for a in args],
        out_specs=pl.BlockSpec((seq, 2 * hid), lambda i: (0, 0)),
        out_shape=jax.ShapeDtypeStruct((seq, 2 * hid), F32),
        compiler_params=_params(1),
        name="filter_mlp",
    )(*args)


def _filter_spec_body(h3_ref, w4f_ref, w4b_ref, dl_ref, d_ref, lhs1_ref, g_ref,
                      o_ref, z_ref, e_ref, *, seq, cw):
    nb = HYENA_NB
    half = seq // nb
    dl = dl_ref[...]
    zero = jnp.zeros_like(w4f_ref[...])
    w4 = jnp.concatenate([jnp.concatenate([w4f_ref[...], zero], axis=1),
                          jnp.concatenate([zero, w4b_ref[...]], axis=1)], axis=0)
    lane = lax.broadcasted_iota(jnp.int32, (1, 2 * cw), 1)
    chunk = ROW_CHUNK if seq % ROW_CHUNK == 0 else seq
    for c in range(seq // chunk):
        row = c * chunk + lax.broadcasted_iota(jnp.int32, (chunk, 1), 0)
        rowf = row.astype(F32)
        h4 = _dot3(h3_ref[pl.ds(c * chunk, chunk), :], w4)
        dec = jnp.concatenate([jnp.exp(-(rowf / (seq - 1.0)) * dl),
                               jnp.exp(-((seq - rowf) / (seq - 1.0)) * dl)], axis=1)
        z_ref[pl.ds(c * chunk, chunk), :] = jnp.where((row == 0) & (lane >= cw), 0.0, h4 * dec)

    _hyena_zero_unused(e_ref, seq)
    _hyena_stage1(z_ref, e_ref, lhs1_ref, seq)
    dsk = d_ref[0]

    def body(ka, carry):
        ein = _hyena_load_ka(e_ref, ka, seq).astype(BF16)
        x = _dot(g_ref[ka], ein)
        sgn = 1.0 - 2.0 * (ka % 2)
        hsp = x[:, 0:cw] + sgn * x[:, cw:2 * cw]
        part = lax.broadcasted_iota(jnp.int32, (2 * nb, 1), 0)
        hsp = hsp + jnp.where(part < nb, dsk, 0.0)
        o_ref[0, pl.ds(pl.multiple_of(ka * 2 * nb, 2 * nb), 2 * nb), :] = hsp.astype(BF16)
        return carry

    _loop(half + 1, body, None)


def _filter_spectrum(h3, fw4, d_skip, lhs1, gmat, *, seq, width, cw):
    hid = h3.shape[1] // 2
    orders = d_skip.shape[0]
    nct = width // cw
    nb = HYENA_NB
    half = seq // nb
    max_decay = math.log(DECAY_TARGET) / FAST_DECAY_PCT
    min_decay = math.log(DECAY_TARGET) / SLOW_DECAY_PCT
    deltas = np.abs(np.linspace(min_decay, max_decay, width)).astype(np.float32).reshape(1, width)
    rows = (half + 1) * 2 * nb
    body = functools.partial(_filter_spec_body, seq=seq, cw=cw)
    return pl.pallas_call(
        body,
        grid=(orders, nct),
        in_specs=[
            _resident(h3.shape),
            pl.BlockSpec((hid, cw), lambda o, c: (0, o * 2 * nct + c)),
            pl.BlockSpec((hid, cw), lambda o, c: (0, o * 2 * nct + nct + c)),
            pl.BlockSpec((1, cw), lambda o, c: (0, c)),
            pl.BlockSpec((1, 1, cw), lambda o, c: (o, 0, c)),
            _resident(lhs1.shape),
            _resident(gmat.shape),
        ],
        out_specs=pl.BlockSpec((1, rows, cw), lambda o, c: (o, 0, c)),
        out_shape=jax.ShapeDtypeStruct((orders, rows, width), BF16),
        scratch_shapes=[pltpu.VMEM((seq, 2 * cw), F32),
                        pltpu.VMEM((2 * (half + 1) * nb, 2 * cw), F32)],
        compiler_params=_params(2),
        name="filter_spectrum",
    )(h3, fw4, fw4, jnp.asarray(deltas), d_skip.reshape(orders, 1, width), lhs1, gmat)


def _hyena_body(v_ref, g1_ref, g2_ref, h_ref, lhs1_ref, g_ref, lhs2_ref, o_ref, z_ref, e_ref, *, seq):
    nb = HYENA_NB
    na = 2 * seq // nb
    half = na // 2
    pair = V7X_BF16_ROWS
    chunk = 512 if seq % 512 == 0 else seq

    def load_v(c, carry):
        r = pl.multiple_of(c * chunk, chunk)
        z_ref[pl.ds(r, chunk), :] = v_ref[pl.ds(r, chunk), :].astype(F32)
        return carry

    lax.fori_loop(0, seq // chunk, load_v, 0)
    _hyena_zero_unused(e_ref, seq)

    for order, gate_ref in enumerate((g1_ref, g2_ref)):
        _hyena_stage1(z_ref, e_ref, lhs1_ref, seq)

        group = 3 if (half + 1) % 3 == 0 else 1

        def spectral(it, carry):
            kas = [it * group + u for u in range(group)]
            eins = [_hyena_load_ka(e_ref, ka, seq).astype(BF16) for ka in kas]
            ys = []
            for ka, ein in zip(kas, eins):
                gk = g_ref[ka]
                x = _dot(gk, ein)
                hsp = h_ref[order, pl.ds(pl.multiple_of(ka * 2 * nb, 2 * nb), 2 * nb), :].astype(F32)
                xre, xim, hre, him = x[:nb], x[nb:], hsp[:nb], hsp[nb:]
                zsp = jnp.concatenate([xre * hre - xim * him, xre * him + xim * hre], axis=0).astype(BF16)
                ys.append(lax.dot_general(gk, zsp, (((0,), (0,)), ((), ())), preferred_element_type=F32))
            for ka, y in zip(kas, ys):
                e_ref[pl.ds(pl.multiple_of(ka * nb, nb), nb), :] = y[:nb]
                keep = jnp.where((ka == 0) | (ka == half), 0.0, 1.0)
                e_ref[pl.ds(pl.multiple_of((half + 1 + ka) * nb, nb), nb), :] = y[nb:] * keep
            return carry

        _loop((half + 1) // group, spectral, None)

        def inverse(i, carry):
            outs = []
            for q in range(pair // V7X_SUBLANES):
                off = i * pair + q * V7X_SUBLANES
                yg = _gather_tiles(e_ref, [_hyena_slot_block(s, na) * nb + off for s in range(na)])
                outs.append(_dot(lhs2_ref[...], yg.astype(BF16)))
            for a in range(half):
                sl = slice(a * V7X_SUBLANES, (a + 1) * V7X_SUBLANES)
                conv = jnp.concatenate([o_[sl] for o_ in outs], axis=0)
                r = pl.multiple_of(a * nb + i * pair, pair)
                res = gate_ref[pl.ds(r, pair), :].astype(F32) * conv
                if order == 0:
                    z_ref[pl.ds(r, pair), :] = res
                else:
                    o_ref[pl.ds(r, pair), :] = res.astype(BF16)
            return carry

        _loop(nb // pair, inverse, None)


def _hyena(hyc, hspec, lhs1, gmat, lhs2, *, seq, width, ct):
    n = hyc.shape[0]
    nct = width // ct
    nb = HYENA_NB
    half = seq // nb
    return pl.pallas_call(
        functools.partial(_hyena_body, seq=seq),
        grid=(nct, n // seq),
        in_specs=[
            pl.BlockSpec((seq, ct), lambda c, b: (b, c)),
            pl.BlockSpec((seq, ct), lambda c, b: (b, nct + c)),
            pl.BlockSpec((seq, ct), lambda c, b: (b, 2 * nct + c)),
            pl.BlockSpec((hspec.shape[0], hspec.shape[1], ct), lambda c, b: (0, 0, c),
                         pipeline_mode=pl.Buffered(1)),
            _resident(lhs1.shape),
            _resident(gmat.shape),
            _resident(lhs2.shape),
        ],
        out_specs=pl.BlockSpec((seq, ct), lambda c, b: (b, c)),
        out_shape=jax.ShapeDtypeStruct((n, width), BF16),
        scratch_shapes=[pltpu.VMEM((seq, ct), F32),
                        pltpu.VMEM((2 * (half + 1) * nb, ct), F32)],
        compiler_params=_params(2),
        name="hyena",
    )(hyc, hyc, hyc, hspec, lhs1, gmat, lhs2)


def _fnet_body(p_ref, q_ref, a1_ref, a2_ref, o_ref, t_ref, *, seq, na, nb):
    sub = V7X_SUBLANES

    def stage1(i, carry):
        off = i * sub
        starts = [na * n2 + off for n2 in range(nb)]
        xin = jnp.concatenate([_gather_tiles(p_ref, starts), _gather_tiles(q_ref, starts)],
                              axis=0).astype(BF16)
        out = _dot(a1_ref[...], xin)
        for part in range(2):
            for lo in range(sub):
                src = (part * sub + lo) * nb
                dst = part * seq + (off + lo) * nb
                t_ref[pl.ds(pl.multiple_of(dst, sub), nb), :] = out[src:src + nb]
        return carry

    _loop(na // sub, stage1, None)

    def stage2(i, carry):
        off = i * sub
        starts = [part * seq + n1 * nb + off for part in range(2) for n1 in range(na)]
        tin = _gather_tiles(t_ref, starts).astype(BF16)
        out = _dot(a2_ref[i], tin)
        for k1 in range(na):
            o_ref[pl.ds(pl.multiple_of(k1 * nb + off, sub), sub), :] = out[k1 * sub:(k1 + 1) * sub]
        return carry

    _loop(nb // sub, stage2, None)


def _fnet(pq, a1, a2, *, seq, fw, ct, na, nb):
    n = pq.shape[0]
    nct = fw // ct
    return pl.pallas_call(
        functools.partial(_fnet_body, seq=seq, na=na, nb=nb),
        grid=(n // seq, nct),
        in_specs=[
            pl.BlockSpec((seq, ct), lambda b, c: (b, c)),
            pl.BlockSpec((seq, ct), lambda b, c: (b, nct + c)),
            _resident(a1.shape),
            _resident(a2.shape),
        ],
        out_specs=pl.BlockSpec((seq, ct), lambda b, c: (b, c)),
        out_shape=jax.ShapeDtypeStruct((n, fw), F32),
        scratch_shapes=[pltpu.VMEM((2 * seq, ct), F32)],
        compiler_params=_params(2),
        name="fnet_seq",
    )(pq, pq, a1, a2)


def _merge_body(x_ref, z_ref, f_ref, po_ref, pp_ref, pn_ref, gt_ref, wha_ref, wfo_ref, wpl_ref,
                ps_ref, wo_ref, gpost_ref, o_ref, ext_ref, m_ref, *, tm, seq, d):
    i = pl.program_id(0)
    sub = V7X_SUBLANES
    has_prev = lax.rem(i * tm, seq) != 0
    has_next = lax.rem((i + 1) * tm, seq) != 0
    ext_ref[pl.ds(0, sub), :] = jnp.where(has_prev, pp_ref[...], 0.0)
    ext_ref[pl.ds(sub, tm), :] = po_ref[...]
    ext_ref[pl.ds(sub + tm, sub), :] = jnp.where(has_next, pn_ref[...], 0.0)

    gate = lambda q, cols: gt_ref[:, q * d + cols.start:q * d + cols.stop].astype(F32)
    fyb = f_ref[...].astype(BF16)
    pos = lax.rem(i * tm, seq) + lax.broadcasted_iota(jnp.int32, (tm, 1), 0)
    n_pool = len(POOL_WINDOWS)
    gw = po_ref.shape[1] // n_pool
    ow = d // n_pool
    for q, w in enumerate(POOL_WINDOWS):
        before = w // 2
        after = w - 1 - before
        cols = slice(q * gw, (q + 1) * gw)
        out = slice(q * ow, (q + 1) * ow)
        tot = ext_ref[pl.ds(sub - before, tm), cols]
        for s in range(-before + 1, after + 1):
            tot = tot + ext_ref[pl.ds(sub + s, tm), cols]
        lo = jnp.maximum(pos - before, 0)
        hi = jnp.minimum(pos + after, seq - 1)
        cnt = (hi - lo + 1).astype(F32)
        mq = (tot / cnt - po_ref[:, cols]).astype(BF16)
        yc = _dot(mq, wpl_ref[q]) * ps_ref[:, out]
        merged = gate(0, out) * _dot(z_ref[...], wha_ref[:, out])
        merged = merged + gate(1, out) * _dot(fyb, wfo_ref[:, out])
        merged = merged + gate(2, out) * yc
        m_ref[:, out] = merged.astype(BF16)

    y = _dot(m_ref[...], wo_ref[...])
    o_ref[...] = x_ref[...] + _rms(y, gpost_ref[...])


def _merge(x2, z2, fy, po, gates, wha, wfo, wpool, pscale, w_out, g_post, layer, *, tm, seq):
    n, d = x2.shape
    pw = po.shape[1]
    sub = V7X_SUBLANES
    hb = tm // sub
    last_hb = n // sub - 1
    body = functools.partial(_merge_body, tm=tm, seq=seq, d=d)
    return pl.pallas_call(
        body,
        grid=(n // tm,),
        in_specs=[
            pl.BlockSpec((tm, d), lambda i: (i, 0)),
            pl.BlockSpec((tm, z2.shape[1]), lambda i: (i, 0)),
            pl.BlockSpec((tm, fy.shape[1]), lambda i: (i, 0)),
            pl.BlockSpec((tm, pw), lambda i: (i, 0)),
            pl.BlockSpec((sub, pw), lambda i: (jnp.maximum(i * hb - 1, 0), 0)),
            pl.BlockSpec((sub, pw), lambda i: (jnp.minimum((i + 1) * hb, last_hb), 0)),
            pl.BlockSpec((tm, gates.shape[1]), lambda i: (i, 0)),
            _resident_layer(wha, layer),
            _resident_layer(wfo, layer),
            _resident_layer(wpool, layer),
            _resident((1, d)),
            _resident_layer(w_out, layer),
            _resident((1, d)),
        ],
        out_specs=pl.BlockSpec((tm, d), lambda i: (i, 0)),
        out_shape=jax.ShapeDtypeStruct((n, d), F32),
        scratch_shapes=[pltpu.VMEM((tm + 2 * sub, pw), F32), pltpu.VMEM((tm, d), BF16)],
        compiler_params=_params(1),
        name="mixer_merge",
    )(x2, z2, fy, po, po, po, gates, wha, wfo, wpool, pscale.reshape(1, d), w_out, g_post.reshape(1, d))


def _kv_body(m_ref, g_ref, w_ref, o_ref):
    o_ref[...] = _dot(_rms(m_ref[...], g_ref[...]).astype(BF16), w_ref[...]).astype(BF16)


def _kv(mem2, g, w_kv, layer, *, n_mem, tn):
    n, d = mem2.shape
    cols = w_kv.shape[2]
    return pl.pallas_call(
        _kv_body,
        grid=(cols // tn, n // n_mem),
        in_specs=[
            pl.BlockSpec((n_mem, d), lambda j, b: (b, 0)),
            pl.BlockSpec((1, d), lambda j, b: (0, 0)),
            pl.BlockSpec((None, d, tn), lambda j, b: (layer, 0, j)),
        ],
        out_specs=pl.BlockSpec((n_mem, tn), lambda j, b: (b, j)),
        out_shape=jax.ShapeDtypeStruct((n, cols), BF16),
        compiler_params=_params(2),
        name="mem_kv",
    )(mem2, g.reshape(1, d), w_kv)


def _attn_body(x_ref, gpre_ref, wq_ref, k_ref, v_ref, wo_ref, gpost_ref, o_ref, *, d):
    dh = d // N_HEADS
    x = x_ref[...]
    q = _dot(_rms(x, gpre_ref[...]).astype(BF16), wq_ref[...])
    heads = []
    for h in range(N_HEADS):
        cols = slice(h * dh, (h + 1) * dh)
        s = lax.dot_general(q[:, cols].astype(BF16), k_ref[:, cols], (((1,), (1,)), ((), ())),
                            preferred_element_type=F32) * (dh ** -0.5)
        e = jnp.exp(s - jnp.max(s, axis=-1, keepdims=True))
        p = e / jnp.sum(e, axis=-1, keepdims=True)
        heads.append(_dot(p.astype(BF16), v_ref[:, cols]))
    o = jnp.concatenate(heads, axis=1).astype(BF16)
    o_ref[...] = x + _rms(_dot(o, wo_ref[...]), gpost_ref[...])


def _attn(x2, g_pre, w_q, kv, w_o, g_post, layer, *, tm, seq, n_mem):
    n, d = x2.shape
    tiles_per_seq = seq // tm
    return pl.pallas_call(
        functools.partial(_attn_body, d=d),
        grid=(n // tm,),
        in_specs=[
            pl.BlockSpec((tm, d), lambda i: (i, 0)),
            _resident((1, d)),
            _resident_layer(w_q, layer),
            pl.BlockSpec((n_mem, d), lambda i: (i // tiles_per_seq, 0)),
            pl.BlockSpec((n_mem, d), lambda i: (i // tiles_per_seq, 1)),
            _resident_layer(w_o, layer),
            _resident((1, d)),
        ],
        out_specs=pl.BlockSpec((tm, d), lambda i: (i, 0)),
        out_shape=jax.ShapeDtypeStruct((n, d), F32),
        compiler_params=_params(1),
        name="mem_attn",
    )(x2, g_pre.reshape(1, d), w_q, kv, kv, w_o, g_post.reshape(1, d))


def kernel(x, mem, g_ffn1_pre, w_ffn1_gu, w_ffn1_down, g_ffn1_post, g_mix_pre, w_in, hyena_conv_w,
           hyena_conv_b, filt_w1, filt_b1, filt_w2, filt_b2, filt_w3, filt_b3, filt_w4, filt_freq,
           hyena_d, w_hyena_out, w_fnet_out, w_pool, pool_scale, w_out, g_mix_post, g_mem_pre,
           g_mem_kv, w_q, w_kv, w_o, g_mem_post, g_ffn2_pre, w_ffn2_gu, w_ffn2_down, g_ffn2_post):
    batch, seq, d = x.shape
    n_mem = mem.shape[1]
    depth = w_in.shape[0]
    hy_w = w_hyena_out.shape[1]
    hy_cols = hyena_conv_w.shape[2]
    fw = w_fnet_out.shape[1]
    pw = w_pool.shape[1] * w_pool.shape[2]

    tm = 512
    tn = fw + pw
    tf = 512
    ct = 256

    lhs1, gmat, lhs2 = _hyena_mats(seq)
    a1, a2, cs, fna, fnb = _fnet_mats(seq, fw // FNET_GROUPS)

    bf = lambda w: w.astype(BF16)
    w_ffn1_gu, w_ffn1_down, w_ffn2_gu, w_ffn2_down = map(bf, (w_ffn1_gu, w_ffn1_down, w_ffn2_gu, w_ffn2_down))
    w_in, w_hyena_out, w_fnet_out, w_pool, w_out = map(bf, (w_in, w_hyena_out, w_fnet_out, w_pool, w_out))
    w_q, w_kv, w_o = map(bf, (w_q, w_kv, w_o))
    x2 = x.reshape(batch * seq, d)
    mem2 = mem.reshape(batch * n_mem, d)
    for l in range(depth):
        x2 = _ffn(x2, g_ffn1_pre[l], w_ffn1_gu, w_ffn1_down, g_ffn1_post[l], l, tm=2 * tm, tf=tf)

        hyc, pq, po, gates = _proj(x2, g_mix_pre[l], w_in, hyena_conv_w[l], hyena_conv_b[l], cs, l,
                                   tm=2 * tm, tn=tn, seq=seq, hy_cols=hy_cols, fw=fw, pw=pw)
        h3 = _filter_mlp(filt_w1[l], filt_b1[l], filt_w2[l], filt_b2[l], filt_w3[l], filt_b3[l],
                         filt_freq[l], seq=seq)
        hspec = _filter_spectrum(h3, filt_w4[l], hyena_d[l], lhs1, gmat, seq=seq, width=hy_w, cw=ct // 2)
        z2 = _hyena(hyc, hspec, lhs1, gmat, lhs2, seq=seq, width=hy_w, ct=ct)
        fy = _fnet(pq, a1, a2, seq=seq, fw=fw, ct=ct, na=fna, nb=fnb)
        x2 = _merge(x2, z2, fy, po, gates, w_hyena_out, w_fnet_out, w_pool,
                    pool_scale[l], w_out, g_mix_post[l], l, tm=tm, seq=seq)

        kv = _kv(mem2, g_mem_kv[l], w_kv, l, n_mem=n_mem, tn=tn)
        x2 = _attn(x2, g_mem_pre[l], w_q, kv, w_o, g_mem_post[l], l, tm=tm, seq=seq, n_mem=n_mem)

        x2 = _ffn(x2, g_ffn2_pre[l], w_ffn2_gu, w_ffn2_down, g_ffn2_post[l], l, tm=2 * tm, tf=tf)
    return x2.reshape(batch, seq, d)
```

```python
import functools
import math

import jax
import jax.numpy as jnp
import numpy as np
from jax import lax
from jax.experimental import pallas as pl
from jax.experimental.pallas import tpu as pltpu

F32 = jnp.float32
BF16 = jnp.bfloat16

RMS_EPS = 1e-6
MACARON_WEIGHT = 0.5
N_HEADS = 4
POOL_WINDOWS = (2, 4, 8, 16)
FNET_GROUPS = 4
FILTER_BANDS = 16
DECAY_TARGET = 1e-2
FAST_DECAY_PCT = 0.3
SLOW_DECAY_PCT = 1.5

V7X_SUBLANES = 8
V7X_BF16_ROWS = 16
V7X_VMEM_LIMIT = 60 * 2**20

HYENA_NB = 128
HALO = V7X_BF16_ROWS
ROW_CHUNK = 256
LANE_BLOCK = 128


def _params(n_axes):
    return pltpu.CompilerParams(
        dimension_semantics=("arbitrary",) * n_axes,
        vmem_limit_bytes=V7X_VMEM_LIMIT,
    )


def _resident(shape):
    zeros = (0,) * len(shape)
    return pl.BlockSpec(shape, lambda *_: zeros, pipeline_mode=pl.Buffered(1))


def _resident_layer(stacked, layer):
    tail = (0,) * (stacked.ndim - 1)
    return pl.BlockSpec((None,) + stacked.shape[1:], lambda *_: (layer,) + tail,
                        pipeline_mode=pl.Buffered(1))


def _rms(x, g):
    ms = jnp.mean(x * x, axis=-1, keepdims=True)
    return x * lax.rsqrt(ms + RMS_EPS) * g


def _sigmoid(x):
    return 0.5 * jnp.tanh(0.5 * x) + 0.5


def _dot(a, b):
    return jnp.dot(a, b, preferred_element_type=F32)


def _dot3(a, b):
    ah = a.astype(BF16)
    al = (a - ah.astype(F32)).astype(BF16)
    bh = b.astype(BF16)
    bl = (b - bh.astype(F32)).astype(BF16)
    return _dot(ah, bh) + (_dot(al, bh) + _dot(ah, bl))


def _row_chunks(rows, fn):
    chunk = ROW_CHUNK if rows % ROW_CHUNK == 0 else rows

    def body(c, carry):
        fn(pl.ds(pl.multiple_of(c * chunk, chunk), chunk))
        return carry

    lax.fori_loop(0, rows // chunk, body, 0)


def _norm_residual_inplace(o_ref, x_ref, g):
    rows, d = o_ref.shape

    def post(r):
        sq = None
        for c in range(0, d, LANE_BLOCK):
            yb = o_ref[r, c:c + LANE_BLOCK]
            sq = yb * yb if sq is None else sq + yb * yb
        inv = lax.rsqrt(jnp.sum(sq, axis=-1, keepdims=True) * (1.0 / d) + RMS_EPS)
        for c in range(0, d, LANE_BLOCK):
            cols = slice(c, c + LANE_BLOCK)
            o_ref[r, cols] = x_ref[r, cols] + o_ref[r, cols] * inv * g[:, cols]

    _row_chunks(rows, post)


def _ffn_body(x_ref, gpre_ref, wg_ref, wu_ref, wd_ref, gpost_ref, o_ref, u_ref):
    k = pl.program_id(1)
    rows = x_ref.shape[0]

    @pl.when(k == 0)
    def _():
        def pre(r):
            u_ref[r, :] = _rms(x_ref[r, :], gpre_ref[...]).astype(BF16)
        _row_chunks(rows, pre)

    u = u_ref[...]
    a = _dot(u, wg_ref[...])
    b = _dot(u, wu_ref[...])
    h = (a * _sigmoid(a) * b).astype(BF16)

    @pl.when(k == 0)
    def _():
        o_ref[...] = _dot(h, wd_ref[...])

    @pl.when(k > 0)
    def _():
        o_ref[...] += _dot(h, wd_ref[...])

    @pl.when(k == pl.num_programs(1) - 1)
    def _():
        _norm_residual_inplace(o_ref, x_ref, MACARON_WEIGHT * gpost_ref[...])


def _ffn(x2, g_pre, w_gu, w_down, g_post, layer, *, tm, tf):
    n, d = x2.shape
    f = w_down.shape[1]
    nk = f // tf
    return pl.pallas_call(
        _ffn_body,
        grid=(n // tm, nk),
        in_specs=[
            pl.BlockSpec((tm, d), lambda i, k: (i, 0)),
            pl.BlockSpec((1, d), lambda i, k: (0, 0)),
            pl.BlockSpec((None, d, tf), lambda i, k: (layer, 0, k)),
            pl.BlockSpec((None, d, tf), lambda i, k: (layer, 0, k + nk)),
            pl.BlockSpec((None, tf, d), lambda i, k: (layer, k, 0)),
            pl.BlockSpec((1, d), lambda i, k: (0, 0)),
        ],
        out_specs=pl.BlockSpec((tm, d), lambda i, k: (i, 0)),
        out_shape=jax.ShapeDtypeStruct((n, d), F32),
        scratch_shapes=[pltpu.VMEM((tm, d), BF16)],
        compiler_params=_params(2),
        name="ffn",
    )(x2, g_pre.reshape(1, d), w_gu, w_gu, w_down, g_post.reshape(1, d))


def _proj_body(x_ref, xp_ref, xn_ref, g_ref, w_ref, cw_ref, cb_ref, cs_ref,
               hy_ref, pq_ref, po_ref, gt_ref, u_ref, *, tm, seq, n_hy, fw):
    i = pl.program_id(0)
    j = pl.program_id(1)

    @pl.when(j == 0)
    def _():
        g = g_ref[...]
        has_prev = lax.rem(i * tm, seq) != 0
        has_next = lax.rem((i + 1) * tm, seq) != 0
        u_ref[pl.ds(0, HALO), :] = jnp.where(has_prev, _rms(xp_ref[...], g), 0.0).astype(BF16)
        chunk = ROW_CHUNK if tm % ROW_CHUNK == 0 else tm

        def pre(c, carry):
            r = pl.multiple_of(c * chunk, chunk)
            u_ref[pl.ds(pl.multiple_of(HALO + r, HALO), chunk), :] = (
                _rms(x_ref[pl.ds(r, chunk), :], g).astype(BF16))
            return carry

        lax.fori_loop(0, tm // chunk, pre, 0)
        u_ref[pl.ds(HALO + tm, HALO), :] = jnp.where(has_next, _rms(xn_ref[...], g), 0.0).astype(BF16)

    @pl.when(j < n_hy)
    def _():
        h = _dot(u_ref[...], w_ref[...])
        rows = h.shape[0]
        prev = pltpu.roll(h, 1, axis=0)[HALO:HALO + tm]
        nxt = pltpu.roll(h, rows - 1, axis=0)[HALO:HALO + tm]
        y = cb_ref[...] + prev * cw_ref[0:1, :]
        y = y + h[HALO:HALO + tm] * cw_ref[1:2, :]
        y = y + nxt * cw_ref[2:3, :]
        hy_ref[...] = y.astype(BF16)

    @pl.when(j == n_hy)
    def _():
        c = _dot(u_ref[pl.ds(HALO, tm), :], w_ref[...])
        po_ref[...] = c[:, fw:]
        fb = c[:, :fw].astype(BF16)
        gw = fw // FNET_GROUPS
        res = [_dot(fb[:, q * gw:(q + 1) * gw], cs_ref[...]) for q in range(FNET_GROUPS)]
        pq_ref[...] = jnp.concatenate([r[:, :gw] for r in res] + [r[:, gw:] for r in res], axis=1)

    @pl.when(j > n_hy)
    def _():
        c = _dot(u_ref[pl.ds(HALO, tm), :], w_ref[...])
        gt_ref[...] = _sigmoid(c).astype(BF16)


def _proj(x2, g, w_in, conv_w, conv_b, cs, layer, *, tm, tn, seq, hy_cols, fw, pw):
    n, d = x2.shape
    cols = w_in.shape[2]
    n_hy = hy_cols // tn
    assert hy_cols % tn == 0 and fw + pw == tn and (cols - hy_cols - tn) % tn == 0
    nj = cols // tn
    n_gate = cols - hy_cols - tn
    hb = tm // HALO
    last_hb = n // HALO - 1
    body = functools.partial(_proj_body, tm=tm, seq=seq, n_hy=n_hy, fw=fw)
    return pl.pallas_call(
        body,
        grid=(n // tm, nj),
        in_specs=[
            pl.BlockSpec((tm, d), lambda i, j: (i, 0)),
            pl.BlockSpec((HALO, d), lambda i, j: (jnp.maximum(i * hb - 1, 0), 0)),
            pl.BlockSpec((HALO, d), lambda i, j: (jnp.minimum((i + 1) * hb, last_hb), 0)),
            pl.BlockSpec((1, d), lambda i, j: (0, 0)),
            pl.BlockSpec((None, d, tn), lambda i, j: (layer, 0, j)),
            pl.BlockSpec((3, tn), lambda i, j: (0, jnp.minimum(j, n_hy - 1))),
            pl.BlockSpec((1, tn), lambda i, j: (0, jnp.minimum(j, n_hy - 1))),
            _resident(cs.shape),
        ],
        out_specs=[
            pl.BlockSpec((tm, tn), lambda i, j: (i, jnp.minimum(j, n_hy - 1))),
            pl.BlockSpec((tm, 2 * fw), lambda i, j: (i, 0)),
            pl.BlockSpec((tm, pw), lambda i, j: (i, 0)),
            pl.BlockSpec((tm, tn), lambda i, j: (i, jnp.maximum(j - n_hy - 1, 0))),
        ],
        out_shape=[
            jax.ShapeDtypeStruct((n, hy_cols), BF16),
            jax.ShapeDtypeStruct((n, 2 * fw), F32),
            jax.ShapeDtypeStruct((n, pw), F32),
            jax.ShapeDtypeStruct((n, n_gate), BF16),
        ],
        scratch_shapes=[pltpu.VMEM((tm + 2 * HALO, d), BF16)],
        compiler_params=_params(2),
        name="mixer_proj",
    )(x2, x2, x2, g.reshape(1, d), w_in, conv_w, conv_b.reshape(1, hy_cols), cs)


def _hyena_slot_block(s, na):
    half = na // 2
    return s if s <= half else (half + 1) + (s - half)


def _hyena_mats(seq):
    nb = HYENA_NB
    na = 2 * seq // nb
    half = na // 2
    eye = np.eye(V7X_SUBLANES)
    a = np.arange(half)[None, :]
    f1 = np.zeros((na, half))
    k_re = np.arange(half + 1)[:, None]
    f1[: half + 1] = np.cos(2 * np.pi * k_re * a / na)
    k_im = np.arange(1, half)[:, None]
    f1[half + 1:] = -np.sin(2 * np.pi * k_im * a / na)
    lhs1 = np.kron(f1, eye)
    b = np.arange(nb)[None, :]
    kb = np.arange(nb)[:, None]
    g = np.zeros((half + 1, 2 * nb, 2 * nb))
    for ka in range(half + 1):
        ang = 2 * np.pi * (kb * b / nb + b * ka / (2 * seq))
        gre, gim = np.cos(ang), -np.sin(ang)
        g[ka] = np.block([[gre, -gim], [gim, gre]])
    a_col = np.arange(half)[:, None]
    f2 = np.zeros((half, na))
    wgt = np.full(half + 1, 2.0)
    wgt[0] = wgt[half] = 1.0
    f2[:, : half + 1] = wgt[None, :] * np.cos(2 * np.pi * a_col * np.arange(half + 1)[None, :] / na)
    f2[:, half + 1:] = -2.0 * np.sin(2 * np.pi * a_col * np.arange(1, half)[None, :] / na)
    lhs2 = np.kron(f2 / (2 * seq), eye)
    return (jnp.asarray(lhs1, BF16), jnp.asarray(g, BF16), jnp.asarray(lhs2, BF16))


def _fnet_mats(seq, group):
    na = int(round(math.sqrt(seq)))
    nb = seq // na
    assert na * nb == seq and nb % V7X_SUBLANES == 0
    eye = np.eye(V7X_SUBLANES)
    n2 = np.arange(nb)[None, :]
    k2 = np.arange(nb)[:, None]
    ang = 2 * np.pi * k2 * n2 / nb
    c, s = np.cos(ang), np.sin(ang)
    blk = np.block([[c, s], [-s, c]])
    a1 = np.zeros((2, V7X_SUBLANES, nb, 2, nb, V7X_SUBLANES))
    for po in range(2):
        for pi_ in range(2):
            sub = blk[po * nb:(po + 1) * nb, pi_ * nb:(pi_ + 1) * nb]
            for lo in range(V7X_SUBLANES):
                a1[po, lo, :, pi_, :, lo] = sub
    a1 = a1.reshape(2 * V7X_SUBLANES * nb, 2 * V7X_SUBLANES * nb)
    a2 = np.zeros((nb // V7X_SUBLANES, na, V7X_SUBLANES, 2, na, V7X_SUBLANES))
    k1 = np.arange(na)[:, None]
    n1 = np.arange(na)[None, :]
    for hi in range(nb // V7X_SUBLANES):
        for lo in range(V7X_SUBLANES):
            k2v = hi * V7X_SUBLANES + lo
            phi = 2 * np.pi * (n1 * k1 / na + n1 * k2v / seq)
            a2[hi, :, lo, 0, :, lo] = np.cos(phi)
            a2[hi, :, lo, 1, :, lo] = np.sin(phi)
    a2 = a2.reshape(nb // V7X_SUBLANES, na * V7X_SUBLANES, 2 * na * V7X_SUBLANES) / math.sqrt(seq)
    cc = np.arange(group)[:, None]
    mm = np.arange(group)[None, :]
    angc = 2 * np.pi * cc * mm / group
    cs = np.concatenate([np.cos(angc), -np.sin(angc)], axis=1) / math.sqrt(group)
    return jnp.asarray(a1, BF16), jnp.asarray(a2, BF16), jnp.asarray(cs, BF16), na, nb


def _loop(n, body, unroll):
    if unroll is None:
        for i in range(n):
            body(i, 0)
        return
    while n % unroll:
        unroll -= 1
    lax.fori_loop(0, n, body, 0, unroll=unroll)


def _gather_tiles(ref, starts):
    tiles = [ref[pl.ds(pl.multiple_of(s, V7X_SUBLANES), V7X_SUBLANES), :] for s in starts]
    return jnp.concatenate(tiles, axis=0)


def _hyena_stage1(z_ref, e_ref, lhs1_ref, seq):
    nb = HYENA_NB
    na = 2 * seq // nb
    half = na // 2

    def body(i, carry):
        off = i * V7X_SUBLANES
        xg = _gather_tiles(z_ref, [nb * a + off for a in range(half)]).astype(BF16)
        out = _dot(lhs1_ref[...], xg)
        for s in range(na):
            row = _hyena_slot_block(s, na) * nb + off
            e_ref[pl.ds(pl.multiple_of(row, V7X_SUBLANES), V7X_SUBLANES), :] = (
                out[s * V7X_SUBLANES:(s + 1) * V7X_SUBLANES])
        return carry

    _loop(nb // V7X_SUBLANES, body, None)


def _hyena_zero_unused(e_ref, seq):
    nb = HYENA_NB
    half = seq // nb
    zero = jnp.zeros((nb, e_ref.shape[1]), F32)
    e_ref[pl.ds((half + 1) * nb, nb), :] = zero
    e_ref[pl.ds((2 * half + 1) * nb, nb), :] = zero


def _hyena_load_ka(e_ref, ka, seq):
    nb = HYENA_NB
    half = seq // nb
    re = e_ref[pl.ds(pl.multiple_of(ka * nb, nb), nb), :]
    im = e_ref[pl.ds(pl.multiple_of((half + 1 + ka) * nb, nb), nb), :]
    return jnp.concatenate([re, im], axis=0)


def _filter_mlp_body(bands_ref, w1t_ref, w1c_ref, w1s_ref, b1_ref, w2_ref, b2_ref, w3_ref, b3_ref,
                     fr_ref, o_ref, *, seq):
    idx = lax.broadcasted_iota(jnp.int32, (1, 2 * seq), 1)
    p = jnp.where(idx < seq, idx, 2 * seq - idx).astype(F32)
    t = p / (seq - 1.0)
    ang = bands_ref[...] * ((2.0 * math.pi / seq) * p)
    fr = fr_ref[...]
    pre = (w1t_ref[...] * t
           + _dot3(w1c_ref[...], jnp.cos(ang))
           + _dot3(w1s_ref[...], -jnp.sin(ang))
           + b1_ref[...])
    h = jnp.sin(fr * pre)
    h = jnp.sin(fr * (_dot3(w2_ref[...], h) + b2_ref[...]))
    h = jnp.sin(fr * (_dot3(w3_ref[...], h) + b3_ref[...]))
    o_ref[...] = jnp.concatenate([h[:, :seq], h[:, seq:]], axis=0).T


def _filter_mlp(fw1, fb1, fw2, fb2, fw3, fb3, freq, *, seq):
    hid = fw2.shape[0]
    bands = np.linspace(1e-4, FILTER_BANDS - 1, FILTER_BANDS).astype(np.float32).reshape(-1, 1)
    col = lambda v: v.reshape(hid, 1)
    args = (jnp.asarray(bands), fw1[0].reshape(hid, 1), fw1[1:1 + FILTER_BANDS].T,
            fw1[1 + FILTER_BANDS:].T, col(fb1), fw2.T, col(fb2), fw3.T, col(fb3), col(freq))
    return pl.pallas_call(
        functools.partial(_filter_mlp_body, seq=seq),
        grid=(1,),
        in_specs=[_resident(a.shape) for a in args],
        out_specs=pl.BlockSpec((seq, 2 * hid), lambda i: (0, 0)),
        out_shape=jax.ShapeDtypeStruct((seq, 2 * hid), F32),
        compiler_params=_params(1),
        name="filter_mlp",
    )(*args)


def _filter_spec_body(h3_ref, w4f_ref, w4b_ref, dl_ref, d_ref, lhs1_ref, g_ref,
                      o_ref, z_ref, e_ref, *, seq, cw):
    nb = HYENA_NB
    half = seq // nb
    dl = dl_ref[...]
    zero = jnp.zeros_like(w4f_ref[...])
    w4 = jnp.concatenate([jnp.concatenate([w4f_ref[...], zero], axis=1),
                          jnp.concatenate([zero, w4b_ref[...]], axis=1)], axis=0)
    lane = lax.broadcasted_iota(jnp.int32, (1, 2 * cw), 1)
    chunk = ROW_CHUNK if seq % ROW_CHUNK == 0 else seq
    for c in range(seq // chunk):
        row = c * chunk + lax.broadcasted_iota(jnp.int32, (chunk, 1), 0)
        rowf = row.astype(F32)
        h4 = _dot3(h3_ref[pl.ds(c * chunk, chunk), :], w4)
        dec = jnp.concatenate([jnp.exp(-(rowf / (seq - 1.0)) * dl),
                               jnp.exp(-((seq - rowf) / (seq - 1.0)) * dl)], axis=1)
        z_ref[pl.ds(c * chunk, chunk), :] = jnp.where((row == 0) & (lane >= cw), 0.0, h4 * dec)

    _hyena_zero_unused(e_ref, seq)
    _hyena_stage1(z_ref, e_ref, lhs1_ref, seq)
    dsk = d_ref[0]

    def body(ka, carry):
        ein = _hyena_load_ka(e_ref, ka, seq).astype(BF16)
        x = _dot(g_ref[ka], ein)
        sgn = 1.0 - 2.0 * (ka % 2)
        hsp = x[:, 0:cw] + sgn * x[:, cw:2 * cw]
        part = lax.broadcasted_iota(jnp.int32, (2 * nb, 1), 0)
        hsp = hsp + jnp.where(part < nb, dsk, 0.0)
        o_ref[0, pl.ds(pl.multiple_of(ka * 2 * nb, 2 * nb), 2 * nb), :] = hsp.astype(BF16)
        return carry

    _loop(half + 1, body, None)


def _filter_spectrum(h3, fw4, d_skip, lhs1, gmat, *, seq, width, cw, ct):
    hid = h3.shape[1] // 2
    orders = d_skip.shape[0]
    nct = width // cw
    per = ct // cw
    nb = HYENA_NB
    half = seq // nb
    max_decay = math.log(DECAY_TARGET) / FAST_DECAY_PCT
    min_decay = math.log(DECAY_TARGET) / SLOW_DECAY_PCT
    deltas = np.abs(np.linspace(min_decay, max_decay, width)).astype(np.float32).reshape(1, width)
    rows = (half + 1) * 2 * nb
    body = functools.partial(_filter_spec_body, seq=seq, cw=cw)
    return pl.pallas_call(
        body,
        grid=(orders, nct),
        in_specs=[
            _resident(h3.shape),
            pl.BlockSpec((hid, cw), lambda o, c: (0, o * 2 * nct + c)),
            pl.BlockSpec((hid, cw), lambda o, c: (0, o * 2 * nct + nct + c)),
            pl.BlockSpec((1, cw), lambda o, c: (0, c)),
            pl.BlockSpec((1, 1, cw), lambda o, c: (o, 0, c)),
            _resident(lhs1.shape),
            _resident(gmat.shape),
        ],
        out_specs=pl.BlockSpec((None, 1, rows, cw), lambda o, c: (c // per, o, 0, c % per)),
        out_shape=jax.ShapeDtypeStruct((width // ct, orders, rows, ct), BF16),
        scratch_shapes=[pltpu.VMEM((seq, 2 * cw), F32),
                        pltpu.VMEM((2 * (half + 1) * nb, 2 * cw), F32)],
        compiler_params=_params(2),
        name="filter_spectrum",
    )(h3, fw4, fw4, jnp.asarray(deltas), d_skip.reshape(orders, 1, width), lhs1, gmat)


def _hyena_body(v_ref, g1_ref, g2_ref, h_ref, lhs1_ref, g_ref, lhs2_ref, o_ref, z_ref, e_ref, *, seq):
    nb = HYENA_NB
    na = 2 * seq // nb
    half = na // 2
    pair = V7X_BF16_ROWS
    chunk = 512 if seq % 512 == 0 else seq

    def load_v(c, carry):
        r = pl.multiple_of(c * chunk, chunk)
        z_ref[pl.ds(r, chunk), :] = v_ref[pl.ds(r, chunk), :].astype(F32)
        return carry

    lax.fori_loop(0, seq // chunk, load_v, 0)
    _hyena_zero_unused(e_ref, seq)

    for order, gate_ref in enumerate((g1_ref, g2_ref)):
        _hyena_stage1(z_ref, e_ref, lhs1_ref, seq)

        group = 3 if (half + 1) % 3 == 0 else 1

        def spectral(it, carry):
            kas = [it * group + u for u in range(group)]
            eins = [_hyena_load_ka(e_ref, ka, seq).astype(BF16) for ka in kas]
            ys = []
            for ka, ein in zip(kas, eins):
                gk = g_ref[ka]
                x = _dot(gk, ein)
                hsp = h_ref[order, pl.ds(pl.multiple_of(ka * 2 * nb, 2 * nb), 2 * nb), :].astype(F32)
                xre, xim, hre, him = x[:nb], x[nb:], hsp[:nb], hsp[nb:]
                zsp = jnp.concatenate([xre * hre - xim * him, xre * him + xim * hre], axis=0).astype(BF16)
                ys.append(lax.dot_general(gk, zsp, (((0,), (0,)), ((), ())), preferred_element_type=F32))
            for ka, y in zip(kas, ys):
                e_ref[pl.ds(pl.multiple_of(ka * nb, nb), nb), :] = y[:nb]
                keep = jnp.where((ka == 0) | (ka == half), 0.0, 1.0)
                e_ref[pl.ds(pl.multiple_of((half + 1 + ka) * nb, nb), nb), :] = y[nb:] * keep
            return carry

        _loop((half + 1) // group, spectral, None)

        def inverse(i, carry):
            outs = []
            for q in range(pair // V7X_SUBLANES):
                off = i * pair + q * V7X_SUBLANES
                yg = _gather_tiles(e_ref, [_hyena_slot_block(s, na) * nb + off for s in range(na)])
                outs.append(_dot(lhs2_ref[...], yg.astype(BF16)))
            for a in range(half):
                sl = slice(a * V7X_SUBLANES, (a + 1) * V7X_SUBLANES)
                conv = jnp.concatenate([o_[sl] for o_ in outs], axis=0)
                r = pl.multiple_of(a * nb + i * pair, pair)
                res = gate_ref[pl.ds(r, pair), :].astype(F32) * conv
                if order == 0:
                    z_ref[pl.ds(r, pair), :] = res
                else:
                    o_ref[pl.ds(r, pair), :] = res.astype(BF16)
            return carry

        _loop(nb // pair, inverse, None)


def _hyena(hyc, hspec, lhs1, gmat, lhs2, *, seq, width, ct):
    n = hyc.shape[0]
    nct = width // ct
    nb = HYENA_NB
    half = seq // nb
    return pl.pallas_call(
        functools.partial(_hyena_body, seq=seq),
        grid=(nct, n // seq),
        in_specs=[
            pl.BlockSpec((seq, ct), lambda c, b: (b, c)),
            pl.BlockSpec((seq, ct), lambda c, b: (b, nct + c)),
            pl.BlockSpec((seq, ct), lambda c, b: (b, 2 * nct + c)),
            pl.BlockSpec((None,) + hspec.shape[1:], lambda c, b: (c, 0, 0, 0)),
            _resident(lhs1.shape),
            _resident(gmat.shape),
            _resident(lhs2.shape),
        ],
        out_specs=pl.BlockSpec((seq, ct), lambda c, b: (b, c)),
        out_shape=jax.ShapeDtypeStruct((n, width), BF16),
        scratch_shapes=[pltpu.VMEM((seq, ct), F32),
                        pltpu.VMEM((2 * (half + 1) * nb, ct), F32)],
        compiler_params=_params(2),
        name="hyena",
    )(hyc, hyc, hyc, hspec, lhs1, gmat, lhs2)


def _fnet_body(p_ref, q_ref, a1_ref, a2_ref, o_ref, t_ref, *, seq, na, nb):
    sub = V7X_SUBLANES

    def stage1(i, carry):
        off = i * sub
        starts = [na * n2 + off for n2 in range(nb)]
        xin = jnp.concatenate([_gather_tiles(p_ref, starts), _gather_tiles(q_ref, starts)],
                              axis=0).astype(BF16)
        out = _dot(a1_ref[...], xin)
        for part in range(2):
            for lo in range(sub):
                src = (part * sub + lo) * nb
                dst = part * seq + (off + lo) * nb
                t_ref[pl.ds(pl.multiple_of(dst, sub), nb), :] = out[src:src + nb]
        return carry

    _loop(na // sub, stage1, None)

    def stage2(i, carry):
        off = i * sub
        starts = [part * seq + n1 * nb + off for part in range(2) for n1 in range(na)]
        tin = _gather_tiles(t_ref, starts).astype(BF16)
        out = _dot(a2_ref[i], tin)
        for k1 in range(na):
            o_ref[pl.ds(pl.multiple_of(k1 * nb + off, sub), sub), :] = out[k1 * sub:(k1 + 1) * sub]
        return carry

    _loop(nb // sub, stage2, None)


def _fnet(pq, a1, a2, *, seq, fw, ct, na, nb):
    n = pq.shape[0]
    nct = fw // ct
    return pl.pallas_call(
        functools.partial(_fnet_body, seq=seq, na=na, nb=nb),
        grid=(n // seq, nct),
        in_specs=[
            pl.BlockSpec((seq, ct), lambda b, c: (b, c)),
            pl.BlockSpec((seq, ct), lambda b, c: (b, nct + c)),
            _resident(a1.shape),
            _resident(a2.shape),
        ],
        out_specs=pl.BlockSpec((seq, ct), lambda b, c: (b, c)),
        out_shape=jax.ShapeDtypeStruct((n, fw), F32),
        scratch_shapes=[pltpu.VMEM((2 * seq, ct), F32)],
        compiler_params=_params(2),
        name="fnet_seq",
    )(pq, pq, a1, a2)


def _merge_body(x_ref, z_ref, f_ref, po_ref, pp_ref, pn_ref, gt_ref, wha_ref, wfo_ref, wpl_ref,
                ps_ref, wo_ref, gpost_ref, o_ref, ext_ref, m_ref, *, tm, seq, d):
    i = pl.program_id(0)
    sub = V7X_SUBLANES
    has_prev = lax.rem(i * tm, seq) != 0
    has_next = lax.rem((i + 1) * tm, seq) != 0
    ext_ref[pl.ds(0, sub), :] = jnp.where(has_prev, pp_ref[...], 0.0)
    ext_ref[pl.ds(sub, tm), :] = po_ref[...]
    ext_ref[pl.ds(sub + tm, sub), :] = jnp.where(has_next, pn_ref[...], 0.0)

    gate = lambda q, lo, hi: gt_ref[:, q * d + lo:q * d + hi].astype(F32)
    m_ref[...] = gate(0, 0, d) * _dot(z_ref[...], wha_ref[...])
    m_ref[...] += gate(1, 0, d) * _dot(f_ref[...].astype(BF16), wfo_ref[...])

    pos = lax.rem(i * tm, seq) + lax.broadcasted_iota(jnp.int32, (tm, 1), 0)
    n_pool = len(POOL_WINDOWS)
    gw = po_ref.shape[1] // n_pool
    ow = d // n_pool
    for q, w in enumerate(POOL_WINDOWS):
        before = w // 2
        after = w - 1 - before
        cols = slice(q * gw, (q + 1) * gw)
        tot = ext_ref[pl.ds(sub - before, tm), cols]
        for s in range(-before + 1, after + 1):
            tot = tot + ext_ref[pl.ds(sub + s, tm), cols]
        lo = jnp.maximum(pos - before, 0)
        hi = jnp.minimum(pos + after, seq - 1)
        cnt = (hi - lo + 1).astype(F32)
        mq = (tot / cnt - po_ref[:, cols]).astype(BF16)
        yc = _dot(mq, wpl_ref[q]) * ps_ref[:, q * ow:(q + 1) * ow]
        m_ref[:, q * ow:(q + 1) * ow] += gate(2, q * ow, (q + 1) * ow) * yc

    y = _dot(m_ref[...].astype(BF16), wo_ref[...])
    o_ref[...] = x_ref[...] + _rms(y, gpost_ref[...])


def _merge(x2, z2, fy, po, gates, wha, wfo, wpool, pscale, w_out, g_post, layer, *, tm, seq):
    n, d = x2.shape
    pw = po.shape[1]
    sub = V7X_SUBLANES
    hb = tm // sub
    last_hb = n // sub - 1
    body = functools.partial(_merge_body, tm=tm, seq=seq, d=d)
    return pl.pallas_call(
        body,
        grid=(n // tm,),
        in_specs=[
            pl.BlockSpec((tm, d), lambda i: (i, 0)),
            pl.BlockSpec((tm, z2.shape[1]), lambda i: (i, 0)),
            pl.BlockSpec((tm, fy.shape[1]), lambda i: (i, 0)),
            pl.BlockSpec((tm, pw), lambda i: (i, 0)),
            pl.BlockSpec((sub, pw), lambda i: (jnp.maximum(i * hb - 1, 0), 0)),
            pl.BlockSpec((sub, pw), lambda i: (jnp.minimum((i + 1) * hb, last_hb), 0)),
            pl.BlockSpec((tm, gates.shape[1]), lambda i: (i, 0)),
            _resident_layer(wha, layer),
            _resident_layer(wfo, layer),
            _resident_layer(wpool, layer),
            _resident((1, d)),
            _resident_layer(w_out, layer),
            _resident((1, d)),
        ],
        out_specs=pl.BlockSpec((tm, d), lambda i: (i, 0)),
        out_shape=jax.ShapeDtypeStruct((n, d), F32),
        scratch_shapes=[pltpu.VMEM((tm + 2 * sub, pw), F32), pltpu.VMEM((tm, d), F32)],
        compiler_params=_params(1),
        name="mixer_merge",
    )(x2, z2, fy, po, po, po, gates, wha, wfo, wpool, pscale.reshape(1, d), w_out, g_post.reshape(1, d))


def _kv_body(m_ref, g_ref, w_ref, o_ref):
    o_ref[...] = _dot(_rms(m_ref[...], g_ref[...]).astype(BF16), w_ref[...].astype(BF16)).astype(BF16)


def _kv(mem2, g, w_kv, layer, *, n_mem, tn):
    n, d = mem2.shape
    cols = w_kv.shape[2]
    return pl.pallas_call(
        _kv_body,
        grid=(cols // tn, n // n_mem),
        in_specs=[
            pl.BlockSpec((n_mem, d), lambda j, b: (b, 0)),
            pl.BlockSpec((1, d), lambda j, b: (0, 0)),
            pl.BlockSpec((None, d, tn), lambda j, b: (layer, 0, j)),
        ],
        out_specs=pl.BlockSpec((n_mem, tn), lambda j, b: (b, j)),
        out_shape=jax.ShapeDtypeStruct((n, cols), BF16),
        compiler_params=_params(2),
        name="mem_kv",
    )(mem2, g.reshape(1, d), w_kv)


def _attn_body(x_ref, gpre_ref, wq_ref, k_ref, v_ref, wo_ref, gpost_ref, o_ref, *, d):
    dh = d // N_HEADS
    x = x_ref[...]
    q = _dot(_rms(x, gpre_ref[...]).astype(BF16), wq_ref[...])
    heads = []
    for h in range(N_HEADS):
        cols = slice(h * dh, (h + 1) * dh)
        s = lax.dot_general(q[:, cols].astype(BF16), k_ref[:, cols], (((1,), (1,)), ((), ())),
                            preferred_element_type=F32) * (dh ** -0.5)
        e = jnp.exp(s - jnp.max(s, axis=-1, keepdims=True))
        p = e / jnp.sum(e, axis=-1, keepdims=True)
        heads.append(_dot(p.astype(BF16), v_ref[:, cols]))
    o = jnp.concatenate(heads, axis=1).astype(BF16)
    o_ref[...] = x + _rms(_dot(o, wo_ref[...]), gpost_ref[...])


def _attn(x2, g_pre, w_q, kv, w_o, g_post, layer, *, tm, seq, n_mem):
    n, d = x2.shape
    tiles_per_seq = seq // tm
    return pl.pallas_call(
        functools.partial(_attn_body, d=d),
        grid=(n // tm,),
        in_specs=[
            pl.BlockSpec((tm, d), lambda i: (i, 0)),
            _resident((1, d)),
            _resident_layer(w_q, layer),
            pl.BlockSpec((n_mem, d), lambda i: (i // tiles_per_seq, 0)),
            pl.BlockSpec((n_mem, d), lambda i: (i // tiles_per_seq, 1)),
            _resident_layer(w_o, layer),
            _resident((1, d)),
        ],
        out_specs=pl.BlockSpec((tm, d), lambda i: (i, 0)),
        out_shape=jax.ShapeDtypeStruct((n, d), F32),
        compiler_params=_params(1),
        name="mem_attn",
    )(x2, g_pre.reshape(1, d), w_q, kv, kv, w_o, g_post.reshape(1, d))


def kernel(x, mem, g_ffn1_pre, w_ffn1_gu, w_ffn1_down, g_ffn1_post, g_mix_pre, w_in, hyena_conv_w,
           hyena_conv_b, filt_w1, filt_b1, filt_w2, filt_b2, filt_w3, filt_b3, filt_w4, filt_freq,
           hyena_d, w_hyena_out, w_fnet_out, w_pool, pool_scale, w_out, g_mix_post, g_mem_pre,
           g_mem_kv, w_q, w_kv, w_o, g_mem_post, g_ffn2_pre, w_ffn2_gu, w_ffn2_down, g_ffn2_post):
    batch, seq, d = x.shape
    n_mem = mem.shape[1]
    depth = w_in.shape[0]
    hy_w = w_hyena_out.shape[1]
    hy_cols = hyena_conv_w.shape[2]
    fw = w_fnet_out.shape[1]
    pw = w_pool.shape[1] * w_pool.shape[2]

    tm = 512
    tn = fw + pw
    tf = 512
    ct = 256

    lhs1, gmat, lhs2 = _hyena_mats(seq)
    a1, a2, cs, fna, fnb = _fnet_mats(seq, fw // FNET_GROUPS)

    bf = lambda w: w.astype(BF16)
    w_ffn1_gu, w_ffn1_down, w_ffn2_gu, w_ffn2_down = map(bf, (w_ffn1_gu, w_ffn1_down, w_ffn2_gu, w_ffn2_down))
    w_in, w_hyena_out, w_fnet_out, w_pool, w_out = map(bf, (w_in, w_hyena_out, w_fnet_out, w_pool, w_out))
    w_q, w_o = map(bf, (w_q, w_o))
    x2 = x.reshape(batch * seq, d)
    mem2 = mem.reshape(batch * n_mem, d)
    for l in range(depth):
        x2 = _ffn(x2, g_ffn1_pre[l], w_ffn1_gu, w_ffn1_down, g_ffn1_post[l], l, tm=2 * tm, tf=tf)

        hyc, pq, po, gates = _proj(x2, g_mix_pre[l], w_in, hyena_conv_w[l], hyena_conv_b[l], cs, l,
                                   tm=2 * tm, tn=tn, seq=seq, hy_cols=hy_cols, fw=fw, pw=pw)
        h3 = _filter_mlp(filt_w1[l], filt_b1[l], filt_w2[l], filt_b2[l], filt_w3[l], filt_b3[l],
                         filt_freq[l], seq=seq)
        hspec = _filter_spectrum(h3, filt_w4[l], hyena_d[l], lhs1, gmat, seq=seq, width=hy_w, cw=ct // 2, ct=ct)
        z2 = _hyena(hyc, hspec, lhs1, gmat, lhs2, seq=seq, width=hy_w, ct=ct)
        fy = _fnet(pq, a1, a2, seq=seq, fw=fw, ct=ct, na=fna, nb=fnb)
        x2 = _merge(x2, z2, fy, po, gates, w_hyena_out, w_fnet_out, w_pool,
                    pool_scale[l], w_out, g_mix_post[l], l, tm=tm, seq=seq)

        kv = _kv(mem2, g_mem_kv[l], w_kv, l, n_mem=batch * n_mem, tn=tn)
        x2 = _attn(x2, g_mem_pre[l], w_q, kv, w_o, g_mem_post[l], l, tm=tm, seq=seq, n_mem=n_mem)

        x2 = _ffn(x2, g_ffn2_pre[l], w_ffn2_gu, w_ffn2_down, g_ffn2_post[l], l, tm=2 * tm, tf=tf)
    return x2.reshape(batch, seq, d)
```

```python
import functools
import math

import jax
import jax.numpy as jnp
import numpy as np
from jax import lax
from jax.experimental import pallas as pl
from jax.experimental.pallas import tpu as pltpu

F32 = jnp.float32
BF16 = jnp.bfloat16

RMS_EPS = 1e-6
MACARON_WEIGHT = 0.5
N_HEADS = 4
POOL_WINDOWS = (2, 4, 8, 16)
FNET_GROUPS = 4
FILTER_BANDS = 16
DECAY_TARGET = 1e-2
FAST_DECAY_PCT = 0.3
SLOW_DECAY_PCT = 1.5

V7X_SUBLANES = 8
V7X_BF16_ROWS = 16
V7X_VMEM_LIMIT = 60 * 2**20

HYENA_NB = 128
HALO = V7X_BF16_ROWS
ROW_CHUNK = 256
LANE_BLOCK = 128


def _params(n_axes):
    return pltpu.CompilerParams(
        dimension_semantics=("arbitrary",) * n_axes,
        vmem_limit_bytes=V7X_VMEM_LIMIT,
    )


def _resident(shape):
    zeros = (0,) * len(shape)
    return pl.BlockSpec(shape, lambda *_: zeros, pipeline_mode=pl.Buffered(1))


def _resident_layer(stacked, layer):
    tail = (0,) * (stacked.ndim - 1)
    return pl.BlockSpec((None,) + stacked.shape[1:], lambda *_: (layer,) + tail,
                        pipeline_mode=pl.Buffered(1))


def _rms(x, g):
    ms = jnp.mean(x * x, axis=-1, keepdims=True)
    return x * lax.rsqrt(ms + RMS_EPS) * g


def _sigmoid(x):
    return 0.5 * jnp.tanh(0.5 * x) + 0.5


def _dot(a, b):
    return jnp.dot(a, b, preferred_element_type=F32)


def _dot3(a, b):
    ah = a.astype(BF16)
    al = (a - ah.astype(F32)).astype(BF16)
    bh = b.astype(BF16)
    bl = (b - bh.astype(F32)).astype(BF16)
    return _dot(ah, bh) + (_dot(al, bh) + _dot(ah, bl))


def _row_chunks(rows, fn):
    chunk = ROW_CHUNK if rows % ROW_CHUNK == 0 else rows

    def body(c, carry):
        fn(pl.ds(pl.multiple_of(c * chunk, chunk), chunk))
        return carry

    lax.fori_loop(0, rows // chunk, body, 0)


def _norm_residual_inplace(o_ref, x_ref, g):
    rows, d = o_ref.shape

    def post(r):
        sq = None
        for c in range(0, d, LANE_BLOCK):
            yb = o_ref[r, c:c + LANE_BLOCK]
            sq = yb * yb if sq is None else sq + yb * yb
        inv = lax.rsqrt(jnp.sum(sq, axis=-1, keepdims=True) * (1.0 / d) + RMS_EPS)
        for c in range(0, d, LANE_BLOCK):
            cols = slice(c, c + LANE_BLOCK)
            o_ref[r, cols] = x_ref[r, cols] + o_ref[r, cols] * inv * g[:, cols]

    _row_chunks(rows, post)


def _ffn_body(x_ref, gpre_ref, wg_ref, wu_ref, wd_ref, gpost_ref, *rest, cast_next):
    if cast_next:
        ngu_ref, ndn_ref, o_ref, ngu_out, ndn_out, u_ref = rest
    else:
        o_ref, u_ref = rest
    k = pl.program_id(1)
    rows = x_ref.shape[0]

    @pl.when(k == 0)
    def _():
        def pre(r):
            u_ref[r, :] = _rms(x_ref[r, :], gpre_ref[...]).astype(BF16)
        _row_chunks(rows, pre)

    u = u_ref[...]
    a = _dot(u, wg_ref[...])
    b = _dot(u, wu_ref[...])
    h = (a * _sigmoid(a) * b).astype(BF16)
    if cast_next:
        ngu_out[...] = ngu_ref[...].astype(BF16)
        ndn_out[...] = ndn_ref[...].astype(BF16)

    @pl.when(k == 0)
    def _():
        o_ref[...] = _dot(h, wd_ref[...])

    @pl.when(k > 0)
    def _():
        o_ref[...] += _dot(h, wd_ref[...])

    @pl.when(k == pl.num_programs(1) - 1)
    def _():
        _norm_residual_inplace(o_ref, x_ref, MACARON_WEIGHT * gpost_ref[...])


def _ffn(x2, g_pre, w_gu, w_down, g_post, *, tm, tf, cast_next=None):
    n, d = x2.shape
    f = w_down.shape[0]
    ni, nk = n // tm, f // tf
    in_specs = [
        pl.BlockSpec((tm, d), lambda i, k: (i, 0)),
        pl.BlockSpec((1, d), lambda i, k: (0, 0)),
        pl.BlockSpec((d, tf), lambda i, k: (0, k)),
        pl.BlockSpec((d, tf), lambda i, k: (0, k + nk)),
        pl.BlockSpec((tf, d), lambda i, k: (k, 0)),
        pl.BlockSpec((1, d), lambda i, k: (0, 0)),
    ]
    args = [x2, g_pre.reshape(1, d), w_gu, w_gu, w_down, g_post.reshape(1, d)]
    out_specs = [pl.BlockSpec((tm, d), lambda i, k: (i, 0))]
    out_shape = [jax.ShapeDtypeStruct((n, d), F32)]
    if cast_next is not None:
        ngu, ndn, layer = cast_next
        gu_blk = (d // ni, 2 * f // nk)
        dn_blk = (f // nk, d // ni)
        assert d % ni == 0 and gu_blk[0] % V7X_BF16_ROWS == 0 and dn_blk[1] % 128 == 0
        in_specs += [pl.BlockSpec((None,) + gu_blk, lambda i, k: (layer, i, k)),
                     pl.BlockSpec((None,) + dn_blk, lambda i, k: (layer, k, i))]
        args += [ngu, ndn]
        out_specs += [pl.BlockSpec(gu_blk, lambda i, k: (i, k)), pl.BlockSpec(dn_blk, lambda i, k: (k, i))]
        out_shape += [jax.ShapeDtypeStruct(ngu.shape[1:], BF16), jax.ShapeDtypeStruct(ndn.shape[1:], BF16)]
    return pl.pallas_call(
        functools.partial(_ffn_body, cast_next=cast_next is not None),
        grid=(ni, nk),
        in_specs=in_specs,
        out_specs=out_specs,
        out_shape=out_shape,
        scratch_shapes=[pltpu.VMEM((tm, d), BF16)],
        compiler_params=_params(2),
        name="ffn",
    )(*args)


def _proj_body(x_ref, xp_ref, xn_ref, g_ref, w_ref, cw_ref, cb_ref, cs_ref,
               hy_ref, pq_ref, po_ref, gt_ref, u_ref, *, tm, seq, n_hy, fw):
    i = pl.program_id(0)
    j = pl.program_id(1)

    @pl.when(j == 0)
    def _():
        g = g_ref[...]
        has_prev = lax.rem(i * tm, seq) != 0
        has_next = lax.rem((i + 1) * tm, seq) != 0
        u_ref[pl.ds(0, HALO), :] = jnp.where(has_prev, _rms(xp_ref[...], g), 0.0).astype(BF16)
        chunk = ROW_CHUNK if tm % ROW_CHUNK == 0 else tm

        def pre(c, carry):
            r = pl.multiple_of(c * chunk, chunk)
            u_ref[pl.ds(pl.multiple_of(HALO + r, HALO), chunk), :] = (
                _rms(x_ref[pl.ds(r, chunk), :], g).astype(BF16))
            return carry

        lax.fori_loop(0, tm // chunk, pre, 0)
        u_ref[pl.ds(HALO + tm, HALO), :] = jnp.where(has_next, _rms(xn_ref[...], g), 0.0).astype(BF16)

    @pl.when(j < n_hy)
    def _():
        h = _dot(u_ref[...], w_ref[...])
        rows = h.shape[0]
        prev = pltpu.roll(h, 1, axis=0)[HALO:HALO + tm]
        nxt = pltpu.roll(h, rows - 1, axis=0)[HALO:HALO + tm]
        y = cb_ref[...] + prev * cw_ref[0:1, :]
        y = y + h[HALO:HALO + tm] * cw_ref[1:2, :]
        y = y + nxt * cw_ref[2:3, :]
        hy_ref[...] = y.astype(BF16)

    @pl.when(j == n_hy)
    def _():
        c = _dot(u_ref[pl.ds(HALO, tm), :], w_ref[...])
        po_ref[...] = c[:, fw:]
        fb = c[:, :fw].astype(BF16)
        gw = fw // FNET_GROUPS
        res = [_dot(fb[:, q * gw:(q + 1) * gw], cs_ref[...]) for q in range(FNET_GROUPS)]
        pq_ref[...] = jnp.concatenate([r[:, :gw] for r in res] + [r[:, gw:] for r in res], axis=1)

    @pl.when(j > n_hy)
    def _():
        c = _dot(u_ref[pl.ds(HALO, tm), :], w_ref[...])
        gt_ref[...] = _sigmoid(c).astype(BF16)


def _proj(x2, g, w_in, conv_w, conv_b, cs, layer, *, tm, tn, seq, hy_cols, fw, pw):
    n, d = x2.shape
    cols = w_in.shape[2]
    n_hy = hy_cols // tn
    assert hy_cols % tn == 0 and fw + pw == tn and (cols - hy_cols - tn) % tn == 0
    nj = cols // tn
    n_gate = cols - hy_cols - tn
    hb = tm // HALO
    last_hb = n // HALO - 1
    body = functools.partial(_proj_body, tm=tm, seq=seq, n_hy=n_hy, fw=fw)
    return pl.pallas_call(
        body,
        grid=(n // tm, nj),
        in_specs=[
            pl.BlockSpec((tm, d), lambda i, j: (i, 0)),
            pl.BlockSpec((HALO, d), lambda i, j: (jnp.maximum(i * hb - 1, 0), 0)),
            pl.BlockSpec((HALO, d), lambda i, j: (jnp.minimum((i + 1) * hb, last_hb), 0)),
            pl.BlockSpec((1, d), lambda i, j: (0, 0)),
            pl.BlockSpec((None, d, tn), lambda i, j: (layer, 0, j)),
            pl.BlockSpec((3, tn), lambda i, j: (0, jnp.minimum(j, n_hy - 1))),
            pl.BlockSpec((1, tn), lambda i, j: (0, jnp.minimum(j, n_hy - 1))),
            _resident(cs.shape),
        ],
        out_specs=[
            pl.BlockSpec((tm, tn), lambda i, j: (i, jnp.minimum(j, n_hy - 1))),
            pl.BlockSpec((tm, 2 * fw), lambda i, j: (i, 0)),
            pl.BlockSpec((tm, pw), lambda i, j: (i, 0)),
            pl.BlockSpec((tm, tn), lambda i, j: (i, jnp.maximum(j - n_hy - 1, 0))),
        ],
        out_shape=[
            jax.ShapeDtypeStruct((n, hy_cols), BF16),
            jax.ShapeDtypeStruct((n, 2 * fw), F32),
            jax.ShapeDtypeStruct((n, pw), F32),
            jax.ShapeDtypeStruct((n, n_gate), BF16),
        ],
        scratch_shapes=[pltpu.VMEM((tm + 2 * HALO, d), BF16)],
        compiler_params=_params(2),
        name="mixer_proj",
    )(x2, x2, x2, g.reshape(1, d), w_in, conv_w, conv_b.reshape(1, hy_cols), cs)


def _hyena_slot_block(s, na):
    half = na // 2
    return s if s <= half else (half + 1) + (s - half)


def _hyena_mats(seq):
    nb = HYENA_NB
    na = 2 * seq // nb
    half = na // 2
    eye = np.eye(V7X_SUBLANES)
    a = np.arange(half)[None, :]
    f1 = np.zeros((na, half))
    k_re = np.arange(half + 1)[:, None]
    f1[: half + 1] = np.cos(2 * np.pi * k_re * a / na)
    k_im = np.arange(1, half)[:, None]
    f1[half + 1:] = -np.sin(2 * np.pi * k_im * a / na)
    lhs1 = np.kron(f1, eye)
    b = np.arange(nb)[None, :]
    kb = np.arange(nb)[:, None]
    g = np.zeros((half + 1, 2 * nb, 2 * nb))
    for ka in range(half + 1):
        ang = 2 * np.pi * (kb * b / nb + b * ka / (2 * seq))
        gre, gim = np.cos(ang), -np.sin(ang)
        g[ka] = np.block([[gre, -gim], [gim, gre]])
    a_col = np.arange(half)[:, None]
    f2 = np.zeros((half, na))
    wgt = np.full(half + 1, 2.0)
    wgt[0] = wgt[half] = 1.0
    f2[:, : half + 1] = wgt[None, :] * np.cos(2 * np.pi * a_col * np.arange(half + 1)[None, :] / na)
    f2[:, half + 1:] = -2.0 * np.sin(2 * np.pi * a_col * np.arange(1, half)[None, :] / na)
    lhs2 = np.kron(f2 / (2 * seq), eye)
    return (jnp.asarray(lhs1, BF16), jnp.asarray(g, BF16), jnp.asarray(lhs2, BF16))


def _fnet_mats(seq, group):
    na = int(round(math.sqrt(seq)))
    nb = seq // na
    assert na * nb == seq and nb % V7X_SUBLANES == 0
    eye = np.eye(V7X_SUBLANES)
    n2 = np.arange(nb)[None, :]
    k2 = np.arange(nb)[:, None]
    ang = 2 * np.pi * k2 * n2 / nb
    c, s = np.cos(ang), np.sin(ang)
    blk = np.block([[c, s], [-s, c]])
    a1 = np.zeros((2, V7X_SUBLANES, nb, 2, nb, V7X_SUBLANES))
    for po in range(2):
        for pi_ in range(2):
            sub = blk[po * nb:(po + 1) * nb, pi_ * nb:(pi_ + 1) * nb]
            for lo in range(V7X_SUBLANES):
                a1[po, lo, :, pi_, :, lo] = sub
    a1 = a1.reshape(2 * V7X_SUBLANES * nb, 2 * V7X_SUBLANES * nb)
    a2 = np.zeros((nb // V7X_SUBLANES, na, V7X_SUBLANES, 2, na, V7X_SUBLANES))
    k1 = np.arange(na)[:, None]
    n1 = np.arange(na)[None, :]
    for hi in range(nb // V7X_SUBLANES):
        for lo in range(V7X_SUBLANES):
            k2v = hi * V7X_SUBLANES + lo
            phi = 2 * np.pi * (n1 * k1 / na + n1 * k2v / seq)
            a2[hi, :, lo, 0, :, lo] = np.cos(phi)
            a2[hi, :, lo, 1, :, lo] = np.sin(phi)
    a2 = a2.reshape(nb // V7X_SUBLANES, na * V7X_SUBLANES, 2 * na * V7X_SUBLANES) / math.sqrt(seq)
    cc = np.arange(group)[:, None]
    mm = np.arange(group)[None, :]
    angc = 2 * np.pi * cc * mm / group
    cs = np.concatenate([np.cos(angc), -np.sin(angc)], axis=1) / math.sqrt(group)
    return jnp.asarray(a1, BF16), jnp.asarray(a2, BF16), jnp.asarray(cs, BF16), na, nb


def _loop(n, body, unroll):
    if unroll is None:
        for i in range(n):
            body(i, 0)
        return
    while n % unroll:
        unroll -= 1
    lax.fori_loop(0, n, body, 0, unroll=unroll)


def _gather_tiles(ref, starts):
    tiles = [ref[pl.ds(pl.multiple_of(s, V7X_SUBLANES), V7X_SUBLANES), :] for s in starts]
    return jnp.concatenate(tiles, axis=0)


def _hyena_stage1(z_ref, e_ref, lhs1_ref, seq):
    nb = HYENA_NB
    na = 2 * seq // nb
    half = na // 2

    def body(i, carry):
        off = i * V7X_SUBLANES
        xg = _gather_tiles(z_ref, [nb * a + off for a in range(half)]).astype(BF16)
        out = _dot(lhs1_ref[...], xg)
        for s in range(na):
            row = _hyena_slot_block(s, na) * nb + off
            e_ref[pl.ds(pl.multiple_of(row, V7X_SUBLANES), V7X_SUBLANES), :] = (
                out[s * V7X_SUBLANES:(s + 1) * V7X_SUBLANES])
        return carry

    _loop(nb // V7X_SUBLANES, body, None)


def _hyena_zero_unused(e_ref, seq):
    nb = HYENA_NB
    half = seq // nb
    zero = jnp.zeros((nb, e_ref.shape[1]), F32)
    e_ref[pl.ds((half + 1) * nb, nb), :] = zero
    e_ref[pl.ds((2 * half + 1) * nb, nb), :] = zero


def _hyena_load_ka(e_ref, ka, seq):
    nb = HYENA_NB
    half = seq // nb
    re = e_ref[pl.ds(pl.multiple_of(ka * nb, nb), nb), :]
    im = e_ref[pl.ds(pl.multiple_of((half + 1 + ka) * nb, nb), nb), :]
    return jnp.concatenate([re, im], axis=0)


def _filter_mlp_body(bands_ref, w1t_ref, w1c_ref, w1s_ref, b1_ref, w2_ref, b2_ref, w3_ref, b3_ref,
                     fr_ref, o_ref, *, seq):
    idx = lax.broadcasted_iota(jnp.int32, (1, 2 * seq), 1)
    p = jnp.where(idx < seq, idx, 2 * seq - idx).astype(F32)
    t = p / (seq - 1.0)
    ang = bands_ref[...] * ((2.0 * math.pi / seq) * p)
    fr = fr_ref[...]
    pre = (w1t_ref[...] * t
           + _dot3(w1c_ref[...], jnp.cos(ang))
           + _dot3(w1s_ref[...], -jnp.sin(ang))
           + b1_ref[...])
    h = jnp.sin(fr * pre)
    h = jnp.sin(fr * (_dot3(w2_ref[...], h) + b2_ref[...]))
    h = jnp.sin(fr * (_dot3(w3_ref[...], h) + b3_ref[...]))
    o_ref[...] = jnp.concatenate([h[:, :seq], h[:, seq:]], axis=0).T


def _filter_mlp(fw1, fb1, fw2, fb2, fw3, fb3, freq, *, seq):
    hid = fw2.shape[0]
    bands = np.linspace(1e-4, FILTER_BANDS - 1, FILTER_BANDS).astype(np.float32).reshape(-1, 1)
    col = lambda v: v.reshape(hid, 1)
    args = (jnp.asarray(bands), fw1[0].reshape(hid, 1), fw1[1:1 + FILTER_BANDS].T,
            fw1[1 + FILTER_BANDS:].T, col(fb1), fw2.T, col(fb2), fw3.T, col(fb3), col(freq))
    return pl.pallas_call(
        functools.partial(_filter_mlp_body, seq=seq),
        grid=(1,),
        in_specs=[_resident(a.shape) for a in args],
        out_specs=pl.BlockSpec((seq, 2 * hid), lambda i: (0, 0)),
        out_shape=jax.ShapeDtypeStruct((seq, 2 * hid), F32),
        compiler_params=_params(1),
        name="filter_mlp",
    )(*args)


def _filter_spec_body(h3_ref, w4f_ref, w4b_ref, dl_ref, d_ref, lhs1_ref, g_ref,
                      o_ref, z_ref, e_ref, *, seq, cw):
    nb = HYENA_NB
    half = seq // nb
    dl = dl_ref[...]
    zero = jnp.zeros_like(w4f_ref[...])
    w4 = jnp.concatenate([jnp.concatenate([w4f_ref[...], zero], axis=1),
                          jnp.concatenate([zero, w4b_ref[...]], axis=1)], axis=0)
    lane = lax.broadcasted_iota(jnp.int32, (1, 2 * cw), 1)
    chunk = ROW_CHUNK if seq % ROW_CHUNK == 0 else seq
    for c in range(seq // chunk):
        row = c * chunk + lax.broadcasted_iota(jnp.int32, (chunk, 1), 0)
        rowf = row.astype(F32)
        h4 = _dot3(h3_ref[pl.ds(c * chunk, chunk), :], w4)
        dec = jnp.concatenate([jnp.exp(-(rowf / (seq - 1.0)) * dl),
                               jnp.exp(-((seq - rowf) / (seq - 1.0)) * dl)], axis=1)
        z_ref[pl.ds(c * chunk, chunk), :] = jnp.where((row == 0) & (lane >= cw), 0.0, h4 * dec)

    _hyena_zero_unused(e_ref, seq)
    _hyena_stage1(z_ref, e_ref, lhs1_ref, seq)
    dsk = d_ref[0]

    def body(ka, carry):
        ein = _hyena_load_ka(e_ref, ka, seq).astype(BF16)
        x = _dot(g_ref[ka], ein)
        sgn = 1.0 - 2.0 * (ka % 2)
        hsp = x[:, 0:cw] + sgn * x[:, cw:2 * cw]
        part = lax.broadcasted_iota(jnp.int32, (2 * nb, 1), 0)
        hsp = hsp + jnp.where(part < nb, dsk, 0.0)
        o_ref[0, pl.ds(pl.multiple_of(ka * 2 * nb, 2 * nb), 2 * nb), :] = hsp.astype(BF16)
        return carry

    _loop(half + 1, body, None)


def _filter_spectrum(h3, fw4, d_skip, lhs1, gmat, *, seq, width, cw, ct):
    hid = h3.shape[1] // 2
    orders = d_skip.shape[0]
    nct = width // cw
    per = ct // cw
    nb = HYENA_NB
    half = seq // nb
    max_decay = math.log(DECAY_TARGET) / FAST_DECAY_PCT
    min_decay = math.log(DECAY_TARGET) / SLOW_DECAY_PCT
    deltas = np.abs(np.linspace(min_decay, max_decay, width)).astype(np.float32).reshape(1, width)
    rows = (half + 1) * 2 * nb
    body = functools.partial(_filter_spec_body, seq=seq, cw=cw)
    return pl.pallas_call(
        body,
        grid=(orders, nct),
        in_specs=[
            _resident(h3.shape),
            pl.BlockSpec((hid, cw), lambda o, c: (0, o * 2 * nct + c)),
            pl.BlockSpec((hid, cw), lambda o, c: (0, o * 2 * nct + nct + c)),
            pl.BlockSpec((1, cw), lambda o, c: (0, c)),
            pl.BlockSpec((1, 1, cw), lambda o, c: (o, 0, c)),
            _resident(lhs1.shape),
            _resident(gmat.shape),
        ],
        out_specs=pl.BlockSpec((None, 1, rows, cw), lambda o, c: (c // per, o, 0, c % per)),
        out_shape=jax.ShapeDtypeStruct((width // ct, orders, rows, ct), BF16),
        scratch_shapes=[pltpu.VMEM((seq, 2 * cw), F32),
                        pltpu.VMEM((2 * (half + 1) * nb, 2 * cw), F32)],
        compiler_params=_params(2),
        name="filter_spectrum",
    )(h3, fw4, fw4, jnp.asarray(deltas), d_skip.reshape(orders, 1, width), lhs1, gmat)


def _hyena_body(v_ref, g1_ref, g2_ref, h_ref, lhs1_ref, g_ref, lhs2_ref, o_ref, z_ref, e_ref, *, seq):
    nb = HYENA_NB
    na = 2 * seq // nb
    half = na // 2
    pair = V7X_BF16_ROWS
    chunk = 512 if seq % 512 == 0 else seq

    def load_v(c, carry):
        r = pl.multiple_of(c * chunk, chunk)
        z_ref[pl.ds(r, chunk), :] = v_ref[pl.ds(r, chunk), :].astype(F32)
        return carry

    lax.fori_loop(0, seq // chunk, load_v, 0)
    _hyena_zero_unused(e_ref, seq)

    for order, gate_ref in enumerate((g1_ref, g2_ref)):
        _hyena_stage1(z_ref, e_ref, lhs1_ref, seq)

        group = 3 if (half + 1) % 3 == 0 else 1

        def spectral(it, carry):
            kas = [it * group + u for u in range(group)]
            eins = [_hyena_load_ka(e_ref, ka, seq).astype(BF16) for ka in kas]
            ys = []
            for ka, ein in zip(kas, eins):
                gk = g_ref[ka]
                x = _dot(gk, ein)
                hsp = h_ref[order, pl.ds(pl.multiple_of(ka * 2 * nb, 2 * nb), 2 * nb), :].astype(F32)
                xre, xim, hre, him = x[:nb], x[nb:], hsp[:nb], hsp[nb:]
                zsp = jnp.concatenate([xre * hre - xim * him, xre * him + xim * hre], axis=0).astype(BF16)
                ys.append(lax.dot_general(gk, zsp, (((0,), (0,)), ((), ())), preferred_element_type=F32))
            for ka, y in zip(kas, ys):
                e_ref[pl.ds(pl.multiple_of(ka * nb, nb), nb), :] = y[:nb]
                keep = jnp.where((ka == 0) | (ka == half), 0.0, 1.0)
                e_ref[pl.ds(pl.multiple_of((half + 1 + ka) * nb, nb), nb), :] = y[nb:] * keep
            return carry

        _loop((half + 1) // group, spectral, None)

        def inverse(i, carry):
            outs = []
            for q in range(pair // V7X_SUBLANES):
                off = i * pair + q * V7X_SUBLANES
                yg = _gather_tiles(e_ref, [_hyena_slot_block(s, na) * nb + off for s in range(na)])
                outs.append(_dot(lhs2_ref[...], yg.astype(BF16)))
            for a in range(half):
                sl = slice(a * V7X_SUBLANES, (a + 1) * V7X_SUBLANES)
                conv = jnp.concatenate([o_[sl] for o_ in outs], axis=0)
                r = pl.multiple_of(a * nb + i * pair, pair)
                res = gate_ref[pl.ds(r, pair), :].astype(F32) * conv
                if order == 0:
                    z_ref[pl.ds(r, pair), :] = res
                else:
                    o_ref[pl.ds(r, pair), :] = res.astype(BF16)
            return carry

        _loop(nb // pair, inverse, None)


def _hyena(hyc, hspec, lhs1, gmat, lhs2, *, seq, width, ct):
    n = hyc.shape[0]
    nct = width // ct
    nb = HYENA_NB
    half = seq // nb
    return pl.pallas_call(
        functools.partial(_hyena_body, seq=seq),
        grid=(nct, n // seq),
        in_specs=[
            pl.BlockSpec((seq, ct), lambda c, b: (b, c)),
            pl.BlockSpec((seq, ct), lambda c, b: (b, nct + c)),
            pl.BlockSpec((seq, ct), lambda c, b: (b, 2 * nct + c)),
            pl.BlockSpec((None,) + hspec.shape[1:], lambda c, b: (c, 0, 0, 0)),
            _resident(lhs1.shape),
            _resident(gmat.shape),
            _resident(lhs2.shape),
        ],
        out_specs=pl.BlockSpec((seq, ct), lambda c, b: (b, c)),
        out_shape=jax.ShapeDtypeStruct((n, width), BF16),
        scratch_shapes=[pltpu.VMEM((seq, ct), F32),
                        pltpu.VMEM((2 * (half + 1) * nb, ct), F32)],
        compiler_params=_params(2),
        name="hyena",
    )(hyc, hyc, hyc, hspec, lhs1, gmat, lhs2)


def _fnet_body(p_ref, q_ref, a1_ref, a2_ref, o_ref, t_ref, *, seq, na, nb):
    sub = V7X_SUBLANES

    def stage1(i, carry):
        off = i * sub
        starts = [na * n2 + off for n2 in range(nb)]
        xin = jnp.concatenate([_gather_tiles(p_ref, starts), _gather_tiles(q_ref, starts)],
                              axis=0).astype(BF16)
        out = _dot(a1_ref[...], xin)
        for part in range(2):
            for lo in range(sub):
                src = (part * sub + lo) * nb
                dst = part * seq + (off + lo) * nb
                t_ref[pl.ds(pl.multiple_of(dst, sub), nb), :] = out[src:src + nb]
        return carry

    _loop(na // sub, stage1, None)

    def stage2(i, carry):
        off = i * sub
        starts = [part * seq + n1 * nb + off for part in range(2) for n1 in range(na)]
        tin = _gather_tiles(t_ref, starts).astype(BF16)
        out = _dot(a2_ref[i], tin)
        for k1 in range(na):
            o_ref[pl.ds(pl.multiple_of(k1 * nb + off, sub), sub), :] = out[k1 * sub:(k1 + 1) * sub]
        return carry

    _loop(nb // sub, stage2, None)


def _fnet(pq, a1, a2, *, seq, fw, ct, na, nb):
    n = pq.shape[0]
    nct = fw // ct
    return pl.pallas_call(
        functools.partial(_fnet_body, seq=seq, na=na, nb=nb),
        grid=(n // seq, nct),
        in_specs=[
            pl.BlockSpec((seq, ct), lambda b, c: (b, c)),
            pl.BlockSpec((seq, ct), lambda b, c: (b, nct + c)),
            _resident(a1.shape),
            _resident(a2.shape),
        ],
        out_specs=pl.BlockSpec((seq, ct), lambda b, c: (b, c)),
        out_shape=jax.ShapeDtypeStruct((n, fw), F32),
        scratch_shapes=[pltpu.VMEM((2 * seq, ct), F32)],
        compiler_params=_params(2),
        name="fnet_seq",
    )(pq, pq, a1, a2)


def _merge_body(x_ref, z_ref, f_ref, po_ref, pp_ref, pn_ref, gt_ref, wha_ref, wfo_ref, wpl_ref,
                ps_ref, wo_ref, gpost_ref, o_ref, ext_ref, m_ref, *, tm, seq, d):
    i = pl.program_id(0)
    sub = V7X_SUBLANES
    has_prev = lax.rem(i * tm, seq) != 0
    has_next = lax.rem((i + 1) * tm, seq) != 0
    ext_ref[pl.ds(0, sub), :] = jnp.where(has_prev, pp_ref[...], 0.0)
    ext_ref[pl.ds(sub, tm), :] = po_ref[...]
    ext_ref[pl.ds(sub + tm, sub), :] = jnp.where(has_next, pn_ref[...], 0.0)

    gate = lambda q, lo, hi: gt_ref[:, q * d + lo:q * d + hi].astype(F32)
    m_ref[...] = gate(0, 0, d) * _dot(z_ref[...], wha_ref[...])
    m_ref[...] += gate(1, 0, d) * _dot(f_ref[...].astype(BF16), wfo_ref[...])

    pos = lax.rem(i * tm, seq) + lax.broadcasted_iota(jnp.int32, (tm, 1), 0)
    n_pool = len(POOL_WINDOWS)
    gw = po_ref.shape[1] // n_pool
    ow = d // n_pool
    for q, w in enumerate(POOL_WINDOWS):
        before = w // 2
        after = w - 1 - before
        cols = slice(q * gw, (q + 1) * gw)
        tot = ext_ref[pl.ds(sub - before, tm), cols]
        for s in range(-before + 1, after + 1):
            tot = tot + ext_ref[pl.ds(sub + s, tm), cols]
        lo = jnp.maximum(pos - before, 0)
        hi = jnp.minimum(pos + after, seq - 1)
        cnt = (hi - lo + 1).astype(F32)
        mq = (tot / cnt - po_ref[:, cols]).astype(BF16)
        yc = _dot(mq, wpl_ref[q]) * ps_ref[:, q * ow:(q + 1) * ow]
        m_ref[:, q * ow:(q + 1) * ow] += gate(2, q * ow, (q + 1) * ow) * yc

    y = _dot(m_ref[...].astype(BF16), wo_ref[...])
    o_ref[...] = x_ref[...] + _rms(y, gpost_ref[...])


def _merge(x2, z2, fy, po, gates, wha, wfo, wpool, pscale, w_out, g_post, layer, *, tm, seq):
    n, d = x2.shape
    pw = po.shape[1]
    sub = V7X_SUBLANES
    hb = tm // sub
    last_hb = n // sub - 1
    body = functools.partial(_merge_body, tm=tm, seq=seq, d=d)
    return pl.pallas_call(
        body,
        grid=(n // tm,),
        in_specs=[
            pl.BlockSpec((tm, d), lambda i: (i, 0)),
            pl.BlockSpec((tm, z2.shape[1]), lambda i: (i, 0)),
            pl.BlockSpec((tm, fy.shape[1]), lambda i: (i, 0)),
            pl.BlockSpec((tm, pw), lambda i: (i, 0)),
            pl.BlockSpec((sub, pw), lambda i: (jnp.maximum(i * hb - 1, 0), 0)),
            pl.BlockSpec((sub, pw), lambda i: (jnp.minimum((i + 1) * hb, last_hb), 0)),
            pl.BlockSpec((tm, gates.shape[1]), lambda i: (i, 0)),
            _resident_layer(wha, layer),
            _resident_layer(wfo, layer),
            _resident_layer(wpool, layer),
            _resident((1, d)),
            _resident_layer(w_out, layer),
            _resident((1, d)),
        ],
        out_specs=pl.BlockSpec((tm, d), lambda i: (i, 0)),
        out_shape=jax.ShapeDtypeStruct((n, d), F32),
        scratch_shapes=[pltpu.VMEM((tm + 2 * sub, pw), F32), pltpu.VMEM((tm, d), F32)],
        compiler_params=_params(1),
        name="mixer_merge",
    )(x2, z2, fy, po, po, po, gates, wha, wfo, wpool, pscale.reshape(1, d), w_out, g_post.reshape(1, d))


def _kv_body(m_ref, g_ref, w_ref, o_ref):
    o_ref[...] = _dot(_rms(m_ref[...], g_ref[...]).astype(BF16), w_ref[...].astype(BF16)).astype(BF16)


def _kv(mem2, g, w_kv, layer, *, n_mem, tn):
    n, d = mem2.shape
    cols = w_kv.shape[2]
    return pl.pallas_call(
        _kv_body,
        grid=(cols // tn, n // n_mem),
        in_specs=[
            pl.BlockSpec((n_mem, d), lambda j, b: (b, 0)),
            pl.BlockSpec((1, d), lambda j, b: (0, 0)),
            pl.BlockSpec((None, d, tn), lambda j, b: (layer, 0, j)),
        ],
        out_specs=pl.BlockSpec((n_mem, tn), lambda j, b: (b, j)),
        out_shape=jax.ShapeDtypeStruct((n, cols), BF16),
        compiler_params=_params(2),
        name="mem_kv",
    )(mem2, g.reshape(1, d), w_kv)


def _attn_body(x_ref, gpre_ref, wq_ref, k_ref, v_ref, wo_ref, gpost_ref, o_ref, *, d):
    dh = d // N_HEADS
    x = x_ref[...]
    q = _dot(_rms(x, gpre_ref[...]).astype(BF16), wq_ref[...])
    heads = []
    for h in range(N_HEADS):
        cols = slice(h * dh, (h + 1) * dh)
        s = lax.dot_general(q[:, cols].astype(BF16), k_ref[:, cols], (((1,), (1,)), ((), ())),
                            preferred_element_type=F32) * (dh ** -0.5)
        e = jnp.exp(s - jnp.max(s, axis=-1, keepdims=True))
        p = e / jnp.sum(e, axis=-1, keepdims=True)
        heads.append(_dot(p.astype(BF16), v_ref[:, cols]))
    o = jnp.concatenate(heads, axis=1).astype(BF16)
    o_ref[...] = x + _rms(_dot(o, wo_ref[...]), gpost_ref[...])


def _attn(x2, g_pre, w_q, kv, w_o, g_post, layer, *, tm, seq, n_mem):
    n, d = x2.shape
    tiles_per_seq = seq // tm
    return pl.pallas_call(
        functools.partial(_attn_body, d=d),
        grid=(n // tm,),
        in_specs=[
            pl.BlockSpec((tm, d), lambda i: (i, 0)),
            _resident((1, d)),
            _resident_layer(w_q, layer),
            pl.BlockSpec((n_mem, d), lambda i: (i // tiles_per_seq, 0)),
            pl.BlockSpec((n_mem, d), lambda i: (i // tiles_per_seq, 1)),
            _resident_layer(w_o, layer),
            _resident((1, d)),
        ],
        out_specs=pl.BlockSpec((tm, d), lambda i: (i, 0)),
        out_shape=jax.ShapeDtypeStruct((n, d), F32),
        compiler_params=_params(1),
        name="mem_attn",
    )(x2, g_pre.reshape(1, d), w_q, kv, kv, w_o, g_post.reshape(1, d))


def kernel(x, mem, g_ffn1_pre, w_ffn1_gu, w_ffn1_down, g_ffn1_post, g_mix_pre, w_in, hyena_conv_w,
           hyena_conv_b, filt_w1, filt_b1, filt_w2, filt_b2, filt_w3, filt_b3, filt_w4, filt_freq,
           hyena_d, w_hyena_out, w_fnet_out, w_pool, pool_scale, w_out, g_mix_post, g_mem_pre,
           g_mem_kv, w_q, w_kv, w_o, g_mem_post, g_ffn2_pre, w_ffn2_gu, w_ffn2_down, g_ffn2_post):
    batch, seq, d = x.shape
    n_mem = mem.shape[1]
    depth = w_in.shape[0]
    hy_w = w_hyena_out.shape[1]
    hy_cols = hyena_conv_w.shape[2]
    fw = w_fnet_out.shape[1]
    pw = w_pool.shape[1] * w_pool.shape[2]

    tm = 512
    tn = fw + pw
    tf = 512
    ct = 256

    lhs1, gmat, lhs2 = _hyena_mats(seq)
    a1, a2, cs, fna, fnb = _fnet_mats(seq, fw // FNET_GROUPS)

    bf = lambda w: w.astype(BF16)
    w_in, w_hyena_out, w_fnet_out, w_pool, w_out = map(bf, (w_in, w_hyena_out, w_fnet_out, w_pool, w_out))
    w_q, w_o = map(bf, (w_q, w_o))
    ffn_w = (bf(w_ffn1_gu[0]), bf(w_ffn1_down[0]))
    x2 = x.reshape(batch * seq, d)
    mem2 = mem.reshape(batch * n_mem, d)
    for l in range(depth):
        x2, *ffn_w = _ffn(x2, g_ffn1_pre[l], *ffn_w, g_ffn1_post[l], tm=2 * tm, tf=tf,
                          cast_next=(w_ffn2_gu, w_ffn2_down, l))

        hyc, pq, po, gates = _proj(x2, g_mix_pre[l], w_in, hyena_conv_w[l], hyena_conv_b[l], cs, l,
                                   tm=2 * tm, tn=tn, seq=seq, hy_cols=hy_cols, fw=fw, pw=pw)
        h3 = _filter_mlp(filt_w1[l], filt_b1[l], filt_w2[l], filt_b2[l], filt_w3[l], filt_b3[l],
                         filt_freq[l], seq=seq)
        hspec = _filter_spectrum(h3, filt_w4[l], hyena_d[l], lhs1, gmat, seq=seq, width=hy_w, cw=ct // 2, ct=ct)
        z2 = _hyena(hyc, hspec, lhs1, gmat, lhs2, seq=seq, width=hy_w, ct=ct)
        fy = _fnet(pq, a1, a2, seq=seq, fw=fw, ct=ct, na=fna, nb=fnb)
        x2 = _merge(x2, z2, fy, po, gates, w_hyena_out, w_fnet_out, w_pool,
                    pool_scale[l], w_out, g_mix_post[l], l, tm=tm, seq=seq)

        kv = _kv(mem2, g_mem_kv[l], w_kv, l, n_mem=batch * n_mem, tn=tn)
        x2 = _attn(x2, g_mem_pre[l], w_q, kv, w_o, g_mem_post[l], l, tm=tm, seq=seq, n_mem=n_mem)

        nxt = (w_ffn1_gu, w_ffn1_down, l + 1) if l + 1 < depth else None
        x2, *ffn_w = _ffn(x2, g_ffn2_pre[l], *ffn_w, g_ffn2_post[l], tm=2 * tm, tf=tf, cast_next=nxt)
    return x2.reshape(batch, seq, d)
```

```python
import functools
import math

import jax
import jax.numpy as jnp
import numpy as np
from jax import lax
from jax.experimental import pallas as pl
from jax.experimental.pallas import tpu as pltpu

F32 = jnp.float32
BF16 = jnp.bfloat16

RMS_EPS = 1e-6
MACARON_WEIGHT = 0.5
N_HEADS = 4
POOL_WINDOWS = (2, 4, 8, 16)
FNET_GROUPS = 4
FILTER_BANDS = 16
DECAY_TARGET = 1e-2
FAST_DECAY_PCT = 0.3
SLOW_DECAY_PCT = 1.5

V7X_SUBLANES = 8
V7X_BF16_ROWS = 16
V7X_VMEM_LIMIT = 60 * 2**20

HYENA_NB = 128
HALO = V7X_BF16_ROWS
ROW_CHUNK = 256
LANE_BLOCK = 128


def _params(n_axes):
    return pltpu.CompilerParams(
        dimension_semantics=("arbitrary",) * n_axes,
        vmem_limit_bytes=V7X_VMEM_LIMIT,
    )


def _resident(shape):
    zeros = (0,) * len(shape)
    return pl.BlockSpec(shape, lambda *_: zeros, pipeline_mode=pl.Buffered(1))


def _resident_layer(stacked, layer):
    tail = (0,) * (stacked.ndim - 1)
    return pl.BlockSpec((None,) + stacked.shape[1:], lambda *_: (layer,) + tail,
                        pipeline_mode=pl.Buffered(1))


def _rms(x, g):
    ms = jnp.mean(x * x, axis=-1, keepdims=True)
    return x * lax.rsqrt(ms + RMS_EPS) * g


def _sigmoid(x):
    return 0.5 * jnp.tanh(0.5 * x) + 0.5


def _dot(a, b):
    return jnp.dot(a, b, preferred_element_type=F32)


def _dot3(a, b):
    ah = a.astype(BF16)
    al = (a - ah.astype(F32)).astype(BF16)
    bh = b.astype(BF16)
    bl = (b - bh.astype(F32)).astype(BF16)
    return _dot(ah, bh) + (_dot(al, bh) + _dot(ah, bl))


def _row_chunks(rows, fn):
    chunk = ROW_CHUNK if rows % ROW_CHUNK == 0 else rows

    def body(c, carry):
        fn(pl.ds(pl.multiple_of(c * chunk, chunk), chunk))
        return carry

    lax.fori_loop(0, rows // chunk, body, 0)


def _norm_residual_inplace(o_ref, x_ref, g):
    rows, d = o_ref.shape

    def post(r):
        sq = None
        for c in range(0, d, LANE_BLOCK):
            yb = o_ref[r, c:c + LANE_BLOCK]
            sq = yb * yb if sq is None else sq + yb * yb
        inv = lax.rsqrt(jnp.sum(sq, axis=-1, keepdims=True) * (1.0 / d) + RMS_EPS)
        for c in range(0, d, LANE_BLOCK):
            cols = slice(c, c + LANE_BLOCK)
            o_ref[r, cols] = x_ref[r, cols] + o_ref[r, cols] * inv * g[:, cols]

    _row_chunks(rows, post)


def _ffn_body(x_ref, gpre_ref, wg_ref, wu_ref, wd_ref, gpost_ref, *rest, n_cast):
    cast_in, o_ref, cast_out, u_ref = rest[:n_cast], rest[n_cast], rest[n_cast + 1:-1], rest[-1]
    k = pl.program_id(1)
    rows = x_ref.shape[0]

    @pl.when(k == 0)
    def _():
        def pre(r):
            u_ref[r, :] = _rms(x_ref[r, :], gpre_ref[...]).astype(BF16)
        _row_chunks(rows, pre)

    u = u_ref[...]
    a = _dot(u, wg_ref[...])
    b = _dot(u, wu_ref[...])
    h = (a * _sigmoid(a) * b).astype(BF16)
    for src, dst in zip(cast_in, cast_out):
        dst[...] = src[...].astype(BF16)

    @pl.when(k == 0)
    def _():
        o_ref[...] = _dot(h, wd_ref[...])

    @pl.when(k > 0)
    def _():
        o_ref[...] += _dot(h, wd_ref[...])

    @pl.when(k == pl.num_programs(1) - 1)
    def _():
        _norm_residual_inplace(o_ref, x_ref, MACARON_WEIGHT * gpost_ref[...])


def _cast_tiling(rows, cols, ni, nk):
    rb = rows // ni
    assert rows % ni == 0 and rb % V7X_BF16_ROWS == 0
    nc = max(c for c in range(1, nk + 1) if cols % c == 0 and (cols // c) % 128 == 0)
    return rb, cols // nc, nc


def _ffn(x2, g_pre, w_gu, w_down, g_post, *, tm, tf, cast=()):
    n, d = x2.shape
    f = w_down.shape[0]
    ni, nk = n // tm, f // tf
    in_specs = [
        pl.BlockSpec((tm, d), lambda i, k: (i, 0)),
        pl.BlockSpec((1, d), lambda i, k: (0, 0)),
        pl.BlockSpec((d, tf), lambda i, k: (0, k)),
        pl.BlockSpec((d, tf), lambda i, k: (0, k + nk)),
        pl.BlockSpec((tf, d), lambda i, k: (k, 0)),
        pl.BlockSpec((1, d), lambda i, k: (0, 0)),
    ]
    args = [x2, g_pre.reshape(1, d), w_gu, w_gu, w_down, g_post.reshape(1, d)]
    out_specs = [pl.BlockSpec((tm, d), lambda i, k: (i, 0))]
    out_shape = [jax.ShapeDtypeStruct((n, d), F32)]
    for w, layer in cast:
        rb, cb, nc = _cast_tiling(w.shape[1], w.shape[2], ni, nk)
        in_specs.append(pl.BlockSpec((None, rb, cb),
                                     lambda i, k, layer=layer, nc=nc: (layer, i, jnp.minimum(k, nc - 1))))
        args.append(w)
        out_specs.append(pl.BlockSpec((rb, cb), lambda i, k, nc=nc: (i, jnp.minimum(k, nc - 1))))
        out_shape.append(jax.ShapeDtypeStruct(w.shape[1:], BF16))
    return pl.pallas_call(
        functools.partial(_ffn_body, n_cast=len(cast)),
        grid=(ni, nk),
        in_specs=in_specs,
        out_specs=out_specs,
        out_shape=out_shape,
        scratch_shapes=[pltpu.VMEM((tm, d), BF16)],
        compiler_params=_params(2),
        name="ffn",
    )(*args)


def _proj_body(x_ref, xp_ref, xn_ref, g_ref, w_ref, cw_ref, cb_ref, cs_ref,
               hy_ref, pq_ref, po_ref, gt_ref, u_ref, *, tm, seq, n_hy, fw):
    i = pl.program_id(0)
    j = pl.program_id(1)

    @pl.when(j == 0)
    def _():
        g = g_ref[...]
        has_prev = lax.rem(i * tm, seq) != 0
        has_next = lax.rem((i + 1) * tm, seq) != 0
        u_ref[pl.ds(0, HALO), :] = jnp.where(has_prev, _rms(xp_ref[...], g), 0.0).astype(BF16)
        chunk = ROW_CHUNK if tm % ROW_CHUNK == 0 else tm

        def pre(c, carry):
            r = pl.multiple_of(c * chunk, chunk)
            u_ref[pl.ds(pl.multiple_of(HALO + r, HALO), chunk), :] = (
                _rms(x_ref[pl.ds(r, chunk), :], g).astype(BF16))
            return carry

        lax.fori_loop(0, tm // chunk, pre, 0)
        u_ref[pl.ds(HALO + tm, HALO), :] = jnp.where(has_next, _rms(xn_ref[...], g), 0.0).astype(BF16)

    @pl.when(j < n_hy)
    def _():
        h = _dot(u_ref[...], w_ref[...])
        rows = h.shape[0]
        prev = pltpu.roll(h, 1, axis=0)[HALO:HALO + tm]
        nxt = pltpu.roll(h, rows - 1, axis=0)[HALO:HALO + tm]
        y = cb_ref[...] + prev * cw_ref[0:1, :]
        y = y + h[HALO:HALO + tm] * cw_ref[1:2, :]
        y = y + nxt * cw_ref[2:3, :]
        hy_ref[...] = y.astype(BF16)

    @pl.when(j == n_hy)
    def _():
        c = _dot(u_ref[pl.ds(HALO, tm), :], w_ref[...])
        po_ref[...] = c[:, fw:]
        fb = c[:, :fw].astype(BF16)
        gw = fw // FNET_GROUPS
        res = [_dot(fb[:, q * gw:(q + 1) * gw], cs_ref[...]) for q in range(FNET_GROUPS)]
        pq_ref[...] = jnp.concatenate([r[:, :gw] for r in res] + [r[:, gw:] for r in res], axis=1)

    @pl.when(j > n_hy)
    def _():
        c = _dot(u_ref[pl.ds(HALO, tm), :], w_ref[...])
        gt_ref[...] = _sigmoid(c).astype(BF16)


def _proj(x2, g, w_in, conv_w, conv_b, cs, *, tm, tn, seq, hy_cols, fw, pw):
    n, d = x2.shape
    cols = w_in.shape[1]
    n_hy = hy_cols // tn
    assert hy_cols % tn == 0 and fw + pw == tn and (cols - hy_cols - tn) % tn == 0
    nj = cols // tn
    n_gate = cols - hy_cols - tn
    hb = tm // HALO
    last_hb = n // HALO - 1
    body = functools.partial(_proj_body, tm=tm, seq=seq, n_hy=n_hy, fw=fw)
    return pl.pallas_call(
        body,
        grid=(n // tm, nj),
        in_specs=[
            pl.BlockSpec((tm, d), lambda i, j: (i, 0)),
            pl.BlockSpec((HALO, d), lambda i, j: (jnp.maximum(i * hb - 1, 0), 0)),
            pl.BlockSpec((HALO, d), lambda i, j: (jnp.minimum((i + 1) * hb, last_hb), 0)),
            pl.BlockSpec((1, d), lambda i, j: (0, 0)),
            pl.BlockSpec((d, tn), lambda i, j: (0, j)),
            pl.BlockSpec((3, tn), lambda i, j: (0, jnp.minimum(j, n_hy - 1))),
            pl.BlockSpec((1, tn), lambda i, j: (0, jnp.minimum(j, n_hy - 1))),
            _resident(cs.shape),
        ],
        out_specs=[
            pl.BlockSpec((tm, tn), lambda i, j: (i, jnp.minimum(j, n_hy - 1))),
            pl.BlockSpec((tm, 2 * fw), lambda i, j: (i, 0)),
            pl.BlockSpec((tm, pw), lambda i, j: (i, 0)),
            pl.BlockSpec((tm, tn), lambda i, j: (i, jnp.maximum(j - n_hy - 1, 0))),
        ],
        out_shape=[
            jax.ShapeDtypeStruct((n, hy_cols), BF16),
            jax.ShapeDtypeStruct((n, 2 * fw), F32),
            jax.ShapeDtypeStruct((n, pw), F32),
            jax.ShapeDtypeStruct((n, n_gate), BF16),
        ],
        scratch_shapes=[pltpu.VMEM((tm + 2 * HALO, d), BF16)],
        compiler_params=_params(2),
        name="mixer_proj",
    )(x2, x2, x2, g.reshape(1, d), w_in, conv_w, conv_b.reshape(1, hy_cols), cs)


def _hyena_slot_block(s, na):
    half = na // 2
    return s if s <= half else (half + 1) + (s - half)


def _hyena_mats(seq):
    nb = HYENA_NB
    na = 2 * seq // nb
    half = na // 2
    eye = np.eye(V7X_SUBLANES)
    a = np.arange(half)[None, :]
    f1 = np.zeros((na, half))
    k_re = np.arange(half + 1)[:, None]
    f1[: half + 1] = np.cos(2 * np.pi * k_re * a / na)
    k_im = np.arange(1, half)[:, None]
    f1[half + 1:] = -np.sin(2 * np.pi * k_im * a / na)
    lhs1 = np.kron(f1, eye)
    b = np.arange(nb)[None, :]
    kb = np.arange(nb)[:, None]
    g = np.zeros((half + 1, 2 * nb, 2 * nb))
    for ka in range(half + 1):
        ang = 2 * np.pi * (kb * b / nb + b * ka / (2 * seq))
        gre, gim = np.cos(ang), -np.sin(ang)
        g[ka] = np.block([[gre, -gim], [gim, gre]])
    a_col = np.arange(half)[:, None]
    f2 = np.zeros((half, na))
    wgt = np.full(half + 1, 2.0)
    wgt[0] = wgt[half] = 1.0
    f2[:, : half + 1] = wgt[None, :] * np.cos(2 * np.pi * a_col * np.arange(half + 1)[None, :] / na)
    f2[:, half + 1:] = -2.0 * np.sin(2 * np.pi * a_col * np.arange(1, half)[None, :] / na)
    lhs2 = np.kron(f2 / (2 * seq), eye)
    return (jnp.asarray(lhs1, BF16), jnp.asarray(g, BF16), jnp.asarray(lhs2, BF16))


def _fnet_mats(seq, group):
    na = int(round(math.sqrt(seq)))
    nb = seq // na
    assert na * nb == seq and nb % V7X_SUBLANES == 0
    eye = np.eye(V7X_SUBLANES)
    n2 = np.arange(nb)[None, :]
    k2 = np.arange(nb)[:, None]
    ang = 2 * np.pi * k2 * n2 / nb
    c, s = np.cos(ang), np.sin(ang)
    blk = np.block([[c, s], [-s, c]])
    a1 = np.zeros((2, V7X_SUBLANES, nb, 2, nb, V7X_SUBLANES))
    for po in range(2):
        for pi_ in range(2):
            sub = blk[po * nb:(po + 1) * nb, pi_ * nb:(pi_ + 1) * nb]
            for lo in range(V7X_SUBLANES):
                a1[po, lo, :, pi_, :, lo] = sub
    a1 = a1.reshape(2 * V7X_SUBLANES * nb, 2 * V7X_SUBLANES * nb)
    a2 = np.zeros((nb // V7X_SUBLANES, na, V7X_SUBLANES, 2, na, V7X_SUBLANES))
    k1 = np.arange(na)[:, None]
    n1 = np.arange(na)[None, :]
    for hi in range(nb // V7X_SUBLANES):
        for lo in range(V7X_SUBLANES):
            k2v = hi * V7X_SUBLANES + lo
            phi = 2 * np.pi * (n1 * k1 / na + n1 * k2v / seq)
            a2[hi, :, lo, 0, :, lo] = np.cos(phi)
            a2[hi, :, lo, 1, :, lo] = np.sin(phi)
    a2 = a2.reshape(nb // V7X_SUBLANES, na * V7X_SUBLANES, 2 * na * V7X_SUBLANES) / math.sqrt(seq)
    cc = np.arange(group)[:, None]
    mm = np.arange(group)[None, :]
    angc = 2 * np.pi * cc * mm / group
    cs = np.concatenate([np.cos(angc), -np.sin(angc)], axis=1) / math.sqrt(group)
    return jnp.asarray(a1, BF16), jnp.asarray(a2, BF16), jnp.asarray(cs, BF16), na, nb


def _loop(n, body, unroll):
    if unroll is None:
        for i in range(n):
            body(i, 0)
        return
    while n % unroll:
        unroll -= 1
    lax.fori_loop(0, n, body, 0, unroll=unroll)


def _gather_tiles(ref, starts):
    tiles = [ref[pl.ds(pl.multiple_of(s, V7X_SUBLANES), V7X_SUBLANES), :] for s in starts]
    return jnp.concatenate(tiles, axis=0)


def _hyena_stage1(z_ref, e_ref, lhs1_ref, seq):
    nb = HYENA_NB
    na = 2 * seq // nb
    half = na // 2

    def body(i, carry):
        off = i * V7X_SUBLANES
        xg = _gather_tiles(z_ref, [nb * a + off for a in range(half)]).astype(BF16)
        out = _dot(lhs1_ref[...], xg)
        for s in range(na):
            row = _hyena_slot_block(s, na) * nb + off
            e_ref[pl.ds(pl.multiple_of(row, V7X_SUBLANES), V7X_SUBLANES), :] = (
                out[s * V7X_SUBLANES:(s + 1) * V7X_SUBLANES])
        return carry

    _loop(nb // V7X_SUBLANES, body, None)


def _hyena_zero_unused(e_ref, seq):
    nb = HYENA_NB
    half = seq // nb
    zero = jnp.zeros((nb, e_ref.shape[1]), F32)
    e_ref[pl.ds((half + 1) * nb, nb), :] = zero
    e_ref[pl.ds((2 * half + 1) * nb, nb), :] = zero


def _hyena_load_ka(e_ref, ka, seq):
    nb = HYENA_NB
    half = seq // nb
    re = e_ref[pl.ds(pl.multiple_of(ka * nb, nb), nb), :]
    im = e_ref[pl.ds(pl.multiple_of((half + 1 + ka) * nb, nb), nb), :]
    return jnp.concatenate([re, im], axis=0)


def _filter_mlp_body(bands_ref, w1t_ref, w1c_ref, w1s_ref, b1_ref, w2_ref, b2_ref, w3_ref, b3_ref,
                     fr_ref, o_ref, *, seq):
    idx = lax.broadcasted_iota(jnp.int32, (1, 2 * seq), 1)
    p = jnp.where(idx < seq, idx, 2 * seq - idx).astype(F32)
    t = p / (seq - 1.0)
    ang = bands_ref[...] * ((2.0 * math.pi / seq) * p)
    fr = fr_ref[...]
    pre = (w1t_ref[...] * t
           + _dot3(w1c_ref[...], jnp.cos(ang))
           + _dot3(w1s_ref[...], -jnp.sin(ang))
           + b1_ref[...])
    h = jnp.sin(fr * pre)
    h = jnp.sin(fr * (_dot3(w2_ref[...], h) + b2_ref[...]))
    h = jnp.sin(fr * (_dot3(w3_ref[...], h) + b3_ref[...]))
    o_ref[...] = jnp.concatenate([h[:, :seq], h[:, seq:]], axis=0).T


def _filter_mlp(fw1, fb1, fw2, fb2, fw3, fb3, freq, *, seq):
    hid = fw2.shape[0]
    bands = np.linspace(1e-4, FILTER_BANDS - 1, FILTER_BANDS).astype(np.float32).reshape(-1, 1)
    col = lambda v: v.reshape(hid, 1)
    args = (jnp.asarray(bands), fw1[0].reshape(hid, 1), fw1[1:1 + FILTER_BANDS].T,
            fw1[1 + FILTER_BANDS:].T, col(fb1), fw2.T, col(fb2), fw3.T, col(fb3), col(freq))
    return pl.pallas_call(
        functools.partial(_filter_mlp_body, seq=seq),
        grid=(1,),
        in_specs=[_resident(a.shape) for a in args],
        out_specs=pl.BlockSpec((seq, 2 * hid), lambda i: (0, 0)),
        out_shape=jax.ShapeDtypeStruct((seq, 2 * hid), F32),
        compiler_params=_params(1),
        name="filter_mlp",
    )(*args)


def _filter_spec_body(h3_ref, w4f_ref, w4b_ref, dl_ref, d_ref, lhs1_ref, g_ref,
                      o_ref, z_ref, e_ref, *, seq, cw):
    nb = HYENA_NB
    half = seq // nb
    dl = dl_ref[...]
    zero = jnp.zeros_like(w4f_ref[...])
    w4 = jnp.concatenate([jnp.concatenate([w4f_ref[...], zero], axis=1),
                          jnp.concatenate([zero, w4b_ref[...]], axis=1)], axis=0)
    lane = lax.broadcasted_iota(jnp.int32, (1, 2 * cw), 1)
    chunk = ROW_CHUNK if seq % ROW_CHUNK == 0 else seq
    for c in range(seq // chunk):
        row = c * chunk + lax.broadcasted_iota(jnp.int32, (chunk, 1), 0)
        rowf = row.astype(F32)
        h4 = _dot3(h3_ref[pl.ds(c * chunk, chunk), :], w4)
        dec = jnp.concatenate([jnp.exp(-(rowf / (seq - 1.0)) * dl),
                               jnp.exp(-((seq - rowf) / (seq - 1.0)) * dl)], axis=1)
        z_ref[pl.ds(c * chunk, chunk), :] = jnp.where((row == 0) & (lane >= cw), 0.0, h4 * dec)

    _hyena_zero_unused(e_ref, seq)
    _hyena_stage1(z_ref, e_ref, lhs1_ref, seq)
    dsk = d_ref[0]

    def body(ka, carry):
        ein = _hyena_load_ka(e_ref, ka, seq).astype(BF16)
        x = _dot(g_ref[ka], ein)
        sgn = 1.0 - 2.0 * (ka % 2)
        hsp = x[:, 0:cw] + sgn * x[:, cw:2 * cw]
        part = lax.broadcasted_iota(jnp.int32, (2 * nb, 1), 0)
        hsp = hsp + jnp.where(part < nb, dsk, 0.0)
        o_ref[0, pl.ds(pl.multiple_of(ka * 2 * nb, 2 * nb), 2 * nb), :] = hsp.astype(BF16)
        return carry

    _loop(half + 1, body, None)


def _filter_spectrum(h3, fw4, d_skip, lhs1, gmat, *, seq, width, cw, ct):
    hid = h3.shape[1] // 2
    orders = d_skip.shape[0]
    nct = width // cw
    per = ct // cw
    nb = HYENA_NB
    half = seq // nb
    max_decay = math.log(DECAY_TARGET) / FAST_DECAY_PCT
    min_decay = math.log(DECAY_TARGET) / SLOW_DECAY_PCT
    deltas = np.abs(np.linspace(min_decay, max_decay, width)).astype(np.float32).reshape(1, width)
    rows = (half + 1) * 2 * nb
    body = functools.partial(_filter_spec_body, seq=seq, cw=cw)
    return pl.pallas_call(
        body,
        grid=(orders, nct),
        in_specs=[
            _resident(h3.shape),
            pl.BlockSpec((hid, cw), lambda o, c: (0, o * 2 * nct + c)),
            pl.BlockSpec((hid, cw), lambda o, c: (0, o * 2 * nct + nct + c)),
            pl.BlockSpec((1, cw), lambda o, c: (0, c)),
            pl.BlockSpec((1, 1, cw), lambda o, c: (o, 0, c)),
            _resident(lhs1.shape),
            _resident(gmat.shape),
        ],
        out_specs=pl.BlockSpec((None, 1, rows, cw), lambda o, c: (c // per, o, 0, c % per)),
        out_shape=jax.ShapeDtypeStruct((width // ct, orders, rows, ct), BF16),
        scratch_shapes=[pltpu.VMEM((seq, 2 * cw), F32),
                        pltpu.VMEM((2 * (half + 1) * nb, 2 * cw), F32)],
        compiler_params=_params(2),
        name="filter_spectrum",
    )(h3, fw4, fw4, jnp.asarray(deltas), d_skip.reshape(orders, 1, width), lhs1, gmat)


def _hyena_body(v_ref, g1_ref, g2_ref, h_ref, lhs1_ref, g_ref, lhs2_ref, o_ref, z_ref, e_ref, *, seq):
    nb = HYENA_NB
    na = 2 * seq // nb
    half = na // 2
    pair = V7X_BF16_ROWS
    chunk = 512 if seq % 512 == 0 else seq

    def load_v(c, carry):
        r = pl.multiple_of(c * chunk, chunk)
        z_ref[pl.ds(r, chunk), :] = v_ref[pl.ds(r, chunk), :].astype(F32)
        return carry

    lax.fori_loop(0, seq // chunk, load_v, 0)
    _hyena_zero_unused(e_ref, seq)

    for order, gate_ref in enumerate((g1_ref, g2_ref)):
        _hyena_stage1(z_ref, e_ref, lhs1_ref, seq)

        group = 3 if (half + 1) % 3 == 0 else 1

        def spectral(it, carry):
            kas = [it * group + u for u in range(group)]
            eins = [_hyena_load_ka(e_ref, ka, seq).astype(BF16) for ka in kas]
            ys = []
            for ka, ein in zip(kas, eins):
                gk = g_ref[ka]
                x = _dot(gk, ein)
                hsp = h_ref[order, pl.ds(pl.multiple_of(ka * 2 * nb, 2 * nb), 2 * nb), :].astype(F32)
                xre, xim, hre, him = x[:nb], x[nb:], hsp[:nb], hsp[nb:]
                zsp = jnp.concatenate([xre * hre - xim * him, xre * him + xim * hre], axis=0).astype(BF16)
                ys.append(lax.dot_general(gk, zsp, (((0,), (0,)), ((), ())), preferred_element_type=F32))
            for ka, y in zip(kas, ys):
                e_ref[pl.ds(pl.multiple_of(ka * nb, nb), nb), :] = y[:nb]
                keep = jnp.where((ka == 0) | (ka == half), 0.0, 1.0)
                e_ref[pl.ds(pl.multiple_of((half + 1 + ka) * nb, nb), nb), :] = y[nb:] * keep
            return carry

        _loop((half + 1) // group, spectral, None)

        def inverse(i, carry):
            outs = []
            for q in range(pair // V7X_SUBLANES):
                off = i * pair + q * V7X_SUBLANES
                yg = _gather_tiles(e_ref, [_hyena_slot_block(s, na) * nb + off for s in range(na)])
                outs.append(_dot(lhs2_ref[...], yg.astype(BF16)))
            for a in range(half):
                sl = slice(a * V7X_SUBLANES, (a + 1) * V7X_SUBLANES)
                conv = jnp.concatenate([o_[sl] for o_ in outs], axis=0)
                r = pl.multiple_of(a * nb + i * pair, pair)
                res = gate_ref[pl.ds(r, pair), :].astype(F32) * conv
                if order == 0:
                    z_ref[pl.ds(r, pair), :] = res
                else:
                    o_ref[pl.ds(r, pair), :] = res.astype(BF16)
            return carry

        _loop(nb // pair, inverse, None)


def _hyena(hyc, hspec, lhs1, gmat, lhs2, *, seq, width, ct):
    n = hyc.shape[0]
    nct = width // ct
    nb = HYENA_NB
    half = seq // nb
    return pl.pallas_call(
        functools.partial(_hyena_body, seq=seq),
        grid=(nct, n // seq),
        in_specs=[
            pl.BlockSpec((seq, ct), lambda c, b: (b, c)),
            pl.BlockSpec((seq, ct), lambda c, b: (b, nct + c)),
            pl.BlockSpec((seq, ct), lambda c, b: (b, 2 * nct + c)),
            pl.BlockSpec((None,) + hspec.shape[1:], lambda c, b: (c, 0, 0, 0)),
            _resident(lhs1.shape),
            _resident(gmat.shape),
            _resident(lhs2.shape),
        ],
        out_specs=pl.BlockSpec((seq, ct), lambda c, b: (b, c)),
        out_shape=jax.ShapeDtypeStruct((n, width), BF16),
        scratch_shapes=[pltpu.VMEM((seq, ct), F32),
                        pltpu.VMEM((2 * (half + 1) * nb, ct), F32)],
        compiler_params=_params(2),
        name="hyena",
    )(hyc, hyc, hyc, hspec, lhs1, gmat, lhs2)


def _fnet_body(p_ref, q_ref, a1_ref, a2_ref, o_ref, t_ref, *, seq, na, nb):
    sub = V7X_SUBLANES

    def stage1(i, carry):
        off = i * sub
        starts = [na * n2 + off for n2 in range(nb)]
        xin = jnp.concatenate([_gather_tiles(p_ref, starts), _gather_tiles(q_ref, starts)],
                              axis=0).astype(BF16)
        out = _dot(a1_ref[...], xin)
        for part in range(2):
            for lo in range(sub):
                src = (part * sub + lo) * nb
                dst = part * seq + (off + lo) * nb
                t_ref[pl.ds(pl.multiple_of(dst, sub), nb), :] = out[src:src + nb]
        return carry

    _loop(na // sub, stage1, None)

    def stage2(i, carry):
        off = i * sub
        starts = [part * seq + n1 * nb + off for part in range(2) for n1 in range(na)]
        tin = _gather_tiles(t_ref, starts).astype(BF16)
        out = _dot(a2_ref[i], tin)
        for k1 in range(na):
            o_ref[pl.ds(pl.multiple_of(k1 * nb + off, sub), sub), :] = out[k1 * sub:(k1 + 1) * sub]
        return carry

    _loop(nb // sub, stage2, None)


def _fnet(pq, a1, a2, *, seq, fw, ct, na, nb):
    n = pq.shape[0]
    nct = fw // ct
    return pl.pallas_call(
        functools.partial(_fnet_body, seq=seq, na=na, nb=nb),
        grid=(n // seq, nct),
        in_specs=[
            pl.BlockSpec((seq, ct), lambda b, c: (b, c)),
            pl.BlockSpec((seq, ct), lambda b, c: (b, nct + c)),
            _resident(a1.shape),
            _resident(a2.shape),
        ],
        out_specs=pl.BlockSpec((seq, ct), lambda b, c: (b, c)),
        out_shape=jax.ShapeDtypeStruct((n, fw), F32),
        scratch_shapes=[pltpu.VMEM((2 * seq, ct), F32)],
        compiler_params=_params(2),
        name="fnet_seq",
    )(pq, pq, a1, a2)


def _merge_body(x_ref, z_ref, f_ref, po_ref, pp_ref, pn_ref, gt_ref, wha_ref, wfo_ref, wpl_ref,
                ps_ref, wo_ref, gpost_ref, o_ref, ext_ref, m_ref, *, tm, seq, d):
    i = pl.program_id(0)
    sub = V7X_SUBLANES
    has_prev = lax.rem(i * tm, seq) != 0
    has_next = lax.rem((i + 1) * tm, seq) != 0
    ext_ref[pl.ds(0, sub), :] = jnp.where(has_prev, pp_ref[...], 0.0)
    ext_ref[pl.ds(sub, tm), :] = po_ref[...]
    ext_ref[pl.ds(sub + tm, sub), :] = jnp.where(has_next, pn_ref[...], 0.0)

    gate = lambda q, lo, hi: gt_ref[:, q * d + lo:q * d + hi].astype(F32)
    m_ref[...] = gate(0, 0, d) * _dot(z_ref[...], wha_ref[...])
    m_ref[...] += gate(1, 0, d) * _dot(f_ref[...].astype(BF16), wfo_ref[...])

    pos = lax.rem(i * tm, seq) + lax.broadcasted_iota(jnp.int32, (tm, 1), 0)
    n_pool = len(POOL_WINDOWS)
    gw = po_ref.shape[1] // n_pool
    ow = d // n_pool
    for q, w in enumerate(POOL_WINDOWS):
        before = w // 2
        after = w - 1 - before
        cols = slice(q * gw, (q + 1) * gw)
        tot = ext_ref[pl.ds(sub - before, tm), cols]
        for s in range(-before + 1, after + 1):
            tot = tot + ext_ref[pl.ds(sub + s, tm), cols]
        lo = jnp.maximum(pos - before, 0)
        hi = jnp.minimum(pos + after, seq - 1)
        cnt = (hi - lo + 1).astype(F32)
        mq = (tot / cnt - po_ref[:, cols]).astype(BF16)
        yc = _dot(mq, wpl_ref[q]) * ps_ref[:, q * ow:(q + 1) * ow]
        m_ref[:, q * ow:(q + 1) * ow] += gate(2, q * ow, (q + 1) * ow) * yc

    y = _dot(m_ref[...].astype(BF16), wo_ref[...])
    o_ref[...] = x_ref[...] + _rms(y, gpost_ref[...])


def _merge(x2, z2, fy, po, gates, wha, wfo, wpool, pscale, w_out, g_post, layer, *, tm, seq):
    n, d = x2.shape
    pw = po.shape[1]
    sub = V7X_SUBLANES
    hb = tm // sub
    last_hb = n // sub - 1
    body = functools.partial(_merge_body, tm=tm, seq=seq, d=d)
    return pl.pallas_call(
        body,
        grid=(n // tm,),
        in_specs=[
            pl.BlockSpec((tm, d), lambda i: (i, 0)),
            pl.BlockSpec((tm, z2.shape[1]), lambda i: (i, 0)),
            pl.BlockSpec((tm, fy.shape[1]), lambda i: (i, 0)),
            pl.BlockSpec((tm, pw), lambda i: (i, 0)),
            pl.BlockSpec((sub, pw), lambda i: (jnp.maximum(i * hb - 1, 0), 0)),
            pl.BlockSpec((sub, pw), lambda i: (jnp.minimum((i + 1) * hb, last_hb), 0)),
            pl.BlockSpec((tm, gates.shape[1]), lambda i: (i, 0)),
            _resident_layer(wha, layer),
            _resident_layer(wfo, layer),
            _resident_layer(wpool, layer),
            _resident((1, d)),
            _resident(w_out.shape),
            _resident((1, d)),
        ],
        out_specs=pl.BlockSpec((tm, d), lambda i: (i, 0)),
        out_shape=jax.ShapeDtypeStruct((n, d), F32),
        scratch_shapes=[pltpu.VMEM((tm + 2 * sub, pw), F32), pltpu.VMEM((tm, d), F32)],
        compiler_params=_params(1),
        name="mixer_merge",
    )(x2, z2, fy, po, po, po, gates, wha, wfo, wpool, pscale.reshape(1, d), w_out, g_post.reshape(1, d))


def _kv_body(m_ref, g_ref, w_ref, o_ref):
    o_ref[...] = _dot(_rms(m_ref[...], g_ref[...]).astype(BF16), w_ref[...].astype(BF16)).astype(BF16)


def _kv(mem2, g, w_kv, layer, *, n_mem, tn):
    n, d = mem2.shape
    cols = w_kv.shape[2]
    return pl.pallas_call(
        _kv_body,
        grid=(cols // tn, n // n_mem),
        in_specs=[
            pl.BlockSpec((n_mem, d), lambda j, b: (b, 0)),
            pl.BlockSpec((1, d), lambda j, b: (0, 0)),
            pl.BlockSpec((None, d, tn), lambda j, b: (layer, 0, j)),
        ],
        out_specs=pl.BlockSpec((n_mem, tn), lambda j, b: (b, j)),
        out_shape=jax.ShapeDtypeStruct((n, cols), BF16),
        compiler_params=_params(2),
        name="mem_kv",
    )(mem2, g.reshape(1, d), w_kv)


def _attn_body(x_ref, gpre_ref, wq_ref, k_ref, v_ref, wo_ref, gpost_ref, o_ref, *, d):
    dh = d // N_HEADS
    x = x_ref[...]
    q = _dot(_rms(x, gpre_ref[...]).astype(BF16), wq_ref[...])
    heads = []
    for h in range(N_HEADS):
        cols = slice(h * dh, (h + 1) * dh)
        s = lax.dot_general(q[:, cols].astype(BF16), k_ref[:, cols], (((1,), (1,)), ((), ())),
                            preferred_element_type=F32) * (dh ** -0.5)
        e = jnp.exp(s - jnp.max(s, axis=-1, keepdims=True))
        p = e / jnp.sum(e, axis=-1, keepdims=True)
        heads.append(_dot(p.astype(BF16), v_ref[:, cols]))
    o = jnp.concatenate(heads, axis=1).astype(BF16)
    o_ref[...] = x + _rms(_dot(o, wo_ref[...]), gpost_ref[...])


def _attn(x2, g_pre, w_q, kv, w_o, g_post, *, tm, seq, n_mem):
    n, d = x2.shape
    tiles_per_seq = seq // tm
    return pl.pallas_call(
        functools.partial(_attn_body, d=d),
        grid=(n // tm,),
        in_specs=[
            pl.BlockSpec((tm, d), lambda i: (i, 0)),
            _resident((1, d)),
            _resident(w_q.shape),
            pl.BlockSpec((n_mem, d), lambda i: (i // tiles_per_seq, 0)),
            pl.BlockSpec((n_mem, d), lambda i: (i // tiles_per_seq, 1)),
            _resident(w_o.shape),
            _resident((1, d)),
        ],
        out_specs=pl.BlockSpec((tm, d), lambda i: (i, 0)),
        out_shape=jax.ShapeDtypeStruct((n, d), F32),
        compiler_params=_params(1),
        name="mem_attn",
    )(x2, g_pre.reshape(1, d), w_q, kv, kv, w_o, g_post.reshape(1, d))


def kernel(x, mem, g_ffn1_pre, w_ffn1_gu, w_ffn1_down, g_ffn1_post, g_mix_pre, w_in, hyena_conv_w,
           hyena_conv_b, filt_w1, filt_b1, filt_w2, filt_b2, filt_w3, filt_b3, filt_w4, filt_freq,
           hyena_d, w_hyena_out, w_fnet_out, w_pool, pool_scale, w_out, g_mix_post, g_mem_pre,
           g_mem_kv, w_q, w_kv, w_o, g_mem_post, g_ffn2_pre, w_ffn2_gu, w_ffn2_down, g_ffn2_post):
    batch, seq, d = x.shape
    n_mem = mem.shape[1]
    depth = w_in.shape[0]
    hy_w = w_hyena_out.shape[1]
    hy_cols = hyena_conv_w.shape[2]
    fw = w_fnet_out.shape[1]
    pw = w_pool.shape[1] * w_pool.shape[2]

    tm = 512
    tn = fw + pw
    tf = 512
    ct = 256

    lhs1, gmat, lhs2 = _hyena_mats(seq)
    a1, a2, cs, fna, fnb = _fnet_mats(seq, fw // FNET_GROUPS)

    bf = lambda w: w.astype(BF16)
    w_hyena_out, w_fnet_out, w_pool = map(bf, (w_hyena_out, w_fnet_out, w_pool))
    ffn_w = (bf(w_ffn1_gu[0]), bf(w_ffn1_down[0]))
    x2 = x.reshape(batch * seq, d)
    mem2 = mem.reshape(batch * n_mem, d)
    for l in range(depth):
        later = (w_ffn2_gu, w_ffn2_down, w_in, w_out, w_q, w_o)
        x2, *cast = _ffn(x2, g_ffn1_pre[l], *ffn_w, g_ffn1_post[l], tm=2 * tm, tf=tf,
                         cast=[(w, l) for w in later])
        *ffn_w, w_in_l, w_out_l, w_q_l, w_o_l = cast

        hyc, pq, po, gates = _proj(x2, g_mix_pre[l], w_in_l, hyena_conv_w[l], hyena_conv_b[l], cs,
                                   tm=2 * tm, tn=tn, seq=seq, hy_cols=hy_cols, fw=fw, pw=pw)
        h3 = _filter_mlp(filt_w1[l], filt_b1[l], filt_w2[l], filt_b2[l], filt_w3[l], filt_b3[l],
                         filt_freq[l], seq=seq)
        hspec = _filter_spectrum(h3, filt_w4[l], hyena_d[l], lhs1, gmat, seq=seq, width=hy_w, cw=ct // 2, ct=ct)
        z2 = _hyena(hyc, hspec, lhs1, gmat, lhs2, seq=seq, width=hy_w, ct=ct)
        fy = _fnet(pq, a1, a2, seq=seq, fw=fw, ct=ct, na=fna, nb=fnb)
        x2 = _merge(x2, z2, fy, po, gates, w_hyena_out, w_fnet_out, w_pool,
                    pool_scale[l], w_out_l, g_mix_post[l], l, tm=tm, seq=seq)

        kv = _kv(mem2, g_mem_kv[l], w_kv, l, n_mem=batch * n_mem, tn=tn)
        x2 = _attn(x2, g_mem_pre[l], w_q_l, kv, w_o_l, g_mem_post[l], tm=tm, seq=seq, n_mem=n_mem)

        nxt = [(w_ffn1_gu, l + 1), (w_ffn1_down, l + 1)] if l + 1 < depth else []
        x2, *ffn_w = _ffn(x2, g_ffn2_pre[l], *ffn_w, g_ffn2_post[l], tm=2 * tm, tf=tf, cast=nxt)
    return x2.reshape(batch, seq, d)
```

```python
import functools
import math

import jax
import jax.numpy as jnp
import numpy as np
from jax import lax
from jax.experimental import pallas as pl
from jax.experimental.pallas import tpu as pltpu

F32 = jnp.float32
BF16 = jnp.bfloat16

RMS_EPS = 1e-6
MACARON_WEIGHT = 0.5
N_HEADS = 4
POOL_WINDOWS = (2, 4, 8, 16)
FNET_GROUPS = 4
FILTER_BANDS = 16
DECAY_TARGET = 1e-2
FAST_DECAY_PCT = 0.3
SLOW_DECAY_PCT = 1.5

V7X_SUBLANES = 8
V7X_BF16_ROWS = 16
V7X_MXU_COLS = 256
V7X_VMEM_LIMIT = 60 * 2**20

HYENA_NB = 128
HALO = V7X_BF16_ROWS
ROW_CHUNK = 256
LANE_BLOCK = 128


def _params(n_axes):
    return pltpu.CompilerParams(
        dimension_semantics=("arbitrary",) * n_axes,
        vmem_limit_bytes=V7X_VMEM_LIMIT,
    )


def _resident(shape):
    zeros = (0,) * len(shape)
    return pl.BlockSpec(shape, lambda *_: zeros, pipeline_mode=pl.Buffered(1))


def _resident_layer(stacked, layer):
    tail = (0,) * (stacked.ndim - 1)
    return pl.BlockSpec((None,) + stacked.shape[1:], lambda *_: (layer,) + tail,
                        pipeline_mode=pl.Buffered(1))


def _rms(x, g):
    ms = jnp.mean(x * x, axis=-1, keepdims=True)
    return x * lax.rsqrt(ms + RMS_EPS) * g


def _sigmoid(x):
    return 0.5 * jnp.tanh(0.5 * x) + 0.5


def _dot(a, b):
    return jnp.dot(a, b, preferred_element_type=F32)


def _dot3(a, b):
    ah = a.astype(BF16)
    al = (a - ah.astype(F32)).astype(BF16)
    bh = b.astype(BF16)
    bl = (b - bh.astype(F32)).astype(BF16)
    return _dot(ah, bh) + (_dot(al, bh) + _dot(ah, bl))


def _row_chunks(rows, fn):
    chunk = ROW_CHUNK if rows % ROW_CHUNK == 0 else rows

    def body(c, carry):
        fn(pl.ds(pl.multiple_of(c * chunk, chunk), chunk))
        return carry

    lax.fori_loop(0, rows // chunk, body, 0)


def _norm_residual_inplace(o_ref, x_ref, g):
    rows, d = o_ref.shape

    def post(r):
        sq = None
        for c in range(0, d, LANE_BLOCK):
            yb = o_ref[r, c:c + LANE_BLOCK]
            sq = yb * yb if sq is None else sq + yb * yb
        inv = lax.rsqrt(jnp.sum(sq, axis=-1, keepdims=True) * (1.0 / d) + RMS_EPS)
        for c in range(0, d, LANE_BLOCK):
            cols = slice(c, c + LANE_BLOCK)
            o_ref[r, cols] = x_ref[r, cols] + o_ref[r, cols] * inv * g[:, cols]

    _row_chunks(rows, post)


def _ffn_body(x_ref, gpre_ref, wg_ref, wu_ref, wd_ref, gpost_ref, *rest, n_cast):
    cast_in, o_ref, cast_out, u_ref = rest[:n_cast], rest[n_cast], rest[n_cast + 1:-1], rest[-1]
    k = pl.program_id(1)
    rows = x_ref.shape[0]

    @pl.when(k == 0)
    def _():
        def pre(r):
            u_ref[r, :] = _rms(x_ref[r, :], gpre_ref[...]).astype(BF16)
        _row_chunks(rows, pre)

    u = u_ref[...]
    a = _dot(u, wg_ref[...])
    b = _dot(u, wu_ref[...])
    h = (a * _sigmoid(a) * b).astype(BF16)
    for src, dst in zip(cast_in, cast_out):
        dst[...] = src[...].astype(BF16)

    @pl.when(k == 0)
    def _():
        o_ref[...] = _dot(h, wd_ref[...])

    @pl.when(k > 0)
    def _():
        o_ref[...] += _dot(h, wd_ref[...])

    @pl.when(k == pl.num_programs(1) - 1)
    def _():
        _norm_residual_inplace(o_ref, x_ref, MACARON_WEIGHT * gpost_ref[...])


def _cast_tiling(rows, cols, ni, nk):
    rb = rows // ni
    assert rows % ni == 0 and rb % V7X_BF16_ROWS == 0
    nc = max(c for c in range(1, nk + 1) if cols % c == 0 and (cols // c) % 128 == 0)
    return rb, cols // nc, nc


def _ffn(x2, g_pre, w_gu, w_down, g_post, *, tm, tf, cast=()):
    n, d = x2.shape
    f = w_down.shape[0]
    ni, nk = n // tm, f // tf
    in_specs = [
        pl.BlockSpec((tm, d), lambda i, k: (i, 0)),
        pl.BlockSpec((1, d), lambda i, k: (0, 0)),
        pl.BlockSpec((d, tf), lambda i, k: (0, k)),
        pl.BlockSpec((d, tf), lambda i, k: (0, k + nk)),
        pl.BlockSpec((tf, d), lambda i, k: (k, 0)),
        pl.BlockSpec((1, d), lambda i, k: (0, 0)),
    ]
    args = [x2, g_pre.reshape(1, d), w_gu, w_gu, w_down, g_post.reshape(1, d)]
    out_specs = [pl.BlockSpec((tm, d), lambda i, k: (i, 0))]
    out_shape = [jax.ShapeDtypeStruct((n, d), F32)]
    for w, layer in cast:
        rb, cb, nc = _cast_tiling(w.shape[1], w.shape[2], ni, nk)
        in_specs.append(pl.BlockSpec((None, rb, cb),
                                     lambda i, k, layer=layer, nc=nc: (layer, i, jnp.minimum(k, nc - 1))))
        args.append(w)
        out_specs.append(pl.BlockSpec((rb, cb), lambda i, k, nc=nc: (i, jnp.minimum(k, nc - 1))))
        out_shape.append(jax.ShapeDtypeStruct(w.shape[1:], BF16))
    return pl.pallas_call(
        functools.partial(_ffn_body, n_cast=len(cast)),
        grid=(ni, nk),
        in_specs=in_specs,
        out_specs=out_specs,
        out_shape=out_shape,
        scratch_shapes=[pltpu.VMEM((tm, d), BF16)],
        compiler_params=_params(2),
        name="ffn",
    )(*args)


def _proj_body(x_ref, xp_ref, xn_ref, g_ref, w_ref, cw_ref, cb_ref, cs_ref,
               hy_ref, pq_ref, po_ref, gt_ref, u_ref, *, tm, seq, n_hy, fw):
    i = pl.program_id(0)
    j = pl.program_id(1)

    @pl.when(j == 0)
    def _():
        g = g_ref[...]
        has_prev = lax.rem(i * tm, seq) != 0
        has_next = lax.rem((i + 1) * tm, seq) != 0
        u_ref[pl.ds(0, HALO), :] = jnp.where(has_prev, _rms(xp_ref[...], g), 0.0).astype(BF16)
        chunk = ROW_CHUNK if tm % ROW_CHUNK == 0 else tm

        def pre(c, carry):
            r = pl.multiple_of(c * chunk, chunk)
            u_ref[pl.ds(pl.multiple_of(HALO + r, HALO), chunk), :] = (
                _rms(x_ref[pl.ds(r, chunk), :], g).astype(BF16))
            return carry

        lax.fori_loop(0, tm // chunk, pre, 0)
        u_ref[pl.ds(HALO + tm, HALO), :] = jnp.where(has_next, _rms(xn_ref[...], g), 0.0).astype(BF16)

    @pl.when(j < n_hy)
    def _():
        h = _dot(u_ref[...], w_ref[...])
        rows = h.shape[0]
        prev = pltpu.roll(h, 1, axis=0)[HALO:HALO + tm]
        nxt = pltpu.roll(h, rows - 1, axis=0)[HALO:HALO + tm]
        y = cb_ref[...] + prev * cw_ref[0:1, :]
        y = y + h[HALO:HALO + tm] * cw_ref[1:2, :]
        y = y + nxt * cw_ref[2:3, :]
        hy_ref[...] = y.astype(BF16)

    @pl.when(j == n_hy)
    def _():
        c = _dot(u_ref[pl.ds(HALO, tm), :], w_ref[...])
        po_ref[...] = c[:, fw:]
        fb = c[:, :fw].astype(BF16)
        gw = fw // FNET_GROUPS
        res = [_dot(fb[:, q * gw:(q + 1) * gw], cs_ref[...]) for q in range(FNET_GROUPS)]
        pq_ref[...] = jnp.concatenate([r[:, :gw] for r in res] + [r[:, gw:] for r in res], axis=1)

    @pl.when(j > n_hy)
    def _():
        c = _dot(u_ref[pl.ds(HALO, tm), :], w_ref[...])
        gt_ref[...] = _sigmoid(c).astype(BF16)


def _proj(x2, g, w_in, conv_w, conv_b, cs, *, tm, tn, seq, hy_cols, fw, pw):
    n, d = x2.shape
    cols = w_in.shape[1]
    n_hy = hy_cols // tn
    assert hy_cols % tn == 0 and fw + pw == tn and (cols - hy_cols - tn) % tn == 0
    nj = cols // tn
    n_gate = cols - hy_cols - tn
    hb = tm // HALO
    last_hb = n // HALO - 1
    body = functools.partial(_proj_body, tm=tm, seq=seq, n_hy=n_hy, fw=fw)
    return pl.pallas_call(
        body,
        grid=(n // tm, nj),
        in_specs=[
            pl.BlockSpec((tm, d), lambda i, j: (i, 0)),
            pl.BlockSpec((HALO, d), lambda i, j: (jnp.maximum(i * hb - 1, 0), 0)),
            pl.BlockSpec((HALO, d), lambda i, j: (jnp.minimum((i + 1) * hb, last_hb), 0)),
            pl.BlockSpec((1, d), lambda i, j: (0, 0)),
            pl.BlockSpec((d, tn), lambda i, j: (0, j)),
            pl.BlockSpec((3, tn), lambda i, j: (0, jnp.minimum(j, n_hy - 1))),
            pl.BlockSpec((1, tn), lambda i, j: (0, jnp.minimum(j, n_hy - 1))),
            _resident(cs.shape),
        ],
        out_specs=[
            pl.BlockSpec((tm, tn), lambda i, j: (i, jnp.minimum(j, n_hy - 1))),
            pl.BlockSpec((tm, 2 * fw), lambda i, j: (i, 0)),
            pl.BlockSpec((tm, pw), lambda i, j: (i, 0)),
            pl.BlockSpec((tm, tn), lambda i, j: (i, jnp.maximum(j - n_hy - 1, 0))),
        ],
        out_shape=[
            jax.ShapeDtypeStruct((n, hy_cols), BF16),
            jax.ShapeDtypeStruct((n, 2 * fw), F32),
            jax.ShapeDtypeStruct((n, pw), F32),
            jax.ShapeDtypeStruct((n, n_gate), BF16),
        ],
        scratch_shapes=[pltpu.VMEM((tm + 2 * HALO, d), BF16)],
        compiler_params=_params(2),
        name="mixer_proj",
    )(x2, x2, x2, g.reshape(1, d), w_in, conv_w, conv_b.reshape(1, hy_cols), cs)


def _hyena_slot_block(s, na):
    half = na // 2
    return s if s <= half else (half + 1) + (s - half)


def _hyena_mats(seq):
    nb = HYENA_NB
    na = 2 * seq // nb
    half = na // 2
    eye = np.eye(V7X_SUBLANES)
    a = np.arange(half)[None, :]
    f1 = np.zeros((na, half))
    k_re = np.arange(half + 1)[:, None]
    f1[: half + 1] = np.cos(2 * np.pi * k_re * a / na)
    k_im = np.arange(1, half)[:, None]
    f1[half + 1:] = -np.sin(2 * np.pi * k_im * a / na)
    lhs1 = np.kron(f1, eye)
    b = np.arange(nb)[None, :]
    kb = np.arange(nb)[:, None]
    g = np.zeros((half + 1, 2 * nb, 2 * nb))
    for ka in range(half + 1):
        ang = 2 * np.pi * (kb * b / nb + b * ka / (2 * seq))
        gre, gim = np.cos(ang), -np.sin(ang)
        g[ka] = np.block([[gre, -gim], [gim, gre]])
    a_col = np.arange(half)[:, None]
    f2 = np.zeros((half, na))
    wgt = np.full(half + 1, 2.0)
    wgt[0] = wgt[half] = 1.0
    f2[:, : half + 1] = wgt[None, :] * np.cos(2 * np.pi * a_col * np.arange(half + 1)[None, :] / na)
    f2[:, half + 1:] = -2.0 * np.sin(2 * np.pi * a_col * np.arange(1, half)[None, :] / na)
    lhs2 = np.kron(f2 / (2 * seq), eye)
    return (jnp.asarray(lhs1, BF16), jnp.asarray(g, BF16), jnp.asarray(lhs2, BF16))


def _fnet_mats(seq, group):
    na = int(round(math.sqrt(seq)))
    nb = seq // na
    assert na * nb == seq and nb % V7X_SUBLANES == 0
    eye = np.eye(V7X_SUBLANES)
    n2 = np.arange(nb)[None, :]
    k2 = np.arange(nb)[:, None]
    ang = 2 * np.pi * k2 * n2 / nb
    c, s = np.cos(ang), np.sin(ang)
    blk = np.block([[c, s], [-s, c]])
    a1 = np.zeros((2, V7X_SUBLANES, nb, 2, nb, V7X_SUBLANES))
    for po in range(2):
        for pi_ in range(2):
            sub = blk[po * nb:(po + 1) * nb, pi_ * nb:(pi_ + 1) * nb]
            for lo in range(V7X_SUBLANES):
                a1[po, lo, :, pi_, :, lo] = sub
    a1 = a1.reshape(2 * V7X_SUBLANES * nb, 2 * V7X_SUBLANES * nb)
    a2 = np.zeros((nb // V7X_SUBLANES, na, V7X_SUBLANES, 2, na, V7X_SUBLANES))
    k1 = np.arange(na)[:, None]
    n1 = np.arange(na)[None, :]
    for hi in range(nb // V7X_SUBLANES):
        for lo in range(V7X_SUBLANES):
            k2v = hi * V7X_SUBLANES + lo
            phi = 2 * np.pi * (n1 * k1 / na + n1 * k2v / seq)
            a2[hi, :, lo, 0, :, lo] = np.cos(phi)
            a2[hi, :, lo, 1, :, lo] = np.sin(phi)
    a2 = a2.reshape(nb // V7X_SUBLANES, na * V7X_SUBLANES, 2 * na * V7X_SUBLANES) / math.sqrt(seq)
    cc = np.arange(group)[:, None]
    mm = np.arange(group)[None, :]
    angc = 2 * np.pi * cc * mm / group
    cs = np.concatenate([np.cos(angc), -np.sin(angc)], axis=1) / math.sqrt(group)
    return jnp.asarray(a1, BF16), jnp.asarray(a2, BF16), jnp.asarray(cs, BF16), na, nb


def _static_loop(n, body):
    for i in range(n):
        body(i, 0)


def _gather_tiles(ref, starts):
    tiles = [ref[pl.ds(pl.multiple_of(s, V7X_SUBLANES), V7X_SUBLANES), :] for s in starts]
    return jnp.concatenate(tiles, axis=0)


def _hyena_stage1(z_ref, e_ref, lhs1_ref, seq):
    nb = HYENA_NB
    na = 2 * seq // nb
    half = na // 2

    def body(i, carry):
        off = i * V7X_SUBLANES
        xg = _gather_tiles(z_ref, [nb * a + off for a in range(half)]).astype(BF16)
        out = _dot(lhs1_ref[...], xg)
        for s in range(na):
            row = _hyena_slot_block(s, na) * nb + off
            e_ref[pl.ds(pl.multiple_of(row, V7X_SUBLANES), V7X_SUBLANES), :] = (
                out[s * V7X_SUBLANES:(s + 1) * V7X_SUBLANES])
        return carry

    _static_loop(nb // V7X_SUBLANES, body)


def _hyena_zero_unused(e_ref, seq):
    nb = HYENA_NB
    half = seq // nb
    zero = jnp.zeros((nb, e_ref.shape[1]), F32)
    e_ref[pl.ds((half + 1) * nb, nb), :] = zero
    e_ref[pl.ds((2 * half + 1) * nb, nb), :] = zero


def _hyena_load_ka(e_ref, ka, seq):
    nb = HYENA_NB
    half = seq // nb
    re = e_ref[pl.ds(pl.multiple_of(ka * nb, nb), nb), :]
    im = e_ref[pl.ds(pl.multiple_of((half + 1 + ka) * nb, nb), nb), :]
    return jnp.concatenate([re, im], axis=0)


def _filter_mlp_body(bands_ref, w1t_ref, w1c_ref, w1s_ref, b1_ref, w2_ref, b2_ref, w3_ref, b3_ref,
                     fr_ref, o_ref, *, seq):
    idx = lax.broadcasted_iota(jnp.int32, (1, 2 * seq), 1)
    p = jnp.where(idx < seq, idx, 2 * seq - idx).astype(F32)
    t = p / (seq - 1.0)
    ang = bands_ref[...] * ((2.0 * math.pi / seq) * p)
    fr = fr_ref[...]
    pre = (w1t_ref[...] * t
           + _dot3(w1c_ref[...], jnp.cos(ang))
           + _dot3(w1s_ref[...], -jnp.sin(ang))
           + b1_ref[...])
    h = jnp.sin(fr * pre)
    h = jnp.sin(fr * (_dot3(w2_ref[...], h) + b2_ref[...]))
    h = jnp.sin(fr * (_dot3(w3_ref[...], h) + b3_ref[...]))
    o_ref[...] = jnp.concatenate([h[:, :seq], h[:, seq:]], axis=0).T


def _filter_mlp(fw1, fb1, fw2, fb2, fw3, fb3, freq, *, seq):
    hid = fw2.shape[0]
    bands = np.linspace(1e-4, FILTER_BANDS - 1, FILTER_BANDS).astype(np.float32).reshape(-1, 1)
    col = lambda v: v.reshape(hid, 1)
    args = (jnp.asarray(bands), fw1[0].reshape(hid, 1), fw1[1:1 + FILTER_BANDS].T,
            fw1[1 + FILTER_BANDS:].T, col(fb1), fw2.T, col(fb2), fw3.T, col(fb3), col(freq))
    return pl.pallas_call(
        functools.partial(_filter_mlp_body, seq=seq),
        grid=(1,),
        in_specs=[_resident(a.shape) for a in args],
        out_specs=pl.BlockSpec((seq, 2 * hid), lambda i: (0, 0)),
        out_shape=jax.ShapeDtypeStruct((seq, 2 * hid), F32),
        compiler_params=_params(1),
        name="filter_mlp",
    )(*args)


def _filter_spec_body(h3_ref, w4f_ref, w4b_ref, dl_ref, d_ref, lhs1_ref, g_ref,
                      o_ref, z_ref, e_ref, *, seq, cw):
    nb = HYENA_NB
    half = seq // nb
    dl = dl_ref[...]
    zero = jnp.zeros_like(w4f_ref[...])
    w4 = jnp.concatenate([jnp.concatenate([w4f_ref[...], zero], axis=1),
                          jnp.concatenate([zero, w4b_ref[...]], axis=1)], axis=0)
    lane = lax.broadcasted_iota(jnp.int32, (1, 2 * cw), 1)
    chunk = ROW_CHUNK if seq % ROW_CHUNK == 0 else seq
    for c in range(seq // chunk):
        row = c * chunk + lax.broadcasted_iota(jnp.int32, (chunk, 1), 0)
        rowf = row.astype(F32)
        h4 = _dot3(h3_ref[pl.ds(c * chunk, chunk), :], w4)
        dec = jnp.concatenate([jnp.exp(-(rowf / (seq - 1.0)) * dl),
                               jnp.exp(-((seq - rowf) / (seq - 1.0)) * dl)], axis=1)
        z_ref[pl.ds(c * chunk, chunk), :] = jnp.where((row == 0) & (lane >= cw), 0.0, h4 * dec)

    _hyena_zero_unused(e_ref, seq)
    _hyena_stage1(z_ref, e_ref, lhs1_ref, seq)
    dsk = d_ref[0]

    def body(ka, carry):
        ein = _hyena_load_ka(e_ref, ka, seq).astype(BF16)
        x = _dot(g_ref[ka], ein)
        sgn = 1.0 - 2.0 * (ka % 2)
        hsp = x[:, 0:cw] + sgn * x[:, cw:2 * cw]
        part = lax.broadcasted_iota(jnp.int32, (2 * nb, 1), 0)
        hsp = hsp + jnp.where(part < nb, dsk, 0.0)
        o_ref[0, pl.ds(pl.multiple_of(ka * 2 * nb, 2 * nb), 2 * nb), :] = hsp.astype(BF16)
        return carry

    _static_loop(half + 1, body)


def _filter_spectrum(h3, fw4, d_skip, lhs1, gmat, *, seq, width, cw, ct):
    hid = h3.shape[1] // 2
    orders = d_skip.shape[0]
    nct = width // cw
    per = ct // cw
    nb = HYENA_NB
    half = seq // nb
    max_decay = math.log(DECAY_TARGET) / FAST_DECAY_PCT
    min_decay = math.log(DECAY_TARGET) / SLOW_DECAY_PCT
    deltas = np.abs(np.linspace(min_decay, max_decay, width)).astype(np.float32).reshape(1, width)
    rows = (half + 1) * 2 * nb
    body = functools.partial(_filter_spec_body, seq=seq, cw=cw)
    return pl.pallas_call(
        body,
        grid=(orders, nct),
        in_specs=[
            _resident(h3.shape),
            pl.BlockSpec((hid, cw), lambda o, c: (0, o * 2 * nct + c)),
            pl.BlockSpec((hid, cw), lambda o, c: (0, o * 2 * nct + nct + c)),
            pl.BlockSpec((1, cw), lambda o, c: (0, c)),
            pl.BlockSpec((1, 1, cw), lambda o, c: (o, 0, c)),
            _resident(lhs1.shape),
            _resident(gmat.shape),
        ],
        out_specs=pl.BlockSpec((None, 1, rows, cw), lambda o, c: (c // per, o, 0, c % per)),
        out_shape=jax.ShapeDtypeStruct((width // ct, orders, rows, ct), BF16),
        scratch_shapes=[pltpu.VMEM((seq, 2 * cw), F32),
                        pltpu.VMEM((2 * (half + 1) * nb, 2 * cw), F32)],
        compiler_params=_params(2),
        name="filter_spectrum",
    )(h3, fw4, fw4, jnp.asarray(deltas), d_skip.reshape(orders, 1, width), lhs1, gmat)


def _hyena_body(v_ref, g1_ref, g2_ref, h_ref, lhs1_ref, g_ref, lhs2_ref, o_ref, z_ref, e_ref, *, seq):
    nb = HYENA_NB
    na = 2 * seq // nb
    half = na // 2
    pair = V7X_BF16_ROWS
    chunk = 512 if seq % 512 == 0 else seq

    def load_v(c, carry):
        r = pl.multiple_of(c * chunk, chunk)
        z_ref[pl.ds(r, chunk), :] = v_ref[pl.ds(r, chunk), :].astype(F32)
        return carry

    lax.fori_loop(0, seq // chunk, load_v, 0)
    _hyena_zero_unused(e_ref, seq)

    for order, gate_ref in enumerate((g1_ref, g2_ref)):
        _hyena_stage1(z_ref, e_ref, lhs1_ref, seq)

        group = 3 if (half + 1) % 3 == 0 else 1

        def spectral(it, carry):
            kas = [it * group + u for u in range(group)]
            eins = [_hyena_load_ka(e_ref, ka, seq).astype(BF16) for ka in kas]
            ys = []
            for ka, ein in zip(kas, eins):
                gk = g_ref[ka]
                x = _dot(gk, ein)
                hsp = h_ref[order, pl.ds(pl.multiple_of(ka * 2 * nb, 2 * nb), 2 * nb), :].astype(F32)
                xre, xim, hre, him = x[:nb], x[nb:], hsp[:nb], hsp[nb:]
                zsp = jnp.concatenate([xre * hre - xim * him, xre * him + xim * hre], axis=0).astype(BF16)
                ys.append(lax.dot_general(gk, zsp, (((0,), (0,)), ((), ())), preferred_element_type=F32))
            for ka, y in zip(kas, ys):
                e_ref[pl.ds(pl.multiple_of(ka * nb, nb), nb), :] = y[:nb]
                keep = jnp.where((ka == 0) | (ka == half), 0.0, 1.0)
                e_ref[pl.ds(pl.multiple_of((half + 1 + ka) * nb, nb), nb), :] = y[nb:] * keep
            return carry

        _static_loop((half + 1) // group, spectral)

        def inverse(i, carry):
            outs = []
            for q in range(pair // V7X_SUBLANES):
                off = i * pair + q * V7X_SUBLANES
                yg = _gather_tiles(e_ref, [_hyena_slot_block(s, na) * nb + off for s in range(na)])
                outs.append(_dot(lhs2_ref[...], yg.astype(BF16)))
            for a in range(half):
                sl = slice(a * V7X_SUBLANES, (a + 1) * V7X_SUBLANES)
                conv = jnp.concatenate([o_[sl] for o_ in outs], axis=0)
                r = pl.multiple_of(a * nb + i * pair, pair)
                res = gate_ref[pl.ds(r, pair), :].astype(F32) * conv
                if order == 0:
                    z_ref[pl.ds(r, pair), :] = res
                else:
                    o_ref[pl.ds(r, pair), :] = res.astype(BF16)
            return carry

        _static_loop(nb // pair, inverse)


def _hyena(hyc, hspec, lhs1, gmat, lhs2, *, seq, width, ct):
    n = hyc.shape[0]
    nct = width // ct
    nb = HYENA_NB
    half = seq // nb
    return pl.pallas_call(
        functools.partial(_hyena_body, seq=seq),
        grid=(nct, n // seq),
        in_specs=[
            pl.BlockSpec((seq, ct), lambda c, b: (b, c)),
            pl.BlockSpec((seq, ct), lambda c, b: (b, nct + c)),
            pl.BlockSpec((seq, ct), lambda c, b: (b, 2 * nct + c)),
            pl.BlockSpec((None,) + hspec.shape[1:], lambda c, b: (c, 0, 0, 0)),
            _resident(lhs1.shape),
            _resident(gmat.shape),
            _resident(lhs2.shape),
        ],
        out_specs=pl.BlockSpec((seq, ct), lambda c, b: (b, c)),
        out_shape=jax.ShapeDtypeStruct((n, width), BF16),
        scratch_shapes=[pltpu.VMEM((seq, ct), F32),
                        pltpu.VMEM((2 * (half + 1) * nb, ct), F32)],
        compiler_params=_params(2),
        name="hyena",
    )(hyc, hyc, hyc, hspec, lhs1, gmat, lhs2)


def _fnet_body(p_ref, q_ref, a1_ref, a2_ref, o_ref, t_ref, *, seq, na, nb):
    sub = V7X_SUBLANES

    def stage1(i, carry):
        off = i * sub
        starts = [na * n2 + off for n2 in range(nb)]
        xin = jnp.concatenate([_gather_tiles(p_ref, starts), _gather_tiles(q_ref, starts)],
                              axis=0).astype(BF16)
        out = _dot(a1_ref[...], xin)
        for part in range(2):
            for lo in range(sub):
                src = (part * sub + lo) * nb
                dst = part * seq + (off + lo) * nb
                t_ref[pl.ds(pl.multiple_of(dst, sub), nb), :] = out[src:src + nb]
        return carry

    _static_loop(na // sub, stage1)

    def stage2(i, carry):
        off = i * sub
        starts = [part * seq + n1 * nb + off for part in range(2) for n1 in range(na)]
        tin = _gather_tiles(t_ref, starts).astype(BF16)
        out = _dot(a2_ref[i], tin)
        for k1 in range(na):
            o_ref[pl.ds(pl.multiple_of(k1 * nb + off, sub), sub), :] = out[k1 * sub:(k1 + 1) * sub]
        return carry

    _static_loop(nb // sub, stage2)


def _fnet(pq, a1, a2, *, seq, fw, ct, na, nb):
    n = pq.shape[0]
    nct = fw // ct
    return pl.pallas_call(
        functools.partial(_fnet_body, seq=seq, na=na, nb=nb),
        grid=(n // seq, nct),
        in_specs=[
            pl.BlockSpec((seq, ct), lambda b, c: (b, c)),
            pl.BlockSpec((seq, ct), lambda b, c: (b, nct + c)),
            _resident(a1.shape),
            _resident(a2.shape),
        ],
        out_specs=pl.BlockSpec((seq, ct), lambda b, c: (b, c)),
        out_shape=jax.ShapeDtypeStruct((n, fw), F32),
        scratch_shapes=[pltpu.VMEM((2 * seq, ct), F32)],
        compiler_params=_params(2),
        name="fnet_seq",
    )(pq, pq, a1, a2)


def _merge_body(x_ref, z_ref, f_ref, po_ref, pp_ref, pn_ref, gt_ref, wha_ref, wfo_ref, wpl_ref,
                ps_ref, wo_ref, gpost_ref, o_ref, ext_ref, m_ref, *, tm, seq, d):
    i = pl.program_id(0)
    sub = V7X_SUBLANES
    has_prev = lax.rem(i * tm, seq) != 0
    has_next = lax.rem((i + 1) * tm, seq) != 0
    ext_ref[pl.ds(0, sub), :] = jnp.where(has_prev, pp_ref[...], 0.0)
    ext_ref[pl.ds(sub, tm), :] = po_ref[...]
    ext_ref[pl.ds(sub + tm, sub), :] = jnp.where(has_next, pn_ref[...], 0.0)

    gate = lambda q, lo, hi: gt_ref[:, q * d + lo:q * d + hi].astype(F32)
    m_ref[...] = gate(0, 0, d) * _dot(z_ref[...], wha_ref[...])
    m_ref[...] += gate(1, 0, d) * _dot(f_ref[...].astype(BF16), wfo_ref[...])

    pos = lax.rem(i * tm, seq) + lax.broadcasted_iota(jnp.int32, (tm, 1), 0)
    n_pool = len(POOL_WINDOWS)
    gw = po_ref.shape[1] // n_pool
    ow = d // n_pool
    for q, w in enumerate(POOL_WINDOWS):
        before = w // 2
        after = w - 1 - before
        cols = slice(q * gw, (q + 1) * gw)
        tot = ext_ref[pl.ds(sub - before, tm), cols]
        for s in range(-before + 1, after + 1):
            tot = tot + ext_ref[pl.ds(sub + s, tm), cols]
        lo = jnp.maximum(pos - before, 0)
        hi = jnp.minimum(pos + after, seq - 1)
        cnt = (hi - lo + 1).astype(F32)
        mq = (tot / cnt - po_ref[:, cols]).astype(BF16)
        yc = _dot(mq, wpl_ref[q]) * ps_ref[:, q * ow:(q + 1) * ow]
        m_ref[:, q * ow:(q + 1) * ow] += gate(2, q * ow, (q + 1) * ow) * yc

    y = _dot(m_ref[...].astype(BF16), wo_ref[...])
    o_ref[...] = x_ref[...] + _rms(y, gpost_ref[...])


def _merge(x2, z2, fy, po, gates, wha, wfo, wpool, pscale, w_out, g_post, layer, *, tm, seq):
    n, d = x2.shape
    pw = po.shape[1]
    sub = V7X_SUBLANES
    hb = tm // sub
    last_hb = n // sub - 1
    body = functools.partial(_merge_body, tm=tm, seq=seq, d=d)
    return pl.pallas_call(
        body,
        grid=(n // tm,),
        in_specs=[
            pl.BlockSpec((tm, d), lambda i: (i, 0)),
            pl.BlockSpec((tm, z2.shape[1]), lambda i: (i, 0)),
            pl.BlockSpec((tm, fy.shape[1]), lambda i: (i, 0)),
            pl.BlockSpec((tm, pw), lambda i: (i, 0)),
            pl.BlockSpec((sub, pw), lambda i: (jnp.maximum(i * hb - 1, 0), 0)),
            pl.BlockSpec((sub, pw), lambda i: (jnp.minimum((i + 1) * hb, last_hb), 0)),
            pl.BlockSpec((tm, gates.shape[1]), lambda i: (i, 0)),
            _resident_layer(wha, layer),
            _resident_layer(wfo, layer),
            _resident_layer(wpool, layer),
            _resident((1, d)),
            _resident(w_out.shape),
            _resident((1, d)),
        ],
        out_specs=pl.BlockSpec((tm, d), lambda i: (i, 0)),
        out_shape=jax.ShapeDtypeStruct((n, d), F32),
        scratch_shapes=[pltpu.VMEM((tm + 2 * sub, pw), F32), pltpu.VMEM((tm, d), F32)],
        compiler_params=_params(1),
        name="mixer_merge",
    )(x2, z2, fy, po, po, po, gates, wha, wfo, wpool, pscale.reshape(1, d), w_out, g_post.reshape(1, d))


def _kv_body(m_ref, g_ref, w_ref, o_ref):
    o_ref[...] = _dot(_rms(m_ref[...], g_ref[...]).astype(BF16), w_ref[...].astype(BF16)).astype(BF16)


def _kv(mem2, g, w_kv, layer, *, n_mem, tn):
    n, d = mem2.shape
    cols = w_kv.shape[2]
    return pl.pallas_call(
        _kv_body,
        grid=(cols // tn, n // n_mem),
        in_specs=[
            pl.BlockSpec((n_mem, d), lambda j, b: (b, 0)),
            pl.BlockSpec((1, d), lambda j, b: (0, 0)),
            pl.BlockSpec((None, d, tn), lambda j, b: (layer, 0, j)),
        ],
        out_specs=pl.BlockSpec((n_mem, tn), lambda j, b: (b, j)),
        out_shape=jax.ShapeDtypeStruct((n, cols), BF16),
        compiler_params=_params(2),
        name="mem_kv",
    )(mem2, g.reshape(1, d), w_kv)


def _attn_body(x_ref, gpre_ref, wq_ref, k_ref, v_ref, wo_ref, gpost_ref, o_ref, *, d):
    dh = d // N_HEADS
    x = x_ref[...]
    q = _dot(_rms(x, gpre_ref[...]).astype(BF16), wq_ref[...])
    heads = []
    for h in range(N_HEADS):
        cols = slice(h * dh, (h + 1) * dh)
        s = lax.dot_general(q[:, cols].astype(BF16), k_ref[:, cols], (((1,), (1,)), ((), ())),
                            preferred_element_type=F32) * (dh ** -0.5)
        e = jnp.exp(s - jnp.max(s, axis=-1, keepdims=True))
        p = e / jnp.sum(e, axis=-1, keepdims=True)
        heads.append(_dot(p.astype(BF16), v_ref[:, cols]))
    o = jnp.concatenate(heads, axis=1).astype(BF16)
    o_ref[...] = x + _rms(_dot(o, wo_ref[...]), gpost_ref[...])


def _attn(x2, g_pre, w_q, kv, w_o, g_post, *, tm, seq, n_mem):
    n, d = x2.shape
    tiles_per_seq = seq // tm
    return pl.pallas_call(
        functools.partial(_attn_body, d=d),
        grid=(n // tm,),
        in_specs=[
            pl.BlockSpec((tm, d), lambda i: (i, 0)),
            _resident((1, d)),
            _resident(w_q.shape),
            pl.BlockSpec((n_mem, d), lambda i: (i // tiles_per_seq, 0)),
            pl.BlockSpec((n_mem, d), lambda i: (i // tiles_per_seq, 1)),
            _resident(w_o.shape),
            _resident((1, d)),
        ],
        out_specs=pl.BlockSpec((tm, d), lambda i: (i, 0)),
        out_shape=jax.ShapeDtypeStruct((n, d), F32),
        compiler_params=_params(1),
        name="mem_attn",
    )(x2, g_pre.reshape(1, d), w_q, kv, kv, w_o, g_post.reshape(1, d))


def kernel(x, mem, g_ffn1_pre, w_ffn1_gu, w_ffn1_down, g_ffn1_post, g_mix_pre, w_in, hyena_conv_w,
           hyena_conv_b, filt_w1, filt_b1, filt_w2, filt_b2, filt_w3, filt_b3, filt_w4, filt_freq,
           hyena_d, w_hyena_out, w_fnet_out, w_pool, pool_scale, w_out, g_mix_post, g_mem_pre,
           g_mem_kv, w_q, w_kv, w_o, g_mem_post, g_ffn2_pre, w_ffn2_gu, w_ffn2_down, g_ffn2_post):
    batch, seq, d = x.shape
    n_mem = mem.shape[1]
    depth = w_in.shape[0]
    hy_w = w_hyena_out.shape[1]
    hy_cols = hyena_conv_w.shape[2]
    fw = w_fnet_out.shape[1]
    pw = w_pool.shape[1] * w_pool.shape[2]

    tm = 512
    tn = fw + pw
    tf = 512
    ct = V7X_MXU_COLS

    lhs1, gmat, lhs2 = _hyena_mats(seq)
    a1, a2, cs, fna, fnb = _fnet_mats(seq, fw // FNET_GROUPS)

    bf = lambda w: w.astype(BF16)
    w_hyena_out, w_fnet_out, w_pool = map(bf, (w_hyena_out, w_fnet_out, w_pool))
    ffn_w = (bf(w_ffn1_gu[0]), bf(w_ffn1_down[0]))
    x2 = x.reshape(batch * seq, d)
    mem2 = mem.reshape(batch * n_mem, d)
    for l in range(depth):
        later = (w_ffn2_gu, w_ffn2_down, w_in, w_out, w_q, w_o)
        x2, *cast = _ffn(x2, g_ffn1_pre[l], *ffn_w, g_ffn1_post[l], tm=2 * tm, tf=tf,
                         cast=[(w, l) for w in later])
        *ffn_w, w_in_l, w_out_l, w_q_l, w_o_l = cast

        hyc, pq, po, gates = _proj(x2, g_mix_pre[l], w_in_l, hyena_conv_w[l], hyena_conv_b[l], cs,
                                   tm=2 * tm, tn=tn, seq=seq, hy_cols=hy_cols, fw=fw, pw=pw)
        h3 = _filter_mlp(filt_w1[l], filt_b1[l], filt_w2[l], filt_b2[l], filt_w3[l], filt_b3[l],
                         filt_freq[l], seq=seq)
        hspec = _filter_spectrum(h3, filt_w4[l], hyena_d[l], lhs1, gmat, seq=seq, width=hy_w, cw=ct // 2, ct=ct)
        z2 = _hyena(hyc, hspec, lhs1, gmat, lhs2, seq=seq, width=hy_w, ct=ct)
        fy = _fnet(pq, a1, a2, seq=seq, fw=fw, ct=ct, na=fna, nb=fnb)
        x2 = _merge(x2, z2, fy, po, gates, w_hyena_out, w_fnet_out, w_pool,
                    pool_scale[l], w_out_l, g_mix_post[l], l, tm=tm, seq=seq)

        kv = _kv(mem2, g_mem_kv[l], w_kv, l, n_mem=batch * n_mem, tn=tn)
        x2 = _attn(x2, g_mem_pre[l], w_q_l, kv, w_o_l, g_mem_post[l], tm=tm, seq=seq, n_mem=n_mem)

        nxt = [(w_ffn1_gu, l + 1), (w_ffn1_down, l + 1)] if l + 1 < depth else []
        x2, *ffn_w = _ffn(x2, g_ffn2_pre[l], *ffn_w, g_ffn2_post[l], tm=2 * tm, tf=tf, cast=nxt)
    return x2.reshape(batch, seq, d)
```

```python
import functools
import math

import jax
import jax.numpy as jnp
import numpy as np
from jax import lax
from jax.experimental import pallas as pl
from jax.experimental.pallas import tpu as pltpu

F32 = jnp.float32
BF16 = jnp.bfloat16

RMS_EPS = 1e-6
MACARON_WEIGHT = 0.5
N_HEADS = 4
POOL_WINDOWS = (2, 4, 8, 16)
FNET_GROUPS = 4
FILTER_BANDS = 16
DECAY_TARGET = 1e-2
FAST_DECAY_PCT = 0.3
SLOW_DECAY_PCT = 1.5

V7X_SUBLANES = 8
V7X_BF16_ROWS = 16
V7X_MXU_COLS = 256
V7X_VMEM_LIMIT = 60 * 2**20

HYENA_NB = 128
HALO = V7X_BF16_ROWS
ROW_CHUNK = 256
LANE_BLOCK = 128


def _params(n_axes):
    return pltpu.CompilerParams(
        dimension_semantics=("arbitrary",) * n_axes,
        vmem_limit_bytes=V7X_VMEM_LIMIT,
    )


def _resident(shape):
    zeros = (0,) * len(shape)
    return pl.BlockSpec(shape, lambda *_: zeros, pipeline_mode=pl.Buffered(1))


def _resident_layer(stacked, layer):
    tail = (0,) * (stacked.ndim - 1)
    return pl.BlockSpec((None,) + stacked.shape[1:], lambda *_: (layer,) + tail,
                        pipeline_mode=pl.Buffered(1))


def _rms(x, g):
    ms = jnp.mean(x * x, axis=-1, keepdims=True)
    return x * lax.rsqrt(ms + RMS_EPS) * g


def _sigmoid(x):
    return 0.5 * jnp.tanh(0.5 * x) + 0.5


def _dot(a, b):
    return jnp.dot(a, b, preferred_element_type=F32)


def _dot3(a, b):
    ah = a.astype(BF16)
    al = (a - ah.astype(F32)).astype(BF16)
    bh = b.astype(BF16)
    bl = (b - bh.astype(F32)).astype(BF16)
    return _dot(ah, bh) + (_dot(al, bh) + _dot(ah, bl))


def _row_chunks(rows, fn):
    chunk = ROW_CHUNK if rows % ROW_CHUNK == 0 else rows

    def body(c, carry):
        fn(pl.ds(pl.multiple_of(c * chunk, chunk), chunk))
        return carry

    lax.fori_loop(0, rows // chunk, body, 0)


def _norm_residual_inplace(o_ref, x_ref, g):
    rows, d = o_ref.shape

    def post(r):
        sq = None
        for c in range(0, d, LANE_BLOCK):
            yb = o_ref[r, c:c + LANE_BLOCK]
            sq = yb * yb if sq is None else sq + yb * yb
        inv = lax.rsqrt(jnp.sum(sq, axis=-1, keepdims=True) * (1.0 / d) + RMS_EPS)
        for c in range(0, d, LANE_BLOCK):
            cols = slice(c, c + LANE_BLOCK)
            o_ref[r, cols] = x_ref[r, cols] + o_ref[r, cols] * inv * g[:, cols]

    _row_chunks(rows, post)


def _ffn_body(x_ref, gpre_ref, wg_ref, wu_ref, wd_ref, gpost_ref, *rest, n_cast):
    cast_in, o_ref, cast_out, u_ref = rest[:n_cast], rest[n_cast], rest[n_cast + 1:-1], rest[-1]
    k = pl.program_id(1)
    rows = x_ref.shape[0]

    @pl.when(k == 0)
    def _():
        def pre(r):
            u_ref[r, :] = _rms(x_ref[r, :], gpre_ref[...]).astype(BF16)
            o_ref[r, :] = jnp.zeros((r.size, o_ref.shape[1]), F32)
        _row_chunks(rows, pre)

    u = u_ref[...]
    a = _dot(u, wg_ref[...])
    b = _dot(u, wu_ref[...])
    h = (a * _sigmoid(a) * b).astype(BF16)
    for src, dst in zip(cast_in, cast_out):
        dst[...] = src[...].astype(BF16)
    o_ref[...] += _dot(h, wd_ref[...])

    @pl.when(k == pl.num_programs(1) - 1)
    def _():
        _norm_residual_inplace(o_ref, x_ref, MACARON_WEIGHT * gpost_ref[...])


def _cast_tiling(rows, cols, ni, nk):
    rb = rows // ni
    assert rows % ni == 0 and rb % V7X_BF16_ROWS == 0
    nc = max(c for c in range(1, nk + 1) if cols % c == 0 and (cols // c) % 128 == 0)
    return rb, cols // nc, nc


def _ffn(x2, g_pre, w_gu, w_down, g_post, *, tm, tf, cast=()):
    n, d = x2.shape
    f = w_down.shape[0]
    ni, nk = n // tm, f // tf
    in_specs = [
        pl.BlockSpec((tm, d), lambda i, k: (i, 0)),
        pl.BlockSpec((1, d), lambda i, k: (0, 0)),
        pl.BlockSpec((d, tf), lambda i, k: (0, k)),
        pl.BlockSpec((d, tf), lambda i, k: (0, k + nk)),
        pl.BlockSpec((tf, d), lambda i, k: (k, 0)),
        pl.BlockSpec((1, d), lambda i, k: (0, 0)),
    ]
    args = [x2, g_pre.reshape(1, d), w_gu, w_gu, w_down, g_post.reshape(1, d)]
    out_specs = [pl.BlockSpec((tm, d), lambda i, k: (i, 0))]
    out_shape = [jax.ShapeDtypeStruct((n, d), F32)]
    for w, layer in cast:
        rb, cb, nc = _cast_tiling(w.shape[1], w.shape[2], ni, nk)
        in_specs.append(pl.BlockSpec((None, rb, cb),
                                     lambda i, k, layer=layer, nc=nc: (layer, i, jnp.minimum(k, nc - 1))))
        args.append(w)
        out_specs.append(pl.BlockSpec((rb, cb), lambda i, k, nc=nc: (i, jnp.minimum(k, nc - 1))))
        out_shape.append(jax.ShapeDtypeStruct(w.shape[1:], BF16))
    return pl.pallas_call(
        functools.partial(_ffn_body, n_cast=len(cast)),
        grid=(ni, nk),
        in_specs=in_specs,
        out_specs=out_specs,
        out_shape=out_shape,
        scratch_shapes=[pltpu.VMEM((tm, d), BF16)],
        compiler_params=_params(2),
        name="ffn",
    )(*args)


def _proj_body(x_ref, xp_ref, xn_ref, g_ref, w_ref, cw_ref, cb_ref, cs_ref,
               hy_ref, pq_ref, po_ref, gt_ref, u_ref, *, tm, seq, n_hy, fw):
    i = pl.program_id(0)
    j = pl.program_id(1)

    @pl.when(j == 0)
    def _():
        g = g_ref[...]
        has_prev = lax.rem(i * tm, seq) != 0
        has_next = lax.rem((i + 1) * tm, seq) != 0
        u_ref[pl.ds(0, HALO), :] = jnp.where(has_prev, _rms(xp_ref[...], g), 0.0).astype(BF16)
        chunk = ROW_CHUNK if tm % ROW_CHUNK == 0 else tm

        def pre(c, carry):
            r = pl.multiple_of(c * chunk, chunk)
            u_ref[pl.ds(pl.multiple_of(HALO + r, HALO), chunk), :] = (
                _rms(x_ref[pl.ds(r, chunk), :], g).astype(BF16))
            return carry

        lax.fori_loop(0, tm // chunk, pre, 0)
        u_ref[pl.ds(HALO + tm, HALO), :] = jnp.where(has_next, _rms(xn_ref[...], g), 0.0).astype(BF16)

    @pl.when(j < n_hy)
    def _():
        h = _dot(u_ref[...], w_ref[...])
        rows = h.shape[0]
        prev = pltpu.roll(h, 1, axis=0)[HALO:HALO + tm]
        nxt = pltpu.roll(h, rows - 1, axis=0)[HALO:HALO + tm]
        y = cb_ref[...] + prev * cw_ref[0:1, :]
        y = y + h[HALO:HALO + tm] * cw_ref[1:2, :]
        y = y + nxt * cw_ref[2:3, :]
        hy_ref[...] = y.astype(BF16)

    @pl.when(j == n_hy)
    def _():
        c = _dot(u_ref[pl.ds(HALO, tm), :], w_ref[...])
        po_ref[...] = c[:, fw:]
        fb = c[:, :fw].astype(BF16)
        gw = fw // FNET_GROUPS
        res = [_dot(fb[:, q * gw:(q + 1) * gw], cs_ref[...]) for q in range(FNET_GROUPS)]
        pq_ref[...] = jnp.concatenate([r[:, :gw] for r in res] + [r[:, gw:] for r in res], axis=1)

    @pl.when(j > n_hy)
    def _():
        c = _dot(u_ref[pl.ds(HALO, tm), :], w_ref[...])
        gt_ref[...] = _sigmoid(c).astype(BF16)


def _proj(x2, g, w_in, conv_w, conv_b, cs, *, tm, tn, seq, hy_cols, fw, pw):
    n, d = x2.shape
    cols = w_in.shape[1]
    n_hy = hy_cols // tn
    assert hy_cols % tn == 0 and fw + pw == tn and (cols - hy_cols - tn) % tn == 0
    nj = cols // tn
    n_gate = cols - hy_cols - tn
    hb = tm // HALO
    last_hb = n // HALO - 1
    body = functools.partial(_proj_body, tm=tm, seq=seq, n_hy=n_hy, fw=fw)
    return pl.pallas_call(
        body,
        grid=(n // tm, nj),
        in_specs=[
            pl.BlockSpec((tm, d), lambda i, j: (i, 0)),
            pl.BlockSpec((HALO, d), lambda i, j: (jnp.maximum(i * hb - 1, 0), 0)),
            pl.BlockSpec((HALO, d), lambda i, j: (jnp.minimum((i + 1) * hb, last_hb), 0)),
            pl.BlockSpec((1, d), lambda i, j: (0, 0)),
            pl.BlockSpec((d, tn), lambda i, j: (0, j)),
            pl.BlockSpec((3, tn), lambda i, j: (0, jnp.minimum(j, n_hy - 1))),
            pl.BlockSpec((1, tn), lambda i, j: (0, jnp.minimum(j, n_hy - 1))),
            _resident(cs.shape),
        ],
        out_specs=[
            pl.BlockSpec((tm, tn), lambda i, j: (i, jnp.minimum(j, n_hy - 1))),
            pl.BlockSpec((tm, 2 * fw), lambda i, j: (i, 0)),
            pl.BlockSpec((tm, pw), lambda i, j: (i, 0)),
            pl.BlockSpec((tm, tn), lambda i, j: (i, jnp.maximum(j - n_hy - 1, 0))),
        ],
        out_shape=[
            jax.ShapeDtypeStruct((n, hy_cols), BF16),
            jax.ShapeDtypeStruct((n, 2 * fw), F32),
            jax.ShapeDtypeStruct((n, pw), F32),
            jax.ShapeDtypeStruct((n, n_gate), BF16),
        ],
        scratch_shapes=[pltpu.VMEM((tm + 2 * HALO, d), BF16)],
        compiler_params=_params(2),
        name="mixer_proj",
    )(x2, x2, x2, g.reshape(1, d), w_in, conv_w, conv_b.reshape(1, hy_cols), cs)


def _hyena_slot_block(s, na):
    half = na // 2
    return s if s <= half else (half + 1) + (s - half)


def _hyena_mats(seq):
    nb = HYENA_NB
    na = 2 * seq // nb
    half = na // 2
    eye = np.eye(V7X_SUBLANES)
    a = np.arange(half)[None, :]
    f1 = np.zeros((na, half))
    k_re = np.arange(half + 1)[:, None]
    f1[: half + 1] = np.cos(2 * np.pi * k_re * a / na)
    k_im = np.arange(1, half)[:, None]
    f1[half + 1:] = -np.sin(2 * np.pi * k_im * a / na)
    lhs1 = np.kron(f1, eye)
    b = np.arange(nb)[None, :]
    kb = np.arange(nb)[:, None]
    g = np.zeros((half + 1, 2 * nb, 2 * nb))
    for ka in range(half + 1):
        ang = 2 * np.pi * (kb * b / nb + b * ka / (2 * seq))
        gre, gim = np.cos(ang), -np.sin(ang)
        g[ka] = np.block([[gre, -gim], [gim, gre]])
    a_col = np.arange(half)[:, None]
    f2 = np.zeros((half, na))
    wgt = np.full(half + 1, 2.0)
    wgt[0] = wgt[half] = 1.0
    f2[:, : half + 1] = wgt[None, :] * np.cos(2 * np.pi * a_col * np.arange(half + 1)[None, :] / na)
    f2[:, half + 1:] = -2.0 * np.sin(2 * np.pi * a_col * np.arange(1, half)[None, :] / na)
    lhs2 = np.kron(f2 / (2 * seq), eye)
    return (jnp.asarray(lhs1, BF16), jnp.asarray(g, BF16), jnp.asarray(lhs2, BF16))


def _fnet_mats(seq, group):
    na = int(round(math.sqrt(seq)))
    nb = seq // na
    assert na * nb == seq and nb % V7X_SUBLANES == 0
    eye = np.eye(V7X_SUBLANES)
    n2 = np.arange(nb)[None, :]
    k2 = np.arange(nb)[:, None]
    ang = 2 * np.pi * k2 * n2 / nb
    c, s = np.cos(ang), np.sin(ang)
    blk = np.block([[c, s], [-s, c]])
    a1 = np.zeros((2, V7X_SUBLANES, nb, 2, nb, V7X_SUBLANES))
    for po in range(2):
        for pi_ in range(2):
            sub = blk[po * nb:(po + 1) * nb, pi_ * nb:(pi_ + 1) * nb]
            for lo in range(V7X_SUBLANES):
                a1[po, lo, :, pi_, :, lo] = sub
    a1 = a1.reshape(2 * V7X_SUBLANES * nb, 2 * V7X_SUBLANES * nb)
    a2 = np.zeros((nb // V7X_SUBLANES, na, V7X_SUBLANES, 2, na, V7X_SUBLANES))
    k1 = np.arange(na)[:, None]
    n1 = np.arange(na)[None, :]
    for hi in range(nb // V7X_SUBLANES):
        for lo in range(V7X_SUBLANES):
            k2v = hi * V7X_SUBLANES + lo
            phi = 2 * np.pi * (n1 * k1 / na + n1 * k2v / seq)
            a2[hi, :, lo, 0, :, lo] = np.cos(phi)
            a2[hi, :, lo, 1, :, lo] = np.sin(phi)
    a2 = a2.reshape(nb // V7X_SUBLANES, na * V7X_SUBLANES, 2 * na * V7X_SUBLANES) / math.sqrt(seq)
    cc = np.arange(group)[:, None]
    mm = np.arange(group)[None, :]
    angc = 2 * np.pi * cc * mm / group
    cs = np.concatenate([np.cos(angc), -np.sin(angc)], axis=1) / math.sqrt(group)
    return jnp.asarray(a1, BF16), jnp.asarray(a2, BF16), jnp.asarray(cs, BF16), na, nb


def _static_loop(n, body):
    for i in range(n):
        body(i, 0)


def _gather_tiles(ref, starts):
    tiles = [ref[pl.ds(pl.multiple_of(s, V7X_SUBLANES), V7X_SUBLANES), :] for s in starts]
    return jnp.concatenate(tiles, axis=0)


def _hyena_stage1(z_ref, e_ref, lhs1_ref, seq):
    nb = HYENA_NB
    na = 2 * seq // nb
    half = na // 2

    def body(i, carry):
        off = i * V7X_SUBLANES
        xg = _gather_tiles(z_ref, [nb * a + off for a in range(half)]).astype(BF16)
        out = _dot(lhs1_ref[...], xg)
        for s in range(na):
            row = _hyena_slot_block(s, na) * nb + off
            e_ref[pl.ds(pl.multiple_of(row, V7X_SUBLANES), V7X_SUBLANES), :] = (
                out[s * V7X_SUBLANES:(s + 1) * V7X_SUBLANES])
        return carry

    _static_loop(nb // V7X_SUBLANES, body)


def _hyena_zero_unused(e_ref, seq):
    nb = HYENA_NB
    half = seq // nb
    zero = jnp.zeros((nb, e_ref.shape[1]), F32)
    e_ref[pl.ds((half + 1) * nb, nb), :] = zero
    e_ref[pl.ds((2 * half + 1) * nb, nb), :] = zero


def _hyena_load_ka(e_ref, ka, seq):
    nb = HYENA_NB
    half = seq // nb
    re = e_ref[pl.ds(pl.multiple_of(ka * nb, nb), nb), :]
    im = e_ref[pl.ds(pl.multiple_of((half + 1 + ka) * nb, nb), nb), :]
    return jnp.concatenate([re, im], axis=0)


def _filter_mlp_body(bands_ref, w1t_ref, w1c_ref, w1s_ref, b1_ref, w2_ref, b2_ref, w3_ref, b3_ref,
                     fr_ref, o_ref, *, seq):
    idx = lax.broadcasted_iota(jnp.int32, (1, 2 * seq), 1)
    p = jnp.where(idx < seq, idx, 2 * seq - idx).astype(F32)
    t = p / (seq - 1.0)
    ang = bands_ref[...] * ((2.0 * math.pi / seq) * p)
    fr = fr_ref[...]
    pre = (w1t_ref[...] * t
           + _dot3(w1c_ref[...], jnp.cos(ang))
           + _dot3(w1s_ref[...], -jnp.sin(ang))
           + b1_ref[...])
    h = jnp.sin(fr * pre)
    h = jnp.sin(fr * (_dot3(w2_ref[...], h) + b2_ref[...]))
    h = jnp.sin(fr * (_dot3(w3_ref[...], h) + b3_ref[...]))
    o_ref[...] = jnp.concatenate([h[:, :seq], h[:, seq:]], axis=0).T


def _filter_mlp(fw1, fb1, fw2, fb2, fw3, fb3, freq, *, seq):
    hid = fw2.shape[0]
    bands = np.linspace(1e-4, FILTER_BANDS - 1, FILTER_BANDS).astype(np.float32).reshape(-1, 1)
    col = lambda v: v.reshape(hid, 1)
    args = (jnp.asarray(bands), fw1[0].reshape(hid, 1), fw1[1:1 + FILTER_BANDS].T,
            fw1[1 + FILTER_BANDS:].T, col(fb1), fw2.T, col(fb2), fw3.T, col(fb3), col(freq))
    return pl.pallas_call(
        functools.partial(_filter_mlp_body, seq=seq),
        grid=(1,),
        in_specs=[_resident(a.shape) for a in args],
        out_specs=pl.BlockSpec((seq, 2 * hid), lambda i: (0, 0)),
        out_shape=jax.ShapeDtypeStruct((seq, 2 * hid), F32),
        compiler_params=_params(1),
        name="filter_mlp",
    )(*args)


def _filter_spec_body(h3_ref, w4f_ref, w4b_ref, dl_ref, d_ref, lhs1_ref, g_ref,
                      o_ref, z_ref, e_ref, *, seq, cw):
    nb = HYENA_NB
    half = seq // nb
    dl = dl_ref[...]
    zero = jnp.zeros_like(w4f_ref[...])
    w4 = jnp.concatenate([jnp.concatenate([w4f_ref[...], zero], axis=1),
                          jnp.concatenate([zero, w4b_ref[...]], axis=1)], axis=0)
    lane = lax.broadcasted_iota(jnp.int32, (1, 2 * cw), 1)
    chunk = ROW_CHUNK if seq % ROW_CHUNK == 0 else seq
    for c in range(seq // chunk):
        row = c * chunk + lax.broadcasted_iota(jnp.int32, (chunk, 1), 0)
        rowf = row.astype(F32)
        h4 = _dot3(h3_ref[pl.ds(c * chunk, chunk), :], w4)
        dec = jnp.concatenate([jnp.exp(-(rowf / (seq - 1.0)) * dl),
                               jnp.exp(-((seq - rowf) / (seq - 1.0)) * dl)], axis=1)
        z_ref[pl.ds(c * chunk, chunk), :] = jnp.where((row == 0) & (lane >= cw), 0.0, h4 * dec)

    _hyena_zero_unused(e_ref, seq)
    _hyena_stage1(z_ref, e_ref, lhs1_ref, seq)
    dsk = d_ref[0]

    def body(ka, carry):
        ein = _hyena_load_ka(e_ref, ka, seq).astype(BF16)
        x = _dot(g_ref[ka], ein)
        sgn = 1.0 - 2.0 * (ka % 2)
        hsp = x[:, 0:cw] + sgn * x[:, cw:2 * cw]
        part = lax.broadcasted_iota(jnp.int32, (2 * nb, 1), 0)
        hsp = hsp + jnp.where(part < nb, dsk, 0.0)
        o_ref[0, pl.ds(pl.multiple_of(ka * 2 * nb, 2 * nb), 2 * nb), :] = hsp.astype(BF16)
        return carry

    _static_loop(half + 1, body)


def _filter_spectrum(h3, fw4, d_skip, lhs1, gmat, *, seq, width, cw, ct):
    hid = h3.shape[1] // 2
    orders = d_skip.shape[0]
    nct = width // cw
    per = ct // cw
    nb = HYENA_NB
    half = seq // nb
    max_decay = math.log(DECAY_TARGET) / FAST_DECAY_PCT
    min_decay = math.log(DECAY_TARGET) / SLOW_DECAY_PCT
    deltas = np.abs(np.linspace(min_decay, max_decay, width)).astype(np.float32).reshape(1, width)
    rows = (half + 1) * 2 * nb
    body = functools.partial(_filter_spec_body, seq=seq, cw=cw)
    return pl.pallas_call(
        body,
        grid=(orders, nct),
        in_specs=[
            _resident(h3.shape),
            pl.BlockSpec((hid, cw), lambda o, c: (0, o * 2 * nct + c)),
            pl.BlockSpec((hid, cw), lambda o, c: (0, o * 2 * nct + nct + c)),
            pl.BlockSpec((1, cw), lambda o, c: (0, c)),
            pl.BlockSpec((1, 1, cw), lambda o, c: (o, 0, c)),
            _resident(lhs1.shape),
            _resident(gmat.shape),
        ],
        out_specs=pl.BlockSpec((None, 1, rows, cw), lambda o, c: (c // per, o, 0, c % per)),
        out_shape=jax.ShapeDtypeStruct((width // ct, orders, rows, ct), BF16),
        scratch_shapes=[pltpu.VMEM((seq, 2 * cw), F32),
                        pltpu.VMEM((2 * (half + 1) * nb, 2 * cw), F32)],
        compiler_params=_params(2),
        name="filter_spectrum",
    )(h3, fw4, fw4, jnp.asarray(deltas), d_skip.reshape(orders, 1, width), lhs1, gmat)


def _hyena_body(v_ref, g1_ref, g2_ref, h_ref, lhs1_ref, g_ref, lhs2_ref, o_ref, z_ref, e_ref, *, seq):
    nb = HYENA_NB
    na = 2 * seq // nb
    half = na // 2
    pair = V7X_BF16_ROWS
    chunk = 512 if seq % 512 == 0 else seq

    def load_v(c, carry):
        r = pl.multiple_of(c * chunk, chunk)
        z_ref[pl.ds(r, chunk), :] = v_ref[pl.ds(r, chunk), :].astype(F32)
        return carry

    lax.fori_loop(0, seq // chunk, load_v, 0)
    _hyena_zero_unused(e_ref, seq)

    for order, gate_ref in enumerate((g1_ref, g2_ref)):
        _hyena_stage1(z_ref, e_ref, lhs1_ref, seq)

        group = 3 if (half + 1) % 3 == 0 else 1

        def spectral(it, carry):
            kas = [it * group + u for u in range(group)]
            eins = [_hyena_load_ka(e_ref, ka, seq).astype(BF16) for ka in kas]
            ys = []
            for ka, ein in zip(kas, eins):
                gk = g_ref[ka]
                x = _dot(gk, ein)
                hsp = h_ref[order, pl.ds(pl.multiple_of(ka * 2 * nb, 2 * nb), 2 * nb), :].astype(F32)
                xre, xim, hre, him = x[:nb], x[nb:], hsp[:nb], hsp[nb:]
                zsp = jnp.concatenate([xre * hre - xim * him, xre * him + xim * hre], axis=0).astype(BF16)
                ys.append(lax.dot_general(gk, zsp, (((0,), (0,)), ((), ())), preferred_element_type=F32))
            for ka, y in zip(kas, ys):
                e_ref[pl.ds(pl.multiple_of(ka * nb, nb), nb), :] = y[:nb]
                keep = jnp.where((ka == 0) | (ka == half), 0.0, 1.0)
                e_ref[pl.ds(pl.multiple_of((half + 1 + ka) * nb, nb), nb), :] = y[nb:] * keep
            return carry

        _static_loop((half + 1) // group, spectral)

        def inverse(i, carry):
            outs = []
            for q in range(pair // V7X_SUBLANES):
                off = i * pair + q * V7X_SUBLANES
                yg = _gather_tiles(e_ref, [_hyena_slot_block(s, na) * nb + off for s in range(na)])
                outs.append(_dot(lhs2_ref[...], yg.astype(BF16)))
            for a in range(half):
                sl = slice(a * V7X_SUBLANES, (a + 1) * V7X_SUBLANES)
                conv = jnp.concatenate([o_[sl] for o_ in outs], axis=0)
                r = pl.multiple_of(a * nb + i * pair, pair)
                res = gate_ref[pl.ds(r, pair), :].astype(F32) * conv
                if order == 0:
                    z_ref[pl.ds(r, pair), :] = res
                else:
                    o_ref[pl.ds(r, pair), :] = res.astype(BF16)
            return carry

        _static_loop(nb // pair, inverse)


def _hyena(hyc, hspec, lhs1, gmat, lhs2, *, seq, width, ct):
    n = hyc.shape[0]
    nct = width // ct
    nb = HYENA_NB
    half = seq // nb
    return pl.pallas_call(
        functools.partial(_hyena_body, seq=seq),
        grid=(nct, n // seq),
        in_specs=[
            pl.BlockSpec((seq, ct), lambda c, b: (b, c)),
            pl.BlockSpec((seq, ct), lambda c, b: (b, nct + c)),
            pl.BlockSpec((seq, ct), lambda c, b: (b, 2 * nct + c)),
            pl.BlockSpec((None,) + hspec.shape[1:], lambda c, b: (c, 0, 0, 0)),
            _resident(lhs1.shape),
            _resident(gmat.shape),
            _resident(lhs2.shape),
        ],
        out_specs=pl.BlockSpec((seq, ct), lambda c, b: (b, c)),
        out_shape=jax.ShapeDtypeStruct((n, width), BF16),
        scratch_shapes=[pltpu.VMEM((seq, ct), F32),
                        pltpu.VMEM((2 * (half + 1) * nb, ct), F32)],
        compiler_params=_params(2),
        name="hyena",
    )(hyc, hyc, hyc, hspec, lhs1, gmat, lhs2)


def _fnet_body(p_ref, q_ref, a1_ref, a2_ref, o_ref, t_ref, *, seq, na, nb):
    sub = V7X_SUBLANES

    def stage1(i, carry):
        off = i * sub
        starts = [na * n2 + off for n2 in range(nb)]
        xin = jnp.concatenate([_gather_tiles(p_ref, starts), _gather_tiles(q_ref, starts)],
                              axis=0).astype(BF16)
        out = _dot(a1_ref[...], xin)
        for part in range(2):
            for lo in range(sub):
                src = (part * sub + lo) * nb
                dst = part * seq + (off + lo) * nb
                t_ref[pl.ds(pl.multiple_of(dst, sub), nb), :] = out[src:src + nb]
        return carry

    _static_loop(na // sub, stage1)

    def stage2(i, carry):
        off = i * sub
        starts = [part * seq + n1 * nb + off for part in range(2) for n1 in range(na)]
        tin = _gather_tiles(t_ref, starts).astype(BF16)
        out = _dot(a2_ref[i], tin)
        for k1 in range(na):
            o_ref[pl.ds(pl.multiple_of(k1 * nb + off, sub), sub), :] = out[k1 * sub:(k1 + 1) * sub]
        return carry

    _static_loop(nb // sub, stage2)


def _fnet(pq, a1, a2, *, seq, fw, ct, na, nb):
    n = pq.shape[0]
    nct = fw // ct
    return pl.pallas_call(
        functools.partial(_fnet_body, seq=seq, na=na, nb=nb),
        grid=(n // seq, nct),
        in_specs=[
            pl.BlockSpec((seq, ct), lambda b, c: (b, c)),
            pl.BlockSpec((seq, ct), lambda b, c: (b, nct + c)),
            _resident(a1.shape),
            _resident(a2.shape),
        ],
        out_specs=pl.BlockSpec((seq, ct), lambda b, c: (b, c)),
        out_shape=jax.ShapeDtypeStruct((n, fw), F32),
        scratch_shapes=[pltpu.VMEM((2 * seq, ct), F32)],
        compiler_params=_params(2),
        name="fnet_seq",
    )(pq, pq, a1, a2)


def _merge_body(x_ref, z_ref, f_ref, po_ref, pp_ref, pn_ref, gt_ref, wha_ref, wfo_ref, wpl_ref,
                ps_ref, wo_ref, gpost_ref, o_ref, ext_ref, m_ref, *, tm, seq, d):
    i = pl.program_id(0)
    sub = V7X_SUBLANES
    has_prev = lax.rem(i * tm, seq) != 0
    has_next = lax.rem((i + 1) * tm, seq) != 0
    ext_ref[pl.ds(0, sub), :] = jnp.where(has_prev, pp_ref[...], 0.0)
    ext_ref[pl.ds(sub, tm), :] = po_ref[...]
    ext_ref[pl.ds(sub + tm, sub), :] = jnp.where(has_next, pn_ref[...], 0.0)

    gate = lambda q, lo, hi: gt_ref[:, q * d + lo:q * d + hi].astype(F32)
    m_ref[...] = gate(0, 0, d) * _dot(z_ref[...], wha_ref[...])
    m_ref[...] += gate(1, 0, d) * _dot(f_ref[...].astype(BF16), wfo_ref[...])

    pos = lax.rem(i * tm, seq) + lax.broadcasted_iota(jnp.int32, (tm, 1), 0)
    n_pool = len(POOL_WINDOWS)
    gw = po_ref.shape[1] // n_pool
    ow = d // n_pool
    for q, w in enumerate(POOL_WINDOWS):
        before = w // 2
        after = w - 1 - before
        cols = slice(q * gw, (q + 1) * gw)
        tot = ext_ref[pl.ds(sub - before, tm), cols]
        for s in range(-before + 1, after + 1):
            tot = tot + ext_ref[pl.ds(sub + s, tm), cols]
        lo = jnp.maximum(pos - before, 0)
        hi = jnp.minimum(pos + after, seq - 1)
        cnt = (hi - lo + 1).astype(F32)
        mq = (tot / cnt - po_ref[:, cols]).astype(BF16)
        yc = _dot(mq, wpl_ref[q]) * ps_ref[:, q * ow:(q + 1) * ow]
        m_ref[:, q * ow:(q + 1) * ow] += gate(2, q * ow, (q + 1) * ow) * yc

    y = _dot(m_ref[...].astype(BF16), wo_ref[...])
    o_ref[...] = x_ref[...] + _rms(y, gpost_ref[...])


def _merge(x2, z2, fy, po, gates, wha, wfo, wpool, pscale, w_out, g_post, layer, *, tm, seq):
    n, d = x2.shape
    pw = po.shape[1]
    sub = V7X_SUBLANES
    hb = tm // sub
    last_hb = n // sub - 1
    body = functools.partial(_merge_body, tm=tm, seq=seq, d=d)
    return pl.pallas_call(
        body,
        grid=(n // tm,),
        in_specs=[
            pl.BlockSpec((tm, d), lambda i: (i, 0)),
            pl.BlockSpec((tm, z2.shape[1]), lambda i: (i, 0)),
            pl.BlockSpec((tm, fy.shape[1]), lambda i: (i, 0)),
            pl.BlockSpec((tm, pw), lambda i: (i, 0)),
            pl.BlockSpec((sub, pw), lambda i: (jnp.maximum(i * hb - 1, 0), 0)),
            pl.BlockSpec((sub, pw), lambda i: (jnp.minimum((i + 1) * hb, last_hb), 0)),
            pl.BlockSpec((tm, gates.shape[1]), lambda i: (i, 0)),
            _resident_layer(wha, layer),
            _resident_layer(wfo, layer),
            _resident_layer(wpool, layer),
            _resident((1, d)),
            _resident(w_out.shape),
            _resident((1, d)),
        ],
        out_specs=pl.BlockSpec((tm, d), lambda i: (i, 0)),
        out_shape=jax.ShapeDtypeStruct((n, d), F32),
        scratch_shapes=[pltpu.VMEM((tm + 2 * sub, pw), F32), pltpu.VMEM((tm, d), F32)],
        compiler_params=_params(1),
        name="mixer_merge",
    )(x2, z2, fy, po, po, po, gates, wha, wfo, wpool, pscale.reshape(1, d), w_out, g_post.reshape(1, d))


def _kv_body(m_ref, g_ref, w_ref, o_ref):
    o_ref[...] = _dot(_rms(m_ref[...], g_ref[...]).astype(BF16), w_ref[...].astype(BF16)).astype(BF16)


def _kv(mem2, g, w_kv, layer, *, n_mem, tn):
    n, d = mem2.shape
    cols = w_kv.shape[2]
    return pl.pallas_call(
        _kv_body,
        grid=(cols // tn, n // n_mem),
        in_specs=[
            pl.BlockSpec((n_mem, d), lambda j, b: (b, 0)),
            pl.BlockSpec((1, d), lambda j, b: (0, 0)),
            pl.BlockSpec((None, d, tn), lambda j, b: (layer, 0, j)),
        ],
        out_specs=pl.BlockSpec((n_mem, tn), lambda j, b: (b, j)),
        out_shape=jax.ShapeDtypeStruct((n, cols), BF16),
        compiler_params=_params(2),
        name="mem_kv",
    )(mem2, g.reshape(1, d), w_kv)


def _attn_body(x_ref, gpre_ref, wq_ref, k_ref, v_ref, wo_ref, gpost_ref, o_ref, *, d):
    dh = d // N_HEADS
    x = x_ref[...]
    q = _dot(_rms(x, gpre_ref[...]).astype(BF16), wq_ref[...])
    heads = []
    for h in range(N_HEADS):
        cols = slice(h * dh, (h + 1) * dh)
        s = lax.dot_general(q[:, cols].astype(BF16), k_ref[:, cols], (((1,), (1,)), ((), ())),
                            preferred_element_type=F32) * (dh ** -0.5)
        e = jnp.exp(s - jnp.max(s, axis=-1, keepdims=True))
        p = e / jnp.sum(e, axis=-1, keepdims=True)
        heads.append(_dot(p.astype(BF16), v_ref[:, cols]))
    o = jnp.concatenate(heads, axis=1).astype(BF16)
    o_ref[...] = x + _rms(_dot(o, wo_ref[...]), gpost_ref[...])


def _attn(x2, g_pre, w_q, kv, w_o, g_post, *, tm, seq, n_mem):
    n, d = x2.shape
    tiles_per_seq = seq // tm
    return pl.pallas_call(
        functools.partial(_attn_body, d=d),
        grid=(n // tm,),
        in_specs=[
            pl.BlockSpec((tm, d), lambda i: (i, 0)),
            _resident((1, d)),
            _resident(w_q.shape),
            pl.BlockSpec((n_mem, d), lambda i: (i // tiles_per_seq, 0)),
            pl.BlockSpec((n_mem, d), lambda i: (i // tiles_per_seq, 1)),
            _resident(w_o.shape),
            _resident((1, d)),
        ],
        out_specs=pl.BlockSpec((tm, d), lambda i: (i, 0)),
        out_shape=jax.ShapeDtypeStruct((n, d), F32),
        compiler_params=_params(1),
        name="mem_attn",
    )(x2, g_pre.reshape(1, d), w_q, kv, kv, w_o, g_post.reshape(1, d))


def kernel(x, mem, g_ffn1_pre, w_ffn1_gu, w_ffn1_down, g_ffn1_post, g_mix_pre, w_in, hyena_conv_w,
           hyena_conv_b, filt_w1, filt_b1, filt_w2, filt_b2, filt_w3, filt_b3, filt_w4, filt_freq,
           hyena_d, w_hyena_out, w_fnet_out, w_pool, pool_scale, w_out, g_mix_post, g_mem_pre,
           g_mem_kv, w_q, w_kv, w_o, g_mem_post, g_ffn2_pre, w_ffn2_gu, w_ffn2_down, g_ffn2_post):
    batch, seq, d = x.shape
    n_mem = mem.shape[1]
    depth = w_in.shape[0]
    hy_w = w_hyena_out.shape[1]
    hy_cols = hyena_conv_w.shape[2]
    fw = w_fnet_out.shape[1]
    pw = w_pool.shape[1] * w_pool.shape[2]

    tm = 512
    tn = fw + pw
    tf = 512
    ct = V7X_MXU_COLS

    lhs1, gmat, lhs2 = _hyena_mats(seq)
    a1, a2, cs, fna, fnb = _fnet_mats(seq, fw // FNET_GROUPS)

    bf = lambda w: w.astype(BF16)
    w_hyena_out, w_fnet_out, w_pool = map(bf, (w_hyena_out, w_fnet_out, w_pool))
    ffn_w = (bf(w_ffn1_gu[0]), bf(w_ffn1_down[0]))
    x2 = x.reshape(batch * seq, d)
    mem2 = mem.reshape(batch * n_mem, d)
    for l in range(depth):
        later = (w_ffn2_gu, w_ffn2_down, w_in, w_out, w_q, w_o)
        x2, *cast = _ffn(x2, g_ffn1_pre[l], *ffn_w, g_ffn1_post[l], tm=2 * tm, tf=tf,
                         cast=[(w, l) for w in later])
        *ffn_w, w_in_l, w_out_l, w_q_l, w_o_l = cast

        hyc, pq, po, gates = _proj(x2, g_mix_pre[l], w_in_l, hyena_conv_w[l], hyena_conv_b[l], cs,
                                   tm=2 * tm, tn=tn, seq=seq, hy_cols=hy_cols, fw=fw, pw=pw)
        h3 = _filter_mlp(filt_w1[l], filt_b1[l], filt_w2[l], filt_b2[l], filt_w3[l], filt_b3[l],
                         filt_freq[l], seq=seq)
        hspec = _filter_spectrum(h3, filt_w4[l], hyena_d[l], lhs1, gmat, seq=seq, width=hy_w, cw=ct // 2, ct=ct)
        z2 = _hyena(hyc, hspec, lhs1, gmat, lhs2, seq=seq, width=hy_w, ct=ct)
        fy = _fnet(pq, a1, a2, seq=seq, fw=fw, ct=ct, na=fna, nb=fnb)
        x2 = _merge(x2, z2, fy, po, gates, w_hyena_out, w_fnet_out, w_pool,
                    pool_scale[l], w_out_l, g_mix_post[l], l, tm=tm, seq=seq)

        kv = _kv(mem2, g_mem_kv[l], w_kv, l, n_mem=batch * n_mem, tn=tn)
        x2 = _attn(x2, g_mem_pre[l], w_q_l, kv, w_o_l, g_mem_post[l], tm=tm, seq=seq, n_mem=n_mem)

        nxt = [(w_ffn1_gu, l + 1), (w_ffn1_down, l + 1)] if l + 1 < depth else []
        x2, *ffn_w = _ffn(x2, g_ffn2_pre[l], *ffn_w, g_ffn2_post[l], tm=2 * tm, tf=tf, cast=nxt)
    return x2.reshape(batch, seq, d)
```

```python
import functools
import math

import jax
import jax.numpy as jnp
import numpy as np
from jax import lax
from jax.experimental import pallas as pl
from jax.experimental.pallas import tpu as pltpu

F32 = jnp.float32
BF16 = jnp.bfloat16

RMS_EPS = 1e-6
MACARON_WEIGHT = 0.5
N_HEADS = 4
POOL_WINDOWS = (2, 4, 8, 16)
FNET_GROUPS = 4
FILTER_BANDS = 16
DECAY_TARGET = 1e-2
FAST_DECAY_PCT = 0.3
SLOW_DECAY_PCT = 1.5

V7X_SUBLANES = 8
V7X_BF16_ROWS = 16
V7X_MXU_COLS = 256
V7X_VMEM_LIMIT = 60 * 2**20

HYENA_NB = 128
HALO = V7X_BF16_ROWS
ROW_CHUNK = 256
LANE_BLOCK = 128


def _params(n_axes):
    return pltpu.CompilerParams(
        dimension_semantics=("arbitrary",) * n_axes,
        vmem_limit_bytes=V7X_VMEM_LIMIT,
    )


def _resident(shape):
    zeros = (0,) * len(shape)
    return pl.BlockSpec(shape, lambda *_: zeros, pipeline_mode=pl.Buffered(1))


def _resident_layer(stacked, layer):
    tail = (0,) * (stacked.ndim - 1)
    return pl.BlockSpec((None,) + stacked.shape[1:], lambda *_: (layer,) + tail,
                        pipeline_mode=pl.Buffered(1))


def _rms(x, g):
    ms = jnp.mean(x * x, axis=-1, keepdims=True)
    return x * lax.rsqrt(ms + RMS_EPS) * g


def _sigmoid(x):
    return 0.5 * jnp.tanh(0.5 * x) + 0.5


def _dot(a, b):
    return jnp.dot(a, b, preferred_element_type=F32)


def _dot3(a, b):
    ah = a.astype(BF16)
    al = (a - ah.astype(F32)).astype(BF16)
    bh = b.astype(BF16)
    bl = (b - bh.astype(F32)).astype(BF16)
    return _dot(ah, bh) + (_dot(al, bh) + _dot(ah, bl))


def _row_chunks(rows, fn):
    chunk = ROW_CHUNK if rows % ROW_CHUNK == 0 else rows

    def body(c, carry):
        fn(pl.ds(pl.multiple_of(c * chunk, chunk), chunk))
        return carry

    lax.fori_loop(0, rows // chunk, body, 0)


def _norm_residual_inplace(o_ref, x_ref, g):
    rows, d = o_ref.shape

    def post(r):
        sq = None
        for c in range(0, d, LANE_BLOCK):
            yb = o_ref[r, c:c + LANE_BLOCK]
            sq = yb * yb if sq is None else sq + yb * yb
        inv = lax.rsqrt(jnp.sum(sq, axis=-1, keepdims=True) * (1.0 / d) + RMS_EPS)
        for c in range(0, d, LANE_BLOCK):
            cols = slice(c, c + LANE_BLOCK)
            o_ref[r, cols] = x_ref[r, cols] + o_ref[r, cols] * inv * g[:, cols]

    _row_chunks(rows, post)


def _ffn_body(x_ref, gpre_ref, wg_ref, wu_ref, wd_ref, gpost_ref, *rest, n_cast):
    cast_in, o_ref, cast_out, u_ref = rest[:n_cast], rest[n_cast], rest[n_cast + 1:-1], rest[-1]
    k = pl.program_id(1)
    rows = x_ref.shape[0]

    @pl.when(k == 0)
    def _():
        def pre(r):
            u_ref[r, :] = _rms(x_ref[r, :], gpre_ref[...]).astype(BF16)
            o_ref[r, :] = jnp.zeros((r.size, o_ref.shape[1]), F32)
        _row_chunks(rows, pre)

    u = u_ref[...]
    a = _dot(u, wg_ref[...])
    b = _dot(u, wu_ref[...])
    h = (a * _sigmoid(a) * b).astype(BF16)
    for src, dst in zip(cast_in, cast_out):
        dst[...] = src[...].astype(BF16)
    o_ref[...] += _dot(h, wd_ref[...])

    @pl.when(k == pl.num_programs(1) - 1)
    def _():
        _norm_residual_inplace(o_ref, x_ref, MACARON_WEIGHT * gpost_ref[...])


def _cast_tiling(rows, cols, ni, nk):
    rb = rows // ni
    assert rows % ni == 0 and rb % V7X_BF16_ROWS == 0
    nc = max(c for c in range(1, nk + 1) if cols % c == 0 and (cols // c) % 128 == 0)
    return rb, cols // nc, nc


def _ffn(x2, g_pre, w_gu, w_down, g_post, *, tm, tf, cast=()):
    n, d = x2.shape
    f = w_down.shape[0]
    ni, nk = n // tm, f // tf
    in_specs = [
        pl.BlockSpec((tm, d), lambda i, k: (i, 0)),
        pl.BlockSpec((1, d), lambda i, k: (0, 0)),
        pl.BlockSpec((d, tf), lambda i, k: (0, k)),
        pl.BlockSpec((d, tf), lambda i, k: (0, k + nk)),
        pl.BlockSpec((tf, d), lambda i, k: (k, 0)),
        pl.BlockSpec((1, d), lambda i, k: (0, 0)),
    ]
    args = [x2, g_pre.reshape(1, d), w_gu, w_gu, w_down, g_post.reshape(1, d)]
    out_specs = [pl.BlockSpec((tm, d), lambda i, k: (i, 0))]
    out_shape = [jax.ShapeDtypeStruct((n, d), F32)]
    for w, layer in cast:
        rb, cb, nc = _cast_tiling(w.shape[1], w.shape[2], ni, nk)
        in_specs.append(pl.BlockSpec((None, rb, cb),
                                     lambda i, k, layer=layer, nc=nc: (layer, i, jnp.minimum(k, nc - 1))))
        args.append(w)
        out_specs.append(pl.BlockSpec((rb, cb), lambda i, k, nc=nc: (i, jnp.minimum(k, nc - 1))))
        out_shape.append(jax.ShapeDtypeStruct(w.shape[1:], BF16))
    return pl.pallas_call(
        functools.partial(_ffn_body, n_cast=len(cast)),
        grid=(ni, nk),
        in_specs=in_specs,
        out_specs=out_specs,
        out_shape=out_shape,
        scratch_shapes=[pltpu.VMEM((tm, d), BF16)],
        compiler_params=_params(2),
        name="ffn",
    )(*args)


def _proj_body(x_ref, xp_ref, xn_ref, g_ref, w_ref, cw_ref, cb_ref, cs_ref,
               hy_ref, pq_ref, po_ref, gt_ref, u_ref, *, tm, seq, n_hy, fw):
    i = pl.program_id(0)
    j = pl.program_id(1)

    @pl.when(j == 0)
    def _():
        g = g_ref[...]
        has_prev = lax.rem(i * tm, seq) != 0
        has_next = lax.rem((i + 1) * tm, seq) != 0
        u_ref[pl.ds(0, HALO), :] = jnp.where(has_prev, _rms(xp_ref[...], g), 0.0).astype(BF16)
        chunk = ROW_CHUNK if tm % ROW_CHUNK == 0 else tm

        def pre(c, carry):
            r = pl.multiple_of(c * chunk, chunk)
            u_ref[pl.ds(pl.multiple_of(HALO + r, HALO), chunk), :] = (
                _rms(x_ref[pl.ds(r, chunk), :], g).astype(BF16))
            return carry

        lax.fori_loop(0, tm // chunk, pre, 0)
        u_ref[pl.ds(HALO + tm, HALO), :] = jnp.where(has_next, _rms(xn_ref[...], g), 0.0).astype(BF16)

    @pl.when(j < n_hy)
    def _():
        h = _dot(u_ref[...], w_ref[...])
        rows = h.shape[0]
        prev = pltpu.roll(h, 1, axis=0)[HALO:HALO + tm]
        nxt = pltpu.roll(h, rows - 1, axis=0)[HALO:HALO + tm]
        y = cb_ref[...] + prev * cw_ref[0:1, :]
        y = y + h[HALO:HALO + tm] * cw_ref[1:2, :]
        y = y + nxt * cw_ref[2:3, :]
        hy_ref[...] = y.astype(BF16)

    @pl.when(j == n_hy)
    def _():
        c = _dot(u_ref[pl.ds(HALO, tm), :], w_ref[...])
        po_ref[...] = c[:, fw:]
        fb = c[:, :fw].astype(BF16)
        gw = fw // FNET_GROUPS
        res = [_dot(fb[:, q * gw:(q + 1) * gw], cs_ref[...]) for q in range(FNET_GROUPS)]
        pq_ref[...] = jnp.concatenate([r[:, :gw] for r in res] + [r[:, gw:] for r in res], axis=1)

    @pl.when(j > n_hy)
    def _():
        c = _dot(u_ref[pl.ds(HALO, tm), :], w_ref[...])
        gt_ref[...] = _sigmoid(c).astype(BF16)


def _proj(x2, g, w_in, conv_w, conv_b, cs, *, tm, tn, seq, hy_cols, fw, pw):
    n, d = x2.shape
    cols = w_in.shape[1]
    n_hy = hy_cols // tn
    assert hy_cols % tn == 0 and fw + pw == tn and (cols - hy_cols - tn) % tn == 0
    nj = cols // tn
    n_gate = cols - hy_cols - tn
    hb = tm // HALO
    last_hb = n // HALO - 1
    body = functools.partial(_proj_body, tm=tm, seq=seq, n_hy=n_hy, fw=fw)
    return pl.pallas_call(
        body,
        grid=(n // tm, nj),
        in_specs=[
            pl.BlockSpec((tm, d), lambda i, j: (i, 0)),
            pl.BlockSpec((HALO, d), lambda i, j: (jnp.maximum(i * hb - 1, 0), 0)),
            pl.BlockSpec((HALO, d), lambda i, j: (jnp.minimum((i + 1) * hb, last_hb), 0)),
            pl.BlockSpec((1, d), lambda i, j: (0, 0)),
            pl.BlockSpec((d, tn), lambda i, j: (0, j)),
            pl.BlockSpec((3, tn), lambda i, j: (0, jnp.minimum(j, n_hy - 1))),
            pl.BlockSpec((1, tn), lambda i, j: (0, jnp.minimum(j, n_hy - 1))),
            _resident(cs.shape),
        ],
        out_specs=[
            pl.BlockSpec((tm, tn), lambda i, j: (i, jnp.minimum(j, n_hy - 1))),
            pl.BlockSpec((tm, 2 * fw), lambda i, j: (i, 0)),
            pl.BlockSpec((tm, pw), lambda i, j: (i, 0)),
            pl.BlockSpec((tm, tn), lambda i, j: (i, jnp.maximum(j - n_hy - 1, 0))),
        ],
        out_shape=[
            jax.ShapeDtypeStruct((n, hy_cols), BF16),
            jax.ShapeDtypeStruct((n, 2 * fw), F32),
            jax.ShapeDtypeStruct((n, pw), F32),
            jax.ShapeDtypeStruct((n, n_gate), BF16),
        ],
        scratch_shapes=[pltpu.VMEM((tm + 2 * HALO, d), BF16)],
        compiler_params=_params(2),
        name="mixer_proj",
    )(x2, x2, x2, g.reshape(1, d), w_in, conv_w, conv_b.reshape(1, hy_cols), cs)


def _hyena_slot_block(s, na):
    half = na // 2
    return s if s <= half else (half + 1) + (s - half)


def _hyena_mats(seq):
    nb = HYENA_NB
    na = 2 * seq // nb
    half = na // 2
    eye = np.eye(V7X_SUBLANES)
    a = np.arange(half)[None, :]
    f1 = np.zeros((na, half))
    k_re = np.arange(half + 1)[:, None]
    f1[: half + 1] = np.cos(2 * np.pi * k_re * a / na)
    k_im = np.arange(1, half)[:, None]
    f1[half + 1:] = -np.sin(2 * np.pi * k_im * a / na)
    lhs1 = np.kron(f1, eye)
    b = np.arange(nb)[None, :]
    kb = np.arange(nb)[:, None]
    g = np.zeros((half + 1, 2 * nb, 2 * nb))
    for ka in range(half + 1):
        ang = 2 * np.pi * (kb * b / nb + b * ka / (2 * seq))
        gre, gim = np.cos(ang), -np.sin(ang)
        g[ka] = np.block([[gre, -gim], [gim, gre]])
    a_col = np.arange(half)[:, None]
    f2 = np.zeros((half, na))
    wgt = np.full(half + 1, 2.0)
    wgt[0] = wgt[half] = 1.0
    f2[:, : half + 1] = wgt[None, :] * np.cos(2 * np.pi * a_col * np.arange(half + 1)[None, :] / na)
    f2[:, half + 1:] = -2.0 * np.sin(2 * np.pi * a_col * np.arange(1, half)[None, :] / na)
    lhs2 = np.kron(f2 / (2 * seq), eye)
    return (jnp.asarray(lhs1, BF16), jnp.asarray(g, BF16), jnp.asarray(lhs2, BF16))


def _fnet_channel_mat(group):
    cc = np.arange(group)[:, None]
    mm = np.arange(group)[None, :]
    angc = 2 * np.pi * cc * mm / group
    return jnp.asarray(np.concatenate([np.cos(angc), -np.sin(angc)], axis=1) / math.sqrt(group), BF16)


def _fnet3_mats(seq):
    r = int(round(seq ** (1.0 / 3.0)))
    sub = V7X_SUBLANES
    assert r ** 3 == seq and r % sub == 0
    idx = np.arange(r)
    ang = 2 * np.pi * idx[:, None] * idx[None, :] / r
    blk = np.array([[np.cos(ang), np.sin(ang)], [-np.sin(ang), np.cos(ang)]])
    ma = np.zeros((sub, 2, r, 2, r, sub))
    for lo in range(sub):
        ma[lo, :, :, :, :, lo] = blk.transpose(0, 2, 1, 3)
    ma = ma.reshape(2 * r * sub, 2 * r * sub)
    mb = np.zeros((r // sub, 2, r, sub, 2, r, sub))
    for hi in range(r // sub):
        for lo in range(sub):
            k2 = hi * sub + lo
            ph = 2 * np.pi * (idx[:, None] * idx[None, :] / r + idx[None, :] * k2 / r ** 2)
            c, s = np.cos(ph), np.sin(ph)
            mb[hi, 0, :, lo, 0, :, lo] = c
            mb[hi, 0, :, lo, 1, :, lo] = s
            mb[hi, 1, :, lo, 0, :, lo] = -s
            mb[hi, 1, :, lo, 1, :, lo] = c
    mb = mb.reshape(r // sub, 2 * r * sub, 2 * r * sub)
    mc = np.zeros((r, r // sub, r, sub, 2, r, sub))
    for k1 in range(r):
        for hi in range(r // sub):
            for lo in range(sub):
                k2 = hi * sub + lo
                ph = 2 * np.pi * (idx[:, None] * idx[None, :] / r
                                  + idx[None, :] * k1 / r ** 2 + idx[None, :] * k2 / seq)
                mc[k1, hi, :, lo, 0, :, lo] = np.cos(ph)
                mc[k1, hi, :, lo, 1, :, lo] = np.sin(ph)
    mc = mc.reshape(r * (r // sub), r * sub, 2 * r * sub) / math.sqrt(seq)
    return jnp.asarray(ma, BF16), jnp.asarray(mb, BF16), jnp.asarray(mc, BF16), r


def _static_loop(n, body):
    for i in range(n):
        body(i, 0)


def _gather_tiles(ref, starts):
    tiles = [ref[pl.ds(pl.multiple_of(s, V7X_SUBLANES), V7X_SUBLANES), :] for s in starts]
    return jnp.concatenate(tiles, axis=0)


def _hyena_stage1(z_ref, e_ref, lhs1_ref, seq):
    nb = HYENA_NB
    na = 2 * seq // nb
    half = na // 2

    def body(i, carry):
        off = i * V7X_SUBLANES
        xg = _gather_tiles(z_ref, [nb * a + off for a in range(half)]).astype(BF16)
        out = _dot(lhs1_ref[...], xg)
        for s in range(na):
            row = _hyena_slot_block(s, na) * nb + off
            e_ref[pl.ds(pl.multiple_of(row, V7X_SUBLANES), V7X_SUBLANES), :] = (
                out[s * V7X_SUBLANES:(s + 1) * V7X_SUBLANES])
        return carry

    _static_loop(nb // V7X_SUBLANES, body)


def _hyena_zero_unused(e_ref, seq):
    nb = HYENA_NB
    half = seq // nb
    zero = jnp.zeros((nb, e_ref.shape[1]), F32)
    e_ref[pl.ds((half + 1) * nb, nb), :] = zero
    e_ref[pl.ds((2 * half + 1) * nb, nb), :] = zero


def _hyena_load_ka(e_ref, ka, seq):
    nb = HYENA_NB
    half = seq // nb
    re = e_ref[pl.ds(pl.multiple_of(ka * nb, nb), nb), :]
    im = e_ref[pl.ds(pl.multiple_of((half + 1 + ka) * nb, nb), nb), :]
    return jnp.concatenate([re, im], axis=0)


def _filter_mlp_body(bands_ref, w1t_ref, w1c_ref, w1s_ref, b1_ref, w2_ref, b2_ref, w3_ref, b3_ref,
                     fr_ref, o_ref, *, seq):
    idx = lax.broadcasted_iota(jnp.int32, (1, 2 * seq), 1)
    p = jnp.where(idx < seq, idx, 2 * seq - idx).astype(F32)
    t = p / (seq - 1.0)
    ang = bands_ref[...] * ((2.0 * math.pi / seq) * p)
    fr = fr_ref[...]
    pre = (w1t_ref[...] * t
           + _dot3(w1c_ref[...], jnp.cos(ang))
           + _dot3(w1s_ref[...], -jnp.sin(ang))
           + b1_ref[...])
    h = jnp.sin(fr * pre)
    h = jnp.sin(fr * (_dot3(w2_ref[...], h) + b2_ref[...]))
    h = jnp.sin(fr * (_dot3(w3_ref[...], h) + b3_ref[...]))
    o_ref[...] = jnp.concatenate([h[:, :seq], h[:, seq:]], axis=0).T


def _filter_mlp(fw1, fb1, fw2, fb2, fw3, fb3, freq, *, seq):
    hid = fw2.shape[0]
    bands = np.linspace(1e-4, FILTER_BANDS - 1, FILTER_BANDS).astype(np.float32).reshape(-1, 1)
    col = lambda v: v.reshape(hid, 1)
    args = (jnp.asarray(bands), fw1[0].reshape(hid, 1), fw1[1:1 + FILTER_BANDS].T,
            fw1[1 + FILTER_BANDS:].T, col(fb1), fw2.T, col(fb2), fw3.T, col(fb3), col(freq))
    return pl.pallas_call(
        functools.partial(_filter_mlp_body, seq=seq),
        grid=(1,),
        in_specs=[_resident(a.shape) for a in args],
        out_specs=pl.BlockSpec((seq, 2 * hid), lambda i: (0, 0)),
        out_shape=jax.ShapeDtypeStruct((seq, 2 * hid), F32),
        compiler_params=_params(1),
        name="filter_mlp",
    )(*args)


def _filter_spec_body(h3_ref, w4f_ref, w4b_ref, dl_ref, d_ref, lhs1_ref, g_ref,
                      o_ref, z_ref, e_ref, *, seq, cw):
    nb = HYENA_NB
    half = seq // nb
    dl = dl_ref[...]
    zero = jnp.zeros_like(w4f_ref[...])
    w4 = jnp.concatenate([jnp.concatenate([w4f_ref[...], zero], axis=1),
                          jnp.concatenate([zero, w4b_ref[...]], axis=1)], axis=0)
    lane = lax.broadcasted_iota(jnp.int32, (1, 2 * cw), 1)
    chunk = ROW_CHUNK if seq % ROW_CHUNK == 0 else seq
    for c in range(seq // chunk):
        row = c * chunk + lax.broadcasted_iota(jnp.int32, (chunk, 1), 0)
        rowf = row.astype(F32)
        h4 = _dot3(h3_ref[pl.ds(c * chunk, chunk), :], w4)
        dec = jnp.concatenate([jnp.exp(-(rowf / (seq - 1.0)) * dl),
                               jnp.exp(-((seq - rowf) / (seq - 1.0)) * dl)], axis=1)
        z_ref[pl.ds(c * chunk, chunk), :] = jnp.where((row == 0) & (lane >= cw), 0.0, h4 * dec)

    _hyena_zero_unused(e_ref, seq)
    _hyena_stage1(z_ref, e_ref, lhs1_ref, seq)
    dsk = d_ref[0]

    def body(ka, carry):
        ein = _hyena_load_ka(e_ref, ka, seq).astype(BF16)
        x = _dot(g_ref[ka], ein)
        sgn = 1.0 - 2.0 * (ka % 2)
        hsp = x[:, 0:cw] + sgn * x[:, cw:2 * cw]
        part = lax.broadcasted_iota(jnp.int32, (2 * nb, 1), 0)
        hsp = hsp + jnp.where(part < nb, dsk, 0.0)
        o_ref[0, pl.ds(pl.multiple_of(ka * 2 * nb, 2 * nb), 2 * nb), :] = hsp.astype(BF16)
        return carry

    _static_loop(half + 1, body)


def _filter_spectrum(h3, fw4, d_skip, lhs1, gmat, *, seq, width, cw, ct):
    hid = h3.shape[1] // 2
    orders = d_skip.shape[0]
    nct = width // cw
    per = ct // cw
    nb = HYENA_NB
    half = seq // nb
    max_decay = math.log(DECAY_TARGET) / FAST_DECAY_PCT
    min_decay = math.log(DECAY_TARGET) / SLOW_DECAY_PCT
    deltas = np.abs(np.linspace(min_decay, max_decay, width)).astype(np.float32).reshape(1, width)
    rows = (half + 1) * 2 * nb
    body = functools.partial(_filter_spec_body, seq=seq, cw=cw)
    return pl.pallas_call(
        body,
        grid=(orders, nct),
        in_specs=[
            _resident(h3.shape),
            pl.BlockSpec((hid, cw), lambda o, c: (0, o * 2 * nct + c)),
            pl.BlockSpec((hid, cw), lambda o, c: (0, o * 2 * nct + nct + c)),
            pl.BlockSpec((1, cw), lambda o, c: (0, c)),
            pl.BlockSpec((1, 1, cw), lambda o, c: (o, 0, c)),
            _resident(lhs1.shape),
            _resident(gmat.shape),
        ],
        out_specs=pl.BlockSpec((None, 1, rows, cw), lambda o, c: (c // per, o, 0, c % per)),
        out_shape=jax.ShapeDtypeStruct((width // ct, orders, rows, ct), BF16),
        scratch_shapes=[pltpu.VMEM((seq, 2 * cw), F32),
                        pltpu.VMEM((2 * (half + 1) * nb, 2 * cw), F32)],
        compiler_params=_params(2),
        name="filter_spectrum",
    )(h3, fw4, fw4, jnp.asarray(deltas), d_skip.reshape(orders, 1, width), lhs1, gmat)


def _hyena_body(v_ref, g1_ref, g2_ref, h_ref, lhs1_ref, g_ref, lhs2_ref, o_ref, z_ref, e_ref, *, seq):
    nb = HYENA_NB
    na = 2 * seq // nb
    half = na // 2
    pair = V7X_BF16_ROWS
    chunk = 512 if seq % 512 == 0 else seq

    def load_v(c, carry):
        r = pl.multiple_of(c * chunk, chunk)
        z_ref[pl.ds(r, chunk), :] = v_ref[pl.ds(r, chunk), :].astype(F32)
        return carry

    lax.fori_loop(0, seq // chunk, load_v, 0)
    _hyena_zero_unused(e_ref, seq)

    for order, gate_ref in enumerate((g1_ref, g2_ref)):
        _hyena_stage1(z_ref, e_ref, lhs1_ref, seq)

        group = 3 if (half + 1) % 3 == 0 else 1

        def spectral(it, carry):
            kas = [it * group + u for u in range(group)]
            eins = [_hyena_load_ka(e_ref, ka, seq).astype(BF16) for ka in kas]
            ys = []
            for ka, ein in zip(kas, eins):
                gk = g_ref[ka]
                x = _dot(gk, ein)
                hsp = h_ref[order, pl.ds(pl.multiple_of(ka * 2 * nb, 2 * nb), 2 * nb), :].astype(F32)
                xre, xim, hre, him = x[:nb], x[nb:], hsp[:nb], hsp[nb:]
                zsp = jnp.concatenate([xre * hre - xim * him, xre * him + xim * hre], axis=0).astype(BF16)
                ys.append(lax.dot_general(gk, zsp, (((0,), (0,)), ((), ())), preferred_element_type=F32))
            for ka, y in zip(kas, ys):
                e_ref[pl.ds(pl.multiple_of(ka * nb, nb), nb), :] = y[:nb]
                keep = jnp.where((ka == 0) | (ka == half), 0.0, 1.0)
                e_ref[pl.ds(pl.multiple_of((half + 1 + ka) * nb, nb), nb), :] = y[nb:] * keep
            return carry

        _static_loop((half + 1) // group, spectral)

        def inverse(i, carry):
            outs = []
            for q in range(pair // V7X_SUBLANES):
                off = i * pair + q * V7X_SUBLANES
                yg = _gather_tiles(e_ref, [_hyena_slot_block(s, na) * nb + off for s in range(na)])
                outs.append(_dot(lhs2_ref[...], yg.astype(BF16)))
            for a in range(half):
                sl = slice(a * V7X_SUBLANES, (a + 1) * V7X_SUBLANES)
                conv = jnp.concatenate([o_[sl] for o_ in outs], axis=0)
                r = pl.multiple_of(a * nb + i * pair, pair)
                res = gate_ref[pl.ds(r, pair), :].astype(F32) * conv
                if order == 0:
                    z_ref[pl.ds(r, pair), :] = res
                else:
                    o_ref[pl.ds(r, pair), :] = res.astype(BF16)
            return carry

        _static_loop(nb // pair, inverse)


def _hyena(hyc, hspec, lhs1, gmat, lhs2, *, seq, width, ct):
    n = hyc.shape[0]
    nct = width // ct
    nb = HYENA_NB
    half = seq // nb
    return pl.pallas_call(
        functools.partial(_hyena_body, seq=seq),
        grid=(nct, n // seq),
        in_specs=[
            pl.BlockSpec((seq, ct), lambda c, b: (b, c)),
            pl.BlockSpec((seq, ct), lambda c, b: (b, nct + c)),
            pl.BlockSpec((seq, ct), lambda c, b: (b, 2 * nct + c)),
            pl.BlockSpec((None,) + hspec.shape[1:], lambda c, b: (c, 0, 0, 0)),
            _resident(lhs1.shape),
            _resident(gmat.shape),
            _resident(lhs2.shape),
        ],
        out_specs=pl.BlockSpec((seq, ct), lambda c, b: (b, c)),
        out_shape=jax.ShapeDtypeStruct((n, width), BF16),
        scratch_shapes=[pltpu.VMEM((seq, ct), F32),
                        pltpu.VMEM((2 * (half + 1) * nb, ct), F32)],
        compiler_params=_params(2),
        name="hyena",
    )(hyc, hyc, hyc, hspec, lhs1, gmat, lhs2)


def _fnet3_body(p_ref, q_ref, ma_ref, mb_ref, mc_ref, o_ref, t_ref, *, seq, r):
    sub = V7X_SUBLANES
    nhi = r // sub
    row = lambda part, x0, x1, x2: part * seq + (x0 * r + x1) * r + x2

    def stage_a(i, carry):
        t1, t0_hi = divmod(i, nhi)
        starts = [r * r * t2 + r * t1 + sub * t0_hi for t2 in range(r)]
        xin = jnp.concatenate([_gather_tiles(p_ref, starts), _gather_tiles(q_ref, starts)],
                              axis=0).astype(BF16)
        out = _dot(ma_ref[...], xin)
        for lo in range(sub):
            for part in range(2):
                src = (lo * 2 + part) * r
                t_ref[pl.ds(row(part, t0_hi * sub + lo, t1, 0), r), :] = out[src:src + r]
        return carry

    _static_loop(r * nhi, stage_a)

    def stage_b(i, carry):
        t0, k2_hi = divmod(i, nhi)
        starts = [row(part, t0, t1, sub * k2_hi) for part in range(2) for t1 in range(r)]
        out = _dot(mb_ref[k2_hi], _gather_tiles(t_ref, starts).astype(BF16))
        for j, s in enumerate(starts):
            t_ref[pl.ds(s, sub), :] = out[j * sub:(j + 1) * sub]
        return carry

    _static_loop(r * nhi, stage_b)

    def stage_c(i, carry):
        k1, k2_hi = divmod(i, nhi)
        starts = [row(part, t0, k1, sub * k2_hi) for part in range(2) for t0 in range(r)]
        out = _dot(mc_ref[i], _gather_tiles(t_ref, starts).astype(BF16))
        for k0 in range(r):
            o_ref[pl.ds(r * r * k0 + r * k1 + sub * k2_hi, sub), :] = out[k0 * sub:(k0 + 1) * sub]
        return carry

    _static_loop(r * nhi, stage_c)


def _fnet3(pq, ma, mb, mc, *, seq, fw, ct, r):
    n = pq.shape[0]
    nct = fw // ct
    return pl.pallas_call(
        functools.partial(_fnet3_body, seq=seq, r=r),
        grid=(n // seq, nct),
        in_specs=[
            pl.BlockSpec((seq, ct), lambda b, c: (b, c)),
            pl.BlockSpec((seq, ct), lambda b, c: (b, nct + c)),
            _resident(ma.shape),
            _resident(mb.shape),
            _resident(mc.shape),
        ],
        out_specs=pl.BlockSpec((seq, ct), lambda b, c: (b, c)),
        out_shape=jax.ShapeDtypeStruct((n, fw), F32),
        scratch_shapes=[pltpu.VMEM((2 * seq, ct), F32)],
        compiler_params=_params(2),
        name="fnet_seq",
    )(pq, pq, ma, mb, mc)


def _merge_body(x_ref, z_ref, f_ref, po_ref, pp_ref, pn_ref, gt_ref, wha_ref, wfo_ref, wpl_ref,
                ps_ref, wo_ref, gpost_ref, o_ref, ext_ref, m_ref, *, tm, seq, d):
    i = pl.program_id(0)
    sub = V7X_SUBLANES
    has_prev = lax.rem(i * tm, seq) != 0
    has_next = lax.rem((i + 1) * tm, seq) != 0
    ext_ref[pl.ds(0, sub), :] = jnp.where(has_prev, pp_ref[...], 0.0)
    ext_ref[pl.ds(sub, tm), :] = po_ref[...]
    ext_ref[pl.ds(sub + tm, sub), :] = jnp.where(has_next, pn_ref[...], 0.0)

    gate = lambda q, lo, hi: gt_ref[:, q * d + lo:q * d + hi].astype(F32)
    m_ref[...] = gate(0, 0, d) * _dot(z_ref[...], wha_ref[...])
    m_ref[...] += gate(1, 0, d) * _dot(f_ref[...].astype(BF16), wfo_ref[...])

    pos = lax.rem(i * tm, seq) + lax.broadcasted_iota(jnp.int32, (tm, 1), 0)
    n_pool = len(POOL_WINDOWS)
    gw = po_ref.shape[1] // n_pool
    ow = d // n_pool
    for q, w in enumerate(POOL_WINDOWS):
        before = w // 2
        after = w - 1 - before
        cols = slice(q * gw, (q + 1) * gw)
        tot = ext_ref[pl.ds(sub - before, tm), cols]
        for s in range(-before + 1, after + 1):
            tot = tot + ext_ref[pl.ds(sub + s, tm), cols]
        lo = jnp.maximum(pos - before, 0)
        hi = jnp.minimum(pos + after, seq - 1)
        cnt = (hi - lo + 1).astype(F32)
        mq = (tot / cnt - po_ref[:, cols]).astype(BF16)
        yc = _dot(mq, wpl_ref[q]) * ps_ref[:, q * ow:(q + 1) * ow]
        m_ref[:, q * ow:(q + 1) * ow] += gate(2, q * ow, (q + 1) * ow) * yc

    y = _dot(m_ref[...].astype(BF16), wo_ref[...])
    o_ref[...] = x_ref[...] + _rms(y, gpost_ref[...])


def _merge(x2, z2, fy, po, gates, wha, wfo, wpool, pscale, w_out, g_post, layer, *, tm, seq):
    n, d = x2.shape
    pw = po.shape[1]
    sub = V7X_SUBLANES
    hb = tm // sub
    last_hb = n // sub - 1
    body = functools.partial(_merge_body, tm=tm, seq=seq, d=d)
    return pl.pallas_call(
        body,
        grid=(n // tm,),
        in_specs=[
            pl.BlockSpec((tm, d), lambda i: (i, 0)),
            pl.BlockSpec((tm, z2.shape[1]), lambda i: (i, 0)),
            pl.BlockSpec((tm, fy.shape[1]), lambda i: (i, 0)),
            pl.BlockSpec((tm, pw), lambda i: (i, 0)),
            pl.BlockSpec((sub, pw), lambda i: (jnp.maximum(i * hb - 1, 0), 0)),
            pl.BlockSpec((sub, pw), lambda i: (jnp.minimum((i + 1) * hb, last_hb), 0)),
            pl.BlockSpec((tm, gates.shape[1]), lambda i: (i, 0)),
            _resident_layer(wha, layer),
            _resident_layer(wfo, layer),
            _resident_layer(wpool, layer),
            _resident((1, d)),
            _resident(w_out.shape),
            _resident((1, d)),
        ],
        out_specs=pl.BlockSpec((tm, d), lambda i: (i, 0)),
        out_shape=jax.ShapeDtypeStruct((n, d), F32),
        scratch_shapes=[pltpu.VMEM((tm + 2 * sub, pw), F32), pltpu.VMEM((tm, d), F32)],
        compiler_params=_params(1),
        name="mixer_merge",
    )(x2, z2, fy, po, po, po, gates, wha, wfo, wpool, pscale.reshape(1, d), w_out, g_post.reshape(1, d))


def _kv_body(m_ref, g_ref, w_ref, o_ref):
    o_ref[...] = _dot(_rms(m_ref[...], g_ref[...]).astype(BF16), w_ref[...].astype(BF16)).astype(BF16)


def _kv(mem2, g, w_kv, layer, *, n_mem, tn):
    n, d = mem2.shape
    cols = w_kv.shape[2]
    return pl.pallas_call(
        _kv_body,
        grid=(cols // tn, n // n_mem),
        in_specs=[
            pl.BlockSpec((n_mem, d), lambda j, b: (b, 0)),
            pl.BlockSpec((1, d), lambda j, b: (0, 0)),
            pl.BlockSpec((None, d, tn), lambda j, b: (layer, 0, j)),
        ],
        out_specs=pl.BlockSpec((n_mem, tn), lambda j, b: (b, j)),
        out_shape=jax.ShapeDtypeStruct((n, cols), BF16),
        compiler_params=_params(2),
        name="mem_kv",
    )(mem2, g.reshape(1, d), w_kv)


def _attn_body(x_ref, gpre_ref, wq_ref, k_ref, v_ref, wo_ref, gpost_ref, o_ref, *, d):
    dh = d // N_HEADS
    x = x_ref[...]
    q = _dot(_rms(x, gpre_ref[...]).astype(BF16), wq_ref[...])
    heads = []
    for h in range(N_HEADS):
        cols = slice(h * dh, (h + 1) * dh)
        s = lax.dot_general(q[:, cols].astype(BF16), k_ref[:, cols], (((1,), (1,)), ((), ())),
                            preferred_element_type=F32) * (dh ** -0.5)
        e = jnp.exp(s - jnp.max(s, axis=-1, keepdims=True))
        p = e / jnp.sum(e, axis=-1, keepdims=True)
        heads.append(_dot(p.astype(BF16), v_ref[:, cols]))
    o = jnp.concatenate(heads, axis=1).astype(BF16)
    o_ref[...] = x + _rms(_dot(o, wo_ref[...]), gpost_ref[...])


def _attn(x2, g_pre, w_q, kv, w_o, g_post, *, tm, seq, n_mem):
    n, d = x2.shape
    tiles_per_seq = seq // tm
    return pl.pallas_call(
        functools.partial(_attn_body, d=d),
        grid=(n // tm,),
        in_specs=[
            pl.BlockSpec((tm, d), lambda i: (i, 0)),
            _resident((1, d)),
            _resident(w_q.shape),
            pl.BlockSpec((n_mem, d), lambda i: (i // tiles_per_seq, 0)),
            pl.BlockSpec((n_mem, d), lambda i: (i // tiles_per_seq, 1)),
            _resident(w_o.shape),
            _resident((1, d)),
        ],
        out_specs=pl.BlockSpec((tm, d), lambda i: (i, 0)),
        out_shape=jax.ShapeDtypeStruct((n, d), F32),
        compiler_params=_params(1),
        name="mem_attn",
    )(x2, g_pre.reshape(1, d), w_q, kv, kv, w_o, g_post.reshape(1, d))


def kernel(x, mem, g_ffn1_pre, w_ffn1_gu, w_ffn1_down, g_ffn1_post, g_mix_pre, w_in, hyena_conv_w,
           hyena_conv_b, filt_w1, filt_b1, filt_w2, filt_b2, filt_w3, filt_b3, filt_w4, filt_freq,
           hyena_d, w_hyena_out, w_fnet_out, w_pool, pool_scale, w_out, g_mix_post, g_mem_pre,
           g_mem_kv, w_q, w_kv, w_o, g_mem_post, g_ffn2_pre, w_ffn2_gu, w_ffn2_down, g_ffn2_post):
    batch, seq, d = x.shape
    n_mem = mem.shape[1]
    depth = w_in.shape[0]
    hy_w = w_hyena_out.shape[1]
    hy_cols = hyena_conv_w.shape[2]
    fw = w_fnet_out.shape[1]
    pw = w_pool.shape[1] * w_pool.shape[2]

    tm = 512
    tn = fw + pw
    tf = 512
    ct = V7X_MXU_COLS

    lhs1, gmat, lhs2 = _hyena_mats(seq)
    cs = _fnet_channel_mat(fw // FNET_GROUPS)
    fma, fmb, fmc, fr = _fnet3_mats(seq)

    bf = lambda w: w.astype(BF16)
    w_hyena_out, w_fnet_out, w_pool = map(bf, (w_hyena_out, w_fnet_out, w_pool))
    ffn_w = (bf(w_ffn1_gu[0]), bf(w_ffn1_down[0]))
    x2 = x.reshape(batch * seq, d)
    mem2 = mem.reshape(batch * n_mem, d)
    for l in range(depth):
        later = (w_ffn2_gu, w_ffn2_down, w_in, w_out, w_q, w_o)
        x2, *cast = _ffn(x2, g_ffn1_pre[l], *ffn_w, g_ffn1_post[l], tm=2 * tm, tf=tf,
                         cast=[(w, l) for w in later])
        *ffn_w, w_in_l, w_out_l, w_q_l, w_o_l = cast

        hyc, pq, po, gates = _proj(x2, g_mix_pre[l], w_in_l, hyena_conv_w[l], hyena_conv_b[l], cs,
                                   tm=2 * tm, tn=tn, seq=seq, hy_cols=hy_cols, fw=fw, pw=pw)
        h3 = _filter_mlp(filt_w1[l], filt_b1[l], filt_w2[l], filt_b2[l], filt_w3[l], filt_b3[l],
                         filt_freq[l], seq=seq)
        hspec = _filter_spectrum(h3, filt_w4[l], hyena_d[l], lhs1, gmat, seq=seq, width=hy_w, cw=ct // 2, ct=ct)
        z2 = _hyena(hyc, hspec, lhs1, gmat, lhs2, seq=seq, width=hy_w, ct=ct)
        fy = _fnet3(pq, fma, fmb, fmc, seq=seq, fw=fw, ct=ct, r=fr)
        x2 = _merge(x2, z2, fy, po, gates, w_hyena_out, w_fnet_out, w_pool,
                    pool_scale[l], w_out_l, g_mix_post[l], l, tm=tm, seq=seq)

        kv = _kv(mem2, g_mem_kv[l], w_kv, l, n_mem=batch * n_mem, tn=tn)
        x2 = _attn(x2, g_mem_pre[l], w_q_l, kv, w_o_l, g_mem_post[l], tm=tm, seq=seq, n_mem=n_mem)

        nxt = [(w_ffn1_gu, l + 1), (w_ffn1_down, l + 1)] if l + 1 < depth else []
        x2, *ffn_w = _ffn(x2, g_ffn2_pre[l], *ffn_w, g_ffn2_post[l], tm=2 * tm, tf=tf, cast=nxt)
    return x2.reshape(batch, seq, d)
```

```python
import functools
import math

import jax
import jax.numpy as jnp
import numpy as np
from jax import lax
from jax.experimental import pallas as pl
from jax.experimental.pallas import tpu as pltpu

F32 = jnp.float32
BF16 = jnp.bfloat16

RMS_EPS = 1e-6
MACARON_WEIGHT = 0.5
N_HEADS = 4
POOL_WINDOWS = (2, 4, 8, 16)
FNET_GROUPS = 4
FILTER_BANDS = 16
DECAY_TARGET = 1e-2
FAST_DECAY_PCT = 0.3
SLOW_DECAY_PCT = 1.5

V7X_SUBLANES = 8
V7X_BF16_ROWS = 16
V7X_MXU_COLS = 256
V7X_VMEM_LIMIT = 60 * 2**20

HYENA_NB = 128
HALO = V7X_BF16_ROWS
ROW_CHUNK = 256
LANE_BLOCK = 128


def _params(n_axes):
    return pltpu.CompilerParams(
        dimension_semantics=("arbitrary",) * n_axes,
        vmem_limit_bytes=V7X_VMEM_LIMIT,
    )


def _resident(shape):
    zeros = (0,) * len(shape)
    return pl.BlockSpec(shape, lambda *_: zeros, pipeline_mode=pl.Buffered(1))


def _resident_layer(stacked, layer):
    tail = (0,) * (stacked.ndim - 1)
    return pl.BlockSpec((None,) + stacked.shape[1:], lambda *_: (layer,) + tail,
                        pipeline_mode=pl.Buffered(1))


def _rms(x, g):
    ms = jnp.mean(x * x, axis=-1, keepdims=True)
    return x * lax.rsqrt(ms + RMS_EPS) * g


def _sigmoid(x):
    return 0.5 * jnp.tanh(0.5 * x) + 0.5


def _dot(a, b):
    return jnp.dot(a, b, preferred_element_type=F32)


def _dot3(a, b):
    ah = a.astype(BF16)
    al = (a - ah.astype(F32)).astype(BF16)
    bh = b.astype(BF16)
    bl = (b - bh.astype(F32)).astype(BF16)
    return _dot(ah, bh) + (_dot(al, bh) + _dot(ah, bl))


def _row_chunks(rows, fn):
    chunk = ROW_CHUNK if rows % ROW_CHUNK == 0 else rows

    def body(c, carry):
        fn(pl.ds(pl.multiple_of(c * chunk, chunk), chunk))
        return carry

    lax.fori_loop(0, rows // chunk, body, 0)


def _norm_residual_inplace(o_ref, x_ref, g):
    rows, d = o_ref.shape

    def post(r):
        sq = None
        for c in range(0, d, LANE_BLOCK):
            yb = o_ref[r, c:c + LANE_BLOCK]
            sq = yb * yb if sq is None else sq + yb * yb
        inv = lax.rsqrt(jnp.sum(sq, axis=-1, keepdims=True) * (1.0 / d) + RMS_EPS)
        for c in range(0, d, LANE_BLOCK):
            cols = slice(c, c + LANE_BLOCK)
            o_ref[r, cols] = x_ref[r, cols] + o_ref[r, cols] * inv * g[:, cols]

    _row_chunks(rows, post)


def _ffn_body(x_ref, gpre_ref, wg_ref, wu_ref, wd_ref, gpost_ref, *rest, n_cast):
    cast_in, o_ref, cast_out, u_ref = rest[:n_cast], rest[n_cast], rest[n_cast + 1:-1], rest[-1]
    k = pl.program_id(1)
    rows = x_ref.shape[0]

    @pl.when(k == 0)
    def _():
        def pre(r):
            u_ref[r, :] = _rms(x_ref[r, :], gpre_ref[...]).astype(BF16)
            o_ref[r, :] = jnp.zeros((r.size, o_ref.shape[1]), F32)
        _row_chunks(rows, pre)

    u = u_ref[...]
    a = _dot(u, wg_ref[...])
    b = _dot(u, wu_ref[...])
    h = (a * _sigmoid(a) * b).astype(BF16)
    for src, dst in zip(cast_in, cast_out):
        dst[...] = src[...].astype(BF16)
    o_ref[...] += _dot(h, wd_ref[...])

    @pl.when(k == pl.num_programs(1) - 1)
    def _():
        _norm_residual_inplace(o_ref, x_ref, MACARON_WEIGHT * gpost_ref[...])


def _cast_tiling(rows, cols, ni, nk):
    rb = rows // ni
    assert rows % ni == 0 and rb % V7X_BF16_ROWS == 0
    nc = max(c for c in range(1, nk + 1) if cols % c == 0 and (cols // c) % 128 == 0)
    return rb, cols // nc, nc


def _ffn(x2, g_pre, w_gu, w_down, g_post, *, tm, tf, cast=()):
    n, d = x2.shape
    f = w_down.shape[0]
    ni, nk = n // tm, f // tf
    in_specs = [
        pl.BlockSpec((tm, d), lambda i, k: (i, 0)),
        pl.BlockSpec((1, d), lambda i, k: (0, 0)),
        pl.BlockSpec((d, tf), lambda i, k: (0, k)),
        pl.BlockSpec((d, tf), lambda i, k: (0, k + nk)),
        pl.BlockSpec((tf, d), lambda i, k: (k, 0)),
        pl.BlockSpec((1, d), lambda i, k: (0, 0)),
    ]
    args = [x2, g_pre.reshape(1, d), w_gu, w_gu, w_down, g_post.reshape(1, d)]
    out_specs = [pl.BlockSpec((tm, d), lambda i, k: (i, 0))]
    out_shape = [jax.ShapeDtypeStruct((n, d), F32)]
    for w, layer in cast:
        rb, cb, nc = _cast_tiling(w.shape[1], w.shape[2], ni, nk)
        in_specs.append(pl.BlockSpec((None, rb, cb),
                                     lambda i, k, layer=layer, nc=nc: (layer, i, jnp.minimum(k, nc - 1))))
        args.append(w)
        out_specs.append(pl.BlockSpec((rb, cb), lambda i, k, nc=nc: (i, jnp.minimum(k, nc - 1))))
        out_shape.append(jax.ShapeDtypeStruct(w.shape[1:], BF16))
    return pl.pallas_call(
        functools.partial(_ffn_body, n_cast=len(cast)),
        grid=(ni, nk),
        in_specs=in_specs,
        out_specs=out_specs,
        out_shape=out_shape,
        scratch_shapes=[pltpu.VMEM((tm, d), BF16)],
        compiler_params=_params(2),
        name="ffn",
    )(*args)


def _proj_body(x_ref, xp_ref, xn_ref, g_ref, w_ref, cw_ref, cb_ref, cs_ref,
               hy_ref, pq_ref, po_ref, gt_ref, u_ref, *, tm, seq, n_hy, fw):
    i = pl.program_id(0)
    j = pl.program_id(1)

    @pl.when(j == 0)
    def _():
        g = g_ref[...]
        has_prev = lax.rem(i * tm, seq) != 0
        has_next = lax.rem((i + 1) * tm, seq) != 0
        u_ref[pl.ds(0, HALO), :] = jnp.where(has_prev, _rms(xp_ref[...], g), 0.0).astype(BF16)
        chunk = ROW_CHUNK if tm % ROW_CHUNK == 0 else tm

        def pre(c, carry):
            r = pl.multiple_of(c * chunk, chunk)
            u_ref[pl.ds(pl.multiple_of(HALO + r, HALO), chunk), :] = (
                _rms(x_ref[pl.ds(r, chunk), :], g).astype(BF16))
            return carry

        lax.fori_loop(0, tm // chunk, pre, 0)
        u_ref[pl.ds(HALO + tm, HALO), :] = jnp.where(has_next, _rms(xn_ref[...], g), 0.0).astype(BF16)

    @pl.when(j < n_hy)
    def _():
        h = _dot(u_ref[...], w_ref[...])
        rows = h.shape[0]
        prev = pltpu.roll(h, 1, axis=0)[HALO:HALO + tm]
        nxt = pltpu.roll(h, rows - 1, axis=0)[HALO:HALO + tm]
        y = cb_ref[...] + prev * cw_ref[0:1, :]
        y = y + h[HALO:HALO + tm] * cw_ref[1:2, :]
        y = y + nxt * cw_ref[2:3, :]
        hy_ref[...] = y.astype(BF16)

    @pl.when(j == n_hy)
    def _():
        c = _dot(u_ref[pl.ds(HALO, tm), :], w_ref[...])
        po_ref[...] = c[:, fw:]
        fb = c[:, :fw].astype(BF16)
        gw = fw // FNET_GROUPS
        res = [_dot(fb[:, q * gw:(q + 1) * gw], cs_ref[...]) for q in range(FNET_GROUPS)]
        pq_ref[...] = jnp.concatenate([r[:, :gw] for r in res] + [r[:, gw:] for r in res], axis=1)

    @pl.when(j > n_hy)
    def _():
        c = _dot(u_ref[pl.ds(HALO, tm), :], w_ref[...])
        gt_ref[...] = _sigmoid(c).astype(BF16)


def _proj(x2, g, w_in, conv_w, conv_b, cs, *, tm, tn, seq, hy_cols, fw, pw):
    n, d = x2.shape
    cols = w_in.shape[1]
    n_hy = hy_cols // tn
    assert hy_cols % tn == 0 and fw + pw == tn and (cols - hy_cols - tn) % tn == 0
    nj = cols // tn
    n_gate = cols - hy_cols - tn
    hb = tm // HALO
    last_hb = n // HALO - 1
    body = functools.partial(_proj_body, tm=tm, seq=seq, n_hy=n_hy, fw=fw)
    return pl.pallas_call(
        body,
        grid=(n // tm, nj),
        in_specs=[
            pl.BlockSpec((tm, d), lambda i, j: (i, 0)),
            pl.BlockSpec((HALO, d), lambda i, j: (jnp.maximum(i * hb - 1, 0), 0)),
            pl.BlockSpec((HALO, d), lambda i, j: (jnp.minimum((i + 1) * hb, last_hb), 0)),
            pl.BlockSpec((1, d), lambda i, j: (0, 0)),
            pl.BlockSpec((d, tn), lambda i, j: (0, j)),
            pl.BlockSpec((3, tn), lambda i, j: (0, jnp.minimum(j, n_hy - 1))),
            pl.BlockSpec((1, tn), lambda i, j: (0, jnp.minimum(j, n_hy - 1))),
            _resident(cs.shape),
        ],
        out_specs=[
            pl.BlockSpec((tm, tn), lambda i, j: (i, jnp.minimum(j, n_hy - 1))),
            pl.BlockSpec((tm, 2 * fw), lambda i, j: (i, 0)),
            pl.BlockSpec((tm, pw), lambda i, j: (i, 0)),
            pl.BlockSpec((tm, tn), lambda i, j: (i, jnp.maximum(j - n_hy - 1, 0))),
        ],
        out_shape=[
            jax.ShapeDtypeStruct((n, hy_cols), BF16),
            jax.ShapeDtypeStruct((n, 2 * fw), F32),
            jax.ShapeDtypeStruct((n, pw), F32),
            jax.ShapeDtypeStruct((n, n_gate), BF16),
        ],
        scratch_shapes=[pltpu.VMEM((tm + 2 * HALO, d), BF16)],
        compiler_params=_params(2),
        name="mixer_proj",
    )(x2, x2, x2, g.reshape(1, d), w_in, conv_w, conv_b.reshape(1, hy_cols), cs)


def _hyena_slot_block(s, na):
    half = na // 2
    return s if s <= half else (half + 1) + (s - half)


def _hyena_mats(seq):
    nb = HYENA_NB
    na = 2 * seq // nb
    half = na // 2
    eye = np.eye(V7X_SUBLANES)
    a = np.arange(half)[None, :]
    f1 = np.zeros((na, half))
    k_re = np.arange(half + 1)[:, None]
    f1[: half + 1] = np.cos(2 * np.pi * k_re * a / na)
    k_im = np.arange(1, half)[:, None]
    f1[half + 1:] = -np.sin(2 * np.pi * k_im * a / na)
    lhs1 = np.kron(f1, eye)
    b = np.arange(nb)[None, :]
    kb = np.arange(nb)[:, None]
    g = np.zeros((half + 1, 2 * nb, 2 * nb))
    for ka in range(half + 1):
        ang = 2 * np.pi * (kb * b / nb + b * ka / (2 * seq))
        gre, gim = np.cos(ang), -np.sin(ang)
        g[ka] = np.block([[gre, -gim], [gim, gre]])
    a_col = np.arange(half)[:, None]
    f2 = np.zeros((half, na))
    wgt = np.full(half + 1, 2.0)
    wgt[0] = wgt[half] = 1.0
    f2[:, : half + 1] = wgt[None, :] * np.cos(2 * np.pi * a_col * np.arange(half + 1)[None, :] / na)
    f2[:, half + 1:] = -2.0 * np.sin(2 * np.pi * a_col * np.arange(1, half)[None, :] / na)
    lhs2 = np.kron(f2 / (2 * seq), eye)
    return (jnp.asarray(lhs1, BF16), jnp.asarray(g, BF16), jnp.asarray(lhs2, BF16))


def _fnet_channel_mat(group):
    cc = np.arange(group)[:, None]
    mm = np.arange(group)[None, :]
    angc = 2 * np.pi * cc * mm / group
    return jnp.asarray(np.concatenate([np.cos(angc), -np.sin(angc)], axis=1) / math.sqrt(group), BF16)


def _fnet3_mats(seq):
    r = int(round(seq ** (1.0 / 3.0)))
    sub = V7X_SUBLANES
    assert r ** 3 == seq and r % sub == 0
    idx = np.arange(r)
    ang = 2 * np.pi * idx[:, None] * idx[None, :] / r
    blk = np.array([[np.cos(ang), np.sin(ang)], [-np.sin(ang), np.cos(ang)]])
    ma = np.zeros((sub, 2, r, 2, r, sub))
    for lo in range(sub):
        ma[lo, :, :, :, :, lo] = blk.transpose(0, 2, 1, 3)
    ma = ma.reshape(2 * r * sub, 2 * r * sub)
    mb = np.zeros((r // sub, 2, r, sub, 2, r, sub))
    for hi in range(r // sub):
        for lo in range(sub):
            k2 = hi * sub + lo
            ph = 2 * np.pi * (idx[:, None] * idx[None, :] / r + idx[None, :] * k2 / r ** 2)
            c, s = np.cos(ph), np.sin(ph)
            mb[hi, 0, :, lo, 0, :, lo] = c
            mb[hi, 0, :, lo, 1, :, lo] = s
            mb[hi, 1, :, lo, 0, :, lo] = -s
            mb[hi, 1, :, lo, 1, :, lo] = c
    mb = mb.reshape(r // sub, 2 * r * sub, 2 * r * sub)
    mc = np.zeros((r, r // sub, r, sub, 2, r, sub))
    for k1 in range(r):
        for hi in range(r // sub):
            for lo in range(sub):
                k2 = hi * sub + lo
                ph = 2 * np.pi * (idx[:, None] * idx[None, :] / r
                                  + idx[None, :] * k1 / r ** 2 + idx[None, :] * k2 / seq)
                mc[k1, hi, :, lo, 0, :, lo] = np.cos(ph)
                mc[k1, hi, :, lo, 1, :, lo] = np.sin(ph)
    mc = mc.reshape(r * (r // sub), r * sub, 2 * r * sub) / math.sqrt(seq)
    return jnp.asarray(ma, BF16), jnp.asarray(mb, BF16), jnp.asarray(mc, BF16), r


def _static_loop(n, body):
    for i in range(n):
        body(i, 0)


def _gather_tiles(ref, starts):
    tiles = [ref[pl.ds(pl.multiple_of(s, V7X_SUBLANES), V7X_SUBLANES), :] for s in starts]
    return jnp.concatenate(tiles, axis=0)


def _hyena_stage1(z_ref, e_ref, lhs1_ref, seq):
    nb = HYENA_NB
    na = 2 * seq // nb
    half = na // 2

    def body(i, carry):
        off = i * V7X_SUBLANES
        xg = _gather_tiles(z_ref, [nb * a + off for a in range(half)]).astype(BF16)
        out = _dot(lhs1_ref[...], xg)
        for s in range(na):
            row = _hyena_slot_block(s, na) * nb + off
            e_ref[pl.ds(pl.multiple_of(row, V7X_SUBLANES), V7X_SUBLANES), :] = (
                out[s * V7X_SUBLANES:(s + 1) * V7X_SUBLANES])
        return carry

    _static_loop(nb // V7X_SUBLANES, body)


def _hyena_zero_unused(e_ref, seq):
    nb = HYENA_NB
    half = seq // nb
    zero = jnp.zeros((nb, e_ref.shape[1]), F32)
    e_ref[pl.ds((half + 1) * nb, nb), :] = zero
    e_ref[pl.ds((2 * half + 1) * nb, nb), :] = zero


def _hyena_load_ka(e_ref, ka, seq):
    nb = HYENA_NB
    half = seq // nb
    re = e_ref[pl.ds(pl.multiple_of(ka * nb, nb), nb), :]
    im = e_ref[pl.ds(pl.multiple_of((half + 1 + ka) * nb, nb), nb), :]
    return jnp.concatenate([re, im], axis=0)


def _filter_mlp_body(bands_ref, w1t_ref, w1c_ref, w1s_ref, b1_ref, w2_ref, b2_ref, w3_ref, b3_ref,
                     fr_ref, o_ref, *, seq):
    idx = lax.broadcasted_iota(jnp.int32, (1, 2 * seq), 1)
    p = jnp.where(idx < seq, idx, 2 * seq - idx).astype(F32)
    t = p / (seq - 1.0)
    ang = bands_ref[...] * ((2.0 * math.pi / seq) * p)
    fr = fr_ref[...]
    pre = (w1t_ref[...] * t
           + _dot3(w1c_ref[...], jnp.cos(ang))
           + _dot3(w1s_ref[...], -jnp.sin(ang))
           + b1_ref[...])
    h = jnp.sin(fr * pre)
    h = jnp.sin(fr * (_dot3(w2_ref[...], h) + b2_ref[...]))
    h = jnp.sin(fr * (_dot3(w3_ref[...], h) + b3_ref[...]))
    o_ref[...] = jnp.concatenate([h[:, :seq], h[:, seq:]], axis=0).T


def _filter_mlp(fw1, fb1, fw2, fb2, fw3, fb3, freq, *, seq):
    hid = fw2.shape[0]
    bands = np.linspace(1e-4, FILTER_BANDS - 1, FILTER_BANDS).astype(np.float32).reshape(-1, 1)
    col = lambda v: v.reshape(hid, 1)
    args = (jnp.asarray(bands), fw1[0].reshape(hid, 1), fw1[1:1 + FILTER_BANDS].T,
            fw1[1 + FILTER_BANDS:].T, col(fb1), fw2.T, col(fb2), fw3.T, col(fb3), col(freq))
    return pl.pallas_call(
        functools.partial(_filter_mlp_body, seq=seq),
        grid=(1,),
        in_specs=[_resident(a.shape) for a in args],
        out_specs=pl.BlockSpec((seq, 2 * hid), lambda i: (0, 0)),
        out_shape=jax.ShapeDtypeStruct((seq, 2 * hid), F32),
        compiler_params=_params(1),
        name="filter_mlp",
    )(*args)


def _filter_spec_body(h3_ref, w4f_ref, w4b_ref, dl_ref, d_ref, lhs1_ref, g_ref, *rest, seq, cw, n_cast):
    cast_in, o_ref, cast_out = rest[:n_cast], rest[n_cast], rest[n_cast + 1:2 * n_cast + 1]
    z_ref, e_ref = rest[-2:]
    for src, dst in zip(cast_in, cast_out):
        dst[...] = src[...].astype(BF16)
    nb = HYENA_NB
    half = seq // nb
    dl = dl_ref[...]
    zero = jnp.zeros_like(w4f_ref[...])
    w4 = jnp.concatenate([jnp.concatenate([w4f_ref[...], zero], axis=1),
                          jnp.concatenate([zero, w4b_ref[...]], axis=1)], axis=0)
    lane = lax.broadcasted_iota(jnp.int32, (1, 2 * cw), 1)
    chunk = ROW_CHUNK if seq % ROW_CHUNK == 0 else seq
    for c in range(seq // chunk):
        row = c * chunk + lax.broadcasted_iota(jnp.int32, (chunk, 1), 0)
        rowf = row.astype(F32)
        h4 = _dot3(h3_ref[pl.ds(c * chunk, chunk), :], w4)
        dec = jnp.concatenate([jnp.exp(-(rowf / (seq - 1.0)) * dl),
                               jnp.exp(-((seq - rowf) / (seq - 1.0)) * dl)], axis=1)
        z_ref[pl.ds(c * chunk, chunk), :] = jnp.where((row == 0) & (lane >= cw), 0.0, h4 * dec)

    _hyena_zero_unused(e_ref, seq)
    _hyena_stage1(z_ref, e_ref, lhs1_ref, seq)
    dsk = d_ref[0]

    def body(ka, carry):
        ein = _hyena_load_ka(e_ref, ka, seq).astype(BF16)
        x = _dot(g_ref[ka], ein)
        sgn = 1.0 - 2.0 * (ka % 2)
        hsp = x[:, 0:cw] + sgn * x[:, cw:2 * cw]
        part = lax.broadcasted_iota(jnp.int32, (2 * nb, 1), 0)
        hsp = hsp + jnp.where(part < nb, dsk, 0.0)
        o_ref[0, pl.ds(pl.multiple_of(ka * 2 * nb, 2 * nb), 2 * nb), :] = hsp.astype(BF16)
        return carry

    _static_loop(half + 1, body)


def _filter_spectrum(h3, fw4, d_skip, lhs1, gmat, *, seq, width, cw, ct, cast=()):
    hid = h3.shape[1] // 2
    orders = d_skip.shape[0]
    nct = width // cw
    per = ct // cw
    nb = HYENA_NB
    half = seq // nb
    max_decay = math.log(DECAY_TARGET) / FAST_DECAY_PCT
    min_decay = math.log(DECAY_TARGET) / SLOW_DECAY_PCT
    deltas = np.abs(np.linspace(min_decay, max_decay, width)).astype(np.float32).reshape(1, width)
    rows = (half + 1) * 2 * nb
    body = functools.partial(_filter_spec_body, seq=seq, cw=cw, n_cast=len(cast))
    in_specs = [
        _resident(h3.shape),
        pl.BlockSpec((hid, cw), lambda o, c: (0, o * 2 * nct + c)),
        pl.BlockSpec((hid, cw), lambda o, c: (0, o * 2 * nct + nct + c)),
        pl.BlockSpec((1, cw), lambda o, c: (0, c)),
        pl.BlockSpec((1, 1, cw), lambda o, c: (o, 0, c)),
        _resident(lhs1.shape),
        _resident(gmat.shape),
    ]
    args = [h3, fw4, fw4, jnp.asarray(deltas), d_skip.reshape(orders, 1, width), lhs1, gmat]
    out_specs = [pl.BlockSpec((None, 1, rows, cw), lambda o, c: (c // per, o, 0, c % per))]
    out_shape = [jax.ShapeDtypeStruct((width // ct, orders, rows, ct), BF16)]
    steps = orders * nct
    for w, layer in cast:
        rb = w.shape[1] // steps
        assert w.shape[1] % steps == 0 and rb % V7X_BF16_ROWS == 0
        in_specs.append(pl.BlockSpec((None, rb, w.shape[2]), lambda o, c, layer=layer: (layer, o * nct + c, 0)))
        args.append(w)
        out_specs.append(pl.BlockSpec((rb, w.shape[2]), lambda o, c: (o * nct + c, 0)))
        out_shape.append(jax.ShapeDtypeStruct(w.shape[1:], BF16))
    return pl.pallas_call(
        body,
        grid=(orders, nct),
        in_specs=in_specs,
        out_specs=out_specs,
        out_shape=out_shape,
        scratch_shapes=[pltpu.VMEM((seq, 2 * cw), F32),
                        pltpu.VMEM((2 * (half + 1) * nb, 2 * cw), F32)],
        compiler_params=_params(2),
        name="filter_spectrum",
    )(*args)


def _hyena_body(v_ref, g1_ref, g2_ref, h_ref, lhs1_ref, g_ref, lhs2_ref, o_ref, z_ref, e_ref, *, seq):
    nb = HYENA_NB
    na = 2 * seq // nb
    half = na // 2
    pair = V7X_BF16_ROWS
    chunk = 512 if seq % 512 == 0 else seq

    def load_v(c, carry):
        r = pl.multiple_of(c * chunk, chunk)
        z_ref[pl.ds(r, chunk), :] = v_ref[pl.ds(r, chunk), :].astype(F32)
        return carry

    lax.fori_loop(0, seq // chunk, load_v, 0)
    _hyena_zero_unused(e_ref, seq)

    for order, gate_ref in enumerate((g1_ref, g2_ref)):
        _hyena_stage1(z_ref, e_ref, lhs1_ref, seq)

        group = 3 if (half + 1) % 3 == 0 else 1

        def spectral(it, carry):
            kas = [it * group + u for u in range(group)]
            eins = [_hyena_load_ka(e_ref, ka, seq).astype(BF16) for ka in kas]
            ys = []
            for ka, ein in zip(kas, eins):
                gk = g_ref[ka]
                x = _dot(gk, ein)
                hsp = h_ref[order, pl.ds(pl.multiple_of(ka * 2 * nb, 2 * nb), 2 * nb), :].astype(F32)
                xre, xim, hre, him = x[:nb], x[nb:], hsp[:nb], hsp[nb:]
                zsp = jnp.concatenate([xre * hre - xim * him, xre * him + xim * hre], axis=0).astype(BF16)
                ys.append(lax.dot_general(gk, zsp, (((0,), (0,)), ((), ())), preferred_element_type=F32))
            for ka, y in zip(kas, ys):
                e_ref[pl.ds(pl.multiple_of(ka * nb, nb), nb), :] = y[:nb]
                keep = jnp.where((ka == 0) | (ka == half), 0.0, 1.0)
                e_ref[pl.ds(pl.multiple_of((half + 1 + ka) * nb, nb), nb), :] = y[nb:] * keep
            return carry

        _static_loop((half + 1) // group, spectral)

        def inverse(i, carry):
            outs = []
            for q in range(pair // V7X_SUBLANES):
                off = i * pair + q * V7X_SUBLANES
                yg = _gather_tiles(e_ref, [_hyena_slot_block(s, na) * nb + off for s in range(na)])
                outs.append(_dot(lhs2_ref[...], yg.astype(BF16)))
            for a in range(half):
                sl = slice(a * V7X_SUBLANES, (a + 1) * V7X_SUBLANES)
                conv = jnp.concatenate([o_[sl] for o_ in outs], axis=0)
                r = pl.multiple_of(a * nb + i * pair, pair)
                res = gate_ref[pl.ds(r, pair), :].astype(F32) * conv
                if order == 0:
                    z_ref[pl.ds(r, pair), :] = res
                else:
                    o_ref[pl.ds(r, pair), :] = res.astype(BF16)
            return carry

        _static_loop(nb // pair, inverse)


def _hyena(hyc, hspec, lhs1, gmat, lhs2, *, seq, width, ct):
    n = hyc.shape[0]
    nct = width // ct
    nb = HYENA_NB
    half = seq // nb
    return pl.pallas_call(
        functools.partial(_hyena_body, seq=seq),
        grid=(nct, n // seq),
        in_specs=[
            pl.BlockSpec((seq, ct), lambda c, b: (b, c)),
            pl.BlockSpec((seq, ct), lambda c, b: (b, nct + c)),
            pl.BlockSpec((seq, ct), lambda c, b: (b, 2 * nct + c)),
            pl.BlockSpec((None,) + hspec.shape[1:], lambda c, b: (c, 0, 0, 0)),
            _resident(lhs1.shape),
            _resident(gmat.shape),
            _resident(lhs2.shape),
        ],
        out_specs=pl.BlockSpec((seq, ct), lambda c, b: (b, c)),
        out_shape=jax.ShapeDtypeStruct((n, width), BF16),
        scratch_shapes=[pltpu.VMEM((seq, ct), F32),
                        pltpu.VMEM((2 * (half + 1) * nb, ct), F32)],
        compiler_params=_params(2),
        name="hyena",
    )(hyc, hyc, hyc, hspec, lhs1, gmat, lhs2)


def _fnet3_body(p_ref, q_ref, ma_ref, mb_ref, mc_ref, o_ref, t_ref, *, seq, r):
    sub = V7X_SUBLANES
    nhi = r // sub
    row = lambda part, x0, x1, x2: part * seq + (x0 * r + x1) * r + x2

    def stage_a(i, carry):
        t1, t0_hi = divmod(i, nhi)
        starts = [r * r * t2 + r * t1 + sub * t0_hi for t2 in range(r)]
        xin = jnp.concatenate([_gather_tiles(p_ref, starts), _gather_tiles(q_ref, starts)],
                              axis=0).astype(BF16)
        out = _dot(ma_ref[...], xin)
        for lo in range(sub):
            for part in range(2):
                src = (lo * 2 + part) * r
                t_ref[pl.ds(row(part, t0_hi * sub + lo, t1, 0), r), :] = out[src:src + r]
        return carry

    _static_loop(r * nhi, stage_a)

    def stage_b(i, carry):
        t0, k2_hi = divmod(i, nhi)
        starts = [row(part, t0, t1, sub * k2_hi) for part in range(2) for t1 in range(r)]
        out = _dot(mb_ref[k2_hi], _gather_tiles(t_ref, starts).astype(BF16))
        for j, s in enumerate(starts):
            t_ref[pl.ds(s, sub), :] = out[j * sub:(j + 1) * sub]
        return carry

    _static_loop(r * nhi, stage_b)

    def stage_c(i, carry):
        k1, k2_hi = divmod(i, nhi)
        starts = [row(part, t0, k1, sub * k2_hi) for part in range(2) for t0 in range(r)]
        out = _dot(mc_ref[i], _gather_tiles(t_ref, starts).astype(BF16))
        for k0 in range(r):
            o_ref[pl.ds(r * r * k0 + r * k1 + sub * k2_hi, sub), :] = out[k0 * sub:(k0 + 1) * sub]
        return carry

    _static_loop(r * nhi, stage_c)


def _fnet3(pq, ma, mb, mc, *, seq, fw, ct, r):
    n = pq.shape[0]
    nct = fw // ct
    return pl.pallas_call(
        functools.partial(_fnet3_body, seq=seq, r=r),
        grid=(n // seq, nct),
        in_specs=[
            pl.BlockSpec((seq, ct), lambda b, c: (b, c)),
            pl.BlockSpec((seq, ct), lambda b, c: (b, nct + c)),
            _resident(ma.shape),
            _resident(mb.shape),
            _resident(mc.shape),
        ],
        out_specs=pl.BlockSpec((seq, ct), lambda b, c: (b, c)),
        out_shape=jax.ShapeDtypeStruct((n, fw), F32),
        scratch_shapes=[pltpu.VMEM((2 * seq, ct), F32)],
        compiler_params=_params(2),
        name="fnet_seq",
    )(pq, pq, ma, mb, mc)


def _merge_body(x_ref, z_ref, f_ref, po_ref, pp_ref, pn_ref, gt_ref, wha_ref, wfo_ref, wpl_ref,
                ps_ref, wo_ref, gpost_ref, o_ref, ext_ref, m_ref, *, tm, seq, d):
    i = pl.program_id(0)
    sub = V7X_SUBLANES
    has_prev = lax.rem(i * tm, seq) != 0
    has_next = lax.rem((i + 1) * tm, seq) != 0
    ext_ref[pl.ds(0, sub), :] = jnp.where(has_prev, pp_ref[...], 0.0)
    ext_ref[pl.ds(sub, tm), :] = po_ref[...]
    ext_ref[pl.ds(sub + tm, sub), :] = jnp.where(has_next, pn_ref[...], 0.0)

    gate = lambda q, lo, hi: gt_ref[:, q * d + lo:q * d + hi].astype(F32)
    m_ref[...] = gate(0, 0, d) * _dot(z_ref[...], wha_ref[...])
    m_ref[...] += gate(1, 0, d) * _dot(f_ref[...].astype(BF16), wfo_ref[...])

    pos = lax.rem(i * tm, seq) + lax.broadcasted_iota(jnp.int32, (tm, 1), 0)
    n_pool = len(POOL_WINDOWS)
    gw = po_ref.shape[1] // n_pool
    ow = d // n_pool
    for q, w in enumerate(POOL_WINDOWS):
        before = w // 2
        after = w - 1 - before
        cols = slice(q * gw, (q + 1) * gw)
        tot = ext_ref[pl.ds(sub - before, tm), cols]
        for s in range(-before + 1, after + 1):
            tot = tot + ext_ref[pl.ds(sub + s, tm), cols]
        lo = jnp.maximum(pos - before, 0)
        hi = jnp.minimum(pos + after, seq - 1)
        cnt = (hi - lo + 1).astype(F32)
        mq = (tot / cnt - po_ref[:, cols]).astype(BF16)
        yc = _dot(mq, wpl_ref[q]) * ps_ref[:, q * ow:(q + 1) * ow]
        m_ref[:, q * ow:(q + 1) * ow] += gate(2, q * ow, (q + 1) * ow) * yc

    y = _dot(m_ref[...].astype(BF16), wo_ref[...])
    o_ref[...] = x_ref[...] + _rms(y, gpost_ref[...])


def _merge(x2, z2, fy, po, gates, wha, wfo, wpool, pscale, w_out, g_post, layer, *, tm, seq):
    n, d = x2.shape
    pw = po.shape[1]
    sub = V7X_SUBLANES
    hb = tm // sub
    last_hb = n // sub - 1
    body = functools.partial(_merge_body, tm=tm, seq=seq, d=d)
    return pl.pallas_call(
        body,
        grid=(n // tm,),
        in_specs=[
            pl.BlockSpec((tm, d), lambda i: (i, 0)),
            pl.BlockSpec((tm, z2.shape[1]), lambda i: (i, 0)),
            pl.BlockSpec((tm, fy.shape[1]), lambda i: (i, 0)),
            pl.BlockSpec((tm, pw), lambda i: (i, 0)),
            pl.BlockSpec((sub, pw), lambda i: (jnp.maximum(i * hb - 1, 0), 0)),
            pl.BlockSpec((sub, pw), lambda i: (jnp.minimum((i + 1) * hb, last_hb), 0)),
            pl.BlockSpec((tm, gates.shape[1]), lambda i: (i, 0)),
            _resident_layer(wha, layer),
            _resident_layer(wfo, layer),
            _resident_layer(wpool, layer),
            _resident((1, d)),
            _resident(w_out.shape),
            _resident((1, d)),
        ],
        out_specs=pl.BlockSpec((tm, d), lambda i: (i, 0)),
        out_shape=jax.ShapeDtypeStruct((n, d), F32),
        scratch_shapes=[pltpu.VMEM((tm + 2 * sub, pw), F32), pltpu.VMEM((tm, d), F32)],
        compiler_params=_params(1),
        name="mixer_merge",
    )(x2, z2, fy, po, po, po, gates, wha, wfo, wpool, pscale.reshape(1, d), w_out, g_post.reshape(1, d))


def _kv_body(m_ref, g_ref, w_ref, o_ref):
    o_ref[...] = _dot(_rms(m_ref[...], g_ref[...]).astype(BF16), w_ref[...].astype(BF16)).astype(BF16)


def _kv(mem2, g, w_kv, layer, *, n_mem, tn):
    n, d = mem2.shape
    cols = w_kv.shape[2]
    return pl.pallas_call(
        _kv_body,
        grid=(cols // tn, n // n_mem),
        in_specs=[
            pl.BlockSpec((n_mem, d), lambda j, b: (b, 0)),
            pl.BlockSpec((1, d), lambda j, b: (0, 0)),
            pl.BlockSpec((None, d, tn), lambda j, b: (layer, 0, j)),
        ],
        out_specs=pl.BlockSpec((n_mem, tn), lambda j, b: (b, j)),
        out_shape=jax.ShapeDtypeStruct((n, cols), BF16),
        compiler_params=_params(2),
        name="mem_kv",
    )(mem2, g.reshape(1, d), w_kv)


def _attn_body(x_ref, gpre_ref, wq_ref, k_ref, v_ref, wo_ref, gpost_ref, o_ref, *, d):
    dh = d // N_HEADS
    x = x_ref[...]
    q = _dot(_rms(x, gpre_ref[...]).astype(BF16), wq_ref[...])
    heads = []
    for h in range(N_HEADS):
        cols = slice(h * dh, (h + 1) * dh)
        s = lax.dot_general(q[:, cols].astype(BF16), k_ref[:, cols], (((1,), (1,)), ((), ())),
                            preferred_element_type=F32) * (dh ** -0.5)
        e = jnp.exp(s - jnp.max(s, axis=-1, keepdims=True))
        p = e / jnp.sum(e, axis=-1, keepdims=True)
        heads.append(_dot(p.astype(BF16), v_ref[:, cols]))
    o = jnp.concatenate(heads, axis=1).astype(BF16)
    o_ref[...] = x + _rms(_dot(o, wo_ref[...]), gpost_ref[...])


def _attn(x2, g_pre, w_q, kv, w_o, g_post, *, tm, seq, n_mem):
    n, d = x2.shape
    tiles_per_seq = seq // tm
    return pl.pallas_call(
        functools.partial(_attn_body, d=d),
        grid=(n // tm,),
        in_specs=[
            pl.BlockSpec((tm, d), lambda i: (i, 0)),
            _resident((1, d)),
            _resident(w_q.shape),
            pl.BlockSpec((n_mem, d), lambda i: (i // tiles_per_seq, 0)),
            pl.BlockSpec((n_mem, d), lambda i: (i // tiles_per_seq, 1)),
            _resident(w_o.shape),
            _resident((1, d)),
        ],
        out_specs=pl.BlockSpec((tm, d), lambda i: (i, 0)),
        out_shape=jax.ShapeDtypeStruct((n, d), F32),
        compiler_params=_params(1),
        name="mem_attn",
    )(x2, g_pre.reshape(1, d), w_q, kv, kv, w_o, g_post.reshape(1, d))


def kernel(x, mem, g_ffn1_pre, w_ffn1_gu, w_ffn1_down, g_ffn1_post, g_mix_pre, w_in, hyena_conv_w,
           hyena_conv_b, filt_w1, filt_b1, filt_w2, filt_b2, filt_w3, filt_b3, filt_w4, filt_freq,
           hyena_d, w_hyena_out, w_fnet_out, w_pool, pool_scale, w_out, g_mix_post, g_mem_pre,
           g_mem_kv, w_q, w_kv, w_o, g_mem_post, g_ffn2_pre, w_ffn2_gu, w_ffn2_down, g_ffn2_post):
    batch, seq, d = x.shape
    n_mem = mem.shape[1]
    depth = w_in.shape[0]
    hy_w = w_hyena_out.shape[1]
    hy_cols = hyena_conv_w.shape[2]
    fw = w_fnet_out.shape[1]
    pw = w_pool.shape[1] * w_pool.shape[2]

    tm = 512
    tn = fw + pw
    tf = 512
    ct = V7X_MXU_COLS

    lhs1, gmat, lhs2 = _hyena_mats(seq)
    cs = _fnet_channel_mat(fw // FNET_GROUPS)
    fma, fmb, fmc, fr = _fnet3_mats(seq)

    bf = lambda w: w.astype(BF16)
    w_hyena_out, w_fnet_out, w_pool = map(bf, (w_hyena_out, w_fnet_out, w_pool))
    x2 = x.reshape(batch * seq, d)
    mem2 = mem.reshape(batch * n_mem, d)
    for l in range(depth):
        h3 = _filter_mlp(filt_w1[l], filt_b1[l], filt_w2[l], filt_b2[l], filt_w3[l], filt_b3[l],
                         filt_freq[l], seq=seq)
        first = [(w_ffn1_gu, 0), (w_ffn1_down, 0)] if l == 0 else []
        hspec, *first_w = _filter_spectrum(h3, filt_w4[l], hyena_d[l], lhs1, gmat, seq=seq, width=hy_w,
                                           cw=ct // 2, ct=ct, cast=first)
        if l == 0:
            ffn_w = first_w

        later = (w_ffn2_gu, w_ffn2_down, w_in, w_out, w_q, w_o)
        x2, *cast = _ffn(x2, g_ffn1_pre[l], *ffn_w, g_ffn1_post[l], tm=2 * tm, tf=tf,
                         cast=[(w, l) for w in later])
        *ffn_w, w_in_l, w_out_l, w_q_l, w_o_l = cast

        hyc, pq, po, gates = _proj(x2, g_mix_pre[l], w_in_l, hyena_conv_w[l], hyena_conv_b[l], cs,
                                   tm=2 * tm, tn=tn, seq=seq, hy_cols=hy_cols, fw=fw, pw=pw)
        z2 = _hyena(hyc, hspec, lhs1, gmat, lhs2, seq=seq, width=hy_w, ct=ct)
        fy = _fnet3(pq, fma, fmb, fmc, seq=seq, fw=fw, ct=ct, r=fr)
        x2 = _merge(x2, z2, fy, po, gates, w_hyena_out, w_fnet_out, w_pool,
                    pool_scale[l], w_out_l, g_mix_post[l], l, tm=tm, seq=seq)

        kv = _kv(mem2, g_mem_kv[l], w_kv, l, n_mem=batch * n_mem, tn=tn)
        x2 = _attn(x2, g_mem_pre[l], w_q_l, kv, w_o_l, g_mem_post[l], tm=tm, seq=seq, n_mem=n_mem)

        nxt = [(w_ffn1_gu, l + 1), (w_ffn1_down, l + 1)] if l + 1 < depth else []
        x2, *ffn_w = _ffn(x2, g_ffn2_pre[l], *ffn_w, g_ffn2_post[l], tm=2 * tm, tf=tf, cast=nxt)
    return x2.reshape(batch, seq, d)
```

```python
import functools
import math

import jax
import jax.numpy as jnp
import numpy as np
from jax import lax
from jax.experimental import pallas as pl
from jax.experimental.pallas import tpu as pltpu

F32 = jnp.float32
BF16 = jnp.bfloat16

RMS_EPS = 1e-6
MACARON_WEIGHT = 0.5
N_HEADS = 4
POOL_WINDOWS = (2, 4, 8, 16)
FNET_GROUPS = 4
FILTER_BANDS = 16
DECAY_TARGET = 1e-2
FAST_DECAY_PCT = 0.3
SLOW_DECAY_PCT = 1.5

V7X_SUBLANES = 8
V7X_BF16_ROWS = 16
V7X_MXU_COLS = 256
V7X_VMEM_LIMIT = 60 * 2**20

HYENA_NB = 128
HALO = V7X_BF16_ROWS
ROW_CHUNK = 256
LANE_BLOCK = 128


def _params(n_axes):
    return pltpu.CompilerParams(
        dimension_semantics=("arbitrary",) * n_axes,
        vmem_limit_bytes=V7X_VMEM_LIMIT,
    )


def _resident(shape):
    zeros = (0,) * len(shape)
    return pl.BlockSpec(shape, lambda *_: zeros, pipeline_mode=pl.Buffered(1))


def _resident_layer(stacked, layer):
    tail = (0,) * (stacked.ndim - 1)
    return pl.BlockSpec((None,) + stacked.shape[1:], lambda *_: (layer,) + tail,
                        pipeline_mode=pl.Buffered(1))


def _rms(x, g):
    ms = jnp.mean(x * x, axis=-1, keepdims=True)
    return x * lax.rsqrt(ms + RMS_EPS) * g


def _sigmoid(x):
    return 0.5 * jnp.tanh(0.5 * x) + 0.5


def _dot(a, b):
    return jnp.dot(a, b, preferred_element_type=F32)


def _dot3(a, b):
    ah = a.astype(BF16)
    al = (a - ah.astype(F32)).astype(BF16)
    bh = b.astype(BF16)
    bl = (b - bh.astype(F32)).astype(BF16)
    return _dot(ah, bh) + (_dot(al, bh) + _dot(ah, bl))


def _row_chunks(rows, fn):
    chunk = ROW_CHUNK if rows % ROW_CHUNK == 0 else rows

    def body(c, carry):
        fn(pl.ds(pl.multiple_of(c * chunk, chunk), chunk))
        return carry

    lax.fori_loop(0, rows // chunk, body, 0)


def _norm_residual_inplace(o_ref, x_ref, g):
    rows, d = o_ref.shape

    def post(r):
        sq = None
        for c in range(0, d, LANE_BLOCK):
            yb = o_ref[r, c:c + LANE_BLOCK]
            sq = yb * yb if sq is None else sq + yb * yb
        inv = lax.rsqrt(jnp.sum(sq, axis=-1, keepdims=True) * (1.0 / d) + RMS_EPS)
        for c in range(0, d, LANE_BLOCK):
            cols = slice(c, c + LANE_BLOCK)
            o_ref[r, cols] = x_ref[r, cols] + o_ref[r, cols] * inv * g[:, cols]

    _row_chunks(rows, post)


def _ffn_body(x_ref, gpre_ref, wg_ref, wu_ref, wd_ref, gpost_ref, *rest, n_cast):
    cast_in, o_ref, cast_out, u_ref = rest[:n_cast], rest[n_cast], rest[n_cast + 1:-1], rest[-1]
    k = pl.program_id(1)
    rows = x_ref.shape[0]

    @pl.when(k == 0)
    def _():
        def pre(r):
            u_ref[r, :] = _rms(x_ref[r, :], gpre_ref[...]).astype(BF16)
            o_ref[r, :] = jnp.zeros((r.size, o_ref.shape[1]), F32)
        _row_chunks(rows, pre)

    u = u_ref[...]
    a = _dot(u, wg_ref[...])
    b = _dot(u, wu_ref[...])
    h = (a * _sigmoid(a) * b).astype(BF16)
    for src, dst in zip(cast_in, cast_out):
        dst[...] = src[...].astype(BF16)
    o_ref[...] += _dot(h, wd_ref[...])

    @pl.when(k == pl.num_programs(1) - 1)
    def _():
        _norm_residual_inplace(o_ref, x_ref, MACARON_WEIGHT * gpost_ref[...])


def _cast_tiling(rows, cols, ni, nk):
    rb = rows // ni
    assert rows % ni == 0 and rb % V7X_BF16_ROWS == 0
    nc = max(c for c in range(1, nk + 1) if cols % c == 0 and (cols // c) % 128 == 0)
    return rb, cols // nc, nc


def _ffn(x2, g_pre, w_gu, w_down, g_post, *, tm, tf, cast=()):
    n, d = x2.shape
    f = w_down.shape[0]
    ni, nk = n // tm, f // tf
    in_specs = [
        pl.BlockSpec((tm, d), lambda i, k: (i, 0)),
        pl.BlockSpec((1, d), lambda i, k: (0, 0)),
        pl.BlockSpec((d, tf), lambda i, k: (0, k)),
        pl.BlockSpec((d, tf), lambda i, k: (0, k + nk)),
        pl.BlockSpec((tf, d), lambda i, k: (k, 0)),
        pl.BlockSpec((1, d), lambda i, k: (0, 0)),
    ]
    args = [x2, g_pre.reshape(1, d), w_gu, w_gu, w_down, g_post.reshape(1, d)]
    out_specs = [pl.BlockSpec((tm, d), lambda i, k: (i, 0))]
    out_shape = [jax.ShapeDtypeStruct((n, d), F32)]
    for w, layer in cast:
        rb, cb, nc = _cast_tiling(w.shape[1], w.shape[2], ni, nk)
        in_specs.append(pl.BlockSpec((None, rb, cb),
                                     lambda i, k, layer=layer, nc=nc: (layer, i, jnp.minimum(k, nc - 1))))
        args.append(w)
        out_specs.append(pl.BlockSpec((rb, cb), lambda i, k, nc=nc: (i, jnp.minimum(k, nc - 1))))
        out_shape.append(jax.ShapeDtypeStruct(w.shape[1:], BF16))
    return pl.pallas_call(
        functools.partial(_ffn_body, n_cast=len(cast)),
        grid=(ni, nk),
        in_specs=in_specs,
        out_specs=out_specs,
        out_shape=out_shape,
        scratch_shapes=[pltpu.VMEM((tm, d), BF16)],
        compiler_params=_params(2),
        name="ffn",
    )(*args)


def _proj_body(x_ref, xp_ref, xn_ref, g_ref, w_ref, cw_ref, cb_ref, cs_ref,
               hy_ref, pq_ref, po_ref, gt_ref, u_ref, *, tm, seq, n_hy, fw):
    i = pl.program_id(0)
    j = pl.program_id(1)

    @pl.when(j == 0)
    def _():
        g = g_ref[...]
        has_prev = lax.rem(i * tm, seq) != 0
        has_next = lax.rem((i + 1) * tm, seq) != 0
        u_ref[pl.ds(0, HALO), :] = jnp.where(has_prev, _rms(xp_ref[...], g), 0.0).astype(BF16)
        chunk = ROW_CHUNK if tm % ROW_CHUNK == 0 else tm

        def pre(c, carry):
            r = pl.multiple_of(c * chunk, chunk)
            u_ref[pl.ds(pl.multiple_of(HALO + r, HALO), chunk), :] = (
                _rms(x_ref[pl.ds(r, chunk), :], g).astype(BF16))
            return carry

        lax.fori_loop(0, tm // chunk, pre, 0)
        u_ref[pl.ds(HALO + tm, HALO), :] = jnp.where(has_next, _rms(xn_ref[...], g), 0.0).astype(BF16)

    @pl.when(j < n_hy)
    def _():
        h = _dot(u_ref[...], w_ref[...])
        rows = h.shape[0]
        prev = pltpu.roll(h, 1, axis=0)[HALO:HALO + tm]
        nxt = pltpu.roll(h, rows - 1, axis=0)[HALO:HALO + tm]
        y = cb_ref[...] + prev * cw_ref[0:1, :]
        y = y + h[HALO:HALO + tm] * cw_ref[1:2, :]
        y = y + nxt * cw_ref[2:3, :]
        hy_ref[...] = y.astype(BF16)

    @pl.when(j == n_hy)
    def _():
        c = _dot(u_ref[pl.ds(HALO, tm), :], w_ref[...])
        po_ref[...] = c[:, fw:]
        fb = c[:, :fw].astype(BF16)
        gw = fw // FNET_GROUPS
        res = [_dot(fb[:, q * gw:(q + 1) * gw], cs_ref[...]) for q in range(FNET_GROUPS)]
        pq_ref[...] = jnp.concatenate([r[:, :gw] for r in res] + [r[:, gw:] for r in res], axis=1)

    @pl.when(j > n_hy)
    def _():
        c = _dot(u_ref[pl.ds(HALO, tm), :], w_ref[...])
        gt_ref[...] = _sigmoid(c).astype(BF16)


def _proj(x2, g, w_in, conv_w, conv_b, cs, *, tm, tn, seq, hy_cols, fw, pw):
    n, d = x2.shape
    cols = w_in.shape[1]
    n_hy = hy_cols // tn
    assert hy_cols % tn == 0 and fw + pw == tn and (cols - hy_cols - tn) % tn == 0
    nj = cols // tn
    n_gate = cols - hy_cols - tn
    hb = tm // HALO
    last_hb = n // HALO - 1
    body = functools.partial(_proj_body, tm=tm, seq=seq, n_hy=n_hy, fw=fw)
    return pl.pallas_call(
        body,
        grid=(n // tm, nj),
        in_specs=[
            pl.BlockSpec((tm, d), lambda i, j: (i, 0)),
            pl.BlockSpec((HALO, d), lambda i, j: (jnp.maximum(i * hb - 1, 0), 0)),
            pl.BlockSpec((HALO, d), lambda i, j: (jnp.minimum((i + 1) * hb, last_hb), 0)),
            pl.BlockSpec((1, d), lambda i, j: (0, 0)),
            pl.BlockSpec((d, tn), lambda i, j: (0, j)),
            pl.BlockSpec((3, tn), lambda i, j: (0, jnp.minimum(j, n_hy - 1))),
            pl.BlockSpec((1, tn), lambda i, j: (0, jnp.minimum(j, n_hy - 1))),
            _resident(cs.shape),
        ],
        out_specs=[
            pl.BlockSpec((tm, tn), lambda i, j: (i, jnp.minimum(j, n_hy - 1))),
            pl.BlockSpec((tm, 2 * fw), lambda i, j: (i, 0)),
            pl.BlockSpec((tm, pw), lambda i, j: (i, 0)),
            pl.BlockSpec((tm, tn), lambda i, j: (i, jnp.maximum(j - n_hy - 1, 0))),
        ],
        out_shape=[
            jax.ShapeDtypeStruct((n, hy_cols), BF16),
            jax.ShapeDtypeStruct((n, 2 * fw), F32),
            jax.ShapeDtypeStruct((n, pw), F32),
            jax.ShapeDtypeStruct((n, n_gate), BF16),
        ],
        scratch_shapes=[pltpu.VMEM((tm + 2 * HALO, d), BF16)],
        compiler_params=_params(2),
        name="mixer_proj",
    )(x2, x2, x2, g.reshape(1, d), w_in, conv_w, conv_b.reshape(1, hy_cols), cs)


def _hyena_slot_block(s, na):
    half = na // 2
    return s if s <= half else (half + 1) + (s - half)


def _hyena_mats(seq):
    nb = HYENA_NB
    na = 2 * seq // nb
    half = na // 2
    eye = np.eye(V7X_SUBLANES)
    a = np.arange(half)[None, :]
    f1 = np.zeros((na, half))
    k_re = np.arange(half + 1)[:, None]
    f1[: half + 1] = np.cos(2 * np.pi * k_re * a / na)
    k_im = np.arange(1, half)[:, None]
    f1[half + 1:] = -np.sin(2 * np.pi * k_im * a / na)
    lhs1 = np.kron(f1, eye)
    b = np.arange(nb)[None, :]
    kb = np.arange(nb)[:, None]
    g = np.zeros((half + 1, 2 * nb, 2 * nb))
    for ka in range(half + 1):
        ang = 2 * np.pi * (kb * b / nb + b * ka / (2 * seq))
        gre, gim = np.cos(ang), -np.sin(ang)
        g[ka] = np.block([[gre, -gim], [gim, gre]])
    a_col = np.arange(half)[:, None]
    f2 = np.zeros((half, na))
    wgt = np.full(half + 1, 2.0)
    wgt[0] = wgt[half] = 1.0
    f2[:, : half + 1] = wgt[None, :] * np.cos(2 * np.pi * a_col * np.arange(half + 1)[None, :] / na)
    f2[:, half + 1:] = -2.0 * np.sin(2 * np.pi * a_col * np.arange(1, half)[None, :] / na)
    lhs2 = np.kron(f2 / (2 * seq), eye)
    return (jnp.asarray(lhs1, BF16), jnp.asarray(g, BF16), jnp.asarray(lhs2, BF16))


def _fnet_channel_mat(group):
    cc = np.arange(group)[:, None]
    mm = np.arange(group)[None, :]
    angc = 2 * np.pi * cc * mm / group
    return jnp.asarray(np.concatenate([np.cos(angc), -np.sin(angc)], axis=1) / math.sqrt(group), BF16)


def _fnet3_mats(seq):
    r = int(round(seq ** (1.0 / 3.0)))
    sub = V7X_SUBLANES
    assert r ** 3 == seq and r % sub == 0
    idx = np.arange(r)
    ang = 2 * np.pi * idx[:, None] * idx[None, :] / r
    blk = np.array([[np.cos(ang), np.sin(ang)], [-np.sin(ang), np.cos(ang)]])
    ma = np.zeros((sub, 2, r, 2, r, sub))
    for lo in range(sub):
        ma[lo, :, :, :, :, lo] = blk.transpose(0, 2, 1, 3)
    ma = ma.reshape(2 * r * sub, 2 * r * sub)
    mb = np.zeros((r // sub, 2, r, sub, 2, r, sub))
    for hi in range(r // sub):
        for lo in range(sub):
            k2 = hi * sub + lo
            ph = 2 * np.pi * (idx[:, None] * idx[None, :] / r + idx[None, :] * k2 / r ** 2)
            c, s = np.cos(ph), np.sin(ph)
            mb[hi, 0, :, lo, 0, :, lo] = c
            mb[hi, 0, :, lo, 1, :, lo] = s
            mb[hi, 1, :, lo, 0, :, lo] = -s
            mb[hi, 1, :, lo, 1, :, lo] = c
    mb = mb.reshape(r // sub, 2 * r * sub, 2 * r * sub)
    mc = np.zeros((r, r // sub, r, sub, 2, r, sub))
    for k1 in range(r):
        for hi in range(r // sub):
            for lo in range(sub):
                k2 = hi * sub + lo
                ph = 2 * np.pi * (idx[:, None] * idx[None, :] / r
                                  + idx[None, :] * k1 / r ** 2 + idx[None, :] * k2 / seq)
                mc[k1, hi, :, lo, 0, :, lo] = np.cos(ph)
                mc[k1, hi, :, lo, 1, :, lo] = np.sin(ph)
    mc = mc.reshape(r * (r // sub), r * sub, 2 * r * sub) / math.sqrt(seq)
    return jnp.asarray(ma, BF16), jnp.asarray(mb, BF16), jnp.asarray(mc, BF16), r


def _static_loop(n, body):
    for i in range(n):
        body(i, 0)


def _gather_tiles(ref, starts):
    tiles = [ref[pl.ds(pl.multiple_of(s, V7X_SUBLANES), V7X_SUBLANES), :] for s in starts]
    return jnp.concatenate(tiles, axis=0)


def _hyena_stage1(z_ref, e_ref, lhs1_ref, seq):
    nb = HYENA_NB
    na = 2 * seq // nb
    half = na // 2

    def body(i, carry):
        off = i * V7X_SUBLANES
        xg = _gather_tiles(z_ref, [nb * a + off for a in range(half)]).astype(BF16)
        out = _dot(lhs1_ref[...], xg)
        for s in range(na):
            row = _hyena_slot_block(s, na) * nb + off
            e_ref[pl.ds(pl.multiple_of(row, V7X_SUBLANES), V7X_SUBLANES), :] = (
                out[s * V7X_SUBLANES:(s + 1) * V7X_SUBLANES])
        return carry

    _static_loop(nb // V7X_SUBLANES, body)


def _hyena_zero_unused(e_ref, seq):
    nb = HYENA_NB
    half = seq // nb
    zero = jnp.zeros((nb, e_ref.shape[1]), F32)
    e_ref[pl.ds((half + 1) * nb, nb), :] = zero
    e_ref[pl.ds((2 * half + 1) * nb, nb), :] = zero


def _hyena_load_ka(e_ref, ka, seq):
    nb = HYENA_NB
    half = seq // nb
    re = e_ref[pl.ds(pl.multiple_of(ka * nb, nb), nb), :]
    im = e_ref[pl.ds(pl.multiple_of((half + 1 + ka) * nb, nb), nb), :]
    return jnp.concatenate([re, im], axis=0)


def _filter_mlp_body(bands_ref, w1t_ref, w1c_ref, w1s_ref, b1_ref, w2_ref, b2_ref, w3_ref, b3_ref,
                     fr_ref, o_ref, *, seq):
    idx = lax.broadcasted_iota(jnp.int32, (1, 2 * seq), 1)
    p = jnp.where(idx < seq, idx, 2 * seq - idx).astype(F32)
    t = p / (seq - 1.0)
    ang = bands_ref[...] * ((2.0 * math.pi / seq) * p)
    fr = fr_ref[...]
    pre = (w1t_ref[...] * t
           + _dot3(w1c_ref[...], jnp.cos(ang))
           + _dot3(w1s_ref[...], -jnp.sin(ang))
           + b1_ref[...])
    h = jnp.sin(fr * pre)
    h = jnp.sin(fr * (_dot3(w2_ref[...], h) + b2_ref[...]))
    h = jnp.sin(fr * (_dot3(w3_ref[...], h) + b3_ref[...]))
    o_ref[...] = jnp.concatenate([h[:, :seq], h[:, seq:]], axis=0).T


def _filter_mlp(fw1, fb1, fw2, fb2, fw3, fb3, freq, *, seq):
    hid = fw2.shape[0]
    bands = np.linspace(1e-4, FILTER_BANDS - 1, FILTER_BANDS).astype(np.float32).reshape(-1, 1)
    col = lambda v: v.reshape(hid, 1)
    args = (jnp.asarray(bands), fw1[0].reshape(hid, 1), fw1[1:1 + FILTER_BANDS].T,
            fw1[1 + FILTER_BANDS:].T, col(fb1), fw2.T, col(fb2), fw3.T, col(fb3), col(freq))
    return pl.pallas_call(
        functools.partial(_filter_mlp_body, seq=seq),
        grid=(1,),
        in_specs=[_resident(a.shape) for a in args],
        out_specs=pl.BlockSpec((seq, 2 * hid), lambda i: (0, 0)),
        out_shape=jax.ShapeDtypeStruct((seq, 2 * hid), F32),
        compiler_params=_params(1),
        name="filter_mlp",
    )(*args)


def _filter_spec_body(h3_ref, w4f_ref, w4b_ref, dl_ref, d_ref, lhs1_ref, g_ref, *rest, seq, cw, n_cast):
    cast_in, o_ref, cast_out = rest[:n_cast], rest[n_cast], rest[n_cast + 1:2 * n_cast + 1]
    z_ref, e_ref = rest[-2:]
    for src, dst in zip(cast_in, cast_out):
        dst[...] = src[...].astype(BF16)
    nb = HYENA_NB
    half = seq // nb
    dl = dl_ref[...]
    zero = jnp.zeros_like(w4f_ref[...])
    w4 = jnp.concatenate([jnp.concatenate([w4f_ref[...], zero], axis=1),
                          jnp.concatenate([zero, w4b_ref[...]], axis=1)], axis=0)
    lane = lax.broadcasted_iota(jnp.int32, (1, 2 * cw), 1)
    chunk = ROW_CHUNK if seq % ROW_CHUNK == 0 else seq
    for c in range(seq // chunk):
        row = c * chunk + lax.broadcasted_iota(jnp.int32, (chunk, 1), 0)
        rowf = row.astype(F32)
        h4 = _dot3(h3_ref[pl.ds(c * chunk, chunk), :], w4)
        dec = jnp.concatenate([jnp.exp(-(rowf / (seq - 1.0)) * dl),
                               jnp.exp(-((seq - rowf) / (seq - 1.0)) * dl)], axis=1)
        z_ref[pl.ds(c * chunk, chunk), :] = jnp.where((row == 0) & (lane >= cw), 0.0, h4 * dec)

    _hyena_zero_unused(e_ref, seq)
    _hyena_stage1(z_ref, e_ref, lhs1_ref, seq)
    dsk = d_ref[0]

    def body(ka, carry):
        ein = _hyena_load_ka(e_ref, ka, seq).astype(BF16)
        x = _dot(g_ref[ka], ein)
        sgn = 1.0 - 2.0 * (ka % 2)
        hsp = x[:, 0:cw] + sgn * x[:, cw:2 * cw]
        part = lax.broadcasted_iota(jnp.int32, (2 * nb, 1), 0)
        hsp = hsp + jnp.where(part < nb, dsk, 0.0)
        o_ref[0, pl.ds(pl.multiple_of(ka * 2 * nb, 2 * nb), 2 * nb), :] = hsp.astype(BF16)
        return carry

    _static_loop(half + 1, body)


def _filter_spectrum(h3, fw4, d_skip, lhs1, gmat, *, seq, width, cw, ct, cast=()):
    hid = h3.shape[1] // 2
    orders = d_skip.shape[0]
    nct = width // cw
    per = ct // cw
    nb = HYENA_NB
    half = seq // nb
    max_decay = math.log(DECAY_TARGET) / FAST_DECAY_PCT
    min_decay = math.log(DECAY_TARGET) / SLOW_DECAY_PCT
    deltas = np.abs(np.linspace(min_decay, max_decay, width)).astype(np.float32).reshape(1, width)
    rows = (half + 1) * 2 * nb
    body = functools.partial(_filter_spec_body, seq=seq, cw=cw, n_cast=len(cast))
    in_specs = [
        _resident(h3.shape),
        pl.BlockSpec((hid, cw), lambda o, c: (0, o * 2 * nct + c)),
        pl.BlockSpec((hid, cw), lambda o, c: (0, o * 2 * nct + nct + c)),
        pl.BlockSpec((1, cw), lambda o, c: (0, c)),
        pl.BlockSpec((1, 1, cw), lambda o, c: (o, 0, c)),
        _resident(lhs1.shape),
        _resident(gmat.shape),
    ]
    args = [h3, fw4, fw4, jnp.asarray(deltas), d_skip.reshape(orders, 1, width), lhs1, gmat]
    out_specs = [pl.BlockSpec((None, 1, rows, cw), lambda o, c: (c // per, o, 0, c % per))]
    out_shape = [jax.ShapeDtypeStruct((width // ct, orders, rows, ct), BF16)]
    steps = orders * nct
    for w, layer in cast:
        rb = w.shape[1] // steps
        assert w.shape[1] % steps == 0 and rb % V7X_BF16_ROWS == 0
        in_specs.append(pl.BlockSpec((None, rb, w.shape[2]), lambda o, c, layer=layer: (layer, o * nct + c, 0)))
        args.append(w)
        out_specs.append(pl.BlockSpec((rb, w.shape[2]), lambda o, c: (o * nct + c, 0)))
        out_shape.append(jax.ShapeDtypeStruct(w.shape[1:], BF16))
    return pl.pallas_call(
        body,
        grid=(orders, nct),
        in_specs=in_specs,
        out_specs=out_specs,
        out_shape=out_shape,
        scratch_shapes=[pltpu.VMEM((seq, 2 * cw), F32),
                        pltpu.VMEM((2 * (half + 1) * nb, 2 * cw), F32)],
        compiler_params=_params(2),
        name="filter_spectrum",
    )(*args)


def _hyena_body(v_ref, g1_ref, g2_ref, h_ref, lhs1_ref, g_ref, lhs2_ref, o_ref, z_ref, e_ref, *, seq):
    nb = HYENA_NB
    na = 2 * seq // nb
    half = na // 2
    pair = V7X_BF16_ROWS
    chunk = 512 if seq % 512 == 0 else seq

    for c in range(seq // chunk):
        z_ref[pl.ds(c * chunk, chunk), :] = v_ref[pl.ds(c * chunk, chunk), :].astype(F32)
    _hyena_zero_unused(e_ref, seq)

    for order, gate_ref in enumerate((g1_ref, g2_ref)):
        _hyena_stage1(z_ref, e_ref, lhs1_ref, seq)

        group = 3 if (half + 1) % 3 == 0 else 1

        def spectral(it, carry):
            kas = [it * group + u for u in range(group)]
            eins = [_hyena_load_ka(e_ref, ka, seq).astype(BF16) for ka in kas]
            ys = []
            for ka, ein in zip(kas, eins):
                gk = g_ref[ka]
                x = _dot(gk, ein)
                hsp = h_ref[order, pl.ds(pl.multiple_of(ka * 2 * nb, 2 * nb), 2 * nb), :].astype(F32)
                xre, xim, hre, him = x[:nb], x[nb:], hsp[:nb], hsp[nb:]
                zsp = jnp.concatenate([xre * hre - xim * him, xre * him + xim * hre], axis=0).astype(BF16)
                ys.append(lax.dot_general(gk, zsp, (((0,), (0,)), ((), ())), preferred_element_type=F32))
            for ka, y in zip(kas, ys):
                e_ref[pl.ds(pl.multiple_of(ka * nb, nb), nb), :] = y[:nb]
                keep = jnp.where((ka == 0) | (ka == half), 0.0, 1.0)
                e_ref[pl.ds(pl.multiple_of((half + 1 + ka) * nb, nb), nb), :] = y[nb:] * keep
            return carry

        _static_loop((half + 1) // group, spectral)

        def inverse(i, carry):
            outs = []
            for q in range(pair // V7X_SUBLANES):
                off = i * pair + q * V7X_SUBLANES
                yg = _gather_tiles(e_ref, [_hyena_slot_block(s, na) * nb + off for s in range(na)])
                outs.append(_dot(lhs2_ref[...], yg.astype(BF16)))
            for a in range(half):
                sl = slice(a * V7X_SUBLANES, (a + 1) * V7X_SUBLANES)
                conv = jnp.concatenate([o_[sl] for o_ in outs], axis=0)
                r = pl.multiple_of(a * nb + i * pair, pair)
                res = gate_ref[pl.ds(r, pair), :].astype(F32) * conv
                if order == 0:
                    z_ref[pl.ds(r, pair), :] = res
                else:
                    o_ref[pl.ds(r, pair), :] = res.astype(BF16)
            return carry

        _static_loop(nb // pair, inverse)


def _hyena(hyc, hspec, lhs1, gmat, lhs2, *, seq, width, ct):
    n = hyc.shape[0]
    nct = width // ct
    nb = HYENA_NB
    half = seq // nb
    return pl.pallas_call(
        functools.partial(_hyena_body, seq=seq),
        grid=(nct, n // seq),
        in_specs=[
            pl.BlockSpec((seq, ct), lambda c, b: (b, c)),
            pl.BlockSpec((seq, ct), lambda c, b: (b, nct + c)),
            pl.BlockSpec((seq, ct), lambda c, b: (b, 2 * nct + c)),
            pl.BlockSpec((None,) + hspec.shape[1:], lambda c, b: (c, 0, 0, 0)),
            _resident(lhs1.shape),
            _resident(gmat.shape),
            _resident(lhs2.shape),
        ],
        out_specs=pl.BlockSpec((seq, ct), lambda c, b: (b, c)),
        out_shape=jax.ShapeDtypeStruct((n, width), BF16),
        scratch_shapes=[pltpu.VMEM((seq, ct), F32),
                        pltpu.VMEM((2 * (half + 1) * nb, ct), F32)],
        compiler_params=_params(2),
        name="hyena",
    )(hyc, hyc, hyc, hspec, lhs1, gmat, lhs2)


def _fnet3_body(p_ref, q_ref, ma_ref, mb_ref, mc_ref, o_ref, t_ref, *, seq, r):
    sub = V7X_SUBLANES
    nhi = r // sub
    row = lambda part, x0, x1, x2: part * seq + (x0 * r + x1) * r + x2

    def stage_a(i, carry):
        t1, t0_hi = divmod(i, nhi)
        starts = [r * r * t2 + r * t1 + sub * t0_hi for t2 in range(r)]
        xin = jnp.concatenate([_gather_tiles(p_ref, starts), _gather_tiles(q_ref, starts)],
                              axis=0).astype(BF16)
        out = _dot(ma_ref[...], xin)
        for lo in range(sub):
            for part in range(2):
                src = (lo * 2 + part) * r
                t_ref[pl.ds(row(part, t0_hi * sub + lo, t1, 0), r), :] = out[src:src + r]
        return carry

    _static_loop(r * nhi, stage_a)

    def stage_b(i, carry):
        t0, k2_hi = divmod(i, nhi)
        starts = [row(part, t0, t1, sub * k2_hi) for part in range(2) for t1 in range(r)]
        out = _dot(mb_ref[k2_hi], _gather_tiles(t_ref, starts).astype(BF16))
        for j, s in enumerate(starts):
            t_ref[pl.ds(s, sub), :] = out[j * sub:(j + 1) * sub]
        return carry

    _static_loop(r * nhi, stage_b)

    def stage_c(i, carry):
        k1, k2_hi = divmod(i, nhi)
        starts = [row(part, t0, k1, sub * k2_hi) for part in range(2) for t0 in range(r)]
        out = _dot(mc_ref[i], _gather_tiles(t_ref, starts).astype(BF16))
        for k0 in range(r):
            o_ref[pl.ds(r * r * k0 + r * k1 + sub * k2_hi, sub), :] = out[k0 * sub:(k0 + 1) * sub]
        return carry

    _static_loop(r * nhi, stage_c)


def _fnet3(pq, ma, mb, mc, *, seq, fw, ct, r):
    n = pq.shape[0]
    nct = fw // ct
    return pl.pallas_call(
        functools.partial(_fnet3_body, seq=seq, r=r),
        grid=(n // seq, nct),
        in_specs=[
            pl.BlockSpec((seq, ct), lambda b, c: (b, c)),
            pl.BlockSpec((seq, ct), lambda b, c: (b, nct + c)),
            _resident(ma.shape),
            _resident(mb.shape),
            _resident(mc.shape),
        ],
        out_specs=pl.BlockSpec((seq, ct), lambda b, c: (b, c)),
        out_shape=jax.ShapeDtypeStruct((n, fw), F32),
        scratch_shapes=[pltpu.VMEM((2 * seq, ct), F32)],
        compiler_params=_params(2),
        name="fnet_seq",
    )(pq, pq, ma, mb, mc)


def _merge_body(x_ref, z_ref, f_ref, po_ref, pp_ref, pn_ref, gt_ref, wha_ref, wfo_ref, wpl_ref,
                ps_ref, wo_ref, gpost_ref, o_ref, ext_ref, m_ref, *, tm, seq, d):
    i = pl.program_id(0)
    sub = V7X_SUBLANES
    has_prev = lax.rem(i * tm, seq) != 0
    has_next = lax.rem((i + 1) * tm, seq) != 0
    ext_ref[pl.ds(0, sub), :] = jnp.where(has_prev, pp_ref[...], 0.0)
    ext_ref[pl.ds(sub, tm), :] = po_ref[...]
    ext_ref[pl.ds(sub + tm, sub), :] = jnp.where(has_next, pn_ref[...], 0.0)

    gate = lambda q, lo, hi: gt_ref[:, q * d + lo:q * d + hi].astype(F32)
    m_ref[...] = gate(0, 0, d) * _dot(z_ref[...], wha_ref[...])
    m_ref[...] += gate(1, 0, d) * _dot(f_ref[...].astype(BF16), wfo_ref[...])

    pos = lax.rem(i * tm, seq) + lax.broadcasted_iota(jnp.int32, (tm, 1), 0)
    n_pool = len(POOL_WINDOWS)
    gw = po_ref.shape[1] // n_pool
    ow = d // n_pool
    for q, w in enumerate(POOL_WINDOWS):
        before = w // 2
        after = w - 1 - before
        cols = slice(q * gw, (q + 1) * gw)
        tot = ext_ref[pl.ds(sub - before, tm), cols]
        for s in range(-before + 1, after + 1):
            tot = tot + ext_ref[pl.ds(sub + s, tm), cols]
        lo = jnp.maximum(pos - before, 0)
        hi = jnp.minimum(pos + after, seq - 1)
        cnt = (hi - lo + 1).astype(F32)
        mq = (tot / cnt - po_ref[:, cols]).astype(BF16)
        yc = _dot(mq, wpl_ref[q]) * ps_ref[:, q * ow:(q + 1) * ow]
        m_ref[:, q * ow:(q + 1) * ow] += gate(2, q * ow, (q + 1) * ow) * yc

    y = _dot(m_ref[...].astype(BF16), wo_ref[...])
    o_ref[...] = x_ref[...] + _rms(y, gpost_ref[...])


def _merge(x2, z2, fy, po, gates, wha, wfo, wpool, pscale, w_out, g_post, layer, *, tm, seq):
    n, d = x2.shape
    pw = po.shape[1]
    sub = V7X_SUBLANES
    hb = tm // sub
    last_hb = n // sub - 1
    body = functools.partial(_merge_body, tm=tm, seq=seq, d=d)
    return pl.pallas_call(
        body,
        grid=(n // tm,),
        in_specs=[
            pl.BlockSpec((tm, d), lambda i: (i, 0)),
            pl.BlockSpec((tm, z2.shape[1]), lambda i: (i, 0)),
            pl.BlockSpec((tm, fy.shape[1]), lambda i: (i, 0)),
            pl.BlockSpec((tm, pw), lambda i: (i, 0)),
            pl.BlockSpec((sub, pw), lambda i: (jnp.maximum(i * hb - 1, 0), 0)),
            pl.BlockSpec((sub, pw), lambda i: (jnp.minimum((i + 1) * hb, last_hb), 0)),
            pl.BlockSpec((tm, gates.shape[1]), lambda i: (i, 0)),
            _resident_layer(wha, layer),
            _resident_layer(wfo, layer),
            _resident_layer(wpool, layer),
            _resident((1, d)),
            _resident(w_out.shape),
            _resident((1, d)),
        ],
        out_specs=pl.BlockSpec((tm, d), lambda i: (i, 0)),
        out_shape=jax.ShapeDtypeStruct((n, d), F32),
        scratch_shapes=[pltpu.VMEM((tm + 2 * sub, pw), F32), pltpu.VMEM((tm, d), F32)],
        compiler_params=_params(1),
        name="mixer_merge",
    )(x2, z2, fy, po, po, po, gates, wha, wfo, wpool, pscale.reshape(1, d), w_out, g_post.reshape(1, d))


def _kv_body(m_ref, g_ref, w_ref, o_ref, mn_ref):
    @pl.when((pl.program_id(0) == 0) | (pl.num_programs(1) > 1))
    def _():
        mn_ref[...] = _rms(m_ref[...], g_ref[...]).astype(BF16)

    o_ref[...] = _dot(mn_ref[...], w_ref[...].astype(BF16)).astype(BF16)


def _kv(mem2, g, w_kv, layer, *, n_mem, tn):
    n, d = mem2.shape
    cols = w_kv.shape[2]
    return pl.pallas_call(
        _kv_body,
        grid=(cols // tn, n // n_mem),
        in_specs=[
            pl.BlockSpec((n_mem, d), lambda j, b: (b, 0)),
            pl.BlockSpec((1, d), lambda j, b: (0, 0)),
            pl.BlockSpec((None, d, tn), lambda j, b: (layer, 0, j)),
        ],
        out_specs=pl.BlockSpec((n_mem, tn), lambda j, b: (b, j)),
        out_shape=jax.ShapeDtypeStruct((n, cols), BF16),
        scratch_shapes=[pltpu.VMEM((n_mem, d), BF16)],
        compiler_params=_params(2),
        name="mem_kv",
    )(mem2, g.reshape(1, d), w_kv)


def _attn_body(x_ref, gpre_ref, wq_ref, k_ref, v_ref, wo_ref, gpost_ref, o_ref, *, d):
    dh = d // N_HEADS
    x = x_ref[...]
    q = _dot(_rms(x, gpre_ref[...]).astype(BF16), wq_ref[...])
    heads = []
    for h in range(N_HEADS):
        cols = slice(h * dh, (h + 1) * dh)
        s = lax.dot_general(q[:, cols].astype(BF16), k_ref[:, cols], (((1,), (1,)), ((), ())),
                            preferred_element_type=F32) * (dh ** -0.5)
        e = jnp.exp(s - jnp.max(s, axis=-1, keepdims=True))
        p = e / jnp.sum(e, axis=-1, keepdims=True)
        heads.append(_dot(p.astype(BF16), v_ref[:, cols]))
    o = jnp.concatenate(heads, axis=1).astype(BF16)
    o_ref[...] = x + _rms(_dot(o, wo_ref[...]), gpost_ref[...])


def _attn(x2, g_pre, w_q, kv, w_o, g_post, *, tm, seq, n_mem):
    n, d = x2.shape
    tiles_per_seq = seq // tm
    return pl.pallas_call(
        functools.partial(_attn_body, d=d),
        grid=(n // tm,),
        in_specs=[
            pl.BlockSpec((tm, d), lambda i: (i, 0)),
            _resident((1, d)),
            _resident(w_q.shape),
            pl.BlockSpec((n_mem, d), lambda i: (i // tiles_per_seq, 0)),
            pl.BlockSpec((n_mem, d), lambda i: (i // tiles_per_seq, 1)),
            _resident(w_o.shape),
            _resident((1, d)),
        ],
        out_specs=pl.BlockSpec((tm, d), lambda i: (i, 0)),
        out_shape=jax.ShapeDtypeStruct((n, d), F32),
        compiler_params=_params(1),
        name="mem_attn",
    )(x2, g_pre.reshape(1, d), w_q, kv, kv, w_o, g_post.reshape(1, d))


def kernel(x, mem, g_ffn1_pre, w_ffn1_gu, w_ffn1_down, g_ffn1_post, g_mix_pre, w_in, hyena_conv_w,
           hyena_conv_b, filt_w1, filt_b1, filt_w2, filt_b2, filt_w3, filt_b3, filt_w4, filt_freq,
           hyena_d, w_hyena_out, w_fnet_out, w_pool, pool_scale, w_out, g_mix_post, g_mem_pre,
           g_mem_kv, w_q, w_kv, w_o, g_mem_post, g_ffn2_pre, w_ffn2_gu, w_ffn2_down, g_ffn2_post):
    batch, seq, d = x.shape
    n_mem = mem.shape[1]
    depth = w_in.shape[0]
    hy_w = w_hyena_out.shape[1]
    hy_cols = hyena_conv_w.shape[2]
    fw = w_fnet_out.shape[1]
    pw = w_pool.shape[1] * w_pool.shape[2]

    tm = 512
    tn = fw + pw
    tf = 512
    ct = V7X_MXU_COLS

    lhs1, gmat, lhs2 = _hyena_mats(seq)
    cs = _fnet_channel_mat(fw // FNET_GROUPS)
    fma, fmb, fmc, fr = _fnet3_mats(seq)

    bf = lambda w: w.astype(BF16)
    w_hyena_out, w_fnet_out, w_pool = map(bf, (w_hyena_out, w_fnet_out, w_pool))
    x2 = x.reshape(batch * seq, d)
    mem2 = mem.reshape(batch * n_mem, d)
    for l in range(depth):
        h3 = _filter_mlp(filt_w1[l], filt_b1[l], filt_w2[l], filt_b2[l], filt_w3[l], filt_b3[l],
                         filt_freq[l], seq=seq)
        first = [(w_ffn1_gu, 0), (w_ffn1_down, 0)] if l == 0 else []
        hspec, *first_w = _filter_spectrum(h3, filt_w4[l], hyena_d[l], lhs1, gmat, seq=seq, width=hy_w,
                                           cw=ct // 2, ct=ct, cast=first)
        if l == 0:
            ffn_w = first_w

        later = (w_ffn2_gu, w_ffn2_down, w_in, w_out, w_q, w_o)
        x2, *cast = _ffn(x2, g_ffn1_pre[l], *ffn_w, g_ffn1_post[l], tm=2 * tm, tf=tf,
                         cast=[(w, l) for w in later])
        *ffn_w, w_in_l, w_out_l, w_q_l, w_o_l = cast

        hyc, pq, po, gates = _proj(x2, g_mix_pre[l], w_in_l, hyena_conv_w[l], hyena_conv_b[l], cs,
                                   tm=2 * tm, tn=tn, seq=seq, hy_cols=hy_cols, fw=fw, pw=pw)
        z2 = _hyena(hyc, hspec, lhs1, gmat, lhs2, seq=seq, width=hy_w, ct=ct)
        fy = _fnet3(pq, fma, fmb, fmc, seq=seq, fw=fw, ct=ct, r=fr)
        x2 = _merge(x2, z2, fy, po, gates, w_hyena_out, w_fnet_out, w_pool,
                    pool_scale[l], w_out_l, g_mix_post[l], l, tm=tm, seq=seq)

        kv = _kv(mem2, g_mem_kv[l], w_kv, l, n_mem=batch * n_mem, tn=tn)
        x2 = _attn(x2, g_mem_pre[l], w_q_l, kv, w_o_l, g_mem_post[l], tm=tm, seq=seq, n_mem=n_mem)

        nxt = [(w_ffn1_gu, l + 1), (w_ffn1_down, l + 1)] if l + 1 < depth else []
        x2, *ffn_w = _ffn(x2, g_ffn2_pre[l], *ffn_w, g_ffn2_post[l], tm=2 * tm, tf=tf, cast=nxt)
    return x2.reshape(batch, seq, d)
```

```python
import functools
import math

import jax
import jax.numpy as jnp
import numpy as np
from jax import lax
from jax.experimental import pallas as pl
from jax.experimental.pallas import tpu as pltpu

F32 = jnp.float32
BF16 = jnp.bfloat16

RMS_EPS = 1e-6
MACARON_WEIGHT = 0.5
N_HEADS = 4
POOL_WINDOWS = (2, 4, 8, 16)
FNET_GROUPS = 4
FILTER_BANDS = 16
DECAY_TARGET = 1e-2
FAST_DECAY_PCT = 0.3
SLOW_DECAY_PCT = 1.5

V7X_SUBLANES = 8
V7X_BF16_ROWS = 16
V7X_MXU_COLS = 256
V7X_VMEM_LIMIT = 60 * 2**20

HYENA_NB = 128
HALO = V7X_BF16_ROWS
ROW_CHUNK = 256
LANE_BLOCK = 128


def _params(n_axes):
    return pltpu.CompilerParams(
        dimension_semantics=("arbitrary",) * n_axes,
        vmem_limit_bytes=V7X_VMEM_LIMIT,
    )


def _resident(shape):
    zeros = (0,) * len(shape)
    return pl.BlockSpec(shape, lambda *_: zeros, pipeline_mode=pl.Buffered(1))


def _resident_layer(stacked, layer):
    tail = (0,) * (stacked.ndim - 1)
    return pl.BlockSpec((None,) + stacked.shape[1:], lambda *_: (layer,) + tail,
                        pipeline_mode=pl.Buffered(1))


def _rms(x, g):
    ms = jnp.mean(x * x, axis=-1, keepdims=True)
    return x * lax.rsqrt(ms + RMS_EPS) * g


def _sigmoid(x):
    return 0.5 * jnp.tanh(0.5 * x) + 0.5


def _dot(a, b):
    return jnp.dot(a, b, preferred_element_type=F32)


def _dot3(a, b):
    ah = a.astype(BF16)
    al = (a - ah.astype(F32)).astype(BF16)
    bh = b.astype(BF16)
    bl = (b - bh.astype(F32)).astype(BF16)
    return _dot(ah, bh) + (_dot(al, bh) + _dot(ah, bl))


def _row_chunks(rows, fn):
    chunk = ROW_CHUNK if rows % ROW_CHUNK == 0 else rows

    def body(c, carry):
        fn(pl.ds(pl.multiple_of(c * chunk, chunk), chunk))
        return carry

    lax.fori_loop(0, rows // chunk, body, 0)


def _norm_residual_inplace(o_ref, x_ref, g):
    rows, d = o_ref.shape

    def post(r):
        sq = None
        for c in range(0, d, LANE_BLOCK):
            yb = o_ref[r, c:c + LANE_BLOCK]
            sq = yb * yb if sq is None else sq + yb * yb
        inv = lax.rsqrt(jnp.sum(sq, axis=-1, keepdims=True) * (1.0 / d) + RMS_EPS)
        for c in range(0, d, LANE_BLOCK):
            cols = slice(c, c + LANE_BLOCK)
            o_ref[r, cols] = x_ref[r, cols] + o_ref[r, cols] * inv * g[:, cols]

    _row_chunks(rows, post)


def _ffn_body(x_ref, gpre_ref, wg_ref, wu_ref, wd_ref, gpost_ref, *rest, n_cast):
    cast_in, o_ref, cast_out, u_ref = rest[:n_cast], rest[n_cast], rest[n_cast + 1:-1], rest[-1]
    k = pl.program_id(1)
    rows = x_ref.shape[0]

    @pl.when(k == 0)
    def _():
        def pre(r):
            u_ref[r, :] = _rms(x_ref[r, :], gpre_ref[...]).astype(BF16)
            o_ref[r, :] = jnp.zeros((r.size, o_ref.shape[1]), F32)
        _row_chunks(rows, pre)

    u = u_ref[...]
    a = _dot(u, wg_ref[...])
    b = _dot(u, wu_ref[...])
    h = (a * _sigmoid(a) * b).astype(BF16)
    for src, dst in zip(cast_in, cast_out):
        dst[...] = src[...].astype(BF16)
    o_ref[...] += _dot(h, wd_ref[...])

    @pl.when(k == pl.num_programs(1) - 1)
    def _():
        _norm_residual_inplace(o_ref, x_ref, MACARON_WEIGHT * gpost_ref[...])


def _cast_tiling(rows, cols, ni, nk):
    rb = rows // ni
    assert rows % ni == 0 and rb % V7X_BF16_ROWS == 0
    nc = max(c for c in range(1, nk + 1) if cols % c == 0 and (cols // c) % 128 == 0)
    return rb, cols // nc, nc


def _ffn(x2, g_pre, w_gu, w_down, g_post, *, tm, tf, cast=()):
    n, d = x2.shape
    f = w_down.shape[0]
    ni, nk = n // tm, f // tf
    in_specs = [
        pl.BlockSpec((tm, d), lambda i, k: (i, 0)),
        pl.BlockSpec((1, d), lambda i, k: (0, 0)),
        pl.BlockSpec((d, tf), lambda i, k: (0, k)),
        pl.BlockSpec((d, tf), lambda i, k: (0, k + nk)),
        pl.BlockSpec((tf, d), lambda i, k: (k, 0)),
        pl.BlockSpec((1, d), lambda i, k: (0, 0)),
    ]
    args = [x2, g_pre.reshape(1, d), w_gu, w_gu, w_down, g_post.reshape(1, d)]
    out_specs = [pl.BlockSpec((tm, d), lambda i, k: (i, 0))]
    out_shape = [jax.ShapeDtypeStruct((n, d), F32)]
    for w, layer in cast:
        rb, cb, nc = _cast_tiling(w.shape[1], w.shape[2], ni, nk)
        in_specs.append(pl.BlockSpec((None, rb, cb),
                                     lambda i, k, layer=layer, nc=nc: (layer, i, jnp.minimum(k, nc - 1))))
        args.append(w)
        out_specs.append(pl.BlockSpec((rb, cb), lambda i, k, nc=nc: (i, jnp.minimum(k, nc - 1))))
        out_shape.append(jax.ShapeDtypeStruct(w.shape[1:], BF16))
    return pl.pallas_call(
        functools.partial(_ffn_body, n_cast=len(cast)),
        grid=(ni, nk),
        in_specs=in_specs,
        out_specs=out_specs,
        out_shape=out_shape,
        scratch_shapes=[pltpu.VMEM((tm, d), BF16)],
        compiler_params=_params(2),
        name="ffn",
    )(*args)


def _ffn_stream_body(x_ref, gpre_ref, gpost_ref, wgu_hbm, wd_hbm, o_ref, u_ref, wg_buf, wu_buf, wd_buf, sem,
                     *, nk, tf, f):
    i = pl.program_id(0)
    last_tile = pl.num_programs(0) - 1
    rows = x_ref.shape[0]

    def copies(k, slot):
        c0 = pl.multiple_of(k * tf, tf)
        return (pltpu.make_async_copy(wgu_hbm.at[:, pl.ds(c0, tf)], wg_buf.at[slot], sem.at[slot, 0]),
                pltpu.make_async_copy(wgu_hbm.at[:, pl.ds(pl.multiple_of(f + c0, tf), tf)], wu_buf.at[slot],
                                      sem.at[slot, 1]),
                pltpu.make_async_copy(wd_hbm.at[pl.ds(c0, tf), :], wd_buf.at[slot], sem.at[slot, 2]))

    base = lax.rem(i * nk, 2)

    @pl.when(i == 0)
    def _():
        for c in copies(0, base):
            c.start()

    def pre(r):
        u_ref[r, :] = _rms(x_ref[r, :], gpre_ref[...]).astype(BF16)
        o_ref[r, :] = jnp.zeros((r.size, o_ref.shape[1]), F32)
    _row_chunks(rows, pre)

    def chunk(k, carry):
        slot = lax.rem(base + k, 2)
        @pl.when((k < nk - 1) | (i < last_tile))
        def _():
            for c in copies(lax.rem(k + 1, nk), 1 - slot):
                c.start()
        for c in copies(k, slot):
            c.wait()
        u = u_ref[...]
        a = _dot(u, wg_buf[slot])
        b = _dot(u, wu_buf[slot])
        h = (a * _sigmoid(a) * b).astype(BF16)
        o_ref[...] += _dot(h, wd_buf[slot])
        return carry

    lax.fori_loop(0, nk, chunk, 0)
    _norm_residual_inplace(o_ref, x_ref, MACARON_WEIGHT * gpost_ref[...])


def _ffn_stream(x2, g_pre, w_gu, w_down, g_post, *, tm, tf):
    n, d = x2.shape
    f = w_down.shape[0]
    nk = f // tf
    return pl.pallas_call(
        functools.partial(_ffn_stream_body, nk=nk, tf=tf, f=f),
        grid=(n // tm,),
        in_specs=[
            pl.BlockSpec((tm, d), lambda i: (i, 0)),
            pl.BlockSpec((1, d), lambda i: (0, 0)),
            pl.BlockSpec((1, d), lambda i: (0, 0)),
            pl.BlockSpec(memory_space=pl.ANY),
            pl.BlockSpec(memory_space=pl.ANY),
        ],
        out_specs=pl.BlockSpec((tm, d), lambda i: (i, 0)),
        out_shape=jax.ShapeDtypeStruct((n, d), F32),
        scratch_shapes=[pltpu.VMEM((tm, d), BF16),
                        pltpu.VMEM((2, d, tf), BF16), pltpu.VMEM((2, d, tf), BF16), pltpu.VMEM((2, tf, d), BF16),
                        pltpu.SemaphoreType.DMA((2, 3))],
        compiler_params=_params(1),
        name="ffn_stream",
    )(x2, g_pre.reshape(1, d), g_post.reshape(1, d), w_gu, w_down)


def _proj_body(x_ref, xp_ref, xn_ref, g_ref, w_ref, cw_ref, cb_ref, cs_ref,
               hy_ref, pq_ref, po_ref, gt_ref, u_ref, *, tm, seq, n_hy, fw):
    i = pl.program_id(0)
    j = pl.program_id(1)

    @pl.when(j == 0)
    def _():
        g = g_ref[...]
        has_prev = lax.rem(i * tm, seq) != 0
        has_next = lax.rem((i + 1) * tm, seq) != 0
        u_ref[pl.ds(0, HALO), :] = jnp.where(has_prev, _rms(xp_ref[...], g), 0.0).astype(BF16)
        chunk = ROW_CHUNK if tm % ROW_CHUNK == 0 else tm

        def pre(c, carry):
            r = pl.multiple_of(c * chunk, chunk)
            u_ref[pl.ds(pl.multiple_of(HALO + r, HALO), chunk), :] = (
                _rms(x_ref[pl.ds(r, chunk), :], g).astype(BF16))
            return carry

        lax.fori_loop(0, tm // chunk, pre, 0)
        u_ref[pl.ds(HALO + tm, HALO), :] = jnp.where(has_next, _rms(xn_ref[...], g), 0.0).astype(BF16)

    @pl.when(j < n_hy)
    def _():
        h = _dot(u_ref[...], w_ref[...])
        rows = h.shape[0]
        prev = pltpu.roll(h, 1, axis=0)[HALO:HALO + tm]
        nxt = pltpu.roll(h, rows - 1, axis=0)[HALO:HALO + tm]
        y = cb_ref[...] + prev * cw_ref[0:1, :]
        y = y + h[HALO:HALO + tm] * cw_ref[1:2, :]
        y = y + nxt * cw_ref[2:3, :]
        hy_ref[...] = y.astype(BF16)

    @pl.when(j == n_hy)
    def _():
        c = _dot(u_ref[pl.ds(HALO, tm), :], w_ref[...])
        po_ref[...] = c[:, fw:]
        fb = c[:, :fw].astype(BF16)
        gw = fw // FNET_GROUPS
        res = [_dot(fb[:, q * gw:(q + 1) * gw], cs_ref[...]) for q in range(FNET_GROUPS)]
        pq_ref[...] = jnp.concatenate([r[:, :gw] for r in res] + [r[:, gw:] for r in res], axis=1)

    @pl.when(j > n_hy)
    def _():
        c = _dot(u_ref[pl.ds(HALO, tm), :], w_ref[...])
        gt_ref[...] = _sigmoid(c).astype(BF16)


def _proj(x2, g, w_in, conv_w, conv_b, cs, *, tm, tn, seq, hy_cols, fw, pw):
    n, d = x2.shape
    cols = w_in.shape[1]
    n_hy = hy_cols // tn
    assert hy_cols % tn == 0 and fw + pw == tn and (cols - hy_cols - tn) % tn == 0
    nj = cols // tn
    n_gate = cols - hy_cols - tn
    hb = tm // HALO
    last_hb = n // HALO - 1
    body = functools.partial(_proj_body, tm=tm, seq=seq, n_hy=n_hy, fw=fw)
    return pl.pallas_call(
        body,
        grid=(n // tm, nj),
        in_specs=[
            pl.BlockSpec((tm, d), lambda i, j: (i, 0)),
            pl.BlockSpec((HALO, d), lambda i, j: (jnp.maximum(i * hb - 1, 0), 0)),
            pl.BlockSpec((HALO, d), lambda i, j: (jnp.minimum((i + 1) * hb, last_hb), 0)),
            pl.BlockSpec((1, d), lambda i, j: (0, 0)),
            pl.BlockSpec((d, tn), lambda i, j: (0, j)),
            pl.BlockSpec((3, tn), lambda i, j: (0, jnp.minimum(j, n_hy - 1))),
            pl.BlockSpec((1, tn), lambda i, j: (0, jnp.minimum(j, n_hy - 1))),
            _resident(cs.shape),
        ],
        out_specs=[
            pl.BlockSpec((tm, tn), lambda i, j: (i, jnp.minimum(j, n_hy - 1))),
            pl.BlockSpec((tm, 2 * fw), lambda i, j: (i, 0)),
            pl.BlockSpec((tm, pw), lambda i, j: (i, 0)),
            pl.BlockSpec((tm, tn), lambda i, j: (i, jnp.maximum(j - n_hy - 1, 0))),
        ],
        out_shape=[
            jax.ShapeDtypeStruct((n, hy_cols), BF16),
            jax.ShapeDtypeStruct((n, 2 * fw), F32),
            jax.ShapeDtypeStruct((n, pw), F32),
            jax.ShapeDtypeStruct((n, n_gate), BF16),
        ],
        scratch_shapes=[pltpu.VMEM((tm + 2 * HALO, d), BF16)],
        compiler_params=_params(2),
        name="mixer_proj",
    )(x2, x2, x2, g.reshape(1, d), w_in, conv_w, conv_b.reshape(1, hy_cols), cs)


def _hyena_slot_block(s, na):
    half = na // 2
    return s if s <= half else (half + 1) + (s - half)


def _hyena_mats(seq):
    nb = HYENA_NB
    na = 2 * seq // nb
    half = na // 2
    eye = np.eye(V7X_SUBLANES)
    a = np.arange(half)[None, :]
    f1 = np.zeros((na, half))
    k_re = np.arange(half + 1)[:, None]
    f1[: half + 1] = np.cos(2 * np.pi * k_re * a / na)
    k_im = np.arange(1, half)[:, None]
    f1[half + 1:] = -np.sin(2 * np.pi * k_im * a / na)
    lhs1 = np.kron(f1, eye)
    b = np.arange(nb)[None, :]
    kb = np.arange(nb)[:, None]
    g = np.zeros((half + 1, 2 * nb, 2 * nb))
    for ka in range(half + 1):
        ang = 2 * np.pi * (kb * b / nb + b * ka / (2 * seq))
        gre, gim = np.cos(ang), -np.sin(ang)
        g[ka] = np.block([[gre, -gim], [gim, gre]])
    a_col = np.arange(half)[:, None]
    f2 = np.zeros((half, na))
    wgt = np.full(half + 1, 2.0)
    wgt[0] = wgt[half] = 1.0
    f2[:, : half + 1] = wgt[None, :] * np.cos(2 * np.pi * a_col * np.arange(half + 1)[None, :] / na)
    f2[:, half + 1:] = -2.0 * np.sin(2 * np.pi * a_col * np.arange(1, half)[None, :] / na)
    lhs2 = np.kron(f2 / (2 * seq), eye)
    return (jnp.asarray(lhs1, BF16), jnp.asarray(g, BF16), jnp.asarray(lhs2, BF16))


def _fnet_channel_mat(group):
    cc = np.arange(group)[:, None]
    mm = np.arange(group)[None, :]
    angc = 2 * np.pi * cc * mm / group
    return jnp.asarray(np.concatenate([np.cos(angc), -np.sin(angc)], axis=1) / math.sqrt(group), BF16)


def _fnet3_mats(seq):
    r = int(round(seq ** (1.0 / 3.0)))
    sub = V7X_SUBLANES
    assert r ** 3 == seq and r % sub == 0
    idx = np.arange(r)
    ang = 2 * np.pi * idx[:, None] * idx[None, :] / r
    blk = np.array([[np.cos(ang), np.sin(ang)], [-np.sin(ang), np.cos(ang)]])
    ma = np.zeros((sub, 2, r, 2, r, sub))
    for lo in range(sub):
        ma[lo, :, :, :, :, lo] = blk.transpose(0, 2, 1, 3)
    ma = ma.reshape(2 * r * sub, 2 * r * sub)
    mb = np.zeros((r // sub, 2, r, sub, 2, r, sub))
    for hi in range(r // sub):
        for lo in range(sub):
            k2 = hi * sub + lo
            ph = 2 * np.pi * (idx[:, None] * idx[None, :] / r + idx[None, :] * k2 / r ** 2)
            c, s = np.cos(ph), np.sin(ph)
            mb[hi, 0, :, lo, 0, :, lo] = c
            mb[hi, 0, :, lo, 1, :, lo] = s
            mb[hi, 1, :, lo, 0, :, lo] = -s
            mb[hi, 1, :, lo, 1, :, lo] = c
    mb = mb.reshape(r // sub, 2 * r * sub, 2 * r * sub)
    mc = np.zeros((r, r // sub, r, sub, 2, r, sub))
    for k1 in range(r):
        for hi in range(r // sub):
            for lo in range(sub):
                k2 = hi * sub + lo
                ph = 2 * np.pi * (idx[:, None] * idx[None, :] / r
                                  + idx[None, :] * k1 / r ** 2 + idx[None, :] * k2 / seq)
                mc[k1, hi, :, lo, 0, :, lo] = np.cos(ph)
                mc[k1, hi, :, lo, 1, :, lo] = np.sin(ph)
    mc = mc.reshape(r * (r // sub), r * sub, 2 * r * sub) / math.sqrt(seq)
    return jnp.asarray(ma, BF16), jnp.asarray(mb, BF16), jnp.asarray(mc, BF16), r


def _static_loop(n, body):
    for i in range(n):
        body(i, 0)


def _gather_tiles(ref, starts):
    tiles = [ref[pl.ds(pl.multiple_of(s, V7X_SUBLANES), V7X_SUBLANES), :] for s in starts]
    return jnp.concatenate(tiles, axis=0)


def _hyena_stage1(z_ref, e_ref, lhs1_ref, seq):
    nb = HYENA_NB
    na = 2 * seq // nb
    half = na // 2

    def body(i, carry):
        off = i * V7X_SUBLANES
        xg = _gather_tiles(z_ref, [nb * a + off for a in range(half)]).astype(BF16)
        out = _dot(lhs1_ref[...], xg)
        for s in range(na):
            row = _hyena_slot_block(s, na) * nb + off
            e_ref[pl.ds(pl.multiple_of(row, V7X_SUBLANES), V7X_SUBLANES), :] = (
                out[s * V7X_SUBLANES:(s + 1) * V7X_SUBLANES])
        return carry

    _static_loop(nb // V7X_SUBLANES, body)


def _hyena_zero_unused(e_ref, seq):
    nb = HYENA_NB
    half = seq // nb
    zero = jnp.zeros((nb, e_ref.shape[1]), F32)
    e_ref[pl.ds((half + 1) * nb, nb), :] = zero
    e_ref[pl.ds((2 * half + 1) * nb, nb), :] = zero


def _hyena_load_ka(e_ref, ka, seq):
    nb = HYENA_NB
    half = seq // nb
    re = e_ref[pl.ds(pl.multiple_of(ka * nb, nb), nb), :]
    im = e_ref[pl.ds(pl.multiple_of((half + 1 + ka) * nb, nb), nb), :]
    return jnp.concatenate([re, im], axis=0)


def _filter_mlp_body(bands_ref, w1t_ref, w1c_ref, w1s_ref, b1_ref, w2_ref, b2_ref, w3_ref, b3_ref,
                     fr_ref, o_ref, *, seq):
    idx = lax.broadcasted_iota(jnp.int32, (1, 2 * seq), 1)
    p = jnp.where(idx < seq, idx, 2 * seq - idx).astype(F32)
    t = p / (seq - 1.0)
    ang = bands_ref[...] * ((2.0 * math.pi / seq) * p)
    fr = fr_ref[...]
    pre = (w1t_ref[...] * t
           + _dot3(w1c_ref[...], jnp.cos(ang))
           + _dot3(w1s_ref[...], -jnp.sin(ang))
           + b1_ref[...])
    h = jnp.sin(fr * pre)
    h = jnp.sin(fr * (_dot3(w2_ref[...], h) + b2_ref[...]))
    h = jnp.sin(fr * (_dot3(w3_ref[...], h) + b3_ref[...]))
    o_ref[...] = jnp.concatenate([h[:, :seq], h[:, seq:]], axis=0).T


def _filter_mlp(fw1, fb1, fw2, fb2, fw3, fb3, freq, *, seq):
    hid = fw2.shape[0]
    bands = np.linspace(1e-4, FILTER_BANDS - 1, FILTER_BANDS).astype(np.float32).reshape(-1, 1)
    col = lambda v: v.reshape(hid, 1)
    args = (jnp.asarray(bands), fw1[0].reshape(hid, 1), fw1[1:1 + FILTER_BANDS].T,
            fw1[1 + FILTER_BANDS:].T, col(fb1), fw2.T, col(fb2), fw3.T, col(fb3), col(freq))
    return pl.pallas_call(
        functools.partial(_filter_mlp_body, seq=seq),
        grid=(1,),
        in_specs=[_resident(a.shape) for a in args],
        out_specs=pl.BlockSpec((seq, 2 * hid), lambda i: (0, 0)),
        out_shape=jax.ShapeDtypeStruct((seq, 2 * hid), F32),
        compiler_params=_params(1),
        name="filter_mlp",
    )(*args)


def _filter_spec_body(h3_ref, w4f_ref, w4b_ref, dl_ref, d_ref, lhs1_ref, g_ref, *rest, seq, cw, n_cast):
    cast_in, o_ref, cast_out = rest[:n_cast], rest[n_cast], rest[n_cast + 1:2 * n_cast + 1]
    z_ref, e_ref = rest[-2:]
    for src, dst in zip(cast_in, cast_out):
        dst[...] = src[...].astype(BF16)
    nb = HYENA_NB
    half = seq // nb
    dl = dl_ref[...]
    zero = jnp.zeros_like(w4f_ref[...])
    w4 = jnp.concatenate([jnp.concatenate([w4f_ref[...], zero], axis=1),
                          jnp.concatenate([zero, w4b_ref[...]], axis=1)], axis=0)
    lane = lax.broadcasted_iota(jnp.int32, (1, 2 * cw), 1)
    chunk = ROW_CHUNK if seq % ROW_CHUNK == 0 else seq
    for c in range(seq // chunk):
        row = c * chunk + lax.broadcasted_iota(jnp.int32, (chunk, 1), 0)
        rowf = row.astype(F32)
        h4 = _dot3(h3_ref[pl.ds(c * chunk, chunk), :], w4)
        dec = jnp.concatenate([jnp.exp(-(rowf / (seq - 1.0)) * dl),
                               jnp.exp(-((seq - rowf) / (seq - 1.0)) * dl)], axis=1)
        z_ref[pl.ds(c * chunk, chunk), :] = jnp.where((row == 0) & (lane >= cw), 0.0, h4 * dec)

    _hyena_zero_unused(e_ref, seq)
    _hyena_stage1(z_ref, e_ref, lhs1_ref, seq)
    dsk = d_ref[0]

    def body(ka, carry):
        ein = _hyena_load_ka(e_ref, ka, seq).astype(BF16)
        x = _dot(g_ref[ka], ein)
        sgn = 1.0 - 2.0 * (ka % 2)
        hsp = x[:, 0:cw] + sgn * x[:, cw:2 * cw]
        part = lax.broadcasted_iota(jnp.int32, (2 * nb, 1), 0)
        hsp = hsp + jnp.where(part < nb, dsk, 0.0)
        o_ref[0, pl.ds(pl.multiple_of(ka * 2 * nb, 2 * nb), 2 * nb), :] = hsp.astype(BF16)
        return carry

    _static_loop(half + 1, body)


def _filter_spectrum(h3, fw4, d_skip, lhs1, gmat, *, seq, width, cw, ct, cast=()):
    hid = h3.shape[1] // 2
    orders = d_skip.shape[0]
    nct = width // cw
    per = ct // cw
    nb = HYENA_NB
    half = seq // nb
    max_decay = math.log(DECAY_TARGET) / FAST_DECAY_PCT
    min_decay = math.log(DECAY_TARGET) / SLOW_DECAY_PCT
    deltas = np.abs(np.linspace(min_decay, max_decay, width)).astype(np.float32).reshape(1, width)
    rows = (half + 1) * 2 * nb
    body = functools.partial(_filter_spec_body, seq=seq, cw=cw, n_cast=len(cast))
    in_specs = [
        _resident(h3.shape),
        pl.BlockSpec((hid, cw), lambda o, c: (0, o * 2 * nct + c)),
        pl.BlockSpec((hid, cw), lambda o, c: (0, o * 2 * nct + nct + c)),
        pl.BlockSpec((1, cw), lambda o, c: (0, c)),
        pl.BlockSpec((1, 1, cw), lambda o, c: (o, 0, c)),
        _resident(lhs1.shape),
        _resident(gmat.shape),
    ]
    args = [h3, fw4, fw4, jnp.asarray(deltas), d_skip.reshape(orders, 1, width), lhs1, gmat]
    out_specs = [pl.BlockSpec((None, 1, rows, cw), lambda o, c: (c // per, o, 0, c % per))]
    out_shape = [jax.ShapeDtypeStruct((width // ct, orders, rows, ct), BF16)]
    steps = orders * nct
    for w, layer in cast:
        rb = w.shape[1] // steps
        assert w.shape[1] % steps == 0 and rb % V7X_BF16_ROWS == 0
        in_specs.append(pl.BlockSpec((None, rb, w.shape[2]), lambda o, c, layer=layer: (layer, o * nct + c, 0)))
        args.append(w)
        out_specs.append(pl.BlockSpec((rb, w.shape[2]), lambda o, c: (o * nct + c, 0)))
        out_shape.append(jax.ShapeDtypeStruct(w.shape[1:], BF16))
    return pl.pallas_call(
        body,
        grid=(orders, nct),
        in_specs=in_specs,
        out_specs=out_specs,
        out_shape=out_shape,
        scratch_shapes=[pltpu.VMEM((seq, 2 * cw), F32),
                        pltpu.VMEM((2 * (half + 1) * nb, 2 * cw), F32)],
        compiler_params=_params(2),
        name="filter_spectrum",
    )(*args)


def _hyena_body(v_ref, g1_ref, g2_ref, h_ref, lhs1_ref, g_ref, lhs2_ref, o_ref, z_ref, e_ref, *, seq):
    nb = HYENA_NB
    na = 2 * seq // nb
    half = na // 2
    pair = V7X_BF16_ROWS
    chunk = 512 if seq % 512 == 0 else seq

    for c in range(seq // chunk):
        z_ref[pl.ds(c * chunk, chunk), :] = v_ref[pl.ds(c * chunk, chunk), :].astype(F32)
    _hyena_zero_unused(e_ref, seq)

    for order, gate_ref in enumerate((g1_ref, g2_ref)):
        _hyena_stage1(z_ref, e_ref, lhs1_ref, seq)

        group = 3 if (half + 1) % 3 == 0 else 1

        def spectral(it, carry):
            kas = [it * group + u for u in range(group)]
            eins = [_hyena_load_ka(e_ref, ka, seq).astype(BF16) for ka in kas]
            ys = []
            for ka, ein in zip(kas, eins):
                gk = g_ref[ka]
                x = _dot(gk, ein)
                hsp = h_ref[order, pl.ds(pl.multiple_of(ka * 2 * nb, 2 * nb), 2 * nb), :].astype(F32)
                xre, xim, hre, him = x[:nb], x[nb:], hsp[:nb], hsp[nb:]
                zsp = jnp.concatenate([xre * hre - xim * him, xre * him + xim * hre], axis=0).astype(BF16)
                ys.append(lax.dot_general(gk, zsp, (((0,), (0,)), ((), ())), preferred_element_type=F32))
            for ka, y in zip(kas, ys):
                e_ref[pl.ds(pl.multiple_of(ka * nb, nb), nb), :] = y[:nb]
                keep = jnp.where((ka == 0) | (ka == half), 0.0, 1.0)
                e_ref[pl.ds(pl.multiple_of((half + 1 + ka) * nb, nb), nb), :] = y[nb:] * keep
            return carry

        _static_loop((half + 1) // group, spectral)

        def inverse(i, carry):
            outs = []
            for q in range(pair // V7X_SUBLANES):
                off = i * pair + q * V7X_SUBLANES
                yg = _gather_tiles(e_ref, [_hyena_slot_block(s, na) * nb + off for s in range(na)])
                outs.append(_dot(lhs2_ref[...], yg.astype(BF16)))
            for a in range(half):
                sl = slice(a * V7X_SUBLANES, (a + 1) * V7X_SUBLANES)
                conv = jnp.concatenate([o_[sl] for o_ in outs], axis=0)
                r = pl.multiple_of(a * nb + i * pair, pair)
                res = gate_ref[pl.ds(r, pair), :].astype(F32) * conv
                if order == 0:
                    z_ref[pl.ds(r, pair), :] = res
                else:
                    o_ref[pl.ds(r, pair), :] = res.astype(BF16)
            return carry

        _static_loop(nb // pair, inverse)


def _hyena(hyc, hspec, lhs1, gmat, lhs2, *, seq, width, ct):
    n = hyc.shape[0]
    nct = width // ct
    nb = HYENA_NB
    half = seq // nb
    return pl.pallas_call(
        functools.partial(_hyena_body, seq=seq),
        grid=(nct, n // seq),
        in_specs=[
            pl.BlockSpec((seq, ct), lambda c, b: (b, c)),
            pl.BlockSpec((seq, ct), lambda c, b: (b, nct + c)),
            pl.BlockSpec((seq, ct), lambda c, b: (b, 2 * nct + c)),
            pl.BlockSpec((None,) + hspec.shape[1:], lambda c, b: (c, 0, 0, 0)),
            _resident(lhs1.shape),
            _resident(gmat.shape),
            _resident(lhs2.shape),
        ],
        out_specs=pl.BlockSpec((seq, ct), lambda c, b: (b, c)),
        out_shape=jax.ShapeDtypeStruct((n, width), BF16),
        scratch_shapes=[pltpu.VMEM((seq, ct), F32),
                        pltpu.VMEM((2 * (half + 1) * nb, ct), F32)],
        compiler_params=_params(2),
        name="hyena",
    )(hyc, hyc, hyc, hspec, lhs1, gmat, lhs2)


def _fnet3_body(p_ref, q_ref, ma_ref, mb_ref, mc_ref, o_ref, t_ref, *, seq, r):
    sub = V7X_SUBLANES
    nhi = r // sub
    row = lambda part, x0, x1, x2: part * seq + (x0 * r + x1) * r + x2

    def stage_a(i, carry):
        t1, t0_hi = divmod(i, nhi)
        starts = [r * r * t2 + r * t1 + sub * t0_hi for t2 in range(r)]
        xin = jnp.concatenate([_gather_tiles(p_ref, starts), _gather_tiles(q_ref, starts)],
                              axis=0).astype(BF16)
        out = _dot(ma_ref[...], xin)
        for lo in range(sub):
            for part in range(2):
                src = (lo * 2 + part) * r
                t_ref[pl.ds(row(part, t0_hi * sub + lo, t1, 0), r), :] = out[src:src + r]
        return carry

    _static_loop(r * nhi, stage_a)

    def stage_b(i, carry):
        t0, k2_hi = divmod(i, nhi)
        starts = [row(part, t0, t1, sub * k2_hi) for part in range(2) for t1 in range(r)]
        out = _dot(mb_ref[k2_hi], _gather_tiles(t_ref, starts).astype(BF16))
        for j, s in enumerate(starts):
            t_ref[pl.ds(s, sub), :] = out[j * sub:(j + 1) * sub]
        return carry

    _static_loop(r * nhi, stage_b)

    def stage_c(i, carry):
        k1, k2_hi = divmod(i, nhi)
        starts = [row(part, t0, k1, sub * k2_hi) for part in range(2) for t0 in range(r)]
        out = _dot(mc_ref[i], _gather_tiles(t_ref, starts).astype(BF16))
        for k0 in range(r):
            o_ref[pl.ds(r * r * k0 + r * k1 + sub * k2_hi, sub), :] = out[k0 * sub:(k0 + 1) * sub]
        return carry

    _static_loop(r * nhi, stage_c)


def _fnet3(pq, ma, mb, mc, *, seq, fw, ct, r):
    n = pq.shape[0]
    nct = fw // ct
    return pl.pallas_call(
        functools.partial(_fnet3_body, seq=seq, r=r),
        grid=(n // seq, nct),
        in_specs=[
            pl.BlockSpec((seq, ct), lambda b, c: (b, c)),
            pl.BlockSpec((seq, ct), lambda b, c: (b, nct + c)),
            _resident(ma.shape),
            _resident(mb.shape),
            _resident(mc.shape),
        ],
        out_specs=pl.BlockSpec((seq, ct), lambda b, c: (b, c)),
        out_shape=jax.ShapeDtypeStruct((n, fw), F32),
        scratch_shapes=[pltpu.VMEM((2 * seq, ct), F32)],
        compiler_params=_params(2),
        name="fnet_seq",
    )(pq, pq, ma, mb, mc)


def _merge_body(x_ref, z_ref, f_ref, po_ref, pp_ref, pn_ref, gt_ref, wha_ref, wfo_ref, wpl_ref,
                ps_ref, wo_ref, gpost_ref, o_ref, ext_ref, m_ref, *, tm, seq, d):
    i = pl.program_id(0)
    sub = V7X_SUBLANES
    has_prev = lax.rem(i * tm, seq) != 0
    has_next = lax.rem((i + 1) * tm, seq) != 0
    ext_ref[pl.ds(0, sub), :] = jnp.where(has_prev, pp_ref[...], 0.0)
    ext_ref[pl.ds(sub, tm), :] = po_ref[...]
    ext_ref[pl.ds(sub + tm, sub), :] = jnp.where(has_next, pn_ref[...], 0.0)

    gate = lambda q, lo, hi: gt_ref[:, q * d + lo:q * d + hi].astype(F32)
    m_ref[...] = gate(0, 0, d) * _dot(z_ref[...], wha_ref[...])
    m_ref[...] += gate(1, 0, d) * _dot(f_ref[...].astype(BF16), wfo_ref[...])

    pos = lax.rem(i * tm, seq) + lax.broadcasted_iota(jnp.int32, (tm, 1), 0)
    n_pool = len(POOL_WINDOWS)
    gw = po_ref.shape[1] // n_pool
    ow = d // n_pool
    for q, w in enumerate(POOL_WINDOWS):
        before = w // 2
        after = w - 1 - before
        cols = slice(q * gw, (q + 1) * gw)
        tot = ext_ref[pl.ds(sub - before, tm), cols]
        for s in range(-before + 1, after + 1):
            tot = tot + ext_ref[pl.ds(sub + s, tm), cols]
        lo = jnp.maximum(pos - before, 0)
        hi = jnp.minimum(pos + after, seq - 1)
        cnt = (hi - lo + 1).astype(F32)
        mq = (tot / cnt - po_ref[:, cols]).astype(BF16)
        yc = _dot(mq, wpl_ref[q]) * ps_ref[:, q * ow:(q + 1) * ow]
        m_ref[:, q * ow:(q + 1) * ow] += gate(2, q * ow, (q + 1) * ow) * yc

    y = _dot(m_ref[...].astype(BF16), wo_ref[...])
    o_ref[...] = x_ref[...] + _rms(y, gpost_ref[...])


def _merge(x2, z2, fy, po, gates, wha, wfo, wpool, pscale, w_out, g_post, layer, *, tm, seq):
    n, d = x2.shape
    pw = po.shape[1]
    sub = V7X_SUBLANES
    hb = tm // sub
    last_hb = n // sub - 1
    body = functools.partial(_merge_body, tm=tm, seq=seq, d=d)
    return pl.pallas_call(
        body,
        grid=(n // tm,),
        in_specs=[
            pl.BlockSpec((tm, d), lambda i: (i, 0)),
            pl.BlockSpec((tm, z2.shape[1]), lambda i: (i, 0)),
            pl.BlockSpec((tm, fy.shape[1]), lambda i: (i, 0)),
            pl.BlockSpec((tm, pw), lambda i: (i, 0)),
            pl.BlockSpec((sub, pw), lambda i: (jnp.maximum(i * hb - 1, 0), 0)),
            pl.BlockSpec((sub, pw), lambda i: (jnp.minimum((i + 1) * hb, last_hb), 0)),
            pl.BlockSpec((tm, gates.shape[1]), lambda i: (i, 0)),
            _resident_layer(wha, layer),
            _resident_layer(wfo, layer),
            _resident_layer(wpool, layer),
            _resident((1, d)),
            _resident(w_out.shape),
            _resident((1, d)),
        ],
        out_specs=pl.BlockSpec((tm, d), lambda i: (i, 0)),
        out_shape=jax.ShapeDtypeStruct((n, d), F32),
        scratch_shapes=[pltpu.VMEM((tm + 2 * sub, pw), F32), pltpu.VMEM((tm, d), F32)],
        compiler_params=_params(1),
        name="mixer_merge",
    )(x2, z2, fy, po, po, po, gates, wha, wfo, wpool, pscale.reshape(1, d), w_out, g_post.reshape(1, d))


def _kv_body(m_ref, g_ref, w_ref, o_ref, mn_ref):
    @pl.when((pl.program_id(0) == 0) | (pl.num_programs(1) > 1))
    def _():
        mn_ref[...] = _rms(m_ref[...], g_ref[...]).astype(BF16)

    o_ref[...] = _dot(mn_ref[...], w_ref[...].astype(BF16)).astype(BF16)


def _kv(mem2, g, w_kv, layer, *, n_mem, tn):
    n, d = mem2.shape
    cols = w_kv.shape[2]
    return pl.pallas_call(
        _kv_body,
        grid=(cols // tn, n // n_mem),
        in_specs=[
            pl.BlockSpec((n_mem, d), lambda j, b: (b, 0)),
            pl.BlockSpec((1, d), lambda j, b: (0, 0)),
            pl.BlockSpec((None, d, tn), lambda j, b: (layer, 0, j)),
        ],
        out_specs=pl.BlockSpec((n_mem, tn), lambda j, b: (b, j)),
        out_shape=jax.ShapeDtypeStruct((n, cols), BF16),
        scratch_shapes=[pltpu.VMEM((n_mem, d), BF16)],
        compiler_params=_params(2),
        name="mem_kv",
    )(mem2, g.reshape(1, d), w_kv)


def _attn_body(x_ref, gpre_ref, wq_ref, k_ref, v_ref, wo_ref, gpost_ref, o_ref, *, d):
    dh = d // N_HEADS
    x = x_ref[...]
    q = _dot(_rms(x, gpre_ref[...]).astype(BF16), wq_ref[...])
    heads = []
    for h in range(N_HEADS):
        cols = slice(h * dh, (h + 1) * dh)
        s = lax.dot_general(q[:, cols].astype(BF16), k_ref[:, cols], (((1,), (1,)), ((), ())),
                            preferred_element_type=F32) * (dh ** -0.5)
        e = jnp.exp(s - jnp.max(s, axis=-1, keepdims=True))
        p = e / jnp.sum(e, axis=-1, keepdims=True)
        heads.append(_dot(p.astype(BF16), v_ref[:, cols]))
    o = jnp.concatenate(heads, axis=1).astype(BF16)
    o_ref[...] = x + _rms(_dot(o, wo_ref[...]), gpost_ref[...])


def _attn(x2, g_pre, w_q, kv, w_o, g_post, *, tm, seq, n_mem):
    n, d = x2.shape
    tiles_per_seq = seq // tm
    return pl.pallas_call(
        functools.partial(_attn_body, d=d),
        grid=(n // tm,),
        in_specs=[
            pl.BlockSpec((tm, d), lambda i: (i, 0)),
            _resident((1, d)),
            _resident(w_q.shape),
            pl.BlockSpec((n_mem, d), lambda i: (i // tiles_per_seq, 0)),
            pl.BlockSpec((n_mem, d), lambda i: (i // tiles_per_seq, 1)),
            _resident(w_o.shape),
            _resident((1, d)),
        ],
        out_specs=pl.BlockSpec((tm, d), lambda i: (i, 0)),
        out_shape=jax.ShapeDtypeStruct((n, d), F32),
        compiler_params=_params(1),
        name="mem_attn",
    )(x2, g_pre.reshape(1, d), w_q, kv, kv, w_o, g_post.reshape(1, d))


def kernel(x, mem, g_ffn1_pre, w_ffn1_gu, w_ffn1_down, g_ffn1_post, g_mix_pre, w_in, hyena_conv_w,
           hyena_conv_b, filt_w1, filt_b1, filt_w2, filt_b2, filt_w3, filt_b3, filt_w4, filt_freq,
           hyena_d, w_hyena_out, w_fnet_out, w_pool, pool_scale, w_out, g_mix_post, g_mem_pre,
           g_mem_kv, w_q, w_kv, w_o, g_mem_post, g_ffn2_pre, w_ffn2_gu, w_ffn2_down, g_ffn2_post):
    batch, seq, d = x.shape
    n_mem = mem.shape[1]
    depth = w_in.shape[0]
    hy_w = w_hyena_out.shape[1]
    hy_cols = hyena_conv_w.shape[2]
    fw = w_fnet_out.shape[1]
    pw = w_pool.shape[1] * w_pool.shape[2]

    tm = 512
    tn = fw + pw
    tf = 512
    ct = V7X_MXU_COLS

    lhs1, gmat, lhs2 = _hyena_mats(seq)
    cs = _fnet_channel_mat(fw // FNET_GROUPS)
    fma, fmb, fmc, fr = _fnet3_mats(seq)

    bf = lambda w: w.astype(BF16)
    w_hyena_out, w_fnet_out, w_pool = map(bf, (w_hyena_out, w_fnet_out, w_pool))
    x2 = x.reshape(batch * seq, d)
    mem2 = mem.reshape(batch * n_mem, d)
    for l in range(depth):
        h3 = _filter_mlp(filt_w1[l], filt_b1[l], filt_w2[l], filt_b2[l], filt_w3[l], filt_b3[l],
                         filt_freq[l], seq=seq)
        first = [(w_ffn1_gu, 0), (w_ffn1_down, 0)] if l == 0 else []
        hspec, *first_w = _filter_spectrum(h3, filt_w4[l], hyena_d[l], lhs1, gmat, seq=seq, width=hy_w,
                                           cw=ct // 2, ct=ct, cast=first)
        if l == 0:
            ffn_w = first_w

        later = (w_ffn2_gu, w_ffn2_down, w_in, w_out, w_q, w_o)
        x2, *cast = _ffn(x2, g_ffn1_pre[l], *ffn_w, g_ffn1_post[l], tm=2 * tm, tf=tf,
                         cast=[(w, l) for w in later])
        *ffn_w, w_in_l, w_out_l, w_q_l, w_o_l = cast

        hyc, pq, po, gates = _proj(x2, g_mix_pre[l], w_in_l, hyena_conv_w[l], hyena_conv_b[l], cs,
                                   tm=2 * tm, tn=tn, seq=seq, hy_cols=hy_cols, fw=fw, pw=pw)
        z2 = _hyena(hyc, hspec, lhs1, gmat, lhs2, seq=seq, width=hy_w, ct=ct)
        fy = _fnet3(pq, fma, fmb, fmc, seq=seq, fw=fw, ct=ct, r=fr)
        x2 = _merge(x2, z2, fy, po, gates, w_hyena_out, w_fnet_out, w_pool,
                    pool_scale[l], w_out_l, g_mix_post[l], l, tm=tm, seq=seq)

        kv = _kv(mem2, g_mem_kv[l], w_kv, l, n_mem=batch * n_mem, tn=tn)
        x2 = _attn(x2, g_mem_pre[l], w_q_l, kv, w_o_l, g_mem_post[l], tm=tm, seq=seq, n_mem=n_mem)

        nxt = [(w_ffn1_gu, l + 1), (w_ffn1_down, l + 1)] if l + 1 < depth else []
        if nxt:
            x2, *ffn_w = _ffn(x2, g_ffn2_pre[l], *ffn_w, g_ffn2_post[l], tm=2 * tm, tf=tf, cast=nxt)
        else:
            x2 = _ffn_stream(x2, g_ffn2_pre[l], *ffn_w, g_ffn2_post[l], tm=2 * tm, tf=tf)
    return x2.reshape(batch, seq, d)
```

```python
import functools
import math

import jax
import jax.numpy as jnp
import numpy as np
from jax import lax
from jax.experimental import pallas as pl
from jax.experimental.pallas import tpu as pltpu

F32 = jnp.float32
BF16 = jnp.bfloat16

RMS_EPS = 1e-6
MACARON_WEIGHT = 0.5
N_HEADS = 4
POOL_WINDOWS = (2, 4, 8, 16)
FNET_GROUPS = 4
FILTER_BANDS = 16
DECAY_TARGET = 1e-2
FAST_DECAY_PCT = 0.3
SLOW_DECAY_PCT = 1.5

V7X_SUBLANES = 8
V7X_BF16_ROWS = 16
V7X_MXU_COLS = 256
V7X_VMEM_LIMIT = 60 * 2**20

HYENA_NB = 128
HALO = V7X_BF16_ROWS
ROW_CHUNK = 256
LANE_BLOCK = 128


def _params(n_axes):
    return pltpu.CompilerParams(
        dimension_semantics=("arbitrary",) * n_axes,
        vmem_limit_bytes=V7X_VMEM_LIMIT,
    )


def _resident(shape):
    zeros = (0,) * len(shape)
    return pl.BlockSpec(shape, lambda *_: zeros, pipeline_mode=pl.Buffered(1))


def _resident_layer(stacked, layer):
    tail = (0,) * (stacked.ndim - 1)
    return pl.BlockSpec((None,) + stacked.shape[1:], lambda *_: (layer,) + tail,
                        pipeline_mode=pl.Buffered(1))


def _rms(x, g):
    ms = jnp.mean(x * x, axis=-1, keepdims=True)
    return x * lax.rsqrt(ms + RMS_EPS) * g


def _sigmoid(x):
    return 0.5 * jnp.tanh(0.5 * x) + 0.5


def _dot(a, b):
    return jnp.dot(a, b, preferred_element_type=F32)


def _dot3(a, b):
    ah = a.astype(BF16)
    al = (a - ah.astype(F32)).astype(BF16)
    bh = b.astype(BF16)
    bl = (b - bh.astype(F32)).astype(BF16)
    return _dot(ah, bh) + (_dot(al, bh) + _dot(ah, bl))


def _row_chunks(rows, fn):
    chunk = ROW_CHUNK if rows % ROW_CHUNK == 0 else rows

    def body(c, carry):
        fn(pl.ds(pl.multiple_of(c * chunk, chunk), chunk))
        return carry

    lax.fori_loop(0, rows // chunk, body, 0)


def _norm_residual_inplace(o_ref, x_ref, g):
    rows, d = o_ref.shape

    def post(r):
        sq = None
        for c in range(0, d, LANE_BLOCK):
            yb = o_ref[r, c:c + LANE_BLOCK]
            sq = yb * yb if sq is None else sq + yb * yb
        inv = lax.rsqrt(jnp.sum(sq, axis=-1, keepdims=True) * (1.0 / d) + RMS_EPS)
        for c in range(0, d, LANE_BLOCK):
            cols = slice(c, c + LANE_BLOCK)
            o_ref[r, cols] = x_ref[r, cols] + o_ref[r, cols] * inv * g[:, cols]

    _row_chunks(rows, post)


def _ffn_body(x_ref, gpre_ref, wg_ref, wu_ref, wd_ref, gpost_ref, *rest, n_cast):
    cast_in, o_ref, cast_out, u_ref = rest[:n_cast], rest[n_cast], rest[n_cast + 1:-1], rest[-1]
    k = pl.program_id(1)
    rows = x_ref.shape[0]

    @pl.when(k == 0)
    def _():
        def pre(r):
            u_ref[r, :] = _rms(x_ref[r, :], gpre_ref[...]).astype(BF16)
            o_ref[r, :] = jnp.zeros((r.size, o_ref.shape[1]), F32)
        _row_chunks(rows, pre)

    u = u_ref[...]
    a = _dot(u, wg_ref[...])
    b = _dot(u, wu_ref[...])
    h = (a * _sigmoid(a) * b).astype(BF16)
    for src, dst in zip(cast_in, cast_out):
        dst[...] = src[...].astype(BF16)
    o_ref[...] += _dot(h, wd_ref[...])

    @pl.when(k == pl.num_programs(1) - 1)
    def _():
        _norm_residual_inplace(o_ref, x_ref, MACARON_WEIGHT * gpost_ref[...])


def _cast_tiling(rows, cols, ni, nk):
    rb = rows // ni
    assert rows % ni == 0 and rb % V7X_BF16_ROWS == 0
    nc = max(c for c in range(1, nk + 1) if cols % c == 0 and (cols // c) % 128 == 0)
    return rb, cols // nc, nc


def _ffn(x2, g_pre, w_gu, w_down, g_post, *, tm, tf, cast=()):
    n, d = x2.shape
    f = w_down.shape[0]
    ni, nk = n // tm, f // tf
    in_specs = [
        pl.BlockSpec((tm, d), lambda i, k: (i, 0)),
        pl.BlockSpec((1, d), lambda i, k: (0, 0)),
        pl.BlockSpec((d, tf), lambda i, k: (0, k)),
        pl.BlockSpec((d, tf), lambda i, k: (0, k + nk)),
        pl.BlockSpec((tf, d), lambda i, k: (k, 0)),
        pl.BlockSpec((1, d), lambda i, k: (0, 0)),
    ]
    args = [x2, g_pre.reshape(1, d), w_gu, w_gu, w_down, g_post.reshape(1, d)]
    out_specs = [pl.BlockSpec((tm, d), lambda i, k: (i, 0))]
    out_shape = [jax.ShapeDtypeStruct((n, d), F32)]
    for w, layer in cast:
        rb, cb, nc = _cast_tiling(w.shape[1], w.shape[2], ni, nk)
        in_specs.append(pl.BlockSpec((None, rb, cb),
                                     lambda i, k, layer=layer, nc=nc: (layer, i, jnp.minimum(k, nc - 1))))
        args.append(w)
        out_specs.append(pl.BlockSpec((rb, cb), lambda i, k, nc=nc: (i, jnp.minimum(k, nc - 1))))
        out_shape.append(jax.ShapeDtypeStruct(w.shape[1:], BF16))
    return pl.pallas_call(
        functools.partial(_ffn_body, n_cast=len(cast)),
        grid=(ni, nk),
        in_specs=in_specs,
        out_specs=out_specs,
        out_shape=out_shape,
        scratch_shapes=[pltpu.VMEM((tm, d), BF16)],
        compiler_params=_params(2),
        name="ffn",
    )(*args)


STREAM_CAST_CHUNKS = 8


def _ffn_stream_body(x_ref, gpre_ref, gpost_ref, wgu_hbm, wd_hbm, *rest, nk, tf, f, layers):
    n_cast = len(layers)
    cast_src, o_ref, cast_dst = rest[:n_cast], rest[n_cast], rest[n_cast + 1:2 * n_cast + 1]
    u_ref, wg_buf, wu_buf, wd_buf, sem = rest[2 * n_cast + 1:2 * n_cast + 6]
    cast_bufs = rest[2 * n_cast + 6:]
    nc = STREAM_CAST_CHUNKS
    i = pl.program_id(0)
    last_tile = pl.num_programs(0) - 1
    rows = x_ref.shape[0]

    def cast_in(j, c, slot):
        cin, _, csem = cast_bufs[3 * j:3 * j + 3]
        rb, cb = cin.shape[1:]
        src = cast_src[j].at[layers[j], pl.ds(pl.multiple_of(i * rb, rb), rb), pl.ds(pl.multiple_of(c * cb, cb), cb)]
        return pltpu.make_async_copy(src, cin.at[slot], csem.at[slot, 0])

    def cast_out(j, c, slot):
        _, cout, csem = cast_bufs[3 * j:3 * j + 3]
        rb, cb = cout.shape[1:]
        dst = cast_dst[j].at[pl.ds(pl.multiple_of(i * rb, rb), rb), pl.ds(pl.multiple_of(c * cb, cb), cb)]
        return pltpu.make_async_copy(cout.at[slot], dst, csem.at[slot, 1])

    @pl.when(i == 0)
    def _():
        for j in range(n_cast):
            cast_bufs[3 * j][...] = jnp.zeros_like(cast_bufs[3 * j])

    def copies(k, slot):
        c0 = pl.multiple_of(k * tf, tf)
        return (pltpu.make_async_copy(wgu_hbm.at[:, pl.ds(c0, tf)], wg_buf.at[slot], sem.at[slot, 0]),
                pltpu.make_async_copy(wgu_hbm.at[:, pl.ds(pl.multiple_of(f + c0, tf), tf)], wu_buf.at[slot],
                                      sem.at[slot, 1]),
                pltpu.make_async_copy(wd_hbm.at[pl.ds(c0, tf), :], wd_buf.at[slot], sem.at[slot, 2]))

    base = lax.rem(i * nk, 2)

    @pl.when(i == 0)
    def _():
        for c in copies(0, base):
            c.start()

    def pre(r):
        u_ref[r, :] = _rms(x_ref[r, :], gpre_ref[...]).astype(BF16)
        o_ref[r, :] = jnp.zeros((r.size, o_ref.shape[1]), F32)
    _row_chunks(rows, pre)

    def chunk(k, carry):
        slot = lax.rem(base + k, 2)
        @pl.when((k < nk - 1) | (i < last_tile))
        def _():
            for c in copies(lax.rem(k + 1, nk), 1 - slot):
                c.start()
        for c in copies(k, slot):
            c.wait()
        if n_cast:
            cur, prv = lax.rem(k, 2), lax.rem(k + 1, 2)

            @pl.when(k < nc)
            def _():
                for j in range(n_cast):
                    cast_in(j, k, cur).start()

            @pl.when((k >= 1) & (k <= nc))
            def _():
                for j in range(n_cast):
                    cast_in(j, k - 1, prv).wait()

            @pl.when((k >= 2) & (k <= nc + 1))
            def _():
                for j in range(n_cast):
                    cast_out(j, k - 2, cur).wait()
        u = u_ref[...]
        a = _dot(u, wg_buf[slot])
        b = _dot(u, wu_buf[slot])
        h = (a * _sigmoid(a) * b).astype(BF16)
        if n_cast:
            for j in range(n_cast):
                cast_bufs[3 * j + 1][prv] = cast_bufs[3 * j][prv].astype(BF16)
        o_ref[...] += _dot(h, wd_buf[slot])
        if n_cast:
            @pl.when((k >= 1) & (k <= nc))
            def _():
                for j in range(n_cast):
                    cast_out(j, k - 1, prv).start()
        return carry

    assert n_cast == 0 or nc + 2 <= nk
    lax.fori_loop(0, nk, chunk, 0)
    _norm_residual_inplace(o_ref, x_ref, MACARON_WEIGHT * gpost_ref[...])


def _ffn_stream(x2, g_pre, w_gu, w_down, g_post, *, tm, tf, cast=()):
    n, d = x2.shape
    f = w_down.shape[0]
    ni, nk = n // tm, f // tf
    any_spec = pl.BlockSpec(memory_space=pl.ANY)
    scratch = [pltpu.VMEM((tm, d), BF16),
               pltpu.VMEM((2, d, tf), BF16), pltpu.VMEM((2, d, tf), BF16), pltpu.VMEM((2, tf, d), BF16),
               pltpu.SemaphoreType.DMA((2, 3))]
    out_shape = [jax.ShapeDtypeStruct((n, d), F32)]
    for w, _ in cast:
        rb, cb = w.shape[1] // ni, w.shape[2] // STREAM_CAST_CHUNKS
        assert w.shape[1] % ni == 0 and rb % V7X_BF16_ROWS == 0
        assert w.shape[2] % STREAM_CAST_CHUNKS == 0 and cb % 128 == 0
        scratch += [pltpu.VMEM((2, rb, cb), F32), pltpu.VMEM((2, rb, cb), BF16), pltpu.SemaphoreType.DMA((2, 2))]
        out_shape.append(jax.ShapeDtypeStruct(w.shape[1:], BF16))
    return pl.pallas_call(
        functools.partial(_ffn_stream_body, nk=nk, tf=tf, f=f, layers=tuple(layer for _, layer in cast)),
        grid=(ni,),
        in_specs=[
            pl.BlockSpec((tm, d), lambda i: (i, 0)),
            pl.BlockSpec((1, d), lambda i: (0, 0)),
            pl.BlockSpec((1, d), lambda i: (0, 0)),
            any_spec,
            any_spec,
        ] + [any_spec] * len(cast),
        out_specs=[pl.BlockSpec((tm, d), lambda i: (i, 0))] + [any_spec] * len(cast),
        out_shape=out_shape,
        scratch_shapes=scratch,
        compiler_params=_params(1),
        name="ffn_stream",
    )(x2, g_pre.reshape(1, d), g_post.reshape(1, d), w_gu, w_down, *[w for w, _ in cast])


def _proj_body(x_ref, xp_ref, xn_ref, g_ref, w_ref, cw_ref, cb_ref, cs_ref,
               hy_ref, pq_ref, po_ref, gt_ref, u_ref, *, tm, seq, n_hy, fw):
    i = pl.program_id(0)
    j = pl.program_id(1)

    @pl.when(j == 0)
    def _():
        g = g_ref[...]
        has_prev = lax.rem(i * tm, seq) != 0
        has_next = lax.rem((i + 1) * tm, seq) != 0
        u_ref[pl.ds(0, HALO), :] = jnp.where(has_prev, _rms(xp_ref[...], g), 0.0).astype(BF16)
        chunk = ROW_CHUNK if tm % ROW_CHUNK == 0 else tm

        def pre(c, carry):
            r = pl.multiple_of(c * chunk, chunk)
            u_ref[pl.ds(pl.multiple_of(HALO + r, HALO), chunk), :] = (
                _rms(x_ref[pl.ds(r, chunk), :], g).astype(BF16))
            return carry

        lax.fori_loop(0, tm // chunk, pre, 0)
        u_ref[pl.ds(HALO + tm, HALO), :] = jnp.where(has_next, _rms(xn_ref[...], g), 0.0).astype(BF16)

    @pl.when(j < n_hy)
    def _():
        h = _dot(u_ref[...], w_ref[...])
        rows = h.shape[0]
        prev = pltpu.roll(h, 1, axis=0)[HALO:HALO + tm]
        nxt = pltpu.roll(h, rows - 1, axis=0)[HALO:HALO + tm]
        y = cb_ref[...] + prev * cw_ref[0:1, :]
        y = y + h[HALO:HALO + tm] * cw_ref[1:2, :]
        y = y + nxt * cw_ref[2:3, :]
        hy_ref[...] = y.astype(BF16)

    @pl.when(j == n_hy)
    def _():
        c = _dot(u_ref[pl.ds(HALO, tm), :], w_ref[...])
        po_ref[...] = c[:, fw:]
        fb = c[:, :fw].astype(BF16)
        gw = fw // FNET_GROUPS
        res = [_dot(fb[:, q * gw:(q + 1) * gw], cs_ref[...]) for q in range(FNET_GROUPS)]
        pq_ref[...] = jnp.concatenate([r[:, :gw] for r in res] + [r[:, gw:] for r in res], axis=1)

    @pl.when(j > n_hy)
    def _():
        c = _dot(u_ref[pl.ds(HALO, tm), :], w_ref[...])
        gt_ref[...] = _sigmoid(c).astype(BF16)


def _proj(x2, g, w_in, conv_w, conv_b, cs, *, tm, tn, seq, hy_cols, fw, pw):
    n, d = x2.shape
    cols = w_in.shape[1]
    n_hy = hy_cols // tn
    assert hy_cols % tn == 0 and fw + pw == tn and (cols - hy_cols - tn) % tn == 0
    nj = cols // tn
    n_gate = cols - hy_cols - tn
    hb = tm // HALO
    last_hb = n // HALO - 1
    body = functools.partial(_proj_body, tm=tm, seq=seq, n_hy=n_hy, fw=fw)
    return pl.pallas_call(
        body,
        grid=(n // tm, nj),
        in_specs=[
            pl.BlockSpec((tm, d), lambda i, j: (i, 0)),
            pl.BlockSpec((HALO, d), lambda i, j: (jnp.maximum(i * hb - 1, 0), 0)),
            pl.BlockSpec((HALO, d), lambda i, j: (jnp.minimum((i + 1) * hb, last_hb), 0)),
            pl.BlockSpec((1, d), lambda i, j: (0, 0)),
            pl.BlockSpec((d, tn), lambda i, j: (0, j)),
            pl.BlockSpec((3, tn), lambda i, j: (0, jnp.minimum(j, n_hy - 1))),
            pl.BlockSpec((1, tn), lambda i, j: (0, jnp.minimum(j, n_hy - 1))),
            _resident(cs.shape),
        ],
        out_specs=[
            pl.BlockSpec((tm, tn), lambda i, j: (i, jnp.minimum(j, n_hy - 1))),
            pl.BlockSpec((tm, 2 * fw), lambda i, j: (i, 0)),
            pl.BlockSpec((tm, pw), lambda i, j: (i, 0)),
            pl.BlockSpec((tm, tn), lambda i, j: (i, jnp.maximum(j - n_hy - 1, 0))),
        ],
        out_shape=[
            jax.ShapeDtypeStruct((n, hy_cols), BF16),
            jax.ShapeDtypeStruct((n, 2 * fw), F32),
            jax.ShapeDtypeStruct((n, pw), F32),
            jax.ShapeDtypeStruct((n, n_gate), BF16),
        ],
        scratch_shapes=[pltpu.VMEM((tm + 2 * HALO, d), BF16)],
        compiler_params=_params(2),
        name="mixer_proj",
    )(x2, x2, x2, g.reshape(1, d), w_in, conv_w, conv_b.reshape(1, hy_cols), cs)


def _hyena_slot_block(s, na):
    half = na // 2
    return s if s <= half else (half + 1) + (s - half)


def _hyena_mats(seq):
    nb = HYENA_NB
    na = 2 * seq // nb
    half = na // 2
    eye = np.eye(V7X_SUBLANES)
    a = np.arange(half)[None, :]
    f1 = np.zeros((na, half))
    k_re = np.arange(half + 1)[:, None]
    f1[: half + 1] = np.cos(2 * np.pi * k_re * a / na)
    k_im = np.arange(1, half)[:, None]
    f1[half + 1:] = -np.sin(2 * np.pi * k_im * a / na)
    lhs1 = np.kron(f1, eye)
    b = np.arange(nb)[None, :]
    kb = np.arange(nb)[:, None]
    g = np.zeros((half + 1, 2 * nb, 2 * nb))
    for ka in range(half + 1):
        ang = 2 * np.pi * (kb * b / nb + b * ka / (2 * seq))
        gre, gim = np.cos(ang), -np.sin(ang)
        g[ka] = np.block([[gre, -gim], [gim, gre]])
    a_col = np.arange(half)[:, None]
    f2 = np.zeros((half, na))
    wgt = np.full(half + 1, 2.0)
    wgt[0] = wgt[half] = 1.0
    f2[:, : half + 1] = wgt[None, :] * np.cos(2 * np.pi * a_col * np.arange(half + 1)[None, :] / na)
    f2[:, half + 1:] = -2.0 * np.sin(2 * np.pi * a_col * np.arange(1, half)[None, :] / na)
    lhs2 = np.kron(f2 / (2 * seq), eye)
    return (jnp.asarray(lhs1, BF16), jnp.asarray(g, BF16), jnp.asarray(lhs2, BF16))


def _fnet_channel_mat(group):
    cc = np.arange(group)[:, None]
    mm = np.arange(group)[None, :]
    angc = 2 * np.pi * cc * mm / group
    return jnp.asarray(np.concatenate([np.cos(angc), -np.sin(angc)], axis=1) / math.sqrt(group), BF16)


def _fnet3_mats(seq):
    r = int(round(seq ** (1.0 / 3.0)))
    sub = V7X_SUBLANES
    assert r ** 3 == seq and r % sub == 0
    idx = np.arange(r)
    ang = 2 * np.pi * idx[:, None] * idx[None, :] / r
    blk = np.array([[np.cos(ang), np.sin(ang)], [-np.sin(ang), np.cos(ang)]])
    ma = np.zeros((sub, 2, r, 2, r, sub))
    for lo in range(sub):
        ma[lo, :, :, :, :, lo] = blk.transpose(0, 2, 1, 3)
    ma = ma.reshape(2 * r * sub, 2 * r * sub)
    mb = np.zeros((r // sub, 2, r, sub, 2, r, sub))
    for hi in range(r // sub):
        for lo in range(sub):
            k2 = hi * sub + lo
            ph = 2 * np.pi * (idx[:, None] * idx[None, :] / r + idx[None, :] * k2 / r ** 2)
            c, s = np.cos(ph), np.sin(ph)
            mb[hi, 0, :, lo, 0, :, lo] = c
            mb[hi, 0, :, lo, 1, :, lo] = s
            mb[hi, 1, :, lo, 0, :, lo] = -s
            mb[hi, 1, :, lo, 1, :, lo] = c
    mb = mb.reshape(r // sub, 2 * r * sub, 2 * r * sub)
    mc = np.zeros((r, r // sub, r, sub, 2, r, sub))
    for k1 in range(r):
        for hi in range(r // sub):
            for lo in range(sub):
                k2 = hi * sub + lo
                ph = 2 * np.pi * (idx[:, None] * idx[None, :] / r
                                  + idx[None, :] * k1 / r ** 2 + idx[None, :] * k2 / seq)
                mc[k1, hi, :, lo, 0, :, lo] = np.cos(ph)
                mc[k1, hi, :, lo, 1, :, lo] = np.sin(ph)
    mc = mc.reshape(r * (r // sub), r * sub, 2 * r * sub) / math.sqrt(seq)
    return jnp.asarray(ma, BF16), jnp.asarray(mb, BF16), jnp.asarray(mc, BF16), r


def _static_loop(n, body):
    for i in range(n):
        body(i, 0)


def _gather_tiles(ref, starts):
    tiles = [ref[pl.ds(pl.multiple_of(s, V7X_SUBLANES), V7X_SUBLANES), :] for s in starts]
    return jnp.concatenate(tiles, axis=0)


def _hyena_stage1(z_ref, e_ref, lhs1_ref, seq):
    nb = HYENA_NB
    na = 2 * seq // nb
    half = na // 2

    def body(i, carry):
        off = i * V7X_SUBLANES
        xg = _gather_tiles(z_ref, [nb * a + off for a in range(half)]).astype(BF16)
        out = _dot(lhs1_ref[...], xg)
        for s in range(na):
            row = _hyena_slot_block(s, na) * nb + off
            e_ref[pl.ds(pl.multiple_of(row, V7X_SUBLANES), V7X_SUBLANES), :] = (
                out[s * V7X_SUBLANES:(s + 1) * V7X_SUBLANES])
        return carry

    _static_loop(nb // V7X_SUBLANES, body)


def _hyena_zero_unused(e_ref, seq):
    nb = HYENA_NB
    half = seq // nb
    zero = jnp.zeros((nb, e_ref.shape[1]), F32)
    e_ref[pl.ds((half + 1) * nb, nb), :] = zero
    e_ref[pl.ds((2 * half + 1) * nb, nb), :] = zero


def _hyena_load_ka(e_ref, ka, seq):
    nb = HYENA_NB
    half = seq // nb
    re = e_ref[pl.ds(pl.multiple_of(ka * nb, nb), nb), :]
    im = e_ref[pl.ds(pl.multiple_of((half + 1 + ka) * nb, nb), nb), :]
    return jnp.concatenate([re, im], axis=0)


def _filter_mlp_body(bands_ref, w1t_ref, w1c_ref, w1s_ref, b1_ref, w2_ref, b2_ref, w3_ref, b3_ref,
                     fr_ref, o_ref, *, seq):
    idx = lax.broadcasted_iota(jnp.int32, (1, 2 * seq), 1)
    p = jnp.where(idx < seq, idx, 2 * seq - idx).astype(F32)
    t = p / (seq - 1.0)
    ang = bands_ref[...] * ((2.0 * math.pi / seq) * p)
    fr = fr_ref[...]
    pre = (w1t_ref[...] * t
           + _dot3(w1c_ref[...], jnp.cos(ang))
           + _dot3(w1s_ref[...], -jnp.sin(ang))
           + b1_ref[...])
    h = jnp.sin(fr * pre)
    h = jnp.sin(fr * (_dot3(w2_ref[...], h) + b2_ref[...]))
    h = jnp.sin(fr * (_dot3(w3_ref[...], h) + b3_ref[...]))
    o_ref[...] = jnp.concatenate([h[:, :seq], h[:, seq:]], axis=0).T


def _filter_mlp(fw1, fb1, fw2, fb2, fw3, fb3, freq, *, seq):
    hid = fw2.shape[0]
    bands = np.linspace(1e-4, FILTER_BANDS - 1, FILTER_BANDS).astype(np.float32).reshape(-1, 1)
    col = lambda v: v.reshape(hid, 1)
    args = (jnp.asarray(bands), fw1[0].reshape(hid, 1), fw1[1:1 + FILTER_BANDS].T,
            fw1[1 + FILTER_BANDS:].T, col(fb1), fw2.T, col(fb2), fw3.T, col(fb3), col(freq))
    return pl.pallas_call(
        functools.partial(_filter_mlp_body, seq=seq),
        grid=(1,),
        in_specs=[_resident(a.shape) for a in args],
        out_specs=pl.BlockSpec((seq, 2 * hid), lambda i: (0, 0)),
        out_shape=jax.ShapeDtypeStruct((seq, 2 * hid), F32),
        compiler_params=_params(1),
        name="filter_mlp",
    )(*args)


def _filter_spec_body(h3_ref, w4f_ref, w4b_ref, dl_ref, d_ref, lhs1_ref, g_ref, *rest, seq, cw, n_cast):
    cast_in, o_ref, cast_out = rest[:n_cast], rest[n_cast], rest[n_cast + 1:2 * n_cast + 1]
    z_ref, e_ref = rest[-2:]
    for src, dst in zip(cast_in, cast_out):
        dst[...] = src[...].astype(BF16)
    nb = HYENA_NB
    half = seq // nb
    dl = dl_ref[...]
    zero = jnp.zeros_like(w4f_ref[...])
    w4 = jnp.concatenate([jnp.concatenate([w4f_ref[...], zero], axis=1),
                          jnp.concatenate([zero, w4b_ref[...]], axis=1)], axis=0)
    lane = lax.broadcasted_iota(jnp.int32, (1, 2 * cw), 1)
    chunk = ROW_CHUNK if seq % ROW_CHUNK == 0 else seq
    for c in range(seq // chunk):
        row = c * chunk + lax.broadcasted_iota(jnp.int32, (chunk, 1), 0)
        rowf = row.astype(F32)
        h4 = _dot3(h3_ref[pl.ds(c * chunk, chunk), :], w4)
        dec = jnp.concatenate([jnp.exp(-(rowf / (seq - 1.0)) * dl),
                               jnp.exp(-((seq - rowf) / (seq - 1.0)) * dl)], axis=1)
        z_ref[pl.ds(c * chunk, chunk), :] = jnp.where((row == 0) & (lane >= cw), 0.0, h4 * dec)

    _hyena_zero_unused(e_ref, seq)
    _hyena_stage1(z_ref, e_ref, lhs1_ref, seq)
    dsk = d_ref[0]

    def body(ka, carry):
        ein = _hyena_load_ka(e_ref, ka, seq).astype(BF16)
        x = _dot(g_ref[ka], ein)
        sgn = 1.0 - 2.0 * (ka % 2)
        hsp = x[:, 0:cw] + sgn * x[:, cw:2 * cw]
        part = lax.broadcasted_iota(jnp.int32, (2 * nb, 1), 0)
        hsp = hsp + jnp.where(part < nb, dsk, 0.0)
        o_ref[0, pl.ds(pl.multiple_of(ka * 2 * nb, 2 * nb), 2 * nb), :] = hsp.astype(BF16)
        return carry

    _static_loop(half + 1, body)


def _filter_spectrum(h3, fw4, d_skip, lhs1, gmat, *, seq, width, cw, ct, cast=()):
    hid = h3.shape[1] // 2
    orders = d_skip.shape[0]
    nct = width // cw
    per = ct // cw
    nb = HYENA_NB
    half = seq // nb
    max_decay = math.log(DECAY_TARGET) / FAST_DECAY_PCT
    min_decay = math.log(DECAY_TARGET) / SLOW_DECAY_PCT
    deltas = np.abs(np.linspace(min_decay, max_decay, width)).astype(np.float32).reshape(1, width)
    rows = (half + 1) * 2 * nb
    body = functools.partial(_filter_spec_body, seq=seq, cw=cw, n_cast=len(cast))
    in_specs = [
        _resident(h3.shape),
        pl.BlockSpec((hid, cw), lambda o, c: (0, o * 2 * nct + c)),
        pl.BlockSpec((hid, cw), lambda o, c: (0, o * 2 * nct + nct + c)),
        pl.BlockSpec((1, cw), lambda o, c: (0, c)),
        pl.BlockSpec((1, 1, cw), lambda o, c: (o, 0, c)),
        _resident(lhs1.shape),
        _resident(gmat.shape),
    ]
    args = [h3, fw4, fw4, jnp.asarray(deltas), d_skip.reshape(orders, 1, width), lhs1, gmat]
    out_specs = [pl.BlockSpec((None, 1, rows, cw), lambda o, c: (c // per, o, 0, c % per))]
    out_shape = [jax.ShapeDtypeStruct((width // ct, orders, rows, ct), BF16)]
    steps = orders * nct
    for w, layer in cast:
        rb = w.shape[1] // steps
        assert w.shape[1] % steps == 0 and rb % V7X_BF16_ROWS == 0
        in_specs.append(pl.BlockSpec((None, rb, w.shape[2]), lambda o, c, layer=layer: (layer, o * nct + c, 0)))
        args.append(w)
        out_specs.append(pl.BlockSpec((rb, w.shape[2]), lambda o, c: (o * nct + c, 0)))
        out_shape.append(jax.ShapeDtypeStruct(w.shape[1:], BF16))
    return pl.pallas_call(
        body,
        grid=(orders, nct),
        in_specs=in_specs,
        out_specs=out_specs,
        out_shape=out_shape,
        scratch_shapes=[pltpu.VMEM((seq, 2 * cw), F32),
                        pltpu.VMEM((2 * (half + 1) * nb, 2 * cw), F32)],
        compiler_params=_params(2),
        name="filter_spectrum",
    )(*args)


def _hyena_body(v_ref, g1_ref, g2_ref, h_ref, lhs1_ref, g_ref, lhs2_ref, o_ref, z_ref, e_ref, *, seq):
    nb = HYENA_NB
    na = 2 * seq // nb
    half = na // 2
    pair = V7X_BF16_ROWS
    chunk = 512 if seq % 512 == 0 else seq

    for c in range(seq // chunk):
        z_ref[pl.ds(c * chunk, chunk), :] = v_ref[pl.ds(c * chunk, chunk), :].astype(F32)
    _hyena_zero_unused(e_ref, seq)

    for order, gate_ref in enumerate((g1_ref, g2_ref)):
        _hyena_stage1(z_ref, e_ref, lhs1_ref, seq)

        group = 3 if (half + 1) % 3 == 0 else 1

        def spectral(it, carry):
            kas = [it * group + u for u in range(group)]
            eins = [_hyena_load_ka(e_ref, ka, seq).astype(BF16) for ka in kas]
            ys = []
            for ka, ein in zip(kas, eins):
                gk = g_ref[ka]
                x = _dot(gk, ein)
                hsp = h_ref[order, pl.ds(pl.multiple_of(ka * 2 * nb, 2 * nb), 2 * nb), :].astype(F32)
                xre, xim, hre, him = x[:nb], x[nb:], hsp[:nb], hsp[nb:]
                zsp = jnp.concatenate([xre * hre - xim * him, xre * him + xim * hre], axis=0).astype(BF16)
                ys.append(lax.dot_general(gk, zsp, (((0,), (0,)), ((), ())), preferred_element_type=F32))
            for ka, y in zip(kas, ys):
                e_ref[pl.ds(pl.multiple_of(ka * nb, nb), nb), :] = y[:nb]
                keep = jnp.where((ka == 0) | (ka == half), 0.0, 1.0)
                e_ref[pl.ds(pl.multiple_of((half + 1 + ka) * nb, nb), nb), :] = y[nb:] * keep
            return carry

        _static_loop((half + 1) // group, spectral)

        def inverse(i, carry):
            outs = []
            for q in range(pair // V7X_SUBLANES):
                off = i * pair + q * V7X_SUBLANES
                yg = _gather_tiles(e_ref, [_hyena_slot_block(s, na) * nb + off for s in range(na)])
                outs.append(_dot(lhs2_ref[...], yg.astype(BF16)))
            for a in range(half):
                sl = slice(a * V7X_SUBLANES, (a + 1) * V7X_SUBLANES)
                conv = jnp.concatenate([o_[sl] for o_ in outs], axis=0)
                r = pl.multiple_of(a * nb + i * pair, pair)
                res = gate_ref[pl.ds(r, pair), :].astype(F32) * conv
                if order == 0:
                    z_ref[pl.ds(r, pair), :] = res
                else:
                    o_ref[pl.ds(r, pair), :] = res.astype(BF16)
            return carry

        _static_loop(nb // pair, inverse)


def _hyena(hyc, hspec, lhs1, gmat, lhs2, *, seq, width, ct):
    n = hyc.shape[0]
    nct = width // ct
    nb = HYENA_NB
    half = seq // nb
    return pl.pallas_call(
        functools.partial(_hyena_body, seq=seq),
        grid=(nct, n // seq),
        in_specs=[
            pl.BlockSpec((seq, ct), lambda c, b: (b, c)),
            pl.BlockSpec((seq, ct), lambda c, b: (b, nct + c)),
            pl.BlockSpec((seq, ct), lambda c, b: (b, 2 * nct + c)),
            pl.BlockSpec((None,) + hspec.shape[1:], lambda c, b: (c, 0, 0, 0)),
            _resident(lhs1.shape),
            _resident(gmat.shape),
            _resident(lhs2.shape),
        ],
        out_specs=pl.BlockSpec((seq, ct), lambda c, b: (b, c)),
        out_shape=jax.ShapeDtypeStruct((n, width), BF16),
        scratch_shapes=[pltpu.VMEM((seq, ct), F32),
                        pltpu.VMEM((2 * (half + 1) * nb, ct), F32)],
        compiler_params=_params(2),
        name="hyena",
    )(hyc, hyc, hyc, hspec, lhs1, gmat, lhs2)


def _fnet3_body(p_ref, q_ref, ma_ref, mb_ref, mc_ref, o_ref, t_ref, *, seq, r):
    sub = V7X_SUBLANES
    nhi = r // sub
    row = lambda part, x0, x1, x2: part * seq + (x0 * r + x1) * r + x2

    def stage_a(i, carry):
        t1, t0_hi = divmod(i, nhi)
        starts = [r * r * t2 + r * t1 + sub * t0_hi for t2 in range(r)]
        xin = jnp.concatenate([_gather_tiles(p_ref, starts), _gather_tiles(q_ref, starts)],
                              axis=0).astype(BF16)
        out = _dot(ma_ref[...], xin)
        for lo in range(sub):
            for part in range(2):
                src = (lo * 2 + part) * r
                t_ref[pl.ds(row(part, t0_hi * sub + lo, t1, 0), r), :] = out[src:src + r]
        return carry

    _static_loop(r * nhi, stage_a)

    def stage_b(i, carry):
        t0, k2_hi = divmod(i, nhi)
        starts = [row(part, t0, t1, sub * k2_hi) for part in range(2) for t1 in range(r)]
        out = _dot(mb_ref[k2_hi], _gather_tiles(t_ref, starts).astype(BF16))
        for j, s in enumerate(starts):
            t_ref[pl.ds(s, sub), :] = out[j * sub:(j + 1) * sub]
        return carry

    _static_loop(r * nhi, stage_b)

    def stage_c(i, carry):
        k1, k2_hi = divmod(i, nhi)
        starts = [row(part, t0, k1, sub * k2_hi) for part in range(2) for t0 in range(r)]
        out = _dot(mc_ref[i], _gather_tiles(t_ref, starts).astype(BF16))
        for k0 in range(r):
            o_ref[pl.ds(r * r * k0 + r * k1 + sub * k2_hi, sub), :] = out[k0 * sub:(k0 + 1) * sub]
        return carry

    _static_loop(r * nhi, stage_c)


def _fnet3(pq, ma, mb, mc, *, seq, fw, ct, r):
    n = pq.shape[0]
    nct = fw // ct
    return pl.pallas_call(
        functools.partial(_fnet3_body, seq=seq, r=r),
        grid=(n // seq, nct),
        in_specs=[
            pl.BlockSpec((seq, ct), lambda b, c: (b, c)),
            pl.BlockSpec((seq, ct), lambda b, c: (b, nct + c)),
            _resident(ma.shape),
            _resident(mb.shape),
            _resident(mc.shape),
        ],
        out_specs=pl.BlockSpec((seq, ct), lambda b, c: (b, c)),
        out_shape=jax.ShapeDtypeStruct((n, fw), F32),
        scratch_shapes=[pltpu.VMEM((2 * seq, ct), F32)],
        compiler_params=_params(2),
        name="fnet_seq",
    )(pq, pq, ma, mb, mc)


def _merge_body(x_ref, z_ref, f_ref, po_ref, pp_ref, pn_ref, gt_ref, wha_ref, wfo_ref, wpl_ref,
                ps_ref, wo_ref, gpost_ref, o_ref, ext_ref, m_ref, *, tm, seq, d):
    i = pl.program_id(0)
    sub = V7X_SUBLANES
    has_prev = lax.rem(i * tm, seq) != 0
    has_next = lax.rem((i + 1) * tm, seq) != 0
    ext_ref[pl.ds(0, sub), :] = jnp.where(has_prev, pp_ref[...], 0.0)
    ext_ref[pl.ds(sub, tm), :] = po_ref[...]
    ext_ref[pl.ds(sub + tm, sub), :] = jnp.where(has_next, pn_ref[...], 0.0)

    gate = lambda q, lo, hi: gt_ref[:, q * d + lo:q * d + hi].astype(F32)
    m_ref[...] = gate(0, 0, d) * _dot(z_ref[...], wha_ref[...])
    m_ref[...] += gate(1, 0, d) * _dot(f_ref[...].astype(BF16), wfo_ref[...])

    pos = lax.rem(i * tm, seq) + lax.broadcasted_iota(jnp.int32, (tm, 1), 0)
    n_pool = len(POOL_WINDOWS)
    gw = po_ref.shape[1] // n_pool
    ow = d // n_pool
    for q, w in enumerate(POOL_WINDOWS):
        before = w // 2
        after = w - 1 - before
        cols = slice(q * gw, (q + 1) * gw)
        tot = ext_ref[pl.ds(sub - before, tm), cols]
        for s in range(-before + 1, after + 1):
            tot = tot + ext_ref[pl.ds(sub + s, tm), cols]
        lo = jnp.maximum(pos - before, 0)
        hi = jnp.minimum(pos + after, seq - 1)
        cnt = (hi - lo + 1).astype(F32)
        mq = (tot / cnt - po_ref[:, cols]).astype(BF16)
        yc = _dot(mq, wpl_ref[q]) * ps_ref[:, q * ow:(q + 1) * ow]
        m_ref[:, q * ow:(q + 1) * ow] += gate(2, q * ow, (q + 1) * ow) * yc

    y = _dot(m_ref[...].astype(BF16), wo_ref[...])
    o_ref[...] = x_ref[...] + _rms(y, gpost_ref[...])


def _merge(x2, z2, fy, po, gates, wha, wfo, wpool, pscale, w_out, g_post, layer, *, tm, seq):
    n, d = x2.shape
    pw = po.shape[1]
    sub = V7X_SUBLANES
    hb = tm // sub
    last_hb = n // sub - 1
    body = functools.partial(_merge_body, tm=tm, seq=seq, d=d)
    return pl.pallas_call(
        body,
        grid=(n // tm,),
        in_specs=[
            pl.BlockSpec((tm, d), lambda i: (i, 0)),
            pl.BlockSpec((tm, z2.shape[1]), lambda i: (i, 0)),
            pl.BlockSpec((tm, fy.shape[1]), lambda i: (i, 0)),
            pl.BlockSpec((tm, pw), lambda i: (i, 0)),
            pl.BlockSpec((sub, pw), lambda i: (jnp.maximum(i * hb - 1, 0), 0)),
            pl.BlockSpec((sub, pw), lambda i: (jnp.minimum((i + 1) * hb, last_hb), 0)),
            pl.BlockSpec((tm, gates.shape[1]), lambda i: (i, 0)),
            _resident_layer(wha, layer),
            _resident_layer(wfo, layer),
            _resident_layer(wpool, layer),
            _resident((1, d)),
            _resident(w_out.shape),
            _resident((1, d)),
        ],
        out_specs=pl.BlockSpec((tm, d), lambda i: (i, 0)),
        out_shape=jax.ShapeDtypeStruct((n, d), F32),
        scratch_shapes=[pltpu.VMEM((tm + 2 * sub, pw), F32), pltpu.VMEM((tm, d), F32)],
        compiler_params=_params(1),
        name="mixer_merge",
    )(x2, z2, fy, po, po, po, gates, wha, wfo, wpool, pscale.reshape(1, d), w_out, g_post.reshape(1, d))


def _kv_body(m_ref, g_ref, w_ref, o_ref, mn_ref):
    @pl.when((pl.program_id(0) == 0) | (pl.num_programs(1) > 1))
    def _():
        mn_ref[...] = _rms(m_ref[...], g_ref[...]).astype(BF16)

    o_ref[...] = _dot(mn_ref[...], w_ref[...].astype(BF16)).astype(BF16)


def _kv(mem2, g, w_kv, layer, *, n_mem, tn):
    n, d = mem2.shape
    cols = w_kv.shape[2]
    return pl.pallas_call(
        _kv_body,
        grid=(cols // tn, n // n_mem),
        in_specs=[
            pl.BlockSpec((n_mem, d), lambda j, b: (b, 0)),
            pl.BlockSpec((1, d), lambda j, b: (0, 0)),
            pl.BlockSpec((None, d, tn), lambda j, b: (layer, 0, j)),
        ],
        out_specs=pl.BlockSpec((n_mem, tn), lambda j, b: (b, j)),
        out_shape=jax.ShapeDtypeStruct((n, cols), BF16),
        scratch_shapes=[pltpu.VMEM((n_mem, d), BF16)],
        compiler_params=_params(2),
        name="mem_kv",
    )(mem2, g.reshape(1, d), w_kv)


def _attn_body(x_ref, gpre_ref, wq_ref, k_ref, v_ref, wo_ref, gpost_ref, o_ref, *, d):
    dh = d // N_HEADS
    x = x_ref[...]
    q = _dot(_rms(x, gpre_ref[...]).astype(BF16), wq_ref[...])
    heads = []
    for h in range(N_HEADS):
        cols = slice(h * dh, (h + 1) * dh)
        s = lax.dot_general(q[:, cols].astype(BF16), k_ref[:, cols], (((1,), (1,)), ((), ())),
                            preferred_element_type=F32) * (dh ** -0.5)
        e = jnp.exp(s - jnp.max(s, axis=-1, keepdims=True))
        p = e / jnp.sum(e, axis=-1, keepdims=True)
        heads.append(_dot(p.astype(BF16), v_ref[:, cols]))
    o = jnp.concatenate(heads, axis=1).astype(BF16)
    o_ref[...] = x + _rms(_dot(o, wo_ref[...]), gpost_ref[...])


def _attn(x2, g_pre, w_q, kv, w_o, g_post, *, tm, seq, n_mem):
    n, d = x2.shape
    tiles_per_seq = seq // tm
    return pl.pallas_call(
        functools.partial(_attn_body, d=d),
        grid=(n // tm,),
        in_specs=[
            pl.BlockSpec((tm, d), lambda i: (i, 0)),
            _resident((1, d)),
            _resident(w_q.shape),
            pl.BlockSpec((n_mem, d), lambda i: (i // tiles_per_seq, 0)),
            pl.BlockSpec((n_mem, d), lambda i: (i // tiles_per_seq, 1)),
            _resident(w_o.shape),
            _resident((1, d)),
        ],
        out_specs=pl.BlockSpec((tm, d), lambda i: (i, 0)),
        out_shape=jax.ShapeDtypeStruct((n, d), F32),
        compiler_params=_params(1),
        name="mem_attn",
    )(x2, g_pre.reshape(1, d), w_q, kv, kv, w_o, g_post.reshape(1, d))


def kernel(x, mem, g_ffn1_pre, w_ffn1_gu, w_ffn1_down, g_ffn1_post, g_mix_pre, w_in, hyena_conv_w,
           hyena_conv_b, filt_w1, filt_b1, filt_w2, filt_b2, filt_w3, filt_b3, filt_w4, filt_freq,
           hyena_d, w_hyena_out, w_fnet_out, w_pool, pool_scale, w_out, g_mix_post, g_mem_pre,
           g_mem_kv, w_q, w_kv, w_o, g_mem_post, g_ffn2_pre, w_ffn2_gu, w_ffn2_down, g_ffn2_post):
    batch, seq, d = x.shape
    n_mem = mem.shape[1]
    depth = w_in.shape[0]
    hy_w = w_hyena_out.shape[1]
    hy_cols = hyena_conv_w.shape[2]
    fw = w_fnet_out.shape[1]
    pw = w_pool.shape[1] * w_pool.shape[2]

    tm = 512
    tn = fw + pw
    tf = 512
    ct = V7X_MXU_COLS

    lhs1, gmat, lhs2 = _hyena_mats(seq)
    cs = _fnet_channel_mat(fw // FNET_GROUPS)
    fma, fmb, fmc, fr = _fnet3_mats(seq)

    bf = lambda w: w.astype(BF16)
    w_hyena_out, w_fnet_out, w_pool = map(bf, (w_hyena_out, w_fnet_out, w_pool))
    x2 = x.reshape(batch * seq, d)
    mem2 = mem.reshape(batch * n_mem, d)
    for l in range(depth):
        h3 = _filter_mlp(filt_w1[l], filt_b1[l], filt_w2[l], filt_b2[l], filt_w3[l], filt_b3[l],
                         filt_freq[l], seq=seq)
        first = [(w_ffn1_gu, 0), (w_ffn1_down, 0)] if l == 0 else []
        hspec, *first_w = _filter_spectrum(h3, filt_w4[l], hyena_d[l], lhs1, gmat, seq=seq, width=hy_w,
                                           cw=ct // 2, ct=ct, cast=first)
        if l == 0:
            ffn_w = first_w

        later = (w_ffn2_gu, w_ffn2_down, w_in, w_out, w_q, w_o)
        x2, *cast = _ffn_stream(x2, g_ffn1_pre[l], *ffn_w, g_ffn1_post[l], tm=2 * tm, tf=tf,
                                cast=[(w, l) for w in later])
        *ffn_w, w_in_l, w_out_l, w_q_l, w_o_l = cast

        hyc, pq, po, gates = _proj(x2, g_mix_pre[l], w_in_l, hyena_conv_w[l], hyena_conv_b[l], cs,
                                   tm=2 * tm, tn=tn, seq=seq, hy_cols=hy_cols, fw=fw, pw=pw)
        z2 = _hyena(hyc, hspec, lhs1, gmat, lhs2, seq=seq, width=hy_w, ct=ct)
        fy = _fnet3(pq, fma, fmb, fmc, seq=seq, fw=fw, ct=ct, r=fr)
        x2 = _merge(x2, z2, fy, po, gates, w_hyena_out, w_fnet_out, w_pool,
                    pool_scale[l], w_out_l, g_mix_post[l], l, tm=tm, seq=seq)

        kv = _kv(mem2, g_mem_kv[l], w_kv, l, n_mem=batch * n_mem, tn=tn)
        x2 = _attn(x2, g_mem_pre[l], w_q_l, kv, w_o_l, g_mem_post[l], tm=tm, seq=seq, n_mem=n_mem)

        nxt = [(w_ffn1_gu, l + 1), (w_ffn1_down, l + 1)] if l + 1 < depth else []
        x2, *ffn_w = _ffn_stream(x2, g_ffn2_pre[l], *ffn_w, g_ffn2_post[l], tm=2 * tm, tf=tf, cast=nxt)
    return x2.reshape(batch, seq, d)
```

```python
import functools
import math

import jax
import jax.numpy as jnp
import numpy as np
from jax import lax
from jax.experimental import pallas as pl
from jax.experimental.pallas import tpu as pltpu

F32 = jnp.float32
BF16 = jnp.bfloat16

RMS_EPS = 1e-6
MACARON_WEIGHT = 0.5
N_HEADS = 4
POOL_WINDOWS = (2, 4, 8, 16)
FNET_GROUPS = 4
FILTER_BANDS = 16
DECAY_TARGET = 1e-2
FAST_DECAY_PCT = 0.3
SLOW_DECAY_PCT = 1.5

V7X_SUBLANES = 8
V7X_BF16_ROWS = 16
V7X_MXU_COLS = 256
V7X_VMEM_LIMIT = 60 * 2**20

HYENA_NB = 128
HALO = V7X_BF16_ROWS
ROW_CHUNK = 256
LANE_BLOCK = 128


def _params(n_axes):
    return pltpu.CompilerParams(
        dimension_semantics=("arbitrary",) * n_axes,
        vmem_limit_bytes=V7X_VMEM_LIMIT,
    )


def _resident(shape):
    zeros = (0,) * len(shape)
    return pl.BlockSpec(shape, lambda *_: zeros, pipeline_mode=pl.Buffered(1))


def _resident_layer(stacked, layer):
    tail = (0,) * (stacked.ndim - 1)
    return pl.BlockSpec((None,) + stacked.shape[1:], lambda *_: (layer,) + tail,
                        pipeline_mode=pl.Buffered(1))


def _rms(x, g):
    ms = jnp.mean(x * x, axis=-1, keepdims=True)
    return x * lax.rsqrt(ms + RMS_EPS) * g


def _sigmoid(x):
    return 0.5 * jnp.tanh(0.5 * x) + 0.5


def _dot(a, b):
    return jnp.dot(a, b, preferred_element_type=F32)


def _dot3(a, b):
    ah = a.astype(BF16)
    al = (a - ah.astype(F32)).astype(BF16)
    bh = b.astype(BF16)
    bl = (b - bh.astype(F32)).astype(BF16)
    return _dot(ah, bh) + (_dot(al, bh) + _dot(ah, bl))


def _row_chunks(rows, fn):
    chunk = ROW_CHUNK if rows % ROW_CHUNK == 0 else rows

    def body(c, carry):
        fn(pl.ds(pl.multiple_of(c * chunk, chunk), chunk))
        return carry

    lax.fori_loop(0, rows // chunk, body, 0)


def _norm_residual_inplace(o_ref, x_ref, g):
    rows, d = o_ref.shape

    def post(r):
        sq = None
        for c in range(0, d, LANE_BLOCK):
            yb = o_ref[r, c:c + LANE_BLOCK]
            sq = yb * yb if sq is None else sq + yb * yb
        inv = lax.rsqrt(jnp.sum(sq, axis=-1, keepdims=True) * (1.0 / d) + RMS_EPS)
        for c in range(0, d, LANE_BLOCK):
            cols = slice(c, c + LANE_BLOCK)
            o_ref[r, cols] = x_ref[r, cols] + o_ref[r, cols] * inv * g[:, cols]

    _row_chunks(rows, post)


def _ffn_body(x_ref, gpre_ref, wg_ref, wu_ref, wd_ref, gpost_ref, *rest, n_cast):
    cast_in, o_ref, cast_out, u_ref = rest[:n_cast], rest[n_cast], rest[n_cast + 1:-1], rest[-1]
    k = pl.program_id(1)
    rows = x_ref.shape[0]

    @pl.when(k == 0)
    def _():
        def pre(r):
            u_ref[r, :] = _rms(x_ref[r, :], gpre_ref[...]).astype(BF16)
            o_ref[r, :] = jnp.zeros((r.size, o_ref.shape[1]), F32)
        _row_chunks(rows, pre)

    u = u_ref[...]
    a = _dot(u, wg_ref[...])
    b = _dot(u, wu_ref[...])
    h = (a * _sigmoid(a) * b).astype(BF16)
    for src, dst in zip(cast_in, cast_out):
        dst[...] = src[...].astype(BF16)
    o_ref[...] += _dot(h, wd_ref[...])

    @pl.when(k == pl.num_programs(1) - 1)
    def _():
        _norm_residual_inplace(o_ref, x_ref, MACARON_WEIGHT * gpost_ref[...])


def _cast_tiling(rows, cols, ni, nk):
    rb = rows // ni
    assert rows % ni == 0 and rb % V7X_BF16_ROWS == 0
    nc = max(c for c in range(1, nk + 1) if cols % c == 0 and (cols // c) % 128 == 0)
    return rb, cols // nc, nc


def _ffn(x2, g_pre, w_gu, w_down, g_post, *, tm, tf, cast=()):
    n, d = x2.shape
    f = w_down.shape[0]
    ni, nk = n // tm, f // tf
    in_specs = [
        pl.BlockSpec((tm, d), lambda i, k: (i, 0)),
        pl.BlockSpec((1, d), lambda i, k: (0, 0)),
        pl.BlockSpec((d, tf), lambda i, k: (0, k)),
        pl.BlockSpec((d, tf), lambda i, k: (0, k + nk)),
        pl.BlockSpec((tf, d), lambda i, k: (k, 0)),
        pl.BlockSpec((1, d), lambda i, k: (0, 0)),
    ]
    args = [x2, g_pre.reshape(1, d), w_gu, w_gu, w_down, g_post.reshape(1, d)]
    out_specs = [pl.BlockSpec((tm, d), lambda i, k: (i, 0))]
    out_shape = [jax.ShapeDtypeStruct((n, d), F32)]
    for w, layer in cast:
        rb, cb, nc = _cast_tiling(w.shape[1], w.shape[2], ni, nk)
        in_specs.append(pl.BlockSpec((None, rb, cb),
                                     lambda i, k, layer=layer, nc=nc: (layer, i, jnp.minimum(k, nc - 1))))
        args.append(w)
        out_specs.append(pl.BlockSpec((rb, cb), lambda i, k, nc=nc: (i, jnp.minimum(k, nc - 1))))
        out_shape.append(jax.ShapeDtypeStruct(w.shape[1:], BF16))
    return pl.pallas_call(
        functools.partial(_ffn_body, n_cast=len(cast)),
        grid=(ni, nk),
        in_specs=in_specs,
        out_specs=out_specs,
        out_shape=out_shape,
        scratch_shapes=[pltpu.VMEM((tm, d), BF16)],
        compiler_params=_params(2),
        name="ffn",
    )(*args)


def _ffn_stream_body(x_ref, gpre_ref, gpost_ref, wgu_hbm, wd_hbm, o_ref, u_ref, wg_buf, wu_buf, wd_buf, sem,
                     *, nk, tf, f):
    i = pl.program_id(0)
    last_tile = pl.num_programs(0) - 1
    rows = x_ref.shape[0]

    def copies(k, slot):
        c0 = pl.multiple_of(k * tf, tf)
        return (pltpu.make_async_copy(wgu_hbm.at[:, pl.ds(c0, tf)], wg_buf.at[slot], sem.at[slot, 0]),
                pltpu.make_async_copy(wgu_hbm.at[:, pl.ds(pl.multiple_of(f + c0, tf), tf)], wu_buf.at[slot],
                                      sem.at[slot, 1]),
                pltpu.make_async_copy(wd_hbm.at[pl.ds(c0, tf), :], wd_buf.at[slot], sem.at[slot, 2]))

    base = lax.rem(i * nk, 2)

    @pl.when(i == 0)
    def _():
        for c in copies(0, base):
            c.start()

    def pre(r):
        u_ref[r, :] = _rms(x_ref[r, :], gpre_ref[...]).astype(BF16)
        o_ref[r, :] = jnp.zeros((r.size, o_ref.shape[1]), F32)
    _row_chunks(rows, pre)

    def chunk(k, carry):
        slot = lax.rem(base + k, 2)
        @pl.when((k < nk - 1) | (i < last_tile))
        def _():
            for c in copies(lax.rem(k + 1, nk), 1 - slot):
                c.start()
        for c in copies(k, slot):
            c.wait()
        u = u_ref[...]
        a = _dot(u, wg_buf[slot])
        b = _dot(u, wu_buf[slot])
        h = (a * _sigmoid(a) * b).astype(BF16)
        o_ref[...] += _dot(h, wd_buf[slot])
        return carry

    lax.fori_loop(0, nk, chunk, 0)
    _norm_residual_inplace(o_ref, x_ref, MACARON_WEIGHT * gpost_ref[...])


def _ffn_stream(x2, g_pre, w_gu, w_down, g_post, *, tm, tf):
    n, d = x2.shape
    f = w_down.shape[0]
    nk = f // tf
    return pl.pallas_call(
        functools.partial(_ffn_stream_body, nk=nk, tf=tf, f=f),
        grid=(n // tm,),
        in_specs=[
            pl.BlockSpec((tm, d), lambda i: (i, 0)),
            pl.BlockSpec((1, d), lambda i: (0, 0)),
            pl.BlockSpec((1, d), lambda i: (0, 0)),
            pl.BlockSpec(memory_space=pl.ANY),
            pl.BlockSpec(memory_space=pl.ANY),
        ],
        out_specs=pl.BlockSpec((tm, d), lambda i: (i, 0)),
        out_shape=jax.ShapeDtypeStruct((n, d), F32),
        scratch_shapes=[pltpu.VMEM((tm, d), BF16),
                        pltpu.VMEM((2, d, tf), BF16), pltpu.VMEM((2, d, tf), BF16), pltpu.VMEM((2, tf, d), BF16),
                        pltpu.SemaphoreType.DMA((2, 3))],
        compiler_params=_params(1),
        name="ffn_stream",
    )(x2, g_pre.reshape(1, d), g_post.reshape(1, d), w_gu, w_down)


def _proj_body(x_ref, xp_ref, xn_ref, g_ref, w_ref, cw_ref, cb_ref, cs_ref,
               hy_ref, pq_ref, po_ref, gt_ref, u_ref, *, tm, seq, n_hy, fw):
    i = pl.program_id(0)
    j = pl.program_id(1)

    @pl.when(j == 0)
    def _():
        g = g_ref[...]
        has_prev = lax.rem(i * tm, seq) != 0
        has_next = lax.rem((i + 1) * tm, seq) != 0
        u_ref[pl.ds(0, HALO), :] = jnp.where(has_prev, _rms(xp_ref[...], g), 0.0).astype(BF16)
        chunk = ROW_CHUNK if tm % ROW_CHUNK == 0 else tm

        def pre(c, carry):
            r = pl.multiple_of(c * chunk, chunk)
            u_ref[pl.ds(pl.multiple_of(HALO + r, HALO), chunk), :] = (
                _rms(x_ref[pl.ds(r, chunk), :], g).astype(BF16))
            return carry

        lax.fori_loop(0, tm // chunk, pre, 0)
        u_ref[pl.ds(HALO + tm, HALO), :] = jnp.where(has_next, _rms(xn_ref[...], g), 0.0).astype(BF16)

    @pl.when(j < n_hy)
    def _():
        h = _dot(u_ref[...], w_ref[...])
        rows = h.shape[0]
        prev = pltpu.roll(h, 1, axis=0)[HALO:HALO + tm]
        nxt = pltpu.roll(h, rows - 1, axis=0)[HALO:HALO + tm]
        y = cb_ref[...] + prev * cw_ref[0:1, :]
        y = y + h[HALO:HALO + tm] * cw_ref[1:2, :]
        y = y + nxt * cw_ref[2:3, :]
        hy_ref[...] = y.astype(BF16)

    @pl.when(j == n_hy)
    def _():
        c = _dot(u_ref[pl.ds(HALO, tm), :], w_ref[...])
        po_ref[...] = c[:, fw:]
        fb = c[:, :fw].astype(BF16)
        gw = fw // FNET_GROUPS
        res = [_dot(fb[:, q * gw:(q + 1) * gw], cs_ref[...]) for q in range(FNET_GROUPS)]
        pq_ref[...] = jnp.concatenate([r[:, :gw] for r in res] + [r[:, gw:] for r in res], axis=1)

    @pl.when(j > n_hy)
    def _():
        c = _dot(u_ref[pl.ds(HALO, tm), :], w_ref[...])
        gt_ref[...] = _sigmoid(c).astype(BF16)


def _proj(x2, g, w_in, conv_w, conv_b, cs, *, tm, tn, seq, hy_cols, fw, pw):
    n, d = x2.shape
    cols = w_in.shape[1]
    n_hy = hy_cols // tn
    assert hy_cols % tn == 0 and fw + pw == tn and (cols - hy_cols - tn) % tn == 0
    nj = cols // tn
    n_gate = cols - hy_cols - tn
    hb = tm // HALO
    last_hb = n // HALO - 1
    body = functools.partial(_proj_body, tm=tm, seq=seq, n_hy=n_hy, fw=fw)
    return pl.pallas_call(
        body,
        grid=(n // tm, nj),
        in_specs=[
            pl.BlockSpec((tm, d), lambda i, j: (i, 0)),
            pl.BlockSpec((HALO, d), lambda i, j: (jnp.maximum(i * hb - 1, 0), 0)),
            pl.BlockSpec((HALO, d), lambda i, j: (jnp.minimum((i + 1) * hb, last_hb), 0)),
            pl.BlockSpec((1, d), lambda i, j: (0, 0)),
            pl.BlockSpec((d, tn), lambda i, j: (0, j)),
            pl.BlockSpec((3, tn), lambda i, j: (0, jnp.minimum(j, n_hy - 1))),
            pl.BlockSpec((1, tn), lambda i, j: (0, jnp.minimum(j, n_hy - 1))),
            _resident(cs.shape),
        ],
        out_specs=[
            pl.BlockSpec((tm, tn), lambda i, j: (i, jnp.minimum(j, n_hy - 1))),
            pl.BlockSpec((tm, 2 * fw), lambda i, j: (i, 0)),
            pl.BlockSpec((tm, pw), lambda i, j: (i, 0)),
            pl.BlockSpec((tm, tn), lambda i, j: (i, jnp.maximum(j - n_hy - 1, 0))),
        ],
        out_shape=[
            jax.ShapeDtypeStruct((n, hy_cols), BF16),
            jax.ShapeDtypeStruct((n, 2 * fw), F32),
            jax.ShapeDtypeStruct((n, pw), F32),
            jax.ShapeDtypeStruct((n, n_gate), BF16),
        ],
        scratch_shapes=[pltpu.VMEM((tm + 2 * HALO, d), BF16)],
        compiler_params=_params(2),
        name="mixer_proj",
    )(x2, x2, x2, g.reshape(1, d), w_in, conv_w, conv_b.reshape(1, hy_cols), cs)


def _hyena_slot_block(s, na):
    half = na // 2
    return s if s <= half else (half + 1) + (s - half)


def _hyena_mats(seq):
    nb = HYENA_NB
    na = 2 * seq // nb
    half = na // 2
    eye = np.eye(V7X_SUBLANES)
    a = np.arange(half)[None, :]
    f1 = np.zeros((na, half))
    k_re = np.arange(half + 1)[:, None]
    f1[: half + 1] = np.cos(2 * np.pi * k_re * a / na)
    k_im = np.arange(1, half)[:, None]
    f1[half + 1:] = -np.sin(2 * np.pi * k_im * a / na)
    lhs1 = np.kron(f1, eye)
    b = np.arange(nb)[None, :]
    kb = np.arange(nb)[:, None]
    g = np.zeros((half + 1, 2 * nb, 2 * nb))
    for ka in range(half + 1):
        ang = 2 * np.pi * (kb * b / nb + b * ka / (2 * seq))
        gre, gim = np.cos(ang), -np.sin(ang)
        g[ka] = np.block([[gre, -gim], [gim, gre]])
    a_col = np.arange(half)[:, None]
    f2 = np.zeros((half, na))
    wgt = np.full(half + 1, 2.0)
    wgt[0] = wgt[half] = 1.0
    f2[:, : half + 1] = wgt[None, :] * np.cos(2 * np.pi * a_col * np.arange(half + 1)[None, :] / na)
    f2[:, half + 1:] = -2.0 * np.sin(2 * np.pi * a_col * np.arange(1, half)[None, :] / na)
    lhs2 = np.kron(f2 / (2 * seq), eye)
    return (jnp.asarray(lhs1, BF16), jnp.asarray(g, BF16), jnp.asarray(lhs2, BF16))


def _fnet_channel_mat(group):
    cc = np.arange(group)[:, None]
    mm = np.arange(group)[None, :]
    angc = 2 * np.pi * cc * mm / group
    return jnp.asarray(np.concatenate([np.cos(angc), -np.sin(angc)], axis=1) / math.sqrt(group), BF16)


def _fnet3_mats(seq):
    r = int(round(seq ** (1.0 / 3.0)))
    sub = V7X_SUBLANES
    assert r ** 3 == seq and r % sub == 0
    idx = np.arange(r)
    ang = 2 * np.pi * idx[:, None] * idx[None, :] / r
    blk = np.array([[np.cos(ang), np.sin(ang)], [-np.sin(ang), np.cos(ang)]])
    ma = np.zeros((sub, 2, r, 2, r, sub))
    for lo in range(sub):
        ma[lo, :, :, :, :, lo] = blk.transpose(0, 2, 1, 3)
    ma = ma.reshape(2 * r * sub, 2 * r * sub)
    mb = np.zeros((r // sub, 2, r, sub, 2, r, sub))
    for hi in range(r // sub):
        for lo in range(sub):
            k2 = hi * sub + lo
            ph = 2 * np.pi * (idx[:, None] * idx[None, :] / r + idx[None, :] * k2 / r ** 2)
            c, s = np.cos(ph), np.sin(ph)
            mb[hi, 0, :, lo, 0, :, lo] = c
            mb[hi, 0, :, lo, 1, :, lo] = s
            mb[hi, 1, :, lo, 0, :, lo] = -s
            mb[hi, 1, :, lo, 1, :, lo] = c
    mb = mb.reshape(r // sub, 2 * r * sub, 2 * r * sub)
    mc = np.zeros((r, r // sub, r, sub, 2, r, sub))
    for k1 in range(r):
        for hi in range(r // sub):
            for lo in range(sub):
                k2 = hi * sub + lo
                ph = 2 * np.pi * (idx[:, None] * idx[None, :] / r
                                  + idx[None, :] * k1 / r ** 2 + idx[None, :] * k2 / seq)
                mc[k1, hi, :, lo, 0, :, lo] = np.cos(ph)
                mc[k1, hi, :, lo, 1, :, lo] = np.sin(ph)
    mc = mc.reshape(r * (r // sub), r * sub, 2 * r * sub) / math.sqrt(seq)
    return jnp.asarray(ma, BF16), jnp.asarray(mb, BF16), jnp.asarray(mc, BF16), r


def _static_loop(n, body):
    for i in range(n):
        body(i, 0)


def _gather_tiles(ref, starts):
    tiles = [ref[pl.ds(pl.multiple_of(s, V7X_SUBLANES), V7X_SUBLANES), :] for s in starts]
    return jnp.concatenate(tiles, axis=0)


def _hyena_stage1(z_ref, e_ref, lhs1_ref, seq):
    nb = HYENA_NB
    na = 2 * seq // nb
    half = na // 2

    def body(i, carry):
        off = i * V7X_SUBLANES
        xg = _gather_tiles(z_ref, [nb * a + off for a in range(half)]).astype(BF16)
        out = _dot(lhs1_ref[...], xg)
        for s in range(na):
            row = _hyena_slot_block(s, na) * nb + off
            e_ref[pl.ds(pl.multiple_of(row, V7X_SUBLANES), V7X_SUBLANES), :] = (
                out[s * V7X_SUBLANES:(s + 1) * V7X_SUBLANES])
        return carry

    _static_loop(nb // V7X_SUBLANES, body)


def _hyena_zero_unused(e_ref, seq):
    nb = HYENA_NB
    half = seq // nb
    zero = jnp.zeros((nb, e_ref.shape[1]), F32)
    e_ref[pl.ds((half + 1) * nb, nb), :] = zero
    e_ref[pl.ds((2 * half + 1) * nb, nb), :] = zero


def _hyena_load_ka(e_ref, ka, seq):
    nb = HYENA_NB
    half = seq // nb
    re = e_ref[pl.ds(pl.multiple_of(ka * nb, nb), nb), :]
    im = e_ref[pl.ds(pl.multiple_of((half + 1 + ka) * nb, nb), nb), :]
    return jnp.concatenate([re, im], axis=0)


def _filter_mlp_body(bands_ref, w1t_ref, w1c_ref, w1s_ref, b1_ref, w2_ref, b2_ref, w3_ref, b3_ref,
                     fr_ref, o_ref, *, seq):
    idx = lax.broadcasted_iota(jnp.int32, (1, 2 * seq), 1)
    p = jnp.where(idx < seq, idx, 2 * seq - idx).astype(F32)
    t = p / (seq - 1.0)
    ang = bands_ref[...] * ((2.0 * math.pi / seq) * p)
    fr = fr_ref[...]
    pre = (w1t_ref[...] * t
           + _dot3(w1c_ref[...], jnp.cos(ang))
           + _dot3(w1s_ref[...], -jnp.sin(ang))
           + b1_ref[...])
    h = jnp.sin(fr * pre)
    h = jnp.sin(fr * (_dot3(w2_ref[...], h) + b2_ref[...]))
    h = jnp.sin(fr * (_dot3(w3_ref[...], h) + b3_ref[...]))
    o_ref[...] = jnp.concatenate([h[:, :seq], h[:, seq:]], axis=0).T


def _filter_mlp(fw1, fb1, fw2, fb2, fw3, fb3, freq, *, seq):
    hid = fw2.shape[0]
    bands = np.linspace(1e-4, FILTER_BANDS - 1, FILTER_BANDS).astype(np.float32).reshape(-1, 1)
    col = lambda v: v.reshape(hid, 1)
    args = (jnp.asarray(bands), fw1[0].reshape(hid, 1), fw1[1:1 + FILTER_BANDS].T,
            fw1[1 + FILTER_BANDS:].T, col(fb1), fw2.T, col(fb2), fw3.T, col(fb3), col(freq))
    return pl.pallas_call(
        functools.partial(_filter_mlp_body, seq=seq),
        grid=(1,),
        in_specs=[_resident(a.shape) for a in args],
        out_specs=pl.BlockSpec((seq, 2 * hid), lambda i: (0, 0)),
        out_shape=jax.ShapeDtypeStruct((seq, 2 * hid), F32),
        compiler_params=_params(1),
        name="filter_mlp",
    )(*args)


def _filter_spec_body(h3_ref, w4f_ref, w4b_ref, dl_ref, d_ref, lhs1_ref, g_ref, *rest, seq, cw, n_cast):
    cast_in, o_ref, cast_out = rest[:n_cast], rest[n_cast], rest[n_cast + 1:2 * n_cast + 1]
    z_ref, e_ref = rest[-2:]
    for src, dst in zip(cast_in, cast_out):
        dst[...] = src[...].astype(BF16)
    nb = HYENA_NB
    half = seq // nb
    dl = dl_ref[...]
    zero = jnp.zeros_like(w4f_ref[...])
    w4 = jnp.concatenate([jnp.concatenate([w4f_ref[...], zero], axis=1),
                          jnp.concatenate([zero, w4b_ref[...]], axis=1)], axis=0)
    lane = lax.broadcasted_iota(jnp.int32, (1, 2 * cw), 1)
    chunk = ROW_CHUNK if seq % ROW_CHUNK == 0 else seq
    for c in range(seq // chunk):
        row = c * chunk + lax.broadcasted_iota(jnp.int32, (chunk, 1), 0)
        rowf = row.astype(F32)
        h4 = _dot3(h3_ref[pl.ds(c * chunk, chunk), :], w4)
        dec = jnp.concatenate([jnp.exp(-(rowf / (seq - 1.0)) * dl),
                               jnp.exp(-((seq - rowf) / (seq - 1.0)) * dl)], axis=1)
        z_ref[pl.ds(c * chunk, chunk), :] = jnp.where((row == 0) & (lane >= cw), 0.0, h4 * dec)

    _hyena_zero_unused(e_ref, seq)
    _hyena_stage1(z_ref, e_ref, lhs1_ref, seq)
    dsk = d_ref[0]

    def body(ka, carry):
        ein = _hyena_load_ka(e_ref, ka, seq).astype(BF16)
        x = _dot(g_ref[ka], ein)
        sgn = 1.0 - 2.0 * (ka % 2)
        hsp = x[:, 0:cw] + sgn * x[:, cw:2 * cw]
        part = lax.broadcasted_iota(jnp.int32, (2 * nb, 1), 0)
        hsp = hsp + jnp.where(part < nb, dsk, 0.0)
        o_ref[0, pl.ds(pl.multiple_of(ka * 2 * nb, 2 * nb), 2 * nb), :] = hsp.astype(BF16)
        return carry

    _static_loop(half + 1, body)


def _filter_spectrum(h3, fw4, d_skip, lhs1, gmat, *, seq, width, cw, ct, cast=()):
    hid = h3.shape[1] // 2
    orders = d_skip.shape[0]
    nct = width // cw
    per = ct // cw
    nb = HYENA_NB
    half = seq // nb
    max_decay = math.log(DECAY_TARGET) / FAST_DECAY_PCT
    min_decay = math.log(DECAY_TARGET) / SLOW_DECAY_PCT
    deltas = np.abs(np.linspace(min_decay, max_decay, width)).astype(np.float32).reshape(1, width)
    rows = (half + 1) * 2 * nb
    body = functools.partial(_filter_spec_body, seq=seq, cw=cw, n_cast=len(cast))
    in_specs = [
        _resident(h3.shape),
        pl.BlockSpec((hid, cw), lambda o, c: (0, o * 2 * nct + c)),
        pl.BlockSpec((hid, cw), lambda o, c: (0, o * 2 * nct + nct + c)),
        pl.BlockSpec((1, cw), lambda o, c: (0, c)),
        pl.BlockSpec((1, 1, cw), lambda o, c: (o, 0, c)),
        _resident(lhs1.shape),
        _resident(gmat.shape),
    ]
    args = [h3, fw4, fw4, jnp.asarray(deltas), d_skip.reshape(orders, 1, width), lhs1, gmat]
    out_specs = [pl.BlockSpec((None, 1, rows, cw), lambda o, c: (c // per, o, 0, c % per))]
    out_shape = [jax.ShapeDtypeStruct((width // ct, orders, rows, ct), BF16)]
    steps = orders * nct
    for w, layer in cast:
        rb = w.shape[1] // steps
        assert w.shape[1] % steps == 0 and rb % V7X_BF16_ROWS == 0
        in_specs.append(pl.BlockSpec((None, rb, w.shape[2]), lambda o, c, layer=layer: (layer, o * nct + c, 0)))
        args.append(w)
        out_specs.append(pl.BlockSpec((rb, w.shape[2]), lambda o, c: (o * nct + c, 0)))
        out_shape.append(jax.ShapeDtypeStruct(w.shape[1:], BF16))
    return pl.pallas_call(
        body,
        grid=(orders, nct),
        in_specs=in_specs,
        out_specs=out_specs,
        out_shape=out_shape,
        scratch_shapes=[pltpu.VMEM((seq, 2 * cw), F32),
                        pltpu.VMEM((2 * (half + 1) * nb, 2 * cw), F32)],
        compiler_params=_params(2),
        name="filter_spectrum",
    )(*args)


def _hyena_body(v_ref, g1_ref, g2_ref, h_ref, lhs1_ref, g_ref, lhs2_ref, o_ref, z_ref, e_ref, *, seq):
    nb = HYENA_NB
    na = 2 * seq // nb
    half = na // 2
    pair = V7X_BF16_ROWS
    chunk = 512 if seq % 512 == 0 else seq

    for c in range(seq // chunk):
        z_ref[pl.ds(c * chunk, chunk), :] = v_ref[pl.ds(c * chunk, chunk), :].astype(F32)
    _hyena_zero_unused(e_ref, seq)

    for order, gate_ref in enumerate((g1_ref, g2_ref)):
        _hyena_stage1(z_ref, e_ref, lhs1_ref, seq)

        group = 3 if (half + 1) % 3 == 0 else 1

        def spectral(it, carry):
            kas = [it * group + u for u in range(group)]
            eins = [_hyena_load_ka(e_ref, ka, seq).astype(BF16) for ka in kas]
            ys = []
            for ka, ein in zip(kas, eins):
                gk = g_ref[ka]
                x = _dot(gk, ein)
                hsp = h_ref[order, pl.ds(pl.multiple_of(ka * 2 * nb, 2 * nb), 2 * nb), :].astype(F32)
                xre, xim, hre, him = x[:nb], x[nb:], hsp[:nb], hsp[nb:]
                zsp = jnp.concatenate([xre * hre - xim * him, xre * him + xim * hre], axis=0).astype(BF16)
                ys.append(lax.dot_general(gk, zsp, (((0,), (0,)), ((), ())), preferred_element_type=F32))
            for ka, y in zip(kas, ys):
                e_ref[pl.ds(pl.multiple_of(ka * nb, nb), nb), :] = y[:nb]
                keep = jnp.where((ka == 0) | (ka == half), 0.0, 1.0)
                e_ref[pl.ds(pl.multiple_of((half + 1 + ka) * nb, nb), nb), :] = y[nb:] * keep
            return carry

        _static_loop((half + 1) // group, spectral)

        def inverse(i, carry):
            outs = []
            for q in range(pair // V7X_SUBLANES):
                off = i * pair + q * V7X_SUBLANES
                yg = _gather_tiles(e_ref, [_hyena_slot_block(s, na) * nb + off for s in range(na)])
                outs.append(_dot(lhs2_ref[...], yg.astype(BF16)))
            for a in range(half):
                sl = slice(a * V7X_SUBLANES, (a + 1) * V7X_SUBLANES)
                conv = jnp.concatenate([o_[sl] for o_ in outs], axis=0)
                r = pl.multiple_of(a * nb + i * pair, pair)
                res = gate_ref[pl.ds(r, pair), :].astype(F32) * conv
                if order == 0:
                    z_ref[pl.ds(r, pair), :] = res
                else:
                    o_ref[pl.ds(r, pair), :] = res.astype(BF16)
            return carry

        _static_loop(nb // pair, inverse)


def _hyena(hyc, hspec, lhs1, gmat, lhs2, *, seq, width, ct):
    n = hyc.shape[0]
    nct = width // ct
    nb = HYENA_NB
    half = seq // nb
    return pl.pallas_call(
        functools.partial(_hyena_body, seq=seq),
        grid=(nct, n // seq),
        in_specs=[
            pl.BlockSpec((seq, ct), lambda c, b: (b, c)),
            pl.BlockSpec((seq, ct), lambda c, b: (b, nct + c)),
            pl.BlockSpec((seq, ct), lambda c, b: (b, 2 * nct + c)),
            pl.BlockSpec((None,) + hspec.shape[1:], lambda c, b: (c, 0, 0, 0)),
            _resident(lhs1.shape),
            _resident(gmat.shape),
            _resident(lhs2.shape),
        ],
        out_specs=pl.BlockSpec((seq, ct), lambda c, b: (b, c)),
        out_shape=jax.ShapeDtypeStruct((n, width), BF16),
        scratch_shapes=[pltpu.VMEM((seq, ct), F32),
                        pltpu.VMEM((2 * (half + 1) * nb, ct), F32)],
        compiler_params=_params(2),
        name="hyena",
    )(hyc, hyc, hyc, hspec, lhs1, gmat, lhs2)


def _fnet3_body(p_ref, q_ref, ma_ref, mb_ref, mc_ref, o_ref, t_ref, *, seq, r):
    sub = V7X_SUBLANES
    nhi = r // sub
    row = lambda part, x0, x1, x2: part * seq + (x0 * r + x1) * r + x2

    def stage_a(i, carry):
        t1, t0_hi = divmod(i, nhi)
        starts = [r * r * t2 + r * t1 + sub * t0_hi for t2 in range(r)]
        xin = jnp.concatenate([_gather_tiles(p_ref, starts), _gather_tiles(q_ref, starts)],
                              axis=0).astype(BF16)
        out = _dot(ma_ref[...], xin)
        for lo in range(sub):
            for part in range(2):
                src = (lo * 2 + part) * r
                t_ref[pl.ds(row(part, t0_hi * sub + lo, t1, 0), r), :] = out[src:src + r]
        return carry

    _static_loop(r * nhi, stage_a)

    def stage_b(i, carry):
        t0, k2_hi = divmod(i, nhi)
        starts = [row(part, t0, t1, sub * k2_hi) for part in range(2) for t1 in range(r)]
        out = _dot(mb_ref[k2_hi], _gather_tiles(t_ref, starts).astype(BF16))
        for j, s in enumerate(starts):
            t_ref[pl.ds(s, sub), :] = out[j * sub:(j + 1) * sub]
        return carry

    _static_loop(r * nhi, stage_b)

    def stage_c(i, carry):
        k1, k2_hi = divmod(i, nhi)
        starts = [row(part, t0, k1, sub * k2_hi) for part in range(2) for t0 in range(r)]
        out = _dot(mc_ref[i], _gather_tiles(t_ref, starts).astype(BF16))
        for k0 in range(r):
            o_ref[pl.ds(r * r * k0 + r * k1 + sub * k2_hi, sub), :] = out[k0 * sub:(k0 + 1) * sub]
        return carry

    _static_loop(r * nhi, stage_c)


def _fnet3(pq, ma, mb, mc, *, seq, fw, ct, r):
    n = pq.shape[0]
    nct = fw // ct
    return pl.pallas_call(
        functools.partial(_fnet3_body, seq=seq, r=r),
        grid=(n // seq, nct),
        in_specs=[
            pl.BlockSpec((seq, ct), lambda b, c: (b, c)),
            pl.BlockSpec((seq, ct), lambda b, c: (b, nct + c)),
            _resident(ma.shape),
            _resident(mb.shape),
            _resident(mc.shape),
        ],
        out_specs=pl.BlockSpec((seq, ct), lambda b, c: (b, c)),
        out_shape=jax.ShapeDtypeStruct((n, fw), F32),
        scratch_shapes=[pltpu.VMEM((2 * seq, ct), F32)],
        compiler_params=_params(2),
        name="fnet_seq",
    )(pq, pq, ma, mb, mc)


def _merge_body(x_ref, z_ref, f_ref, po_ref, pp_ref, pn_ref, gt_ref, wha_ref, wfo_ref, wpl_ref,
                ps_ref, wo_ref, gpost_ref, o_ref, ext_ref, m_ref, *, tm, seq, d):
    i = pl.program_id(0)
    sub = V7X_SUBLANES
    has_prev = lax.rem(i * tm, seq) != 0
    has_next = lax.rem((i + 1) * tm, seq) != 0
    ext_ref[pl.ds(0, sub), :] = jnp.where(has_prev, pp_ref[...], 0.0)
    ext_ref[pl.ds(sub, tm), :] = po_ref[...]
    ext_ref[pl.ds(sub + tm, sub), :] = jnp.where(has_next, pn_ref[...], 0.0)

    gate = lambda q, lo, hi: gt_ref[:, q * d + lo:q * d + hi].astype(F32)
    m_ref[...] = gate(0, 0, d) * _dot(z_ref[...], wha_ref[...])
    m_ref[...] += gate(1, 0, d) * _dot(f_ref[...].astype(BF16), wfo_ref[...])

    pos = lax.rem(i * tm, seq) + lax.broadcasted_iota(jnp.int32, (tm, 1), 0)
    n_pool = len(POOL_WINDOWS)
    gw = po_ref.shape[1] // n_pool
    ow = d // n_pool
    for q, w in enumerate(POOL_WINDOWS):
        before = w // 2
        after = w - 1 - before
        cols = slice(q * gw, (q + 1) * gw)
        tot = ext_ref[pl.ds(sub - before, tm), cols]
        for s in range(-before + 1, after + 1):
            tot = tot + ext_ref[pl.ds(sub + s, tm), cols]
        lo = jnp.maximum(pos - before, 0)
        hi = jnp.minimum(pos + after, seq - 1)
        cnt = (hi - lo + 1).astype(F32)
        mq = (tot / cnt - po_ref[:, cols]).astype(BF16)
        yc = _dot(mq, wpl_ref[q]) * ps_ref[:, q * ow:(q + 1) * ow]
        m_ref[:, q * ow:(q + 1) * ow] += gate(2, q * ow, (q + 1) * ow) * yc

    y = _dot(m_ref[...].astype(BF16), wo_ref[...])
    o_ref[...] = x_ref[...] + _rms(y, gpost_ref[...])


def _merge(x2, z2, fy, po, gates, wha, wfo, wpool, pscale, w_out, g_post, layer, *, tm, seq):
    n, d = x2.shape
    pw = po.shape[1]
    sub = V7X_SUBLANES
    hb = tm // sub
    last_hb = n // sub - 1
    body = functools.partial(_merge_body, tm=tm, seq=seq, d=d)
    return pl.pallas_call(
        body,
        grid=(n // tm,),
        in_specs=[
            pl.BlockSpec((tm, d), lambda i: (i, 0)),
            pl.BlockSpec((tm, z2.shape[1]), lambda i: (i, 0)),
            pl.BlockSpec((tm, fy.shape[1]), lambda i: (i, 0)),
            pl.BlockSpec((tm, pw), lambda i: (i, 0)),
            pl.BlockSpec((sub, pw), lambda i: (jnp.maximum(i * hb - 1, 0), 0)),
            pl.BlockSpec((sub, pw), lambda i: (jnp.minimum((i + 1) * hb, last_hb), 0)),
            pl.BlockSpec((tm, gates.shape[1]), lambda i: (i, 0)),
            _resident_layer(wha, layer),
            _resident_layer(wfo, layer),
            _resident_layer(wpool, layer),
            _resident((1, d)),
            _resident(w_out.shape),
            _resident((1, d)),
        ],
        out_specs=pl.BlockSpec((tm, d), lambda i: (i, 0)),
        out_shape=jax.ShapeDtypeStruct((n, d), F32),
        scratch_shapes=[pltpu.VMEM((tm + 2 * sub, pw), F32), pltpu.VMEM((tm, d), F32)],
        compiler_params=_params(1),
        name="mixer_merge",
    )(x2, z2, fy, po, po, po, gates, wha, wfo, wpool, pscale.reshape(1, d), w_out, g_post.reshape(1, d))


def _kv_body(m_ref, g_ref, w_ref, o_ref, mn_ref):
    @pl.when((pl.program_id(0) == 0) | (pl.num_programs(1) > 1))
    def _():
        mn_ref[...] = _rms(m_ref[...], g_ref[...]).astype(BF16)

    o_ref[...] = _dot(mn_ref[...], w_ref[...].astype(BF16)).astype(BF16)


def _kv(mem2, g, w_kv, layer, *, n_mem, tn):
    n, d = mem2.shape
    cols = w_kv.shape[2]
    return pl.pallas_call(
        _kv_body,
        grid=(cols // tn, n // n_mem),
        in_specs=[
            pl.BlockSpec((n_mem, d), lambda j, b: (b, 0)),
            pl.BlockSpec((1, d), lambda j, b: (0, 0)),
            pl.BlockSpec((None, d, tn), lambda j, b: (layer, 0, j)),
        ],
        out_specs=pl.BlockSpec((n_mem, tn), lambda j, b: (b, j)),
        out_shape=jax.ShapeDtypeStruct((n, cols), BF16),
        scratch_shapes=[pltpu.VMEM((n_mem, d), BF16)],
        compiler_params=_params(2),
        name="mem_kv",
    )(mem2, g.reshape(1, d), w_kv)


def _attn_body(x_ref, gpre_ref, wq_ref, k_ref, v_ref, wo_ref, gpost_ref, o_ref, *, d):
    dh = d // N_HEADS
    x = x_ref[...]
    q = _dot(_rms(x, gpre_ref[...]).astype(BF16), wq_ref[...])
    heads = []
    for h in range(N_HEADS):
        cols = slice(h * dh, (h + 1) * dh)
        s = lax.dot_general(q[:, cols].astype(BF16), k_ref[:, cols], (((1,), (1,)), ((), ())),
                            preferred_element_type=F32) * (dh ** -0.5)
        e = jnp.exp(s - jnp.max(s, axis=-1, keepdims=True))
        p = e / jnp.sum(e, axis=-1, keepdims=True)
        heads.append(_dot(p.astype(BF16), v_ref[:, cols]))
    o = jnp.concatenate(heads, axis=1).astype(BF16)
    o_ref[...] = x + _rms(_dot(o, wo_ref[...]), gpost_ref[...])


def _attn(x2, g_pre, w_q, kv, w_o, g_post, *, tm, seq, n_mem):
    n, d = x2.shape
    tiles_per_seq = seq // tm
    return pl.pallas_call(
        functools.partial(_attn_body, d=d),
        grid=(n // tm,),
        in_specs=[
            pl.BlockSpec((tm, d), lambda i: (i, 0)),
            _resident((1, d)),
            _resident(w_q.shape),
            pl.BlockSpec((n_mem, d), lambda i: (i // tiles_per_seq, 0)),
            pl.BlockSpec((n_mem, d), lambda i: (i // tiles_per_seq, 1)),
            _resident(w_o.shape),
            _resident((1, d)),
        ],
        out_specs=pl.BlockSpec((tm, d), lambda i: (i, 0)),
        out_shape=jax.ShapeDtypeStruct((n, d), F32),
        compiler_params=_params(1),
        name="mem_attn",
    )(x2, g_pre.reshape(1, d), w_q, kv, kv, w_o, g_post.reshape(1, d))


def kernel(x, mem, g_ffn1_pre, w_ffn1_gu, w_ffn1_down, g_ffn1_post, g_mix_pre, w_in, hyena_conv_w,
           hyena_conv_b, filt_w1, filt_b1, filt_w2, filt_b2, filt_w3, filt_b3, filt_w4, filt_freq,
           hyena_d, w_hyena_out, w_fnet_out, w_pool, pool_scale, w_out, g_mix_post, g_mem_pre,
           g_mem_kv, w_q, w_kv, w_o, g_mem_post, g_ffn2_pre, w_ffn2_gu, w_ffn2_down, g_ffn2_post):
    batch, seq, d = x.shape
    n_mem = mem.shape[1]
    depth = w_in.shape[0]
    hy_w = w_hyena_out.shape[1]
    hy_cols = hyena_conv_w.shape[2]
    fw = w_fnet_out.shape[1]
    pw = w_pool.shape[1] * w_pool.shape[2]

    tm = 512
    tn = fw + pw
    tf = 512
    ct = V7X_MXU_COLS

    lhs1, gmat, lhs2 = _hyena_mats(seq)
    cs = _fnet_channel_mat(fw // FNET_GROUPS)
    fma, fmb, fmc, fr = _fnet3_mats(seq)

    bf = lambda w: w.astype(BF16)
    w_hyena_out, w_fnet_out, w_pool = map(bf, (w_hyena_out, w_fnet_out, w_pool))
    x2 = x.reshape(batch * seq, d)
    mem2 = mem.reshape(batch * n_mem, d)
    for l in range(depth):
        h3 = _filter_mlp(filt_w1[l], filt_b1[l], filt_w2[l], filt_b2[l], filt_w3[l], filt_b3[l],
                         filt_freq[l], seq=seq)
        hspec, *ffn_w = _filter_spectrum(h3, filt_w4[l], hyena_d[l], lhs1, gmat, seq=seq, width=hy_w,
                                         cw=ct // 2, ct=ct, cast=[(w_ffn1_gu, l), (w_ffn1_down, l)])

        later = (w_ffn2_gu, w_ffn2_down, w_in, w_out, w_q, w_o)
        x2, *cast = _ffn(x2, g_ffn1_pre[l], *ffn_w, g_ffn1_post[l], tm=2 * tm, tf=tf,
                         cast=[(w, l) for w in later])
        *ffn_w, w_in_l, w_out_l, w_q_l, w_o_l = cast

        hyc, pq, po, gates = _proj(x2, g_mix_pre[l], w_in_l, hyena_conv_w[l], hyena_conv_b[l], cs,
                                   tm=2 * tm, tn=tn, seq=seq, hy_cols=hy_cols, fw=fw, pw=pw)
        z2 = _hyena(hyc, hspec, lhs1, gmat, lhs2, seq=seq, width=hy_w, ct=ct)
        fy = _fnet3(pq, fma, fmb, fmc, seq=seq, fw=fw, ct=ct, r=fr)
        x2 = _merge(x2, z2, fy, po, gates, w_hyena_out, w_fnet_out, w_pool,
                    pool_scale[l], w_out_l, g_mix_post[l], l, tm=tm, seq=seq)

        kv = _kv(mem2, g_mem_kv[l], w_kv, l, n_mem=batch * n_mem, tn=tn)
        x2 = _attn(x2, g_mem_pre[l], w_q_l, kv, w_o_l, g_mem_post[l], tm=tm, seq=seq, n_mem=n_mem)

        x2 = _ffn_stream(x2, g_ffn2_pre[l], *ffn_w, g_ffn2_post[l], tm=2 * tm, tf=tf)
    return x2.reshape(batch, seq, d)
```

```python
import functools
import math

import jax
import jax.numpy as jnp
import numpy as np
from jax import lax
from jax.experimental import pallas as pl
from jax.experimental.pallas import tpu as pltpu

F32 = jnp.float32
BF16 = jnp.bfloat16

RMS_EPS = 1e-6
MACARON_WEIGHT = 0.5
N_HEADS = 4
POOL_WINDOWS = (2, 4, 8, 16)
FNET_GROUPS = 4
FILTER_BANDS = 16
DECAY_TARGET = 1e-2
FAST_DECAY_PCT = 0.3
SLOW_DECAY_PCT = 1.5

V7X_SUBLANES = 8
V7X_BF16_ROWS = 16
V7X_MXU_COLS = 256
V7X_VMEM_LIMIT = 60 * 2**20

HYENA_NB = 128
HALO = V7X_BF16_ROWS
ROW_CHUNK = 256
LANE_BLOCK = 128


def _params(n_axes):
    return pltpu.CompilerParams(
        dimension_semantics=("arbitrary",) * n_axes,
        vmem_limit_bytes=V7X_VMEM_LIMIT,
    )


def _resident(shape):
    zeros = (0,) * len(shape)
    return pl.BlockSpec(shape, lambda *_: zeros, pipeline_mode=pl.Buffered(1))


def _resident_layer(stacked, layer):
    tail = (0,) * (stacked.ndim - 1)
    return pl.BlockSpec((None,) + stacked.shape[1:], lambda *_: (layer,) + tail,
                        pipeline_mode=pl.Buffered(1))


def _rms(x, g):
    ms = jnp.mean(x * x, axis=-1, keepdims=True)
    return x * lax.rsqrt(ms + RMS_EPS) * g


def _sigmoid(x):
    return 0.5 * jnp.tanh(0.5 * x) + 0.5


def _dot(a, b):
    return jnp.dot(a, b, preferred_element_type=F32)


def _dot3(a, b):
    ah = a.astype(BF16)
    al = (a - ah.astype(F32)).astype(BF16)
    bh = b.astype(BF16)
    bl = (b - bh.astype(F32)).astype(BF16)
    return _dot(ah, bh) + (_dot(al, bh) + _dot(ah, bl))


def _row_chunks(rows, fn):
    chunk = ROW_CHUNK if rows % ROW_CHUNK == 0 else rows

    def body(c, carry):
        fn(pl.ds(pl.multiple_of(c * chunk, chunk), chunk))
        return carry

    lax.fori_loop(0, rows // chunk, body, 0)


def _norm_residual_inplace(o_ref, x_ref, g):
    rows, d = o_ref.shape

    def post(r):
        sq = None
        for c in range(0, d, LANE_BLOCK):
            yb = o_ref[r, c:c + LANE_BLOCK]
            sq = yb * yb if sq is None else sq + yb * yb
        inv = lax.rsqrt(jnp.sum(sq, axis=-1, keepdims=True) * (1.0 / d) + RMS_EPS)
        for c in range(0, d, LANE_BLOCK):
            cols = slice(c, c + LANE_BLOCK)
            o_ref[r, cols] = x_ref[r, cols] + o_ref[r, cols] * inv * g[:, cols]

    _row_chunks(rows, post)


def _ffn_body(x_ref, gpre_ref, wg_ref, wu_ref, wd_ref, gpost_ref, *rest, n_cast):
    cast_in, o_ref, cast_out, u_ref = rest[:n_cast], rest[n_cast], rest[n_cast + 1:-1], rest[-1]
    k = pl.program_id(1)
    rows = x_ref.shape[0]

    @pl.when(k == 0)
    def _():
        def pre(r):
            u_ref[r, :] = _rms(x_ref[r, :], gpre_ref[...]).astype(BF16)
            o_ref[r, :] = jnp.zeros((r.size, o_ref.shape[1]), F32)
        _row_chunks(rows, pre)

    u = u_ref[...]
    a = _dot(u, wg_ref[...])
    b = _dot(u, wu_ref[...])
    h = (a * _sigmoid(a) * b).astype(BF16)
    for src, dst in zip(cast_in, cast_out):
        dst[...] = src[...].astype(BF16)
    o_ref[...] += _dot(h, wd_ref[...])

    @pl.when(k == pl.num_programs(1) - 1)
    def _():
        _norm_residual_inplace(o_ref, x_ref, MACARON_WEIGHT * gpost_ref[...])


def _cast_tiling(rows, cols, ni, nk):
    rb = rows // ni
    assert rows % ni == 0 and rb % V7X_BF16_ROWS == 0
    nc = max(c for c in range(1, nk + 1) if cols % c == 0 and (cols // c) % 128 == 0)
    return rb, cols // nc, nc


def _ffn(x2, g_pre, w_gu, w_down, g_post, *, tm, tf, cast=()):
    n, d = x2.shape
    f = w_down.shape[0]
    ni, nk = n // tm, f // tf
    in_specs = [
        pl.BlockSpec((tm, d), lambda i, k: (i, 0)),
        pl.BlockSpec((1, d), lambda i, k: (0, 0)),
        pl.BlockSpec((d, tf), lambda i, k: (0, k)),
        pl.BlockSpec((d, tf), lambda i, k: (0, k + nk)),
        pl.BlockSpec((tf, d), lambda i, k: (k, 0)),
        pl.BlockSpec((1, d), lambda i, k: (0, 0)),
    ]
    args = [x2, g_pre.reshape(1, d), w_gu, w_gu, w_down, g_post.reshape(1, d)]
    out_specs = [pl.BlockSpec((tm, d), lambda i, k: (i, 0))]
    out_shape = [jax.ShapeDtypeStruct((n, d), F32)]
    for w, layer in cast:
        rb, cb, nc = _cast_tiling(w.shape[1], w.shape[2], ni, nk)
        in_specs.append(pl.BlockSpec((None, rb, cb),
                                     lambda i, k, layer=layer, nc=nc: (layer, i, jnp.minimum(k, nc - 1))))
        args.append(w)
        out_specs.append(pl.BlockSpec((rb, cb), lambda i, k, nc=nc: (i, jnp.minimum(k, nc - 1))))
        out_shape.append(jax.ShapeDtypeStruct(w.shape[1:], BF16))
    return pl.pallas_call(
        functools.partial(_ffn_body, n_cast=len(cast)),
        grid=(ni, nk),
        in_specs=in_specs,
        out_specs=out_specs,
        out_shape=out_shape,
        scratch_shapes=[pltpu.VMEM((tm, d), BF16)],
        compiler_params=_params(2),
        name="ffn",
    )(*args)


def _ffn_stream_body(x_ref, gpre_ref, gpost_ref, wgu_hbm, wd_hbm, o_ref, u_ref, wg_buf, wu_buf, wd_buf, sem,
                     *, nk, tf, f):
    i = pl.program_id(0)
    last_tile = pl.num_programs(0) - 1
    rows = x_ref.shape[0]

    def copies(k, slot):
        c0 = pl.multiple_of(k * tf, tf)
        return (pltpu.make_async_copy(wgu_hbm.at[:, pl.ds(c0, tf)], wg_buf.at[slot], sem.at[slot, 0]),
                pltpu.make_async_copy(wgu_hbm.at[:, pl.ds(pl.multiple_of(f + c0, tf), tf)], wu_buf.at[slot],
                                      sem.at[slot, 1]),
                pltpu.make_async_copy(wd_hbm.at[pl.ds(c0, tf), :], wd_buf.at[slot], sem.at[slot, 2]))

    base = lax.rem(i * nk, 2)

    @pl.when(i == 0)
    def _():
        for c in copies(0, base):
            c.start()

    def pre(r):
        u_ref[r, :] = _rms(x_ref[r, :], gpre_ref[...]).astype(BF16)
    _row_chunks(rows, pre)

    def chunk(k, carry, first=False):
        slot = lax.rem(base + k, 2)
        @pl.when((k < nk - 1) | (i < last_tile))
        def _():
            for c in copies(lax.rem(k + 1, nk), 1 - slot):
                c.start()
        for c in copies(k, slot):
            c.wait()
        u = u_ref[...]
        a = _dot(u, wg_buf[slot])
        b = _dot(u, wu_buf[slot])
        h = (a * _sigmoid(a) * b).astype(BF16)
        if first:
            o_ref[...] = _dot(h, wd_buf[slot])
        else:
            o_ref[...] += _dot(h, wd_buf[slot])
        return carry

    lax.fori_loop(0, 1, functools.partial(chunk, first=True), 0)
    lax.fori_loop(1, nk, chunk, 0)
    _norm_residual_inplace(o_ref, x_ref, MACARON_WEIGHT * gpost_ref[...])


def _ffn_stream(x2, g_pre, w_gu, w_down, g_post, *, tm, tf):
    n, d = x2.shape
    f = w_down.shape[0]
    nk = f // tf
    return pl.pallas_call(
        functools.partial(_ffn_stream_body, nk=nk, tf=tf, f=f),
        grid=(n // tm,),
        in_specs=[
            pl.BlockSpec((tm, d), lambda i: (i, 0)),
            pl.BlockSpec((1, d), lambda i: (0, 0)),
            pl.BlockSpec((1, d), lambda i: (0, 0)),
            pl.BlockSpec(memory_space=pl.ANY),
            pl.BlockSpec(memory_space=pl.ANY),
        ],
        out_specs=pl.BlockSpec((tm, d), lambda i: (i, 0)),
        out_shape=jax.ShapeDtypeStruct((n, d), F32),
        scratch_shapes=[pltpu.VMEM((tm, d), BF16),
                        pltpu.VMEM((2, d, tf), BF16), pltpu.VMEM((2, d, tf), BF16), pltpu.VMEM((2, tf, d), BF16),
                        pltpu.SemaphoreType.DMA((2, 3))],
        compiler_params=_params(1),
        name="ffn_stream",
    )(x2, g_pre.reshape(1, d), g_post.reshape(1, d), w_gu, w_down)


def _proj_body(x_ref, xp_ref, xn_ref, g_ref, w_ref, cw_ref, cb_ref, cs_ref,
               hy_ref, pq_ref, po_ref, gt_ref, u_ref, *, tm, seq, n_hy, fw):
    i = pl.program_id(0)
    j = pl.program_id(1)

    @pl.when(j == 0)
    def _():
        g = g_ref[...]
        has_prev = lax.rem(i * tm, seq) != 0
        has_next = lax.rem((i + 1) * tm, seq) != 0
        u_ref[pl.ds(0, HALO), :] = jnp.where(has_prev, _rms(xp_ref[...], g), 0.0).astype(BF16)
        chunk = ROW_CHUNK if tm % ROW_CHUNK == 0 else tm

        def pre(c, carry):
            r = pl.multiple_of(c * chunk, chunk)
            u_ref[pl.ds(pl.multiple_of(HALO + r, HALO), chunk), :] = (
                _rms(x_ref[pl.ds(r, chunk), :], g).astype(BF16))
            return carry

        lax.fori_loop(0, tm // chunk, pre, 0)
        u_ref[pl.ds(HALO + tm, HALO), :] = jnp.where(has_next, _rms(xn_ref[...], g), 0.0).astype(BF16)

    @pl.when(j < n_hy)
    def _():
        h = _dot(u_ref[...], w_ref[...])
        rows = h.shape[0]
        prev = pltpu.roll(h, 1, axis=0)[HALO:HALO + tm]
        nxt = pltpu.roll(h, rows - 1, axis=0)[HALO:HALO + tm]
        y = cb_ref[...] + prev * cw_ref[0:1, :]
        y = y + h[HALO:HALO + tm] * cw_ref[1:2, :]
        y = y + nxt * cw_ref[2:3, :]
        hy_ref[...] = y.astype(BF16)

    @pl.when(j == n_hy)
    def _():
        c = _dot(u_ref[pl.ds(HALO, tm), :], w_ref[...])
        po_ref[...] = c[:, fw:]
        fb = c[:, :fw].astype(BF16)
        gw = fw // FNET_GROUPS
        res = [_dot(fb[:, q * gw:(q + 1) * gw], cs_ref[...]) for q in range(FNET_GROUPS)]
        pq_ref[...] = jnp.concatenate([r[:, :gw] for r in res] + [r[:, gw:] for r in res], axis=1)

    @pl.when(j > n_hy)
    def _():
        c = _dot(u_ref[pl.ds(HALO, tm), :], w_ref[...])
        gt_ref[...] = _sigmoid(c).astype(BF16)


def _proj(x2, g, w_in, conv_w, conv_b, cs, *, tm, tn, seq, hy_cols, fw, pw):
    n, d = x2.shape
    cols = w_in.shape[1]
    n_hy = hy_cols // tn
    assert hy_cols % tn == 0 and fw + pw == tn and (cols - hy_cols - tn) % tn == 0
    nj = cols // tn
    n_gate = cols - hy_cols - tn
    hb = tm // HALO
    last_hb = n // HALO - 1
    body = functools.partial(_proj_body, tm=tm, seq=seq, n_hy=n_hy, fw=fw)
    return pl.pallas_call(
        body,
        grid=(n // tm, nj),
        in_specs=[
            pl.BlockSpec((tm, d), lambda i, j: (i, 0)),
            pl.BlockSpec((HALO, d), lambda i, j: (jnp.maximum(i * hb - 1, 0), 0)),
            pl.BlockSpec((HALO, d), lambda i, j: (jnp.minimum((i + 1) * hb, last_hb), 0)),
            pl.BlockSpec((1, d), lambda i, j: (0, 0)),
            pl.BlockSpec((d, tn), lambda i, j: (0, j)),
            pl.BlockSpec((3, tn), lambda i, j: (0, jnp.minimum(j, n_hy - 1))),
            pl.BlockSpec((1, tn), lambda i, j: (0, jnp.minimum(j, n_hy - 1))),
            _resident(cs.shape),
        ],
        out_specs=[
            pl.BlockSpec((tm, tn), lambda i, j: (i, jnp.minimum(j, n_hy - 1))),
            pl.BlockSpec((tm, 2 * fw), lambda i, j: (i, 0)),
            pl.BlockSpec((tm, pw), lambda i, j: (i, 0)),
            pl.BlockSpec((tm, tn), lambda i, j: (i, jnp.maximum(j - n_hy - 1, 0))),
        ],
        out_shape=[
            jax.ShapeDtypeStruct((n, hy_cols), BF16),
            jax.ShapeDtypeStruct((n, 2 * fw), F32),
            jax.ShapeDtypeStruct((n, pw), F32),
            jax.ShapeDtypeStruct((n, n_gate), BF16),
        ],
        scratch_shapes=[pltpu.VMEM((tm + 2 * HALO, d), BF16)],
        compiler_params=_params(2),
        name="mixer_proj",
    )(x2, x2, x2, g.reshape(1, d), w_in, conv_w, conv_b.reshape(1, hy_cols), cs)


def _hyena_slot_block(s, na):
    half = na // 2
    return s if s <= half else (half + 1) + (s - half)


def _hyena_mats(seq):
    nb = HYENA_NB
    na = 2 * seq // nb
    half = na // 2
    eye = np.eye(V7X_SUBLANES)
    a = np.arange(half)[None, :]
    f1 = np.zeros((na, half))
    k_re = np.arange(half + 1)[:, None]
    f1[: half + 1] = np.cos(2 * np.pi * k_re * a / na)
    k_im = np.arange(1, half)[:, None]
    f1[half + 1:] = -np.sin(2 * np.pi * k_im * a / na)
    lhs1 = np.kron(f1, eye)
    b = np.arange(nb)[None, :]
    kb = np.arange(nb)[:, None]
    g = np.zeros((half + 1, 2 * nb, 2 * nb))
    for ka in range(half + 1):
        ang = 2 * np.pi * (kb * b / nb + b * ka / (2 * seq))
        gre, gim = np.cos(ang), -np.sin(ang)
        g[ka] = np.block([[gre, -gim], [gim, gre]])
    a_col = np.arange(half)[:, None]
    f2 = np.zeros((half, na))
    wgt = np.full(half + 1, 2.0)
    wgt[0] = wgt[half] = 1.0
    f2[:, : half + 1] = wgt[None, :] * np.cos(2 * np.pi * a_col * np.arange(half + 1)[None, :] / na)
    f2[:, half + 1:] = -2.0 * np.sin(2 * np.pi * a_col * np.arange(1, half)[None, :] / na)
    lhs2 = np.kron(f2 / (2 * seq), eye)
    return (jnp.asarray(lhs1, BF16), jnp.asarray(g, BF16), jnp.asarray(lhs2, BF16))


def _fnet_channel_mat(group):
    cc = np.arange(group)[:, None]
    mm = np.arange(group)[None, :]
    angc = 2 * np.pi * cc * mm / group
    return jnp.asarray(np.concatenate([np.cos(angc), -np.sin(angc)], axis=1) / math.sqrt(group), BF16)


def _fnet3_mats(seq):
    r = int(round(seq ** (1.0 / 3.0)))
    sub = V7X_SUBLANES
    assert r ** 3 == seq and r % sub == 0
    idx = np.arange(r)
    ang = 2 * np.pi * idx[:, None] * idx[None, :] / r
    blk = np.array([[np.cos(ang), np.sin(ang)], [-np.sin(ang), np.cos(ang)]])
    ma = np.zeros((sub, 2, r, 2, r, sub))
    for lo in range(sub):
        ma[lo, :, :, :, :, lo] = blk.transpose(0, 2, 1, 3)
    ma = ma.reshape(2 * r * sub, 2 * r * sub)
    mb = np.zeros((r // sub, 2, r, sub, 2, r, sub))
    for hi in range(r // sub):
        for lo in range(sub):
            k2 = hi * sub + lo
            ph = 2 * np.pi * (idx[:, None] * idx[None, :] / r + idx[None, :] * k2 / r ** 2)
            c, s = np.cos(ph), np.sin(ph)
            mb[hi, 0, :, lo, 0, :, lo] = c
            mb[hi, 0, :, lo, 1, :, lo] = s
            mb[hi, 1, :, lo, 0, :, lo] = -s
            mb[hi, 1, :, lo, 1, :, lo] = c
    mb = mb.reshape(r // sub, 2 * r * sub, 2 * r * sub)
    mc = np.zeros((r, r // sub, r, sub, 2, r, sub))
    for k1 in range(r):
        for hi in range(r // sub):
            for lo in range(sub):
                k2 = hi * sub + lo
                ph = 2 * np.pi * (idx[:, None] * idx[None, :] / r
                                  + idx[None, :] * k1 / r ** 2 + idx[None, :] * k2 / seq)
                mc[k1, hi, :, lo, 0, :, lo] = np.cos(ph)
                mc[k1, hi, :, lo, 1, :, lo] = np.sin(ph)
    mc = mc.reshape(r * (r // sub), r * sub, 2 * r * sub) / math.sqrt(seq)
    return jnp.asarray(ma, BF16), jnp.asarray(mb, BF16), jnp.asarray(mc, BF16), r


def _static_loop(n, body):
    for i in range(n):
        body(i, 0)


def _gather_tiles(ref, starts):
    tiles = [ref[pl.ds(pl.multiple_of(s, V7X_SUBLANES), V7X_SUBLANES), :] for s in starts]
    return jnp.concatenate(tiles, axis=0)


def _hyena_stage1(z_ref, e_ref, lhs1_ref, seq):
    nb = HYENA_NB
    na = 2 * seq // nb
    half = na // 2

    def body(i, carry):
        off = i * V7X_SUBLANES
        xg = _gather_tiles(z_ref, [nb * a + off for a in range(half)]).astype(BF16)
        out = _dot(lhs1_ref[...], xg)
        for s in range(na):
            row = _hyena_slot_block(s, na) * nb + off
            e_ref[pl.ds(pl.multiple_of(row, V7X_SUBLANES), V7X_SUBLANES), :] = (
                out[s * V7X_SUBLANES:(s + 1) * V7X_SUBLANES])
        return carry

    _static_loop(nb // V7X_SUBLANES, body)


def _hyena_zero_unused(e_ref, seq):
    nb = HYENA_NB
    half = seq // nb
    zero = jnp.zeros((nb, e_ref.shape[1]), F32)
    e_ref[pl.ds((half + 1) * nb, nb), :] = zero
    e_ref[pl.ds((2 * half + 1) * nb, nb), :] = zero


def _hyena_load_ka(e_ref, ka, seq):
    nb = HYENA_NB
    half = seq // nb
    re = e_ref[pl.ds(pl.multiple_of(ka * nb, nb), nb), :]
    im = e_ref[pl.ds(pl.multiple_of((half + 1 + ka) * nb, nb), nb), :]
    return jnp.concatenate([re, im], axis=0)


def _filter_mlp_body(bands_ref, w1t_ref, w1c_ref, w1s_ref, b1_ref, w2_ref, b2_ref, w3_ref, b3_ref,
                     fr_ref, o_ref, *, seq):
    idx = lax.broadcasted_iota(jnp.int32, (1, 2 * seq), 1)
    p = jnp.where(idx < seq, idx, 2 * seq - idx).astype(F32)
    t = p / (seq - 1.0)
    ang = bands_ref[...] * ((2.0 * math.pi / seq) * p)
    fr = fr_ref[...]
    pre = (w1t_ref[...] * t
           + _dot3(w1c_ref[...], jnp.cos(ang))
           + _dot3(w1s_ref[...], -jnp.sin(ang))
           + b1_ref[...])
    h = jnp.sin(fr * pre)
    h = jnp.sin(fr * (_dot3(w2_ref[...], h) + b2_ref[...]))
    h = jnp.sin(fr * (_dot3(w3_ref[...], h) + b3_ref[...]))
    o_ref[...] = jnp.concatenate([h[:, :seq], h[:, seq:]], axis=0).T


def _filter_mlp(fw1, fb1, fw2, fb2, fw3, fb3, freq, *, seq):
    hid = fw2.shape[0]
    bands = np.linspace(1e-4, FILTER_BANDS - 1, FILTER_BANDS).astype(np.float32).reshape(-1, 1)
    col = lambda v: v.reshape(hid, 1)
    args = (jnp.asarray(bands), fw1[0].reshape(hid, 1), fw1[1:1 + FILTER_BANDS].T,
            fw1[1 + FILTER_BANDS:].T, col(fb1), fw2.T, col(fb2), fw3.T, col(fb3), col(freq))
    return pl.pallas_call(
        functools.partial(_filter_mlp_body, seq=seq),
        grid=(1,),
        in_specs=[_resident(a.shape) for a in args],
        out_specs=pl.BlockSpec((seq, 2 * hid), lambda i: (0, 0)),
        out_shape=jax.ShapeDtypeStruct((seq, 2 * hid), F32),
        compiler_params=_params(1),
        name="filter_mlp",
    )(*args)


def _filter_spec_body(h3_ref, w4f_ref, w4b_ref, dl_ref, d_ref, lhs1_ref, g_ref, *rest, seq, cw, n_cast):
    cast_in, o_ref, cast_out = rest[:n_cast], rest[n_cast], rest[n_cast + 1:2 * n_cast + 1]
    z_ref, e_ref = rest[-2:]
    for src, dst in zip(cast_in, cast_out):
        dst[...] = src[...].astype(BF16)
    nb = HYENA_NB
    half = seq // nb
    dl = dl_ref[...]
    zero = jnp.zeros_like(w4f_ref[...])
    w4 = jnp.concatenate([jnp.concatenate([w4f_ref[...], zero], axis=1),
                          jnp.concatenate([zero, w4b_ref[...]], axis=1)], axis=0)
    lane = lax.broadcasted_iota(jnp.int32, (1, 2 * cw), 1)
    chunk = ROW_CHUNK if seq % ROW_CHUNK == 0 else seq
    for c in range(seq // chunk):
        row = c * chunk + lax.broadcasted_iota(jnp.int32, (chunk, 1), 0)
        rowf = row.astype(F32)
        h4 = _dot3(h3_ref[pl.ds(c * chunk, chunk), :], w4)
        dec = jnp.concatenate([jnp.exp(-(rowf / (seq - 1.0)) * dl),
                               jnp.exp(-((seq - rowf) / (seq - 1.0)) * dl)], axis=1)
        z_ref[pl.ds(c * chunk, chunk), :] = jnp.where((row == 0) & (lane >= cw), 0.0, h4 * dec)

    _hyena_zero_unused(e_ref, seq)
    _hyena_stage1(z_ref, e_ref, lhs1_ref, seq)
    dsk = d_ref[0]

    def body(ka, carry):
        ein = _hyena_load_ka(e_ref, ka, seq).astype(BF16)
        x = _dot(g_ref[ka], ein)
        sgn = 1.0 - 2.0 * (ka % 2)
        hsp = x[:, 0:cw] + sgn * x[:, cw:2 * cw]
        part = lax.broadcasted_iota(jnp.int32, (2 * nb, 1), 0)
        hsp = hsp + jnp.where(part < nb, dsk, 0.0)
        o_ref[0, pl.ds(pl.multiple_of(ka * 2 * nb, 2 * nb), 2 * nb), :] = hsp.astype(BF16)
        return carry

    _static_loop(half + 1, body)


def _filter_spectrum(h3, fw4, d_skip, lhs1, gmat, *, seq, width, cw, ct, cast=()):
    hid = h3.shape[1] // 2
    orders = d_skip.shape[0]
    nct = width // cw
    per = ct // cw
    nb = HYENA_NB
    half = seq // nb
    max_decay = math.log(DECAY_TARGET) / FAST_DECAY_PCT
    min_decay = math.log(DECAY_TARGET) / SLOW_DECAY_PCT
    deltas = np.abs(np.linspace(min_decay, max_decay, width)).astype(np.float32).reshape(1, width)
    rows = (half + 1) * 2 * nb
    body = functools.partial(_filter_spec_body, seq=seq, cw=cw, n_cast=len(cast))
    in_specs = [
        _resident(h3.shape),
        pl.BlockSpec((hid, cw), lambda o, c: (0, o * 2 * nct + c)),
        pl.BlockSpec((hid, cw), lambda o, c: (0, o * 2 * nct + nct + c)),
        pl.BlockSpec((1, cw), lambda o, c: (0, c)),
        pl.BlockSpec((1, 1, cw), lambda o, c: (o, 0, c)),
        _resident(lhs1.shape),
        _resident(gmat.shape),
    ]
    args = [h3, fw4, fw4, jnp.asarray(deltas), d_skip.reshape(orders, 1, width), lhs1, gmat]
    out_specs = [pl.BlockSpec((None, 1, rows, cw), lambda o, c: (c // per, o, 0, c % per))]
    out_shape = [jax.ShapeDtypeStruct((width // ct, orders, rows, ct), BF16)]
    steps = orders * nct
    for w, layer in cast:
        rb = w.shape[1] // steps
        assert w.shape[1] % steps == 0 and rb % V7X_BF16_ROWS == 0
        in_specs.append(pl.BlockSpec((None, rb, w.shape[2]), lambda o, c, layer=layer: (layer, o * nct + c, 0)))
        args.append(w)
        out_specs.append(pl.BlockSpec((rb, w.shape[2]), lambda o, c: (o * nct + c, 0)))
        out_shape.append(jax.ShapeDtypeStruct(w.shape[1:], BF16))
    return pl.pallas_call(
        body,
        grid=(orders, nct),
        in_specs=in_specs,
        out_specs=out_specs,
        out_shape=out_shape,
        scratch_shapes=[pltpu.VMEM((seq, 2 * cw), F32),
                        pltpu.VMEM((2 * (half + 1) * nb, 2 * cw), F32)],
        compiler_params=_params(2),
        name="filter_spectrum",
    )(*args)


def _hyena_body(v_ref, g1_ref, g2_ref, h_ref, lhs1_ref, g_ref, lhs2_ref, o_ref, z_ref, e_ref, *, seq):
    nb = HYENA_NB
    na = 2 * seq // nb
    half = na // 2
    pair = V7X_BF16_ROWS
    chunk = 512 if seq % 512 == 0 else seq

    for c in range(seq // chunk):
        z_ref[pl.ds(c * chunk, chunk), :] = v_ref[pl.ds(c * chunk, chunk), :].astype(F32)
    _hyena_zero_unused(e_ref, seq)

    for order, gate_ref in enumerate((g1_ref, g2_ref)):
        _hyena_stage1(z_ref, e_ref, lhs1_ref, seq)

        group = 3 if (half + 1) % 3 == 0 else 1

        def spectral(it, carry):
            kas = [it * group + u for u in range(group)]
            eins = [_hyena_load_ka(e_ref, ka, seq).astype(BF16) for ka in kas]
            ys = []
            for ka, ein in zip(kas, eins):
                gk = g_ref[ka]
                x = _dot(gk, ein)
                hsp = h_ref[order, pl.ds(pl.multiple_of(ka * 2 * nb, 2 * nb), 2 * nb), :].astype(F32)
                xre, xim, hre, him = x[:nb], x[nb:], hsp[:nb], hsp[nb:]
                zsp = jnp.concatenate([xre * hre - xim * him, xre * him + xim * hre], axis=0).astype(BF16)
                ys.append(lax.dot_general(gk, zsp, (((0,), (0,)), ((), ())), preferred_element_type=F32))
            for ka, y in zip(kas, ys):
                e_ref[pl.ds(pl.multiple_of(ka * nb, nb), nb), :] = y[:nb]
                keep = jnp.where((ka == 0) | (ka == half), 0.0, 1.0)
                e_ref[pl.ds(pl.multiple_of((half + 1 + ka) * nb, nb), nb), :] = y[nb:] * keep
            return carry

        _static_loop((half + 1) // group, spectral)

        def inverse(i, carry):
            outs = []
            for q in range(pair // V7X_SUBLANES):
                off = i * pair + q * V7X_SUBLANES
                yg = _gather_tiles(e_ref, [_hyena_slot_block(s, na) * nb + off for s in range(na)])
                outs.append(_dot(lhs2_ref[...], yg.astype(BF16)))
            for a in range(half):
                sl = slice(a * V7X_SUBLANES, (a + 1) * V7X_SUBLANES)
                conv = jnp.concatenate([o_[sl] for o_ in outs], axis=0)
                r = pl.multiple_of(a * nb + i * pair, pair)
                res = gate_ref[pl.ds(r, pair), :].astype(F32) * conv
                if order == 0:
                    z_ref[pl.ds(r, pair), :] = res
                else:
                    o_ref[pl.ds(r, pair), :] = res.astype(BF16)
            return carry

        _static_loop(nb // pair, inverse)


def _hyena(hyc, hspec, lhs1, gmat, lhs2, *, seq, width, ct):
    n = hyc.shape[0]
    nct = width // ct
    nb = HYENA_NB
    half = seq // nb
    return pl.pallas_call(
        functools.partial(_hyena_body, seq=seq),
        grid=(nct, n // seq),
        in_specs=[
            pl.BlockSpec((seq, ct), lambda c, b: (b, c)),
            pl.BlockSpec((seq, ct), lambda c, b: (b, nct + c)),
            pl.BlockSpec((seq, ct), lambda c, b: (b, 2 * nct + c)),
            pl.BlockSpec((None,) + hspec.shape[1:], lambda c, b: (c, 0, 0, 0)),
            _resident(lhs1.shape),
            _resident(gmat.shape),
            _resident(lhs2.shape),
        ],
        out_specs=pl.BlockSpec((seq, ct), lambda c, b: (b, c)),
        out_shape=jax.ShapeDtypeStruct((n, width), BF16),
        scratch_shapes=[pltpu.VMEM((seq, ct), F32),
                        pltpu.VMEM((2 * (half + 1) * nb, ct), F32)],
        compiler_params=_params(2),
        name="hyena",
    )(hyc, hyc, hyc, hspec, lhs1, gmat, lhs2)


def _fnet3_body(p_ref, q_ref, ma_ref, mb_ref, mc_ref, o_ref, t_ref, *, seq, r):
    sub = V7X_SUBLANES
    nhi = r // sub
    row = lambda part, x0, x1, x2: part * seq + (x0 * r + x1) * r + x2

    def stage_a(i, carry):
        t1, t0_hi = divmod(i, nhi)
        starts = [r * r * t2 + r * t1 + sub * t0_hi for t2 in range(r)]
        xin = jnp.concatenate([_gather_tiles(p_ref, starts), _gather_tiles(q_ref, starts)],
                              axis=0).astype(BF16)
        out = _dot(ma_ref[...], xin)
        for lo in range(sub):
            for part in range(2):
                src = (lo * 2 + part) * r
                t_ref[pl.ds(row(part, t0_hi * sub + lo, t1, 0), r), :] = out[src:src + r]
        return carry

    _static_loop(r * nhi, stage_a)

    def stage_b(i, carry):
        t0, k2_hi = divmod(i, nhi)
        starts = [row(part, t0, t1, sub * k2_hi) for part in range(2) for t1 in range(r)]
        out = _dot(mb_ref[k2_hi], _gather_tiles(t_ref, starts).astype(BF16))
        for j, s in enumerate(starts):
            t_ref[pl.ds(s, sub), :] = out[j * sub:(j + 1) * sub]
        return carry

    _static_loop(r * nhi, stage_b)

    def stage_c(i, carry):
        k1, k2_hi = divmod(i, nhi)
        starts = [row(part, t0, k1, sub * k2_hi) for part in range(2) for t0 in range(r)]
        out = _dot(mc_ref[i], _gather_tiles(t_ref, starts).astype(BF16))
        for k0 in range(r):
            o_ref[pl.ds(r * r * k0 + r * k1 + sub * k2_hi, sub), :] = out[k0 * sub:(k0 + 1) * sub]
        return carry

    _static_loop(r * nhi, stage_c)


def _fnet3(pq, ma, mb, mc, *, seq, fw, ct, r):
    n = pq.shape[0]
    nct = fw // ct
    return pl.pallas_call(
        functools.partial(_fnet3_body, seq=seq, r=r),
        grid=(n // seq, nct),
        in_specs=[
            pl.BlockSpec((seq, ct), lambda b, c: (b, c)),
            pl.BlockSpec((seq, ct), lambda b, c: (b, nct + c)),
            _resident(ma.shape),
            _resident(mb.shape),
            _resident(mc.shape),
        ],
        out_specs=pl.BlockSpec((seq, ct), lambda b, c: (b, c)),
        out_shape=jax.ShapeDtypeStruct((n, fw), F32),
        scratch_shapes=[pltpu.VMEM((2 * seq, ct), F32)],
        compiler_params=_params(2),
        name="fnet_seq",
    )(pq, pq, ma, mb, mc)


def _merge_body(x_ref, z_ref, f_ref, po_ref, pp_ref, pn_ref, gt_ref, wha_ref, wfo_ref, wpl_ref,
                ps_ref, wo_ref, gpost_ref, o_ref, ext_ref, m_ref, *, tm, seq, d):
    i = pl.program_id(0)
    sub = V7X_SUBLANES
    has_prev = lax.rem(i * tm, seq) != 0
    has_next = lax.rem((i + 1) * tm, seq) != 0
    ext_ref[pl.ds(0, sub), :] = jnp.where(has_prev, pp_ref[...], 0.0)
    ext_ref[pl.ds(sub, tm), :] = po_ref[...]
    ext_ref[pl.ds(sub + tm, sub), :] = jnp.where(has_next, pn_ref[...], 0.0)

    gate = lambda q, lo, hi: gt_ref[:, q * d + lo:q * d + hi].astype(F32)
    m_ref[...] = gate(0, 0, d) * _dot(z_ref[...], wha_ref[...])
    m_ref[...] += gate(1, 0, d) * _dot(f_ref[...].astype(BF16), wfo_ref[...])

    pos = lax.rem(i * tm, seq) + lax.broadcasted_iota(jnp.int32, (tm, 1), 0)
    n_pool = len(POOL_WINDOWS)
    gw = po_ref.shape[1] // n_pool
    ow = d // n_pool
    for q, w in enumerate(POOL_WINDOWS):
        before = w // 2
        after = w - 1 - before
        cols = slice(q * gw, (q + 1) * gw)
        tot = ext_ref[pl.ds(sub - before, tm), cols]
        for s in range(-before + 1, after + 1):
            tot = tot + ext_ref[pl.ds(sub + s, tm), cols]
        lo = jnp.maximum(pos - before, 0)
        hi = jnp.minimum(pos + after, seq - 1)
        cnt = (hi - lo + 1).astype(F32)
        mq = (tot / cnt - po_ref[:, cols]).astype(BF16)
        yc = _dot(mq, wpl_ref[q]) * ps_ref[:, q * ow:(q + 1) * ow]
        m_ref[:, q * ow:(q + 1) * ow] += gate(2, q * ow, (q + 1) * ow) * yc

    y = _dot(m_ref[...].astype(BF16), wo_ref[...])
    o_ref[...] = x_ref[...] + _rms(y, gpost_ref[...])


def _merge(x2, z2, fy, po, gates, wha, wfo, wpool, pscale, w_out, g_post, layer, *, tm, seq):
    n, d = x2.shape
    pw = po.shape[1]
    sub = V7X_SUBLANES
    hb = tm // sub
    last_hb = n // sub - 1
    body = functools.partial(_merge_body, tm=tm, seq=seq, d=d)
    return pl.pallas_call(
        body,
        grid=(n // tm,),
        in_specs=[
            pl.BlockSpec((tm, d), lambda i: (i, 0)),
            pl.BlockSpec((tm, z2.shape[1]), lambda i: (i, 0)),
            pl.BlockSpec((tm, fy.shape[1]), lambda i: (i, 0)),
            pl.BlockSpec((tm, pw), lambda i: (i, 0)),
            pl.BlockSpec((sub, pw), lambda i: (jnp.maximum(i * hb - 1, 0), 0)),
            pl.BlockSpec((sub, pw), lambda i: (jnp.minimum((i + 1) * hb, last_hb), 0)),
            pl.BlockSpec((tm, gates.shape[1]), lambda i: (i, 0)),
            _resident_layer(wha, layer),
            _resident_layer(wfo, layer),
            _resident_layer(wpool, layer),
            _resident((1, d)),
            _resident(w_out.shape),
            _resident((1, d)),
        ],
        out_specs=pl.BlockSpec((tm, d), lambda i: (i, 0)),
        out_shape=jax.ShapeDtypeStruct((n, d), F32),
        scratch_shapes=[pltpu.VMEM((tm + 2 * sub, pw), F32), pltpu.VMEM((tm, d), F32)],
        compiler_params=_params(1),
        name="mixer_merge",
    )(x2, z2, fy, po, po, po, gates, wha, wfo, wpool, pscale.reshape(1, d), w_out, g_post.reshape(1, d))


def _kv_body(m_ref, g_ref, w_ref, o_ref, mn_ref):
    @pl.when((pl.program_id(0) == 0) | (pl.num_programs(1) > 1))
    def _():
        mn_ref[...] = _rms(m_ref[...], g_ref[...]).astype(BF16)

    o_ref[...] = _dot(mn_ref[...], w_ref[...].astype(BF16)).astype(BF16)


def _kv(mem2, g, w_kv, layer, *, n_mem, tn):
    n, d = mem2.shape
    cols = w_kv.shape[2]
    return pl.pallas_call(
        _kv_body,
        grid=(cols // tn, n // n_mem),
        in_specs=[
            pl.BlockSpec((n_mem, d), lambda j, b: (b, 0)),
            pl.BlockSpec((1, d), lambda j, b: (0, 0)),
            pl.BlockSpec((None, d, tn), lambda j, b: (layer, 0, j)),
        ],
        out_specs=pl.BlockSpec((n_mem, tn), lambda j, b: (b, j)),
        out_shape=jax.ShapeDtypeStruct((n, cols), BF16),
        scratch_shapes=[pltpu.VMEM((n_mem, d), BF16)],
        compiler_params=_params(2),
        name="mem_kv",
    )(mem2, g.reshape(1, d), w_kv)


def _attn_body(x_ref, gpre_ref, wq_ref, k_ref, v_ref, wo_ref, gpost_ref, o_ref, *, d):
    dh = d // N_HEADS
    x = x_ref[...]
    q = _dot(_rms(x, gpre_ref[...]).astype(BF16), wq_ref[...])
    heads = []
    for h in range(N_HEADS):
        cols = slice(h * dh, (h + 1) * dh)
        s = lax.dot_general(q[:, cols].astype(BF16), k_ref[:, cols], (((1,), (1,)), ((), ())),
                            preferred_element_type=F32) * (dh ** -0.5)
        e = jnp.exp(s - jnp.max(s, axis=-1, keepdims=True))
        p = e / jnp.sum(e, axis=-1, keepdims=True)
        heads.append(_dot(p.astype(BF16), v_ref[:, cols]))
    o = jnp.concatenate(heads, axis=1).astype(BF16)
    o_ref[...] = x + _rms(_dot(o, wo_ref[...]), gpost_ref[...])


def _attn(x2, g_pre, w_q, kv, w_o, g_post, *, tm, seq, n_mem):
    n, d = x2.shape
    tiles_per_seq = seq // tm
    return pl.pallas_call(
        functools.partial(_attn_body, d=d),
        grid=(n // tm,),
        in_specs=[
            pl.BlockSpec((tm, d), lambda i: (i, 0)),
            _resident((1, d)),
            _resident(w_q.shape),
            pl.BlockSpec((n_mem, d), lambda i: (i // tiles_per_seq, 0)),
            pl.BlockSpec((n_mem, d), lambda i: (i // tiles_per_seq, 1)),
            _resident(w_o.shape),
            _resident((1, d)),
        ],
        out_specs=pl.BlockSpec((tm, d), lambda i: (i, 0)),
        out_shape=jax.ShapeDtypeStruct((n, d), F32),
        compiler_params=_params(1),
        name="mem_attn",
    )(x2, g_pre.reshape(1, d), w_q, kv, kv, w_o, g_post.reshape(1, d))


def kernel(x, mem, g_ffn1_pre, w_ffn1_gu, w_ffn1_down, g_ffn1_post, g_mix_pre, w_in, hyena_conv_w,
           hyena_conv_b, filt_w1, filt_b1, filt_w2, filt_b2, filt_w3, filt_b3, filt_w4, filt_freq,
           hyena_d, w_hyena_out, w_fnet_out, w_pool, pool_scale, w_out, g_mix_post, g_mem_pre,
           g_mem_kv, w_q, w_kv, w_o, g_mem_post, g_ffn2_pre, w_ffn2_gu, w_ffn2_down, g_ffn2_post):
    batch, seq, d = x.shape
    n_mem = mem.shape[1]
    depth = w_in.shape[0]
    hy_w = w_hyena_out.shape[1]
    hy_cols = hyena_conv_w.shape[2]
    fw = w_fnet_out.shape[1]
    pw = w_pool.shape[1] * w_pool.shape[2]

    tm = 512
    tn = fw + pw
    tf = 512
    ct = V7X_MXU_COLS

    lhs1, gmat, lhs2 = _hyena_mats(seq)
    cs = _fnet_channel_mat(fw // FNET_GROUPS)
    fma, fmb, fmc, fr = _fnet3_mats(seq)

    bf = lambda w: w.astype(BF16)
    w_hyena_out, w_fnet_out, w_pool = map(bf, (w_hyena_out, w_fnet_out, w_pool))
    x2 = x.reshape(batch * seq, d)
    mem2 = mem.reshape(batch * n_mem, d)
    for l in range(depth):
        h3 = _filter_mlp(filt_w1[l], filt_b1[l], filt_w2[l], filt_b2[l], filt_w3[l], filt_b3[l],
                         filt_freq[l], seq=seq)
        hspec, *ffn_w = _filter_spectrum(h3, filt_w4[l], hyena_d[l], lhs1, gmat, seq=seq, width=hy_w,
                                         cw=ct // 2, ct=ct, cast=[(w_ffn1_gu, l), (w_ffn1_down, l)])

        later = (w_ffn2_gu, w_ffn2_down, w_in, w_out, w_q, w_o)
        x2, *cast = _ffn(x2, g_ffn1_pre[l], *ffn_w, g_ffn1_post[l], tm=2 * tm, tf=tf,
                         cast=[(w, l) for w in later])
        *ffn_w, w_in_l, w_out_l, w_q_l, w_o_l = cast

        hyc, pq, po, gates = _proj(x2, g_mix_pre[l], w_in_l, hyena_conv_w[l], hyena_conv_b[l], cs,
                                   tm=2 * tm, tn=tn, seq=seq, hy_cols=hy_cols, fw=fw, pw=pw)
        z2 = _hyena(hyc, hspec, lhs1, gmat, lhs2, seq=seq, width=hy_w, ct=ct)
        fy = _fnet3(pq, fma, fmb, fmc, seq=seq, fw=fw, ct=ct, r=fr)
        x2 = _merge(x2, z2, fy, po, gates, w_hyena_out, w_fnet_out, w_pool,
                    pool_scale[l], w_out_l, g_mix_post[l], l, tm=tm, seq=seq)

        kv = _kv(mem2, g_mem_kv[l], w_kv, l, n_mem=batch * n_mem, tn=tn)
        x2 = _attn(x2, g_mem_pre[l], w_q_l, kv, w_o_l, g_mem_post[l], tm=tm, seq=seq, n_mem=n_mem)

        x2 = _ffn_stream(x2, g_ffn2_pre[l], *ffn_w, g_ffn2_post[l], tm=2 * tm, tf=tf)
    return x2.reshape(batch, seq, d)
```
